```python
import math
import jax, jax.numpy as jnp
from jax import lax
import numpy as np

D_MODEL = 2048
BATCH = 4
SEQ = 2048
DEPTH = 1
DEC_BATCH = 128
DEC_SEQ = 1
PAST_LEN = 16384
PAGE_SIZE = 128

D_MIX = D_MODEL
D_S5 = D_MIX // 2
D_ML = D_MIX - D_S5
S5_CH = 16
N_GROUPS = D_S5 // S5_CH
S5_STATE = 64
N_HEADS_ML = 4
DH = D_ML // N_HEADS_ML
CONV_W = 4
MLSTM_CHUNK = 128
N_META = 16
N_EXPERTS = 32
TOP_K = 4
D_FF = D_MODEL
SWIGLU_LIMIT = 7.0
SWIGLU_ALPHA = 1.702
MOE_BLOCK = 128
LN_EPS = 1e-5
DEEPNORM_ALPHA = (2 * DEPTH) ** 0.25
DEEPNORM_BETA = (8 * DEPTH) ** -0.25
N_IN = D_S5 + 3 * D_ML + 2 * N_HEADS_ML

kernel_name = 'hymba_s5_mlstm_moe_deepnorm_step'


def layer_norm(x, g, b):
    xf = x.astype(jnp.float32)
    mu = xf.mean(-1, keepdims=True)
    var = jnp.square(xf - mu).mean(-1, keepdims=True)
    y = (xf - mu) * lax.rsqrt(var + LN_EPS) * g.astype(jnp.float32) + b.astype(jnp.float32)
    return y.astype(x.dtype)


def _linear_combine(e1, e2):
    a1, b1 = e1
    a2, b2 = e2
    return a2 * a1, a2 * b1 + b2


def s5_mixer(u, x0_re, x0_im, a_re, a_im, log_dt, b_re, b_im, c_re, c_im, d_skip, w_glu):
    f32 = jnp.float32
    bsz, t, _ = u.shape
    uf = u.astype(f32)
    ug = uf.reshape(bsz, t, N_GROUPS, S5_CH)
    lam = lax.complex(a_re.astype(f32), a_im.astype(f32))
    dt = jnp.exp(log_dt.astype(f32))[:, None]
    a_bar = jnp.exp(lam * dt)
    b_bar = ((a_bar - 1.0) / lam)[..., None] * lax.complex(b_re.astype(f32), b_im.astype(f32))
    bu = lax.complex(jnp.einsum('gpc,btgc->btgp', b_bar.real, ug),
                     jnp.einsum('gpc,btgc->btgp', b_bar.imag, ug))
    x0 = lax.complex(x0_re.astype(f32), x0_im.astype(f32))
    bu = bu.at[:, 0].add(a_bar * x0)
    a_seq = jnp.broadcast_to(a_bar, bu.shape)
    _, xs = lax.associative_scan(_linear_combine, (a_seq, bu), axis=1)
    y = (jnp.einsum('gcp,btgp->btgc', c_re.astype(f32), xs.real)
         - jnp.einsum('gcp,btgp->btgc', c_im.astype(f32), xs.imag))
    y = y.reshape(bsz, t, D_S5) + d_skip.astype(f32) * uf
    y = jax.nn.gelu(y, approximate=False)
    y = y * jax.nn.sigmoid(y @ w_glu.astype(f32))
    x_last = xs[:, -1]
    return y.astype(u.dtype), x_last.real, x_last.imag


def _mlstm_chunk(carry, inp):
    c, n, m = carry
    q, k, v, ig, lf = inp
    L = q.shape[2]
    b = jnp.cumsum(lf, axis=-1)
    causal = jnp.tril(jnp.ones((L, L), dtype=bool))
    dlog = jnp.where(causal, b[..., :, None] - b[..., None, :] + ig[..., None, :], -jnp.inf)
    inter = b + m[..., None]
    m_t = jnp.maximum(dlog.max(-1), inter)
    w = jnp.exp(dlog - m_t[..., None])
    g = jnp.exp(inter - m_t)
    s = jnp.einsum('bhtk,bhsk->bhts', q, k) * w
    num = jnp.einsum('bhts,bhsv->bhtv', s, v) + g[..., None] * jnp.einsum('bhvk,bhtk->bhtv', c, q)
    den = s.sum(-1) + g * jnp.einsum('bhk,bhtk->bht', n, q)
    h = num / jnp.maximum(jnp.abs(den), jnp.exp(-m_t))[..., None]
    wlog = b[..., -1:] - b + ig
    m_new = jnp.maximum(b[..., -1] + m, wlog.max(-1))
    w_end = jnp.exp(wlog - m_new[..., None])
    g_end = jnp.exp(b[..., -1] + m - m_new)
    c_new = g_end[..., None, None] * c + jnp.einsum('bhsv,bhsk->bhvk', v * w_end[..., None], k)
    n_new = g_end[..., None] * n + jnp.einsum('bhs,bhsk->bhk', w_end, k)
    return (c_new, n_new, m_new), h


def mlstm_mixer(xm, v, o_pre, ig, fg, conv0, c0, n0, m0, conv_w, conv_b, wq, wk, norm_g, lead, chunk):
    f32 = jnp.float32
    bsz, t, _ = xm.shape
    xcat = jnp.concatenate([conv0.astype(xm.dtype), xm], axis=1)
    xc = conv_b.astype(f32)
    for j in range(CONV_W):
        xc = xc + xcat[:, j:j + t].astype(f32) * conv_w[j].astype(f32)
    xc = jax.nn.silu(xc).reshape(bsz, t, N_HEADS_ML, DH)
    q = jnp.einsum('bthd,hde->bhte', xc, wq.astype(f32))
    k = jnp.einsum('bthd,hde->bhte', xc, wk.astype(f32)) * DH ** -0.5
    vh = v.astype(f32).reshape(bsz, t, N_HEADS_ML, DH).transpose(0, 2, 1, 3)
    igh = ig.astype(f32).transpose(0, 2, 1)
    lfh = jax.nn.log_sigmoid(fg.astype(f32)).transpose(0, 2, 1)
    carry = (c0.astype(f32), n0.astype(f32), m0.astype(f32))
    pieces = []
    if lead > 0:
        carry, h_lead = _mlstm_chunk(carry, (q[:, :, :lead], k[:, :, :lead], vh[:, :, :lead],
                                             igh[:, :, :lead], lfh[:, :, :lead]))
        pieces.append(h_lead)
    n_chunks = (t - lead) // chunk

    def to_chunks(a):
        a = a[:, :, lead:]
        a = a.reshape(a.shape[:2] + (n_chunks, chunk) + a.shape[3:])
        return jnp.moveaxis(a, 2, 0)

    carry, h_rest = lax.scan(_mlstm_chunk, carry,
                             (to_chunks(q), to_chunks(k), to_chunks(vh), to_chunks(igh), to_chunks(lfh)))
    pieces.append(jnp.moveaxis(h_rest, 0, 2).reshape(bsz, N_HEADS_ML, n_chunks * chunk, DH))
    h = jnp.concatenate(pieces, axis=2)
    mu = h.mean(-1, keepdims=True)
    var = jnp.square(h - mu).mean(-1, keepdims=True)
    h = ((h - mu) * lax.rsqrt(var + LN_EPS)).transpose(0, 2, 1, 3).reshape(bsz, t, D_ML) * norm_g.astype(f32)
    out = jax.nn.sigmoid(o_pre.astype(f32)) * h
    c_new, n_new, m_new = carry
    return out.astype(xm.dtype), c_new, n_new, m_new, xcat[:, t:]


def token_mixer(h, s5_re0, s5_im0, c0, n0, m0, conv0, w_in, b_in, a_re, a_im, log_dt, b_re, b_im,
                c_re, c_im, d_skip, w_glu, conv_w, conv_b, wq, wk, norm_g, w_out, lead, chunk):
    z = h @ w_in + b_in
    i0 = D_S5
    i1 = i0 + D_ML
    i2 = i1 + D_ML
    i3 = i2 + D_ML
    i4 = i3 + N_HEADS_ML
    y_s5, s5_re, s5_im = s5_mixer(z[..., :i0], s5_re0, s5_im0, a_re, a_im, log_dt, b_re, b_im,
                                  c_re, c_im, d_skip, w_glu)
    y_ml, c, n, m, conv = mlstm_mixer(z[..., i0:i1], z[..., i1:i2], z[..., i2:i3], z[..., i3:i4], z[..., i4:],
                                      conv0, c0, n0, m0, conv_w, conv_b, wq, wk, norm_g, lead, chunk)
    y = jnp.concatenate([y_s5.astype(h.dtype), y_ml.astype(h.dtype)], axis=-1) @ w_out
    return y, (s5_re, s5_im, c, n, m, conv)


def moe_ffn(x, router_w, router_b, w_gu, b_gu, w_dn, b_dn):
    f32 = jnp.float32
    t, dm = x.shape
    logits = (x @ router_w).astype(f32) + router_b.astype(f32)
    top_v, top_e = lax.top_k(logits, TOP_K)
    gates = jax.nn.softmax(top_v, axis=-1)
    n = t * TOP_K
    flat_e = top_e.reshape(-1)
    order = jnp.argsort(flat_e)
    e_sorted = flat_e[order]
    tok_sorted = order // TOP_K
    gate_sorted = gates.reshape(-1)[order]
    sizes = jnp.bincount(flat_e, length=N_EXPERTS)
    padded = (sizes + MOE_BLOCK - 1) // MOE_BLOCK * MOE_BLOCK
    raw_start = jnp.cumsum(sizes) - sizes
    pad_end = jnp.cumsum(padded)
    pad_start = pad_end - padded
    dest = pad_start[e_sorted] + jnp.arange(n) - raw_start[e_sorted]
    n_blocks = -(-n // MOE_BLOCK) + N_EXPERTS
    rows = n_blocks * MOE_BLOCK
    xbuf = jnp.zeros((rows, dm), x.dtype).at[dest].set(x[tok_sorted])
    blk_e = jnp.minimum(jnp.searchsorted(pad_end, jnp.arange(n_blocks) * MOE_BLOCK, side='right'),
                        N_EXPERTS - 1)

    def expert_block(args):
        xb, e = args
        hgu = xb @ w_gu[e] + b_gu[e]
        x_glu = jnp.minimum(hgu[:, :D_FF], SWIGLU_LIMIT)
        x_lin = jnp.clip(hgu[:, D_FF:], -SWIGLU_LIMIT, SWIGLU_LIMIT)
        act = x_glu * jax.nn.sigmoid(SWIGLU_ALPHA * x_glu) * (x_lin + 1.0)
        return act @ w_dn[e] + b_dn[e]

    ybuf = lax.map(expert_block, (xbuf.reshape(n_blocks, MOE_BLOCK, dm), blk_e)).reshape(rows, dm)
    contrib = ybuf[dest].astype(f32) * gate_sorted[:, None]
    y = jax.ops.segment_sum(contrib, tok_sorted, num_segments=t)
    return y.astype(x.dtype)


def setup_inputs(seed: int = 0) -> dict:
    key = jax.random.key(seed)
    ks = iter(jax.random.split(key, 64))
    f32 = jnp.float32
    L = DEPTH

    def nrm(shape, scale=1.0):
        return scale * jax.random.normal(next(ks), shape, f32)

    x_prompt = nrm((BATCH, SEQ, D_MODEL))
    x_sample = nrm((DEC_BATCH, DEC_SEQ, D_MODEL))
    state_s5_re = nrm((L, DEC_BATCH, N_GROUPS, S5_STATE), 0.1)
    state_s5_im = nrm((L, DEC_BATCH, N_GROUPS, S5_STATE), 0.1)
    state_mlstm_c = nrm((L, DEC_BATCH, N_HEADS_ML, DH, DH), 0.1)
    state_mlstm_n = nrm((L, DEC_BATCH, N_HEADS_ML, DH), 0.1)
    state_mlstm_m = nrm((L, DEC_BATCH, N_HEADS_ML))
    state_mlstm_conv = nrm((L, DEC_BATCH, CONV_W - 1, D_ML))
    meta_tokens = nrm((N_META, D_MODEL))
    ln_in_g = 1.0 + nrm((D_MODEL,), 0.02)
    ln_in_b = nrm((D_MODEL,), 0.02)
    w_in = nrm((L, D_MODEL, N_IN), D_MODEL ** -0.5)
    b_in = jnp.concatenate([nrm((L, D_S5 + 3 * D_ML), 0.02),
                            nrm((L, N_HEADS_ML), 0.1),
                            jnp.linspace(3.0, 6.0, N_HEADS_ML)[None] + nrm((L, N_HEADS_ML), 0.1)], axis=-1)
    s5_a_re = -0.5 + nrm((L, N_GROUPS, S5_STATE), 0.01)
    s5_a_im = jnp.pi * jnp.arange(S5_STATE, dtype=f32) + nrm((L, N_GROUPS, S5_STATE), 0.01)
    s5_log_dt = jax.random.uniform(next(ks), (L, N_GROUPS), f32, math.log(1e-3), math.log(1e-1))
    s5_b_re = nrm((L, N_GROUPS, S5_STATE, S5_CH), (2 * S5_CH) ** -0.5)
    s5_b_im = nrm((L, N_GROUPS, S5_STATE, S5_CH), (2 * S5_CH) ** -0.5)
    s5_c_re = nrm((L, N_GROUPS, S5_CH, S5_STATE), S5_STATE ** -0.5)
    s5_c_im = nrm((L, N_GROUPS, S5_CH, S5_STATE), S5_STATE ** -0.5)
    s5_d = nrm((L, D_S5))
    s5_w_glu = nrm((L, D_S5, D_S5), D_S5 ** -0.5)
    mlstm_conv_w = nrm((L, CONV_W, D_ML), CONV_W ** -0.5)
    mlstm_conv_b = nrm((L, D_ML), 0.02)
    mlstm_wq = nrm((L, N_HEADS_ML, DH, DH), DH ** -0.5)
    mlstm_wk = nrm((L, N_HEADS_ML, DH, DH), DH ** -0.5)
    mlstm_norm_g = 1.0 + nrm((L, D_ML), 0.02)
    w_out = nrm((L, D_MIX, D_MODEL), D_MIX ** -0.5 * DEEPNORM_BETA)
    ln1_g = 1.0 + nrm((L, D_MODEL), 0.02)
    ln1_b = nrm((L, D_MODEL), 0.02)
    router_w = nrm((L, D_MODEL, N_EXPERTS), D_MODEL ** -0.5)
    router_b = nrm((L, N_EXPERTS), 0.01)
    w_gate_up = nrm((L, N_EXPERTS, D_MODEL, 2 * D_FF), D_MODEL ** -0.5)
    b_gate_up = nrm((L, N_EXPERTS, 2 * D_FF), 0.02)
    w_down = nrm((L, N_EXPERTS, D_FF, D_MODEL), D_FF ** -0.5 * DEEPNORM_BETA)
    b_down = nrm((L, N_EXPERTS, D_MODEL), 0.02)
    ln2_g = 1.0 + nrm((L, D_MODEL), 0.02)
    ln2_b = nrm((L, D_MODEL), 0.02)
    return {'x_prompt': x_prompt, 'x_sample': x_sample,
            'state_s5_re': state_s5_re, 'state_s5_im': state_s5_im,
            'state_mlstm_c': state_mlstm_c, 'state_mlstm_n': state_mlstm_n,
            'state_mlstm_m': state_mlstm_m, 'state_mlstm_conv': state_mlstm_conv,
            'meta_tokens': meta_tokens, 'ln_in_g': ln_in_g, 'ln_in_b': ln_in_b,
            'w_in': w_in, 'b_in': b_in,
            's5_a_re': s5_a_re, 's5_a_im': s5_a_im, 's5_log_dt': s5_log_dt,
            's5_b_re': s5_b_re, 's5_b_im': s5_b_im, 's5_c_re': s5_c_re, 's5_c_im': s5_c_im,
            's5_d': s5_d, 's5_w_glu': s5_w_glu,
            'mlstm_conv_w': mlstm_conv_w, 'mlstm_conv_b': mlstm_conv_b,
            'mlstm_wq': mlstm_wq, 'mlstm_wk': mlstm_wk, 'mlstm_norm_g': mlstm_norm_g,
            'w_out': w_out, 'ln1_g': ln1_g, 'ln1_b': ln1_b,
            'router_w': router_w, 'router_b': router_b,
            'w_gate_up': w_gate_up, 'b_gate_up': b_gate_up, 'w_down': w_down, 'b_down': b_down,
            'ln2_g': ln2_g, 'ln2_b': ln2_b}


def reference(x_prompt, x_sample, state_s5_re, state_s5_im, state_mlstm_c, state_mlstm_n, state_mlstm_m,
              state_mlstm_conv, meta_tokens, ln_in_g, ln_in_b, w_in, b_in, s5_a_re, s5_a_im, s5_log_dt,
              s5_b_re, s5_b_im, s5_c_re, s5_c_im, s5_d, s5_w_glu, mlstm_conv_w, mlstm_conv_b, mlstm_wq,
              mlstm_wk, mlstm_norm_g, w_out, ln1_g, ln1_b, router_w, router_b, w_gate_up, b_gate_up,
              w_down, b_down, ln2_g, ln2_b):
    f32 = jnp.float32
    bsz = x_prompt.shape[0]
    dec_b, dec_t, _ = x_sample.shape
    meta = jnp.broadcast_to(meta_tokens.astype(x_prompt.dtype)[None], (bsz, N_META, D_MODEL))
    hp = layer_norm(jnp.concatenate([meta, x_prompt], axis=1), ln_in_g, ln_in_b)
    hs = layer_norm(x_sample, ln_in_g, ln_in_b)
    zero_state = (jnp.zeros((bsz, N_GROUPS, S5_STATE), f32), jnp.zeros((bsz, N_GROUPS, S5_STATE), f32),
                  jnp.zeros((bsz, N_HEADS_ML, DH, DH), f32), jnp.zeros((bsz, N_HEADS_ML, DH), f32),
                  jnp.zeros((bsz, N_HEADS_ML), f32), jnp.zeros((bsz, CONV_W - 1, D_ML), x_prompt.dtype))
    new_p = ([], [], [], [], [], [])
    new_s = ([], [], [], [], [], [])
    for l in range(DEPTH):
        mix = (w_in[l], b_in[l], s5_a_re[l], s5_a_im[l], s5_log_dt[l], s5_b_re[l], s5_b_im[l],
               s5_c_re[l], s5_c_im[l], s5_d[l], s5_w_glu[l], mlstm_conv_w[l], mlstm_conv_b[l],
               mlstm_wq[l], mlstm_wk[l], mlstm_norm_g[l], w_out[l])
        mp, st_p = token_mixer(hp, *zero_state, *mix, lead=N_META, chunk=MLSTM_CHUNK)
        ms, st_s = token_mixer(hs, state_s5_re[l], state_s5_im[l], state_mlstm_c[l], state_mlstm_n[l],
                               state_mlstm_m[l], state_mlstm_conv[l], *mix, lead=0, chunk=dec_t)
        hp = layer_norm(DEEPNORM_ALPHA * hp + mp, ln1_g[l], ln1_b[l])
        hs = layer_norm(DEEPNORM_ALPHA * hs + ms, ln1_g[l], ln1_b[l])
        n_p = bsz * hp.shape[1]
        tokens = jnp.concatenate([hp.reshape(n_p, D_MODEL), hs.reshape(dec_b * dec_t, D_MODEL)], axis=0)
        f = moe_ffn(tokens, router_w[l], router_b[l], w_gate_up[l], b_gate_up[l], w_down[l], b_down[l])
        hp = layer_norm(DEEPNORM_ALPHA * hp + f[:n_p].reshape(hp.shape), ln2_g[l], ln2_b[l])
        hs = layer_norm(DEEPNORM_ALPHA * hs + f[n_p:].reshape(hs.shape), ln2_g[l], ln2_b[l])
        for i in range(6):
            new_p[i].append(st_p[i])
            new_s[i].append(st_s[i])
    y_prompt = hp[:, N_META:]
    y_sample = hs
    s5_re_p = jnp.stack(new_p[0]).astype(state_s5_re.dtype)
    s5_im_p = jnp.stack(new_p[1]).astype(state_s5_im.dtype)
    mlstm_c_p = jnp.stack(new_p[2]).astype(state_mlstm_c.dtype)
    mlstm_n_p = jnp.stack(new_p[3]).astype(state_mlstm_n.dtype)
    mlstm_m_p = jnp.stack(new_p[4]).astype(state_mlstm_m.dtype)
    mlstm_conv_p = jnp.stack(new_p[5]).astype(state_mlstm_conv.dtype)
    s5_re_s = jnp.stack(new_s[0]).astype(state_s5_re.dtype)
    s5_im_s = jnp.stack(new_s[1]).astype(state_s5_im.dtype)
    mlstm_c_s = jnp.stack(new_s[2]).astype(state_mlstm_c.dtype)
    mlstm_n_s = jnp.stack(new_s[3]).astype(state_mlstm_n.dtype)
    mlstm_m_s = jnp.stack(new_s[4]).astype(state_mlstm_m.dtype)
    mlstm_conv_s = jnp.stack(new_s[5]).astype(state_mlstm_conv.dtype)
    return (y_prompt, y_sample, s5_re_p, s5_im_p, mlstm_c_p, mlstm_n_p, mlstm_m_p, mlstm_conv_p,
            s5_re_s, s5_im_s, mlstm_c_s, mlstm_n_s, mlstm_m_s, mlstm_conv_s)
```

```python
import functools
import math

import jax
import jax.numpy as jnp
from jax import lax
from jax.experimental import pallas as pl
from jax.experimental.pallas import tpu as pltpu

F32 = jnp.float32
BF16 = jnp.bfloat16

D_MODEL = 2048
N_META = 16
D_S5 = 1024
D_ML = 1024
S5_CH = 16
N_GROUPS = 64
S5_STATE = 64
N_STATE = N_GROUPS * S5_STATE
N_HEADS = 4
DH = 256
CONV_W = 4
N_EXPERTS = 32
TOP_K = 4
D_FF = 2048
SWIGLU_LIMIT = 7.0
SWIGLU_ALPHA = 1.702
LN_EPS = 1e-5
DEEPNORM_ALPHA = 2.0 ** 0.25
N_IN = D_S5 + 3 * D_ML + 2 * N_HEADS

LANES = 128
SUBLANES = 8
MXU_DIM = 256

TILE_T = 128
SEG_LEN = TILE_T // SUBLANES
N_IN_PAD = 33 * LANES
GATE_COL = 4 * D_S5 // LANES
S5_KCH = D_S5 // MXU_DIM
S5_SCH = N_STATE // S5_KCH
SCAN_LW = 256


def _cparams(sem, vmem_mb=None):
    kw = dict(dimension_semantics=sem)
    if vmem_mb is not None:
        kw["vmem_limit_bytes"] = vmem_mb * 1024 * 1024
    return pltpu.CompilerParams(**kw)


def _layer_norm(x, g, b):
    mu = jnp.mean(x, axis=-1, keepdims=True)
    xc = x - mu
    var = jnp.mean(xc * xc, axis=-1, keepdims=True)
    return xc * lax.rsqrt(var + LN_EPS) * g + b


def _inproj_body(x_ref, g_ref, b_ref, w_ref, bias_ref, z_ref, hn_ref):
    @pl.when(pl.program_id(1) == 0)
    def _():
        hn_ref[...] = _layer_norm(x_ref[...], g_ref[...], b_ref[...]).astype(BF16)

    z_ref[...] = jnp.dot(hn_ref[...], w_ref[...], preferred_element_type=F32) + bias_ref[...]


def _inproj(x, g, b, w, bias, tm, tn):
    rows, d = x.shape
    n = w.shape[1]
    return pl.pallas_call(
        _inproj_body,
        grid=(rows // tm, n // tn),
        in_specs=[
            pl.BlockSpec((tm, d), lambda i, j: (i, 0)),
            pl.BlockSpec((1, d), lambda i, j: (0, 0)),
            pl.BlockSpec((1, d), lambda i, j: (0, 0)),
            pl.BlockSpec((d, tn), lambda i, j: (0, j)),
            pl.BlockSpec((1, tn), lambda i, j: (0, j)),
        ],
        out_specs=pl.BlockSpec((tm, tn), lambda i, j: (i, j)),
        out_shape=jax.ShapeDtypeStruct((rows, n), F32),
        scratch_shapes=[pltpu.VMEM((tm, d), BF16)],
        compiler_params=_cparams(("arbitrary", "arbitrary"), 48),
        name="inproj",
    )(x, g, b, w, bias)


def _cmul(ar, ai, br, bi):
    return ar * br - ai * bi, ar * bi + ai * br


def _s5_prep_body(are_ref, aim_ref, dt_ref, bre_ref, bim_ref,
                  pre_ref, pim_ref, hre_ref, him_ref, qre_ref, qim_ref, bbre_ref, bbim_ref):
    lr = are_ref[...]
    li = aim_ref[...]
    dt = jnp.exp(dt_ref[...])
    mag = jnp.exp(lr * dt)
    ar = mag * jnp.cos(li * dt)
    ai = mag * jnp.sin(li * dt)
    nr = ar - 1.0
    ni = ai
    den = lr * lr + li * li
    cr = (nr * lr + ni * li) / den
    ci = (ni * lr - nr * li) / den
    bbre_ref[...] = cr * bre_ref[...] - ci * bim_ref[...]
    bbim_ref[...] = cr * bim_ref[...] + ci * bre_ref[...]
    pr, pi = ar, ai
    for j in range(SEG_LEN):
        pre_ref[j:j + 1, :] = pr
        pim_ref[j:j + 1, :] = pi
        if j + 1 < SEG_LEN:
            pr, pi = _cmul(pr, pi, ar, ai)
    row = lax.broadcasted_iota(jnp.int32, (SUBLANES, N_STATE), 0)
    kr, ki = pr, pi
    for idx, k in enumerate((1, 2, 4)):
        hre_ref[idx * 8:(idx + 1) * 8, :] = jnp.where(row >= k, jnp.broadcast_to(kr, (SUBLANES, N_STATE)), 0.0)
        him_ref[idx * 8:(idx + 1) * 8, :] = jnp.where(row >= k, jnp.broadcast_to(ki, (SUBLANES, N_STATE)), 0.0)
        kr, ki = _cmul(kr, ki, kr, ki)
    qr, qi = pr, pi
    for s in range(SUBLANES):
        qre_ref[s:s + 1, :] = qr
        qim_ref[s:s + 1, :] = qi
        if s + 1 < SUBLANES:
            qr, qi = _cmul(qr, qi, pr, pi)


def _s5_prep(a_re, a_im, log_dt, b_re, b_im):
    n = N_STATE
    are = a_re.reshape(1, n)
    aim = a_im.reshape(1, n)
    dtl = jnp.broadcast_to(log_dt[:, None], (N_GROUPS, S5_STATE)).reshape(1, n)
    bre = b_re.transpose(2, 0, 1).reshape(S5_CH, n)
    bim = b_im.transpose(2, 0, 1).reshape(S5_CH, n)
    shp = lambda r: jax.ShapeDtypeStruct((r, n), F32)
    return pl.pallas_call(
        _s5_prep_body,
        out_shape=(shp(SEG_LEN), shp(SEG_LEN), shp(24), shp(24), shp(8), shp(8), shp(S5_CH), shp(S5_CH)),
        name="s5_prep",
    )(are, aim, dtl, bre, bim)


def _s5_block_diag(bb_re, bb_im, c_re, c_im):
    gpc = 16
    eye = jnp.eye(gpc, dtype=bool)

    def bd_in(bb):
        t = bb.reshape(S5_CH, S5_KCH, gpc, S5_STATE).transpose(1, 2, 0, 3)
        t = jnp.where(eye[None, :, None, :, None], t[:, :, :, None, :], 0.0)
        return t.reshape(S5_KCH, gpc * S5_CH, gpc * S5_STATE)

    def bd_out(c):
        t = c.reshape(S5_KCH, gpc, S5_CH, S5_STATE).transpose(0, 1, 3, 2)
        t = jnp.where(eye[None, :, None, :, None], t[:, :, :, None, :], 0.0)
        return t.reshape(S5_KCH, gpc * S5_STATE, gpc * S5_CH)

    bd_b = jnp.concatenate([bd_in(bb_re), bd_in(bb_im)], axis=2).astype(BF16)
    bd_c = jnp.concatenate([bd_out(c_re), bd_out(-c_im)], axis=1).astype(BF16)
    return bd_b, bd_c


def _gelu_glu(y, wglu_ref):
    y = 0.5 * y * (1.0 + lax.erf(y * math.sqrt(0.5)))
    gate = jnp.dot(y.astype(BF16), wglu_ref[...], preferred_element_type=F32)
    return y * jax.nn.sigmoid(gate)


def _s5_in_proj(u_bf, bdb_ref, bur_ref, bui_ref):
    for c in range(S5_KCH):
        r = jnp.dot(u_bf[:, c * MXU_DIM:(c + 1) * MXU_DIM], bdb_ref[c], preferred_element_type=F32)
        bur_ref[:, c * S5_SCH:(c + 1) * S5_SCH] = r[:, :S5_SCH]
        bui_ref[:, c * S5_SCH:(c + 1) * S5_SCH] = r[:, S5_SCH:]


def _s5_out_proj(xr_ref, xi_ref, bdc_ref):
    ys = []
    for c in range(S5_KCH):
        xr = xr_ref[:, c * S5_SCH:(c + 1) * S5_SCH].astype(BF16)
        xi = xi_ref[:, c * S5_SCH:(c + 1) * S5_SCH].astype(BF16)
        ys.append(jnp.dot(xr, bdc_ref[c, :S5_SCH, :], preferred_element_type=F32)
                  + jnp.dot(xi, bdc_ref[c, S5_SCH:, :], preferred_element_type=F32))
    return jnp.concatenate(ys, axis=1)


def _s5_seq_body(n_pad, us_ref, up_ref, x0r_ref, x0i_ref, bdb_ref, bdc_ref, pre_ref, pim_ref,
                 hre_ref, him_ref, qre_ref, qim_ref, d_ref, wglu_ref,
                 y_ref, xr_out_ref, xi_out_ref, bur_ref, bui_ref, cr_ref, ci_ref, perm_ref):
    t = pl.program_id(1)
    n = N_STATE

    @pl.when(t == 0)
    def _():
        cr_ref[...] = jnp.broadcast_to(x0r_ref[0], (SUBLANES, n))
        ci_ref[...] = jnp.broadcast_to(x0i_ref[0], (SUBLANES, n))

    n_lc = D_S5 // LANES

    def load_perm(ref):
        for c in range(n_lc):
            perm_ref[c] = ref[:, c * LANES:(c + 1) * LANES]
        return jnp.concatenate(
            [jnp.concatenate([perm_ref[c, pl.ds(j, SUBLANES, stride=SEG_LEN), :] for c in range(n_lc)], axis=1)
             for j in range(SEG_LEN)], axis=0)

    prow = lax.broadcasted_iota(jnp.int32, (TILE_T, 1), 0)
    time = (prow % SUBLANES) * SEG_LEN + prow // SUBLANES
    u_first = jnp.where(time >= n_pad, load_perm(us_ref), 0.0)
    u = jnp.where(t == 0, u_first, load_perm(up_ref))
    _s5_in_proj(u.astype(BF16), bdb_ref, bur_ref, bui_ref)

    row8 = lax.broadcasted_iota(jnp.int32, (SUBLANES, SCAN_LW), 0)

    def scan_lanes(lc, carry):
        ls = pl.ds(pl.multiple_of(lc * SCAN_LW, SCAN_LW), SCAN_LW)
        bc = lambda ref, j: jnp.broadcast_to(ref[j:j + 1, ls], (SUBLANES, SCAN_LW))
        ar, ai = bc(pre_ref, 0), bc(pim_ref, 0)
        xr = jnp.zeros((SUBLANES, SCAN_LW), F32)
        xi = jnp.zeros((SUBLANES, SCAN_LW), F32)
        for j in range(SEG_LEN):
            rs = slice(j * SUBLANES, (j + 1) * SUBLANES)
            nr = ar * xr - ai * xi + bur_ref[rs, ls]
            ni = ar * xi + ai * xr + bui_ref[rs, ls]
            xr, xi = nr, ni
            bur_ref[rs, ls] = xr
            bui_ref[rs, ls] = xi
        er, ei = xr, xi
        for idx, k in enumerate((1, 2, 4)):
            sr = pltpu.roll(er, k, axis=0)
            si = pltpu.roll(ei, k, axis=0)
            hr = hre_ref[idx * 8:(idx + 1) * 8, ls]
            hi = him_ref[idx * 8:(idx + 1) * 8, ls]
            er, ei = er + (hr * sr - hi * si), ei + (hr * si + hi * sr)
        cpr = cr_ref[:, ls]
        cpi = ci_ref[:, ls]
        qr = qre_ref[:, ls]
        qi = qim_ref[:, ls]
        er, ei = er + (qr * cpr - qi * cpi), ei + (qr * cpi + qi * cpr)
        inr = jnp.where(row8 == 0, cpr, pltpu.roll(er, 1, axis=0))
        ini = jnp.where(row8 == 0, cpi, pltpu.roll(ei, 1, axis=0))
        cr_ref[:, ls] = jnp.broadcast_to(er[SUBLANES - 1:SUBLANES, :], (SUBLANES, SCAN_LW))
        ci_ref[:, ls] = jnp.broadcast_to(ei[SUBLANES - 1:SUBLANES, :], (SUBLANES, SCAN_LW))
        for j in range(SEG_LEN):
            rs = slice(j * SUBLANES, (j + 1) * SUBLANES)
            pr, pi = bc(pre_ref, j), bc(pim_ref, j)
            bur_ref[rs, ls] = bur_ref[rs, ls] + (pr * inr - pi * ini)
            bui_ref[rs, ls] = bui_ref[rs, ls] + (pr * ini + pi * inr)
        return carry

    lax.fori_loop(0, n // SCAN_LW, scan_lanes, 0)

    y = _s5_out_proj(bur_ref, bui_ref, bdc_ref) + d_ref[...] * u
    y = _gelu_glu(y, wglu_ref)
    for j in range(SEG_LEN):
        for c in range(n_lc):
            perm_ref[c, pl.ds(j, SUBLANES, stride=SEG_LEN), :] = y[j * SUBLANES:(j + 1) * SUBLANES,
                                                                   c * LANES:(c + 1) * LANES]
    for c in range(n_lc):
        y_ref[:, c * LANES:(c + 1) * LANES] = perm_ref[c]

    @pl.when(t == pl.num_programs(1) - 1)
    def _():
        xr_out_ref[0] = cr_ref[0:1, :]
        xi_out_ref[0] = ci_ref[0:1, :]


def _s5_seq(z_small, z_seq, x0r, x0i, tabs, bd_b, bd_c, d_skip, wglu_bf, n_batch, n_tiles, n_pad):
    pre, pim, hre, him, qre, qim = tabs
    n = N_STATE
    full = lambda a: pl.BlockSpec(a.shape, lambda b, t: (0,) * a.ndim)
    per_seq = n_tiles - 1
    return pl.pallas_call(
        functools.partial(_s5_seq_body, n_pad),
        grid=(n_batch, n_tiles),
        in_specs=[
            pl.BlockSpec((TILE_T, D_S5), lambda b, t: (0, 0)),
            pl.BlockSpec((TILE_T, D_S5), lambda b, t: (b * per_seq + jnp.maximum(t - 1, 0), 0)),
            pl.BlockSpec((1, 1, n), lambda b, t: (b, 0, 0)),
            pl.BlockSpec((1, 1, n), lambda b, t: (b, 0, 0)),
            full(bd_b), full(bd_c), full(pre), full(pim), full(hre), full(him), full(qre), full(qim),
            full(d_skip), full(wglu_bf),
        ],
        out_specs=[
            pl.BlockSpec((TILE_T, D_S5), lambda b, t: (b * n_tiles + t, 0)),
            pl.BlockSpec((1, 1, n), lambda b, t: (b, 0, 0)),
            pl.BlockSpec((1, 1, n), lambda b, t: (b, 0, 0)),
        ],
        out_shape=[
            jax.ShapeDtypeStruct((n_batch * n_tiles * TILE_T, D_S5), F32),
            jax.ShapeDtypeStruct((n_batch, 1, n), F32),
            jax.ShapeDtypeStruct((n_batch, 1, n), F32),
        ],
        scratch_shapes=[pltpu.VMEM((TILE_T, n), F32), pltpu.VMEM((TILE_T, n), F32),
                        pltpu.VMEM((SUBLANES, n), F32), pltpu.VMEM((SUBLANES, n), F32),
                        pltpu.VMEM((D_S5 // LANES, TILE_T, LANES), F32)],
        compiler_params=_cparams(("arbitrary", "arbitrary"), 48),
        name="s5_seq",
    )(z_small, z_seq, x0r, x0i, bd_b, bd_c, pre, pim, hre, him, qre, qim, d_skip, wglu_bf)


def _s5_step_body(u_ref, x0r_ref, x0i_ref, bdb_ref, bdc_ref, pre_ref, pim_ref, d_ref, wglu_ref,
                  y_ref, xr_ref, xi_ref):
    u = u_ref[...]
    _s5_in_proj(u.astype(BF16), bdb_ref, xr_ref, xi_ref)
    ar = pre_ref[0:1, :]
    ai = pim_ref[0:1, :]
    x0r = x0r_ref[...]
    x0i = x0i_ref[...]
    xr_ref[...] = xr_ref[...] + (ar * x0r - ai * x0i)
    xi_ref[...] = xi_ref[...] + (ar * x0i + ai * x0r)
    y = _s5_out_proj(xr_ref, xi_ref, bdc_ref) + d_ref[...] * u
    y_ref[...] = _gelu_glu(y, wglu_ref)


def _s5_step(z_small, row_blk, x0r, x0i, tabs, bd_b, bd_c, d_skip, wglu_bf):
    pre, pim = tabs[0], tabs[1]
    rows = x0r.shape[0]
    n = N_STATE
    full = lambda a: pl.BlockSpec(a.shape, lambda i: (0,) * a.ndim)
    return pl.pallas_call(
        _s5_step_body,
        grid=(1,),
        in_specs=[pl.BlockSpec((rows, D_S5), lambda i: (row_blk, 0)), full(x0r), full(x0i),
                  full(bd_b), full(bd_c), full(pre), full(pim), full(d_skip), full(wglu_bf)],
        out_specs=[pl.BlockSpec((rows, D_S5), lambda i: (0, 0)),
                   pl.BlockSpec((rows, n), lambda i: (0, 0)), pl.BlockSpec((rows, n), lambda i: (0, 0))],
        out_shape=[jax.ShapeDtypeStruct((rows, D_S5), F32), jax.ShapeDtypeStruct((rows, n), F32),
                   jax.ShapeDtypeStruct((rows, n), F32)],
        compiler_params=_cparams(("arbitrary",), 48),
        name="s5_step",
    )(z_small, x0r, x0i, bd_b, bd_c, pre, pim, d_skip, wglu_bf)


NEG_INF = float("-inf")
N_STEP_SCALARS = 5 * N_HEADS


def _log_sigmoid(x):
    return jnp.minimum(x, 0.0) - jnp.log1p(jnp.exp(-jnp.abs(x)))


def _split3(x):
    hi = x.astype(BF16)
    r1 = x - hi.astype(F32)
    mid = r1.astype(BF16)
    lo = (r1 - mid.astype(F32)).astype(BF16)
    return hi, mid, lo


def _head_norm_gate(h, o, g):
    mu = jnp.mean(h, axis=-1, keepdims=True)
    hc = h - mu
    var = jnp.mean(hc * hc, axis=-1, keepdims=True)
    return jax.nn.sigmoid(o) * (hc * lax.rsqrt(var + LN_EPS) * g)


def _dot_nt(a, b):
    return lax.dot_general(a, b, (((1,), (1,)), ((), ())), preferred_element_type=F32)


def _dot_tn(a, b):
    return lax.dot_general(a, b, (((0,), (0,)), ((), ())), preferred_element_type=F32)


def _mlstm_seq_body(n_pad, sxm_ref, sv_ref, so_ref, sg_ref, pxm_ref, pv_ref, po_ref, pg_ref,
                    cw_ref, cb_ref, wq_ref, wk_ref, ng_ref,
                    y_ref, c_out_ref, n_out_ref, m_out_ref, c_ref, n_ref, m_ref, prev_ref):
    t = pl.program_id(1)
    L = TILE_T

    @pl.when(t == 0)
    def _():
        c_ref[...] = jnp.zeros_like(c_ref)
        n_ref[...] = jnp.zeros_like(n_ref)
        m_ref[...] = jnp.zeros_like(m_ref)
        prev_ref[...] = jnp.zeros_like(prev_ref)

    first = t == 0
    row = lax.broadcasted_iota(jnp.int32, (L, 1), 0)
    valid = jnp.logical_or(jnp.logical_not(first), row >= n_pad)
    xm = jnp.where(valid, jnp.where(first, sxm_ref[...], pxm_ref[...]), 0.0)
    v = jnp.where(first, sv_ref[...], pv_ref[...])
    o = jnp.where(first, so_ref[...], po_ref[...])
    gt = jnp.where(first, sg_ref[...], pg_ref[...])

    prev = prev_ref[...]

    def shifted(j):
        if j == 0:
            return xm
        return pltpu.roll(jnp.where(row >= L - j, prev, xm), j, axis=0)

    xc = cb_ref[...]
    for j in range(CONV_W):
        xc = xc + shifted(CONV_W - 1 - j) * cw_ref[j:j + 1, :]
    prev_ref[...] = xm
    xc = xc * jax.nn.sigmoid(xc)

    ig = jnp.where(valid, gt, NEG_INF)
    lf = jnp.where(valid, _log_sigmoid(gt), 0.0)
    ti = lax.broadcasted_iota(jnp.int32, (L, L), 0)
    si = lax.broadcasted_iota(jnp.int32, (L, L), 1)
    causal = si <= ti
    tri = jnp.where(causal, 1.0, 0.0).astype(BF16)
    bc = sum(jnp.dot(tri, p, preferred_element_type=F32) for p in _split3(lf))
    ig_t = ig.T
    bc_t = bc.T

    for h in range(N_HEADS):
        hs = slice(h * DH, (h + 1) * DH)
        b_col = bc[:, N_HEADS + h:N_HEADS + h + 1]
        b_row = bc_t[N_HEADS + h:N_HEADS + h + 1, :]
        ig_row = ig_t[h:h + 1, :]
        ig_col = ig[:, h:h + 1]
        m_prev = m_ref[h:h + 1, 0:1]
        dlog = jnp.where(causal, b_col - b_row + ig_row, NEG_INF)
        inter = b_col + m_prev
        m_t = jnp.maximum(jnp.max(dlog, axis=1, keepdims=True), inter)
        w = jnp.exp(dlog - m_t)
        g = jnp.exp(inter - m_t)
        xh = xc[:, hs].astype(BF16)
        q = jnp.dot(xh, wq_ref[h], preferred_element_type=F32)
        k = jnp.dot(xh, wk_ref[h], preferred_element_type=F32) * (DH ** -0.5)
        qb = q.astype(BF16)
        kb = k.astype(BF16)
        s = _dot_nt(qb, kb) * w
        vh = v[:, hs]
        cmat = c_ref[h]
        n_row = n_ref[h]
        num = jnp.dot(s.astype(BF16), vh.astype(BF16), preferred_element_type=F32) \
            + g * _dot_nt(qb, cmat.astype(BF16))
        den = jnp.sum(s, axis=1, keepdims=True) + g * jnp.sum(q * n_row, axis=1, keepdims=True)
        hh = num / jnp.maximum(jnp.abs(den), jnp.exp(-m_t))
        b_last = b_col[L - 1:L, :]
        wlog = b_last - b_col + ig_col
        m_new = jnp.maximum(b_last + m_prev, jnp.max(wlog, axis=0, keepdims=True))
        w_end = jnp.exp(wlog - m_new)
        g_end = jnp.exp(b_last + m_prev - m_new)
        c_ref[h] = g_end * cmat + _dot_tn((vh * w_end).astype(BF16), kb)
        n_ref[h] = g_end * n_row + jnp.sum(w_end * k, axis=0, keepdims=True)
        m_ref[h:h + 1, :] = jnp.broadcast_to(m_new, (1, LANES))
        y_ref[:, hs] = _head_norm_gate(hh, o[:, hs], ng_ref[:, hs])

    @pl.when(t == pl.num_programs(1) - 1)
    def _():
        c_out_ref[0] = c_ref[...]
        n_out_ref[0] = n_ref[...]
        m_out_ref[0] = m_ref[...]


def _mlstm_seq(z_small, z_seq, conv_w, conv_b, wq_bf, wk_bf, norm_g, n_batch, n_tiles, n_pad):
    per_seq = n_tiles - 1
    full = lambda a: pl.BlockSpec(a.shape, lambda b, t: (0,) * a.ndim)
    nb = D_ML // LANES

    def small(col, width):
        return pl.BlockSpec((TILE_T, width), lambda b, t: (0, col))

    def seq(col, width):
        return pl.BlockSpec((TILE_T, width), lambda b, t: (b * per_seq + jnp.maximum(t - 1, 0), col))

    return pl.pallas_call(
        functools.partial(_mlstm_seq_body, n_pad),
        grid=(n_batch, n_tiles),
        in_specs=[small(1, D_ML), small(2, D_ML), small(3, D_ML), small(GATE_COL, LANES),
                  seq(1, D_ML), seq(2, D_ML), seq(3, D_ML), seq(GATE_COL, LANES),
                  full(conv_w), full(conv_b), full(wq_bf), full(wk_bf), full(norm_g)],
        out_specs=[
            pl.BlockSpec((TILE_T, D_ML), lambda b, t: (b * n_tiles + t, 0)),
            pl.BlockSpec((1, N_HEADS, DH, DH), lambda b, t: (b, 0, 0, 0)),
            pl.BlockSpec((1, N_HEADS, 1, DH), lambda b, t: (b, 0, 0, 0)),
            pl.BlockSpec((1, SUBLANES, LANES), lambda b, t: (b, 0, 0)),
        ],
        out_shape=[
            jax.ShapeDtypeStruct((n_batch * n_tiles * TILE_T, D_ML), F32),
            jax.ShapeDtypeStruct((n_batch, N_HEADS, DH, DH), F32),
            jax.ShapeDtypeStruct((n_batch, N_HEADS, 1, DH), F32),
            jax.ShapeDtypeStruct((n_batch, SUBLANES, LANES), F32),
        ],
        scratch_shapes=[pltpu.VMEM((N_HEADS, DH, DH), F32), pltpu.VMEM((N_HEADS, 1, DH), F32),
                        pltpu.VMEM((SUBLANES, LANES), F32), pltpu.VMEM((TILE_T, D_ML), F32)],
        compiler_params=_cparams(("arbitrary", "arbitrary"), 48),
        name="mlstm_seq",
    )(z_small, z_small, z_small, z_small, z_seq, z_seq, z_seq, z_seq, conv_w, conv_b, wq_bf, wk_bf, norm_g)


def _mlstm_step_a_body(xm_ref, g_ref, conv0_ref, m0_ref, cw_ref, cb_ref, wq_ref, wk_ref,
                       q_ref, k_ref, sc_ref):
    xc = cb_ref[...]
    for j in range(CONV_W - 1):
        xc = xc + conv0_ref[j] * cw_ref[j:j + 1, :]
    xc = xc + xm_ref[...] * cw_ref[CONV_W - 1:CONV_W, :]
    xc = xc * jax.nn.sigmoid(xc)
    gt = g_ref[...]
    ig = gt[:, 0:N_HEADS]
    lf = _log_sigmoid(gt[:, N_HEADS:2 * N_HEADS])
    inter = lf + m0_ref[...]
    m_t = jnp.maximum(ig, inter)
    w = jnp.exp(ig - m_t)
    g = jnp.exp(inter - m_t)
    qks = []
    for h in range(N_HEADS):
        hs = slice(h * DH, (h + 1) * DH)
        xh = xc[:, hs].astype(BF16)
        q = jnp.dot(xh, wq_ref[h], preferred_element_type=F32)
        k = jnp.dot(xh, wk_ref[h], preferred_element_type=F32) * (DH ** -0.5)
        q_ref[:, hs] = q
        k_ref[:, hs] = k
        qks.append(jnp.sum(q * k, axis=1, keepdims=True))
    s = jnp.concatenate(qks, axis=1) * w
    rows = s.shape[0]
    sc_ref[...] = jnp.concatenate(
        [s, w, g, m_t, jnp.exp(-m_t), jnp.zeros((rows, LANES - 5 * N_HEADS), F32)], axis=1)


def _mlstm_step_b_body(bb, sc_ref, q_ref, k_ref, n_ref, v_ref, o_ref, ng_ref, c_ref,
                       y_ref, c_out_ref, n_out_ref):
    i0 = pl.program_id(0) * bb
    for i in range(bb):
        for h in range(N_HEADS):
            hs = slice(h * DH, (h + 1) * DH)
            base = (i0 + i) * N_STEP_SCALARS
            s = sc_ref[base + h]
            w = sc_ref[base + N_HEADS + h]
            g = sc_ref[base + 2 * N_HEADS + h]
            em = sc_ref[base + 4 * N_HEADS + h]
            rsel = pl.ds(i0 + i, 1)
            q_row = q_ref[rsel, hs]
            k_row = k_ref[rsel, hs]
            n_row = n_ref[rsel, hs]
            v_col = v_ref[i, hs, :]
            cmat = c_ref[i, h]
            num = s * v_col + g * jnp.sum(cmat * q_row, axis=1, keepdims=True)
            den = s + g * jnp.sum(n_row * q_row, axis=1, keepdims=True)
            hh = num / jnp.maximum(jnp.abs(den), em)
            c_out_ref[i, h] = g * cmat + (w * v_col) * k_row
            n_out_ref[rsel, hs] = g * n_row + w * k_row
            mu = jnp.mean(hh, axis=0, keepdims=True)
            hc = hh - mu
            var = jnp.mean(hc * hc, axis=0, keepdims=True)
            y_ref[i, hs, :] = jax.nn.sigmoid(o_ref[i, hs, :]) * (hc * lax.rsqrt(var + LN_EPS) * ng_ref[hs, :])


def _mlstm_step(z_small, row_blk, conv0, c0, n0, m0, conv_w, conv_b, wq_bf, wk_bf, norm_g, bb=4):
    rows = c0.shape[0]
    full = lambda a: pl.BlockSpec(a.shape, lambda i: (0,) * a.ndim)
    conv0_t = conv0.transpose(1, 0, 2)
    q, k, sc = pl.pallas_call(
        _mlstm_step_a_body,
        grid=(1,),
        in_specs=[pl.BlockSpec((rows, D_ML), lambda i: (row_blk, 1)),
                  pl.BlockSpec((rows, LANES), lambda i: (row_blk, GATE_COL)),
                  full(conv0_t), full(m0), full(conv_w), full(conv_b), full(wq_bf), full(wk_bf)],
        out_specs=[pl.BlockSpec((rows, D_ML), lambda i: (0, 0)), pl.BlockSpec((rows, D_ML), lambda i: (0, 0)),
                   pl.BlockSpec((rows, LANES), lambda i: (0, 0))],
        out_shape=[jax.ShapeDtypeStruct((rows, D_ML), F32), jax.ShapeDtypeStruct((rows, D_ML), F32),
                   jax.ShapeDtypeStruct((rows, LANES), F32)],
        compiler_params=_cparams(("arbitrary",), 48),
        name="mlstm_step_a",
    )(z_small, z_small, conv0_t, m0, conv_w, conv_b, wq_bf, wk_bf)
    r0 = row_blk * rows
    v_col = z_small[r0:r0 + rows, 2 * D_ML:3 * D_ML].reshape(rows, D_ML, 1)
    o_col = z_small[r0:r0 + rows, 3 * D_ML:4 * D_ML].reshape(rows, D_ML, 1)
    ng_col = norm_g.reshape(D_ML, 1)
    row_spec = pl.BlockSpec((rows, D_ML), lambda i: (0, 0))
    col_spec = pl.BlockSpec((bb, D_ML, 1), lambda i: (i, 0, 0))
    c_spec = pl.BlockSpec((bb, N_HEADS, DH, DH), lambda i: (i, 0, 0, 0))
    y_col, c_new, n_new = pl.pallas_call(
        functools.partial(_mlstm_step_b_body, bb),
        grid=(rows // bb,),
        in_specs=[pl.BlockSpec(memory_space=pltpu.SMEM), row_spec, row_spec, row_spec, col_spec, col_spec,
                  pl.BlockSpec((D_ML, 1), lambda i: (0, 0)), c_spec],
        out_specs=[col_spec, c_spec, row_spec],
        out_shape=[jax.ShapeDtypeStruct((rows, D_ML, 1), F32), jax.ShapeDtypeStruct(c0.shape, F32),
                   jax.ShapeDtypeStruct((rows, D_ML), F32)],
        compiler_params=_cparams(("arbitrary",), 56),
        name="mlstm_step_b",
    )(sc[:, :N_STEP_SCALARS].reshape(rows * N_STEP_SCALARS), q, k, n0, v_col, o_col, ng_col, c0)
    return y_col.reshape(rows, D_ML), c_new, n_new, sc[:, 3 * N_HEADS:4 * N_HEADS]


def _mix_out_body(xp_ref, xs_ref, ysp_ref, yss_ref, ymp_ref, yms_ref,
                  gin_ref, bin_ref, wout_ref, g1_ref, b1_ref, rwh_ref, rwl_ref, rb_ref,
                  h1_ref, e_ref, gate_ref):
    is_step = pl.program_id(0) == pl.num_programs(0) - 1
    x = jnp.where(is_step, xs_ref[...], xp_ref[...])
    ys = jnp.where(is_step, yss_ref[...], ysp_ref[...])
    ym = jnp.where(is_step, yms_ref[...], ymp_ref[...])
    hp = _layer_norm(x, gin_ref[...], bin_ref[...])
    ycat = jnp.concatenate([ys, ym], axis=1).astype(BF16)
    mix = jnp.dot(ycat, wout_ref[...], preferred_element_type=F32)
    h1 = _layer_norm(DEEPNORM_ALPHA * hp + mix, g1_ref[...], b1_ref[...])
    h1_ref[...] = h1
    xh = h1.astype(BF16)
    xl = (h1 - xh.astype(F32)).astype(BF16)
    logits = (jnp.dot(xh, rwh_ref[...], preferred_element_type=F32)
              + jnp.dot(xh, rwl_ref[...], preferred_element_type=F32)
              + jnp.dot(xl, rwh_ref[...], preferred_element_type=F32)) + rb_ref[...]
    rows = logits.shape[0]
    lane = lax.broadcasted_iota(jnp.int32, (rows, LANES), 1)
    logits = jnp.where(lane < N_EXPERTS, logits, NEG_INF)
    e_acc = jnp.zeros((rows, LANES), jnp.int32)
    v_acc = jnp.full((rows, LANES), NEG_INF, F32)
    for k in range(TOP_K):
        mx = jnp.max(logits, axis=1, keepdims=True)
        idx = jnp.min(jnp.where(logits == mx, lane, LANES), axis=1, keepdims=True)
        e_acc = jnp.where(lane == k, idx, e_acc)
        v_acc = jnp.where(lane == k, mx, v_acc)
        logits = jnp.where(lane == idx, NEG_INF, logits)
    p = jnp.exp(v_acc - jnp.max(v_acc, axis=1, keepdims=True))
    e_ref[...] = e_acc
    gate_ref[...] = p / jnp.sum(p, axis=1, keepdims=True)


def _mix_out(x_p, x_s, ys_p, ys_s, ym_p, ym_s, consts, tiles_per_seq):
    d = D_MODEL
    full = lambda a: pl.BlockSpec(a.shape, lambda i: (0,) * a.ndim)
    n_p = x_p.shape[0] // TILE_T
    n_tiles = n_p + 1
    rows = n_tiles * TILE_T

    def p_idx(i):
        return jnp.minimum(i, n_p - 1)

    def frame_idx(i):
        j = p_idx(i)
        return (j // tiles_per_seq) * (tiles_per_seq + 1) + j % tiles_per_seq + 1

    return pl.pallas_call(
        _mix_out_body,
        grid=(n_tiles,),
        in_specs=[pl.BlockSpec((TILE_T, d), lambda i: (p_idx(i), 0)),
                  pl.BlockSpec((TILE_T, d), lambda i: (0, 0)),
                  pl.BlockSpec((TILE_T, D_S5), lambda i: (frame_idx(i), 0)),
                  pl.BlockSpec((TILE_T, D_S5), lambda i: (0, 0)),
                  pl.BlockSpec((TILE_T, D_ML), lambda i: (frame_idx(i), 0)),
                  pl.BlockSpec((TILE_T, D_ML), lambda i: (0, 0))] + [full(a) for a in consts],
        out_specs=[pl.BlockSpec((TILE_T, d), lambda i: (i, 0)),
                   pl.BlockSpec((TILE_T, LANES), lambda i: (i, 0)),
                   pl.BlockSpec((TILE_T, LANES), lambda i: (i, 0))],
        out_shape=[jax.ShapeDtypeStruct((rows, d), F32), jax.ShapeDtypeStruct((rows, LANES), jnp.int32),
                   jax.ShapeDtypeStruct((rows, LANES), F32)],
        compiler_params=_cparams(("arbitrary",), 48),
        name="mix_out",
    )(x_p, x_s, ys_p, ys_s, ym_p, ym_s, *consts)


MOE_BLK = 128
MOE_TM = 1536
MOE_TF = 256
MOE_SUB = (512, 256, 128)


def _moe_routing(top_e, n_tok):
    n_pairs = n_tok * TOP_K
    rows_pad = n_pairs + N_EXPERTS * MOE_BLK
    n_pass = N_EXPERTS + rows_pad // MOE_TM
    onehot = (top_e[:, :, None] == jnp.arange(N_EXPERTS, dtype=jnp.int32)).astype(jnp.int32).sum(1)
    incl = jnp.cumsum(onehot, axis=0)
    counts = incl[-1]
    rank = jnp.take_along_axis(incl - onehot, top_e, axis=1)
    padded = (counts + MOE_BLK - 1) // MOE_BLK * MOE_BLK
    pad_end = jnp.cumsum(padded)
    pad_start = pad_end - padded
    dest = pad_start[top_e] + rank
    tok = jnp.broadcast_to(jnp.arange(n_tok, dtype=jnp.int32)[:, None], (n_tok, TOP_K))
    src = jnp.zeros((rows_pad,), jnp.int32).at[dest.reshape(-1)].set(tok.reshape(-1))
    passes_e = (padded + MOE_TM - 1) // MOE_TM
    pass_end = jnp.cumsum(passes_e)
    u = jnp.arange(n_pass, dtype=jnp.int32)
    e_u = jnp.minimum(jnp.searchsorted(pass_end, u, side="right"), N_EXPERTS - 1).astype(jnp.int32)
    j_u = u - (pass_end - passes_e)[e_u]
    rem = padded[e_u] - j_u * MOE_TM
    nblk = jnp.where(u < pass_end[-1], jnp.clip(rem, 0, MOE_TM) // MOE_BLK, 0).astype(jnp.int32)
    blk0 = ((pad_start[e_u] + j_u * MOE_TM) // MOE_BLK).astype(jnp.int32)
    blk0 = jnp.where(nblk > 0, blk0, 0)
    last_e = e_u[jnp.maximum(pass_end[-1] - 1, 0)]
    e_u = jnp.where(nblk > 0, e_u, last_e)
    nblk = jnp.concatenate([nblk, (pad_end[-1:] // MOE_BLK).astype(jnp.int32)])
    return dest.astype(jnp.int32), src, e_u, blk0, nblk


def _moe_ffn_body(e_ref, blk0_ref, nblk_ref, src_ref, h1_ref, wg_ref, wu_ref, bg_ref, bu_ref, wd_ref, bd_ref,
                  ybuf_ref, xg_ref, xb_ref, acc_ref, gsem, osem):
    u = pl.program_id(0)
    f = pl.program_id(1)
    nblk = nblk_ref[u]
    row0 = blk0_ref[u] * MOE_BLK

    def gather_copy(r):
        return pltpu.make_async_copy(h1_ref.at[pl.ds(src_ref[row0 + r], 1)], xg_ref.at[pl.ds(r, 1)], gsem)

    def out_copy(b):
        rs = pl.ds(pl.multiple_of(b * MOE_BLK, MOE_BLK), MOE_BLK)
        dst = pl.ds(pl.multiple_of(row0 + b * MOE_BLK, MOE_BLK), MOE_BLK)
        return pltpu.make_async_copy(acc_ref.at[rs], ybuf_ref.at[dst], osem)

    @pl.when(jnp.logical_and(nblk > 0, f == 0))
    def _():
        def start(r, c):
            gather_copy(r).start()
            return c
        lax.fori_loop(0, nblk * MOE_BLK, start, 0)

        def wait(r, c):
            gather_copy(r).wait()
            return c
        lax.fori_loop(0, nblk * MOE_BLK, wait, 0)

        def cast(b, c):
            rs = pl.ds(pl.multiple_of(b * MOE_BLK, MOE_BLK), MOE_BLK)
            xb_ref[rs, :] = xg_ref[rs, :].astype(BF16)
            return c
        lax.fori_loop(0, nblk, cast, 0)

    @pl.when(nblk > 0)
    def _():
        wg = wg_ref[0].astype(BF16)
        wu = wu_ref[0].astype(BF16)
        wd = wd_ref[0].astype(BF16)
        bg = bg_ref[0]
        bu = bu_ref[0]

        def ffn_rows(r0, m):
            rs = pl.ds(pl.multiple_of(r0, MOE_BLK), m)
            x = xb_ref[rs, :]
            x_glu = jnp.minimum(jnp.dot(x, wg, preferred_element_type=F32) + bg, SWIGLU_LIMIT)
            x_lin = jnp.clip(jnp.dot(x, wu, preferred_element_type=F32) + bu, -SWIGLU_LIMIT, SWIGLU_LIMIT)
            act = x_glu * jax.nn.sigmoid(SWIGLU_ALPHA * x_glu) * (x_lin + 1.0)
            part = jnp.dot(act.astype(BF16), wd, preferred_element_type=F32)

            @pl.when(f == 0)
            def _():
                acc_ref[rs, :] = part

            @pl.when(f > 0)
            def _():
                acc_ref[rs, :] = acc_ref[rs, :] + part

        big = MOE_SUB[0] // MOE_BLK

        def big_loop(i, c):
            ffn_rows(i * MOE_SUB[0], MOE_SUB[0])
            return c
        n_big = nblk // big
        lax.fori_loop(0, n_big, big_loop, 0)
        done = n_big * big
        for m in MOE_SUB[1:]:
            mb = m // MOE_BLK
            take = (nblk - done) >= mb

            @pl.when(take)
            def _(done=done, m=m):
                ffn_rows(done * MOE_BLK, m)
            done = done + jnp.where(take, mb, 0)

    @pl.when(jnp.logical_and(nblk > 0, f == pl.num_programs(1) - 1))
    def _():
        def bias(b, c):
            rs = pl.ds(pl.multiple_of(b * MOE_BLK, MOE_BLK), MOE_BLK)
            acc_ref[rs, :] = acc_ref[rs, :] + bd_ref[0]
            out_copy(b).start()
            return c
        lax.fori_loop(0, nblk, bias, 0)

        def wait(b, c):
            out_copy(b).wait()
            return c
        lax.fori_loop(0, nblk, wait, 0)

    @pl.when(jnp.logical_and(u == pl.num_programs(0) - 1, f == pl.num_programs(1) - 1))
    def _():
        used = nblk_ref[pl.num_programs(0)]
        total = ybuf_ref.shape[0] // MOE_BLK
        acc_ref[0:MOE_BLK, :] = jnp.zeros((MOE_BLK, ybuf_ref.shape[1]), F32)

        def zero_copy(b):
            dst = pl.ds(pl.multiple_of(b * MOE_BLK, MOE_BLK), MOE_BLK)
            return pltpu.make_async_copy(acc_ref.at[0:MOE_BLK], ybuf_ref.at[dst], osem)

        def start(b, c):
            zero_copy(b).start()
            return c
        lax.fori_loop(used, total, start, 0)

        def wait(b, c):
            zero_copy(b).wait()
            return c
        lax.fori_loop(used, total, wait, 0)


def _moe_ffn(h1, src, e_u, blk0, nblk, w_gu, b_gu, w_dn, b_dn, rows_pad):
    d = D_MODEL
    n_pass = e_u.shape[0]
    n_f = D_FF // MOE_TF
    last_f = n_f - 1

    def fsel(u, f, nblk_ref):
        return jnp.where(nblk_ref[u] > 0, f, last_f)

    grid_spec = pltpu.PrefetchScalarGridSpec(
        num_scalar_prefetch=4,
        grid=(n_pass, n_f),
        in_specs=[
            pl.BlockSpec(memory_space=pl.ANY),
            pl.BlockSpec((1, d, MOE_TF), lambda u, f, e, b0, nb, s: (e[u], 0, fsel(u, f, nb))),
            pl.BlockSpec((1, d, MOE_TF), lambda u, f, e, b0, nb, s: (e[u], 0, n_f + fsel(u, f, nb))),
            pl.BlockSpec((1, 1, MOE_TF), lambda u, f, e, b0, nb, s: (e[u], 0, fsel(u, f, nb))),
            pl.BlockSpec((1, 1, MOE_TF), lambda u, f, e, b0, nb, s: (e[u], 0, n_f + fsel(u, f, nb))),
            pl.BlockSpec((1, MOE_TF, d), lambda u, f, e, b0, nb, s: (e[u], fsel(u, f, nb), 0)),
            pl.BlockSpec((1, 1, d), lambda u, f, e, b0, nb, s: (e[u], 0, 0)),
        ],
        out_specs=pl.BlockSpec(memory_space=pl.ANY),
        scratch_shapes=[pltpu.VMEM((MOE_TM, d), F32), pltpu.VMEM((MOE_TM, d), BF16), pltpu.VMEM((MOE_TM, d), F32),
                        pltpu.SemaphoreType.DMA(()), pltpu.SemaphoreType.DMA(())],
    )
    return pl.pallas_call(
        _moe_ffn_body,
        grid_spec=grid_spec,
        out_shape=jax.ShapeDtypeStruct((rows_pad, d), F32),
        compiler_params=_cparams(("arbitrary", "arbitrary"), 60),
        name="moe_ffn",
    )(e_u, blk0, nblk, src, h1, w_gu, w_gu, b_gu.reshape(N_EXPERTS, 1, 2 * D_FF), b_gu.reshape(N_EXPERTS, 1, 2 * D_FF),
      w_dn, b_dn.reshape(N_EXPERTS, 1, d))


def _moe_combine_body(dest_ref, h1_ref, gate_ref, g2_ref, b2_ref, ybuf_ref, outp_ref, outs_ref, buf_ref, sem):
    i = pl.program_id(0)
    base = i * (TILE_T * TOP_K)

    def copy(j):
        r = j // TOP_K
        k = j % TOP_K
        return pltpu.make_async_copy(ybuf_ref.at[pl.ds(dest_ref[base + j], 1)],
                                     buf_ref.at[k, pl.ds(r, 1)], sem)

    def start(j, c):
        copy(j).start()
        return c
    lax.fori_loop(0, TILE_T * TOP_K, start, 0)

    def wait(j, c):
        copy(j).wait()
        return c
    lax.fori_loop(0, TILE_T * TOP_K, wait, 0)

    gate = gate_ref[...]
    f = buf_ref[0] * gate[:, 0:1]
    for k in range(1, TOP_K):
        f = f + buf_ref[k] * gate[:, k:k + 1]
    out = _layer_norm(DEEPNORM_ALPHA * h1_ref[...] + f, g2_ref[...], b2_ref[...])
    is_step = i == pl.num_programs(0) - 1

    @pl.when(jnp.logical_not(is_step))
    def _():
        outp_ref[...] = out

    @pl.when(is_step)
    def _():
        outs_ref[...] = out


def _moe_combine(dest, h1, gates, g2, b2, ybuf):
    n_tok, d = h1.shape
    n_tiles = n_tok // TILE_T
    grid_spec = pltpu.PrefetchScalarGridSpec(
        num_scalar_prefetch=1,
        grid=(n_tiles,),
        in_specs=[pl.BlockSpec((TILE_T, d), lambda i, s: (i, 0)),
                  pl.BlockSpec((TILE_T, LANES), lambda i, s: (i, 0)),
                  pl.BlockSpec((1, d), lambda i, s: (0, 0)),
                  pl.BlockSpec((1, d), lambda i, s: (0, 0)),
                  pl.BlockSpec(memory_space=pl.ANY)],
        out_specs=[pl.BlockSpec((TILE_T, d), lambda i, s: (jnp.minimum(i, n_tiles - 2), 0)),
                   pl.BlockSpec((TILE_T, d), lambda i, s: (0, 0))],
        scratch_shapes=[pltpu.VMEM((TOP_K, TILE_T, d), F32), pltpu.SemaphoreType.DMA(())],
    )
    return pl.pallas_call(
        _moe_combine_body,
        grid_spec=grid_spec,
        out_shape=[jax.ShapeDtypeStruct((n_tok - TILE_T, d), F32), jax.ShapeDtypeStruct((TILE_T, d), F32)],
        compiler_params=_cparams(("arbitrary",), 48),
        name="moe_combine",
    )(dest.reshape(-1), h1, gates, g2, b2, ybuf)


def kernel(x_prompt, x_sample, state_s5_re, state_s5_im, state_mlstm_c, state_mlstm_n, state_mlstm_m, state_mlstm_conv, meta_tokens, ln_in_g, ln_in_b, w_in, b_in, s5_a_re, s5_a_im, s5_log_dt, s5_b_re, s5_b_im, s5_c_re, s5_c_im, s5_d, s5_w_glu, mlstm_conv_w, mlstm_conv_b, mlstm_wq, mlstm_wk, mlstm_norm_g, w_out, ln1_g, ln1_b, router_w, router_b, w_gate_up, b_gate_up, w_down, b_down, ln2_g, ln2_b):
    bsz, seq, d = x_prompt.shape
    dec_b = x_sample.shape[0]
    n_pad = TILE_T - N_META
    x_small = jnp.concatenate([jnp.zeros((n_pad, d), F32), meta_tokens, x_sample.reshape(dec_b, d)], axis=0)
    w_in_p = jnp.pad(w_in[0], ((0, 0), (0, N_IN_PAD - N_IN))).astype(BF16)
    b_in_p = jnp.pad(b_in[0], (0, N_IN_PAD - N_IN)).reshape(1, N_IN_PAD)
    g_in = ln_in_g.reshape(1, d)
    bb_in = ln_in_b.reshape(1, d)
    z_p = _inproj(x_prompt.reshape(bsz * seq, d), g_in, bb_in, w_in_p, b_in_p, 512, 1408)
    z_s = _inproj(x_small, g_in, bb_in, w_in_p, b_in_p, 256, 1408)
    tabs = _s5_prep(s5_a_re[0], s5_a_im[0], s5_log_dt[0], s5_b_re[0], s5_b_im[0])
    bd_b, bd_c = _s5_block_diag(tabs[6], tabs[7], s5_c_re[0], s5_c_im[0])
    wglu_bf = s5_w_glu[0].astype(BF16)
    d_skip = s5_d[0].reshape(1, D_S5)
    n_tiles = seq // TILE_T + 1
    zero_state = jnp.zeros((bsz, 1, N_STATE), F32)
    y_s5_p, s5r_p, s5i_p = _s5_seq(z_s, z_p, zero_state, zero_state, tabs[:6], bd_b, bd_c, d_skip, wglu_bf,
                                   bsz, n_tiles, n_pad)
    y_s5_s, s5r_s, s5i_s = _s5_step(z_s, 1, state_s5_re[0].reshape(dec_b, N_STATE),
                                    state_s5_im[0].reshape(dec_b, N_STATE), tabs[:6], bd_b, bd_c, d_skip, wglu_bf)

    conv_w = mlstm_conv_w[0]
    conv_b = mlstm_conv_b[0].reshape(1, D_ML)
    wq_bf = mlstm_wq[0].astype(BF16)
    wk_bf = mlstm_wk[0].astype(BF16)
    norm_g = mlstm_norm_g[0].reshape(1, D_ML)
    y_ml_p, c_p, n_p, m_p = _mlstm_seq(z_s, z_p, conv_w, conv_b, wq_bf, wk_bf, norm_g, bsz, n_tiles, n_pad)
    conv0 = state_mlstm_conv[0]
    y_ml_s, c_s, n_s, m_s = _mlstm_step(z_s, 1, conv0, state_mlstm_c[0], state_mlstm_n[0].reshape(dec_b, D_ML),
                                        state_mlstm_m[0], conv_w, conv_b, wq_bf, wk_bf, norm_g)

    rw = jnp.pad(router_w[0], ((0, 0), (0, LANES - N_EXPERTS)))
    rw_hi = rw.astype(BF16)
    rw_lo = (rw - rw_hi.astype(F32)).astype(BF16)
    rb = jnp.pad(router_b[0], (0, LANES - N_EXPERTS)).reshape(1, LANES)
    consts = (g_in, bb_in, w_out[0].astype(BF16), ln1_g[0].reshape(1, d), ln1_b[0].reshape(1, d), rw_hi, rw_lo, rb)
    h1, top_e, gates = _mix_out(x_prompt.reshape(bsz * seq, d), x_sample.reshape(dec_b, d),
                                y_s5_p, y_s5_s, y_ml_p, y_ml_s, consts, seq // TILE_T)
    n_tok = h1.shape[0]
    dest, src, e_u, blk0, nblk = _moe_routing(top_e[:, :TOP_K], n_tok)
    ybuf = _moe_ffn(h1, src, e_u, blk0, nblk, w_gate_up[0], b_gate_up[0], w_down[0], b_down[0], src.shape[0])
    out_p, out_s = _moe_combine(dest, h1, gates, ln2_g[0].reshape(1, d), ln2_b[0].reshape(1, d), ybuf)

    y_prompt = out_p.reshape(bsz, seq, d)
    y_sample = out_s.reshape(dec_b, 1, d)
    xm_p = z_p.reshape(bsz, seq, N_IN_PAD)[:, seq - (CONV_W - 1):, D_S5:D_S5 + D_ML]
    xm_s = z_s[TILE_T:, D_S5:D_S5 + D_ML]
    conv_s = jnp.concatenate([conv0[:, 1:], xm_s[:, None, :]], axis=1)
    return (y_prompt, y_sample,
            s5r_p.reshape(1, bsz, N_GROUPS, S5_STATE), s5i_p.reshape(1, bsz, N_GROUPS, S5_STATE),
            c_p[None], n_p.reshape(1, bsz, N_HEADS, DH), m_p[None, :, :N_HEADS, 0], xm_p[None],
            s5r_s.reshape(1, dec_b, N_GROUPS, S5_STATE), s5i_s.reshape(1, dec_b, N_GROUPS, S5_STATE),
            c_s[None], n_s.reshape(1, dec_b, N_HEADS, DH), m_s[None], conv_s[None])
```

```python
import functools
import math

import jax
import jax.numpy as jnp
from jax import lax
from jax.experimental import pallas as pl
from jax.experimental.pallas import tpu as pltpu

F32 = jnp.float32
BF16 = jnp.bfloat16

D_MODEL = 2048
N_META = 16
D_S5 = 1024
D_ML = 1024
S5_CH = 16
N_GROUPS = 64
S5_STATE = 64
N_STATE = N_GROUPS * S5_STATE
N_HEADS = 4
DH = 256
CONV_W = 4
N_EXPERTS = 32
TOP_K = 4
D_FF = 2048
SWIGLU_LIMIT = 7.0
SWIGLU_ALPHA = 1.702
LN_EPS = 1e-5
DEEPNORM_ALPHA = 2.0 ** 0.25
N_IN = D_S5 + 3 * D_ML + 2 * N_HEADS

LANES = 128
SUBLANES = 8
MXU_DIM = 256
MXU_ROWS = 16

TILE_T = 128
SEG_LEN = TILE_T // SUBLANES
N_IN_PAD = 33 * LANES
GATE_COL = 4 * D_S5 // LANES
S5_KCH = D_S5 // MXU_DIM
S5_SCH = N_STATE // S5_KCH
SCAN_LW = 256


def _cparams(sem, vmem_mb=None):
    kw = dict(dimension_semantics=sem)
    if vmem_mb is not None:
        kw["vmem_limit_bytes"] = vmem_mb * 1024 * 1024
    return pltpu.CompilerParams(**kw)


def _layer_norm(x, g, b):
    mu = jnp.mean(x, axis=-1, keepdims=True)
    xc = x - mu
    var = jnp.mean(xc * xc, axis=-1, keepdims=True)
    return xc * lax.rsqrt(var + LN_EPS) * g + b


def _inproj_body(x_ref, g_ref, b_ref, w_ref, bias_ref, z_ref, hn_ref):
    @pl.when(pl.program_id(1) == 0)
    def _():
        hn_ref[...] = _layer_norm(x_ref[...], g_ref[...], b_ref[...]).astype(BF16)

    z_ref[...] = jnp.dot(hn_ref[...], w_ref[...], preferred_element_type=F32) + bias_ref[...]


def _inproj(x, g, b, w, bias, tm, tn):
    rows, d = x.shape
    n = w.shape[1]
    return pl.pallas_call(
        _inproj_body,
        grid=(rows // tm, n // tn),
        in_specs=[
            pl.BlockSpec((tm, d), lambda i, j: (i, 0)),
            pl.BlockSpec((1, d), lambda i, j: (0, 0)),
            pl.BlockSpec((1, d), lambda i, j: (0, 0)),
            pl.BlockSpec((d, tn), lambda i, j: (0, j)),
            pl.BlockSpec((1, tn), lambda i, j: (0, j)),
        ],
        out_specs=pl.BlockSpec((tm, tn), lambda i, j: (i, j)),
        out_shape=jax.ShapeDtypeStruct((rows, n), F32),
        scratch_shapes=[pltpu.VMEM((tm, d), BF16)],
        compiler_params=_cparams(("arbitrary", "arbitrary"), 48),
        name="inproj",
    )(x, g, b, w, bias)


def _cmul(ar, ai, br, bi):
    return ar * br - ai * bi, ar * bi + ai * br


def _s5_prep_body(are_ref, aim_ref, dt_ref, bre_ref, bim_ref,
                  pre_ref, pim_ref, hre_ref, him_ref, qre_ref, qim_ref, bbre_ref, bbim_ref):
    lr = are_ref[...]
    li = aim_ref[...]
    dt = jnp.exp(dt_ref[...])
    mag = jnp.exp(lr * dt)
    ar = mag * jnp.cos(li * dt)
    ai = mag * jnp.sin(li * dt)
    nr = ar - 1.0
    ni = ai
    den = lr * lr + li * li
    cr = (nr * lr + ni * li) / den
    ci = (ni * lr - nr * li) / den
    bbre_ref[...] = cr * bre_ref[...] - ci * bim_ref[...]
    bbim_ref[...] = cr * bim_ref[...] + ci * bre_ref[...]
    pr, pi = ar, ai
    for j in range(SEG_LEN):
        pre_ref[j:j + 1, :] = pr
        pim_ref[j:j + 1, :] = pi
        if j + 1 < SEG_LEN:
            pr, pi = _cmul(pr, pi, ar, ai)
    row = lax.broadcasted_iota(jnp.int32, (SUBLANES, N_STATE), 0)
    kr, ki = pr, pi
    for idx, k in enumerate((1, 2, 4)):
        hre_ref[idx * 8:(idx + 1) * 8, :] = jnp.where(row >= k, jnp.broadcast_to(kr, (SUBLANES, N_STATE)), 0.0)
        him_ref[idx * 8:(idx + 1) * 8, :] = jnp.where(row >= k, jnp.broadcast_to(ki, (SUBLANES, N_STATE)), 0.0)
        kr, ki = _cmul(kr, ki, kr, ki)
    qr, qi = pr, pi
    for s in range(SUBLANES):
        qre_ref[s:s + 1, :] = qr
        qim_ref[s:s + 1, :] = qi
        if s + 1 < SUBLANES:
            qr, qi = _cmul(qr, qi, pr, pi)


def _s5_prep(a_re, a_im, log_dt, b_re, b_im):
    n = N_STATE
    are = a_re.reshape(1, n)
    aim = a_im.reshape(1, n)
    dtl = jnp.broadcast_to(log_dt[:, None], (N_GROUPS, S5_STATE)).reshape(1, n)
    bre = b_re.transpose(2, 0, 1).reshape(S5_CH, n)
    bim = b_im.transpose(2, 0, 1).reshape(S5_CH, n)
    shp = lambda r: jax.ShapeDtypeStruct((r, n), F32)
    return pl.pallas_call(
        _s5_prep_body,
        out_shape=(shp(SEG_LEN), shp(SEG_LEN), shp(24), shp(24), shp(8), shp(8), shp(S5_CH), shp(S5_CH)),
        name="s5_prep",
    )(are, aim, dtl, bre, bim)


def _s5_block_diag(bb_re, bb_im, c_re, c_im):
    gpc = 16
    eye = jnp.eye(gpc, dtype=bool)

    def bd_in(bb):
        t = bb.reshape(S5_CH, S5_KCH, gpc, S5_STATE).transpose(1, 2, 0, 3)
        t = jnp.where(eye[None, :, None, :, None], t[:, :, :, None, :], 0.0)
        return t.reshape(S5_KCH, gpc * S5_CH, gpc * S5_STATE)

    def bd_out(c):
        t = c.reshape(S5_KCH, gpc, S5_CH, S5_STATE).transpose(0, 1, 3, 2)
        t = jnp.where(eye[None, :, None, :, None], t[:, :, :, None, :], 0.0)
        return t.reshape(S5_KCH, gpc * S5_STATE, gpc * S5_CH)

    bd_b = jnp.concatenate([bd_in(bb_re), bd_in(bb_im)], axis=2).astype(BF16)
    bd_c = jnp.concatenate([bd_out(c_re), bd_out(-c_im)], axis=1).astype(BF16)
    return bd_b, bd_c


def _gelu_glu(y, wglu_ref):
    y = 0.5 * y * (1.0 + lax.erf(y * math.sqrt(0.5)))
    gate = jnp.dot(y.astype(BF16), wglu_ref[...], preferred_element_type=F32)
    return y * jax.nn.sigmoid(gate)


def _s5_in_proj(u_bf, bdb_ref, bur_ref, bui_ref):
    for c in range(S5_KCH):
        r = jnp.dot(u_bf[:, c * MXU_DIM:(c + 1) * MXU_DIM], bdb_ref[c], preferred_element_type=F32)
        bur_ref[:, c * S5_SCH:(c + 1) * S5_SCH] = r[:, :S5_SCH]
        bui_ref[:, c * S5_SCH:(c + 1) * S5_SCH] = r[:, S5_SCH:]


def _s5_out_proj(xr_ref, xi_ref, bdc_ref):
    ys = []
    for c in range(S5_KCH):
        xr = xr_ref[:, c * S5_SCH:(c + 1) * S5_SCH].astype(BF16)
        xi = xi_ref[:, c * S5_SCH:(c + 1) * S5_SCH].astype(BF16)
        ys.append(jnp.dot(xr, bdc_ref[c, :S5_SCH, :], preferred_element_type=F32)
                  + jnp.dot(xi, bdc_ref[c, S5_SCH:, :], preferred_element_type=F32))
    return jnp.concatenate(ys, axis=1)


def _s5_seq_body(n_pad, us_ref, up_ref, x0r_ref, x0i_ref, bdb_ref, bdc_ref, pre_ref, pim_ref,
                 hre_ref, him_ref, qre_ref, qim_ref, d_ref, wglu_ref,
                 y_ref, xr_out_ref, xi_out_ref, bur_ref, bui_ref, cr_ref, ci_ref, perm_ref):
    t = pl.program_id(1)
    n = N_STATE

    @pl.when(t == 0)
    def _():
        cr_ref[...] = jnp.broadcast_to(x0r_ref[0], (SUBLANES, n))
        ci_ref[...] = jnp.broadcast_to(x0i_ref[0], (SUBLANES, n))

    n_lc = D_S5 // LANES

    def load_perm(ref):
        for c in range(n_lc):
            perm_ref[c] = ref[:, c * LANES:(c + 1) * LANES]
        return jnp.concatenate(
            [jnp.concatenate([perm_ref[c, pl.ds(j, SUBLANES, stride=SEG_LEN), :] for c in range(n_lc)], axis=1)
             for j in range(SEG_LEN)], axis=0)

    prow = lax.broadcasted_iota(jnp.int32, (TILE_T, 1), 0)
    time = (prow % SUBLANES) * SEG_LEN + prow // SUBLANES
    u_first = jnp.where(time >= n_pad, load_perm(us_ref), 0.0)
    u = jnp.where(t == 0, u_first, load_perm(up_ref))
    _s5_in_proj(u.astype(BF16), bdb_ref, bur_ref, bui_ref)

    row8 = lax.broadcasted_iota(jnp.int32, (SUBLANES, SCAN_LW), 0)

    def scan_lanes(lc, carry):
        ls = pl.ds(pl.multiple_of(lc * SCAN_LW, SCAN_LW), SCAN_LW)
        bc = lambda ref, j: jnp.broadcast_to(ref[j:j + 1, ls], (SUBLANES, SCAN_LW))
        ar, ai = bc(pre_ref, 0), bc(pim_ref, 0)
        xr = jnp.zeros((SUBLANES, SCAN_LW), F32)
        xi = jnp.zeros((SUBLANES, SCAN_LW), F32)
        for j in range(SEG_LEN):
            rs = slice(j * SUBLANES, (j + 1) * SUBLANES)
            nr = ar * xr - ai * xi + bur_ref[rs, ls]
            ni = ar * xi + ai * xr + bui_ref[rs, ls]
            xr, xi = nr, ni
            bur_ref[rs, ls] = xr
            bui_ref[rs, ls] = xi
        er, ei = xr, xi
        for idx, k in enumerate((1, 2, 4)):
            sr = pltpu.roll(er, k, axis=0)
            si = pltpu.roll(ei, k, axis=0)
            hr = hre_ref[idx * 8:(idx + 1) * 8, ls]
            hi = him_ref[idx * 8:(idx + 1) * 8, ls]
            er, ei = er + (hr * sr - hi * si), ei + (hr * si + hi * sr)
        cpr = cr_ref[:, ls]
        cpi = ci_ref[:, ls]
        qr = qre_ref[:, ls]
        qi = qim_ref[:, ls]
        er, ei = er + (qr * cpr - qi * cpi), ei + (qr * cpi + qi * cpr)
        inr = jnp.where(row8 == 0, cpr, pltpu.roll(er, 1, axis=0))
        ini = jnp.where(row8 == 0, cpi, pltpu.roll(ei, 1, axis=0))
        cr_ref[:, ls] = jnp.broadcast_to(er[SUBLANES - 1:SUBLANES, :], (SUBLANES, SCAN_LW))
        ci_ref[:, ls] = jnp.broadcast_to(ei[SUBLANES - 1:SUBLANES, :], (SUBLANES, SCAN_LW))
        for j in range(SEG_LEN):
            rs = slice(j * SUBLANES, (j + 1) * SUBLANES)
            pr, pi = bc(pre_ref, j), bc(pim_ref, j)
            bur_ref[rs, ls] = bur_ref[rs, ls] + (pr * inr - pi * ini)
            bui_ref[rs, ls] = bui_ref[rs, ls] + (pr * ini + pi * inr)
        return carry

    lax.fori_loop(0, n // SCAN_LW, scan_lanes, 0)

    y = _s5_out_proj(bur_ref, bui_ref, bdc_ref) + d_ref[...] * u
    y = _gelu_glu(y, wglu_ref)
    for j in range(SEG_LEN):
        for c in range(n_lc):
            perm_ref[c, pl.ds(j, SUBLANES, stride=SEG_LEN), :] = y[j * SUBLANES:(j + 1) * SUBLANES,
                                                                   c * LANES:(c + 1) * LANES]
    for c in range(n_lc):
        y_ref[:, c * LANES:(c + 1) * LANES] = perm_ref[c]

    @pl.when(t == pl.num_programs(1) - 1)
    def _():
        xr_out_ref[0] = cr_ref[0:1, :]
        xi_out_ref[0] = ci_ref[0:1, :]


def _s5_seq(z_small, z_seq, x0r, x0i, tabs, bd_b, bd_c, d_skip, wglu_bf, n_batch, n_tiles, n_pad):
    pre, pim, hre, him, qre, qim = tabs
    n = N_STATE
    full = lambda a: pl.BlockSpec(a.shape, lambda b, t: (0,) * a.ndim)
    per_seq = n_tiles - 1
    return pl.pallas_call(
        functools.partial(_s5_seq_body, n_pad),
        grid=(n_batch, n_tiles),
        in_specs=[
            pl.BlockSpec((TILE_T, D_S5), lambda b, t: (0, 0)),
            pl.BlockSpec((TILE_T, D_S5), lambda b, t: (b * per_seq + jnp.maximum(t - 1, 0), 0)),
            pl.BlockSpec((1, 1, n), lambda b, t: (b, 0, 0)),
            pl.BlockSpec((1, 1, n), lambda b, t: (b, 0, 0)),
            full(bd_b), full(bd_c), full(pre), full(pim), full(hre), full(him), full(qre), full(qim),
            full(d_skip), full(wglu_bf),
        ],
        out_specs=[
            pl.BlockSpec((TILE_T, D_S5), lambda b, t: (b * n_tiles + t, 0)),
            pl.BlockSpec((1, 1, n), lambda b, t: (b, 0, 0)),
            pl.BlockSpec((1, 1, n), lambda b, t: (b, 0, 0)),
        ],
        out_shape=[
            jax.ShapeDtypeStruct((n_batch * n_tiles * TILE_T, D_S5), F32),
            jax.ShapeDtypeStruct((n_batch, 1, n), F32),
            jax.ShapeDtypeStruct((n_batch, 1, n), F32),
        ],
        scratch_shapes=[pltpu.VMEM((TILE_T, n), F32), pltpu.VMEM((TILE_T, n), F32),
                        pltpu.VMEM((SUBLANES, n), F32), pltpu.VMEM((SUBLANES, n), F32),
                        pltpu.VMEM((D_S5 // LANES, TILE_T, LANES), F32)],
        compiler_params=_cparams(("arbitrary", "arbitrary"), 48),
        name="s5_seq",
    )(z_small, z_seq, x0r, x0i, bd_b, bd_c, pre, pim, hre, him, qre, qim, d_skip, wglu_bf)


def _s5_step_body(u_ref, x0r_ref, x0i_ref, bdb_ref, bdc_ref, pre_ref, pim_ref, d_ref, wglu_ref,
                  y_ref, xr_ref, xi_ref):
    u = u_ref[...]
    _s5_in_proj(u.astype(BF16), bdb_ref, xr_ref, xi_ref)
    ar = pre_ref[0:1, :]
    ai = pim_ref[0:1, :]
    x0r = x0r_ref[...]
    x0i = x0i_ref[...]
    xr_ref[...] = xr_ref[...] + (ar * x0r - ai * x0i)
    xi_ref[...] = xi_ref[...] + (ar * x0i + ai * x0r)
    y = _s5_out_proj(xr_ref, xi_ref, bdc_ref) + d_ref[...] * u
    y_ref[...] = _gelu_glu(y, wglu_ref)


def _s5_step(z_small, row_blk, x0r, x0i, tabs, bd_b, bd_c, d_skip, wglu_bf):
    pre, pim = tabs[0], tabs[1]
    rows = x0r.shape[0]
    n = N_STATE
    full = lambda a: pl.BlockSpec(a.shape, lambda i: (0,) * a.ndim)
    return pl.pallas_call(
        _s5_step_body,
        grid=(1,),
        in_specs=[pl.BlockSpec((rows, D_S5), lambda i: (row_blk, 0)), full(x0r), full(x0i),
                  full(bd_b), full(bd_c), full(pre), full(pim), full(d_skip), full(wglu_bf)],
        out_specs=[pl.BlockSpec((rows, D_S5), lambda i: (0, 0)),
                   pl.BlockSpec((rows, n), lambda i: (0, 0)), pl.BlockSpec((rows, n), lambda i: (0, 0))],
        out_shape=[jax.ShapeDtypeStruct((rows, D_S5), F32), jax.ShapeDtypeStruct((rows, n), F32),
                   jax.ShapeDtypeStruct((rows, n), F32)],
        compiler_params=_cparams(("arbitrary",), 48),
        name="s5_step",
    )(z_small, x0r, x0i, bd_b, bd_c, pre, pim, d_skip, wglu_bf)


NEG_INF = float("-inf")
N_STEP_SCALARS = 5 * N_HEADS


def _log_sigmoid(x):
    return jnp.minimum(x, 0.0) - jnp.log1p(jnp.exp(-jnp.abs(x)))


def _split3(x):
    hi = x.astype(BF16)
    r1 = x - hi.astype(F32)
    mid = r1.astype(BF16)
    lo = (r1 - mid.astype(F32)).astype(BF16)
    return hi, mid, lo


def _head_norm_gate(h, o, g):
    mu = jnp.mean(h, axis=-1, keepdims=True)
    hc = h - mu
    var = jnp.mean(hc * hc, axis=-1, keepdims=True)
    return jax.nn.sigmoid(o) * (hc * lax.rsqrt(var + LN_EPS) * g)


def _dot_nt(a, b):
    return lax.dot_general(a, b, (((1,), (1,)), ((), ())), preferred_element_type=F32)


def _dot_tn(a, b):
    return lax.dot_general(a, b, (((0,), (0,)), ((), ())), preferred_element_type=F32)


def _mlstm_seq_body(n_pad, sxm_ref, sv_ref, so_ref, sg_ref, pxm_ref, pv_ref, po_ref, pg_ref,
                    cw_ref, cb_ref, wq_ref, wk_ref, ng_ref,
                    y_ref, c_out_ref, n_out_ref, m_out_ref, c_ref, n_ref, m_ref, prev_ref):
    t = pl.program_id(1)
    L = TILE_T

    @pl.when(t == 0)
    def _():
        c_ref[...] = jnp.zeros_like(c_ref)
        n_ref[...] = jnp.zeros_like(n_ref)
        m_ref[...] = jnp.zeros_like(m_ref)
        prev_ref[...] = jnp.zeros_like(prev_ref)

    first = t == 0
    row = lax.broadcasted_iota(jnp.int32, (L, 1), 0)
    valid = jnp.logical_or(jnp.logical_not(first), row >= n_pad)
    xm = jnp.where(valid, jnp.where(first, sxm_ref[...], pxm_ref[...]), 0.0)
    v = jnp.where(first, sv_ref[...], pv_ref[...])
    o = jnp.where(first, so_ref[...], po_ref[...])
    gt = jnp.where(first, sg_ref[...], pg_ref[...])

    prev = prev_ref[...]

    def shifted(j):
        if j == 0:
            return xm
        return pltpu.roll(jnp.where(row >= L - j, prev, xm), j, axis=0)

    xc = cb_ref[...]
    for j in range(CONV_W):
        xc = xc + shifted(CONV_W - 1 - j) * cw_ref[j:j + 1, :]
    prev_ref[...] = xm
    xc = xc * jax.nn.sigmoid(xc)

    ig = jnp.where(valid, gt, NEG_INF)
    lf = jnp.where(valid, _log_sigmoid(gt), 0.0)
    ti = lax.broadcasted_iota(jnp.int32, (L, L), 0)
    si = lax.broadcasted_iota(jnp.int32, (L, L), 1)
    causal = si <= ti
    tri = jnp.where(causal, 1.0, 0.0).astype(BF16)
    bc = sum(jnp.dot(tri, p, preferred_element_type=F32) for p in _split3(lf))
    ig_t = ig.T
    bc_t = bc.T

    for h in range(N_HEADS):
        hs = slice(h * DH, (h + 1) * DH)
        b_col = bc[:, N_HEADS + h:N_HEADS + h + 1]
        b_row = bc_t[N_HEADS + h:N_HEADS + h + 1, :]
        ig_row = ig_t[h:h + 1, :]
        ig_col = ig[:, h:h + 1]
        m_prev = m_ref[h:h + 1, 0:1]
        dlog = jnp.where(causal, b_col - b_row + ig_row, NEG_INF)
        inter = b_col + m_prev
        m_t = jnp.maximum(jnp.max(dlog, axis=1, keepdims=True), inter)
        w = jnp.exp(dlog - m_t)
        g = jnp.exp(inter - m_t)
        xh = xc[:, hs].astype(BF16)
        q = jnp.dot(xh, wq_ref[h], preferred_element_type=F32)
        k = jnp.dot(xh, wk_ref[h], preferred_element_type=F32) * (DH ** -0.5)
        qb = q.astype(BF16)
        kb = k.astype(BF16)
        s = _dot_nt(qb, kb) * w
        vh = v[:, hs]
        cmat = c_ref[h]
        n_row = n_ref[h]
        num = jnp.dot(s.astype(BF16), vh.astype(BF16), preferred_element_type=F32) \
            + g * _dot_nt(qb, cmat.astype(BF16))
        den = jnp.sum(s, axis=1, keepdims=True) + g * jnp.sum(q * n_row, axis=1, keepdims=True)
        hh = num / jnp.maximum(jnp.abs(den), jnp.exp(-m_t))
        b_last = b_col[L - 1:L, :]
        wlog = b_last - b_col + ig_col
        m_new = jnp.maximum(b_last + m_prev, jnp.max(wlog, axis=0, keepdims=True))
        w_end = jnp.exp(wlog - m_new)
        g_end = jnp.exp(b_last + m_prev - m_new)
        c_ref[h] = g_end * cmat + _dot_tn((vh * w_end).astype(BF16), kb)
        n_ref[h] = g_end * n_row + jnp.sum(w_end * k, axis=0, keepdims=True)
        m_ref[h:h + 1, :] = jnp.broadcast_to(m_new, (1, LANES))
        y_ref[:, hs] = _head_norm_gate(hh, o[:, hs], ng_ref[:, hs])

    @pl.when(t == pl.num_programs(1) - 1)
    def _():
        c_out_ref[0] = c_ref[...]
        n_out_ref[0] = n_ref[...]
        m_out_ref[0] = m_ref[...]


def _mlstm_seq(z_small, z_seq, conv_w, conv_b, wq_bf, wk_bf, norm_g, n_batch, n_tiles, n_pad):
    per_seq = n_tiles - 1
    full = lambda a: pl.BlockSpec(a.shape, lambda b, t: (0,) * a.ndim)
    nb = D_ML // LANES

    def small(col, width):
        return pl.BlockSpec((TILE_T, width), lambda b, t: (0, col))

    def seq(col, width):
        return pl.BlockSpec((TILE_T, width), lambda b, t: (b * per_seq + jnp.maximum(t - 1, 0), col))

    return pl.pallas_call(
        functools.partial(_mlstm_seq_body, n_pad),
        grid=(n_batch, n_tiles),
        in_specs=[small(1, D_ML), small(2, D_ML), small(3, D_ML), small(GATE_COL, LANES),
                  seq(1, D_ML), seq(2, D_ML), seq(3, D_ML), seq(GATE_COL, LANES),
                  full(conv_w), full(conv_b), full(wq_bf), full(wk_bf), full(norm_g)],
        out_specs=[
            pl.BlockSpec((TILE_T, D_ML), lambda b, t: (b * n_tiles + t, 0)),
            pl.BlockSpec((1, N_HEADS, DH, DH), lambda b, t: (b, 0, 0, 0)),
            pl.BlockSpec((1, N_HEADS, 1, DH), lambda b, t: (b, 0, 0, 0)),
            pl.BlockSpec((1, SUBLANES, LANES), lambda b, t: (b, 0, 0)),
        ],
        out_shape=[
            jax.ShapeDtypeStruct((n_batch * n_tiles * TILE_T, D_ML), F32),
            jax.ShapeDtypeStruct((n_batch, N_HEADS, DH, DH), F32),
            jax.ShapeDtypeStruct((n_batch, N_HEADS, 1, DH), F32),
            jax.ShapeDtypeStruct((n_batch, SUBLANES, LANES), F32),
        ],
        scratch_shapes=[pltpu.VMEM((N_HEADS, DH, DH), F32), pltpu.VMEM((N_HEADS, 1, DH), F32),
                        pltpu.VMEM((SUBLANES, LANES), F32), pltpu.VMEM((TILE_T, D_ML), F32)],
        compiler_params=_cparams(("arbitrary", "arbitrary"), 48),
        name="mlstm_seq",
    )(z_small, z_small, z_small, z_small, z_seq, z_seq, z_seq, z_seq, conv_w, conv_b, wq_bf, wk_bf, norm_g)


def _mlstm_step_a_body(xm_ref, g_ref, conv0_ref, m0_ref, cw_ref, cb_ref, wq_ref, wk_ref,
                       q_ref, k_ref, sc_ref):
    xc = cb_ref[...]
    for j in range(CONV_W - 1):
        xc = xc + conv0_ref[j] * cw_ref[j:j + 1, :]
    xc = xc + xm_ref[...] * cw_ref[CONV_W - 1:CONV_W, :]
    xc = xc * jax.nn.sigmoid(xc)
    gt = g_ref[...]
    ig = gt[:, 0:N_HEADS]
    lf = _log_sigmoid(gt[:, N_HEADS:2 * N_HEADS])
    inter = lf + m0_ref[...]
    m_t = jnp.maximum(ig, inter)
    w = jnp.exp(ig - m_t)
    g = jnp.exp(inter - m_t)
    qks = []
    for h in range(N_HEADS):
        hs = slice(h * DH, (h + 1) * DH)
        xh = xc[:, hs].astype(BF16)
        q = jnp.dot(xh, wq_ref[h], preferred_element_type=F32)
        k = jnp.dot(xh, wk_ref[h], preferred_element_type=F32) * (DH ** -0.5)
        q_ref[:, hs] = q
        k_ref[:, hs] = k
        qks.append(jnp.sum(q * k, axis=1, keepdims=True))
    s = jnp.concatenate(qks, axis=1) * w
    rows = s.shape[0]
    sc_ref[...] = jnp.concatenate(
        [s, w, g, m_t, jnp.exp(-m_t), jnp.zeros((rows, LANES - 5 * N_HEADS), F32)], axis=1)


def _mlstm_step_b_body(bb, sc_ref, q_ref, k_ref, n_ref, v_ref, o_ref, ng_ref, c_ref,
                       y_ref, c_out_ref, n_out_ref):
    i0 = pl.program_id(0) * bb
    row = lax.broadcasted_iota(jnp.int32, (MXU_ROWS, DH), 0)

    def hi_lo(x):
        hi = x.astype(BF16).astype(F32)
        return jnp.broadcast_to(hi, (MXU_ROWS, DH)), jnp.broadcast_to(x - hi, (MXU_ROWS, DH))

    for i in range(bb):
        for h in range(N_HEADS):
            hs = slice(h * DH, (h + 1) * DH)
            base = (i0 + i) * N_STEP_SCALARS
            s = sc_ref[base + h]
            w = sc_ref[base + N_HEADS + h]
            g = sc_ref[base + 2 * N_HEADS + h]
            em = sc_ref[base + 4 * N_HEADS + h]
            rsel = pl.ds(i0 + i, 1)
            q_row = q_ref[rsel, hs]
            k_row = k_ref[rsel, hs]
            n_row = n_ref[rsel, hs]
            v_row = v_ref[rsel, hs]
            cmat = c_ref[i, h]
            qh, ql = hi_lo(q_row)
            qmat = jnp.where(row == 0, qh, jnp.where(row == 1, ql, 0.0)).astype(BF16)
            cq = _dot_nt(qmat, cmat.astype(BF16))
            num = s * v_row + g * (cq[0:1, :] + cq[1:2, :])
            den = s + g * jnp.sum(n_row * q_row, axis=1, keepdims=True)
            hh = num / jnp.maximum(jnp.abs(den), em)
            vh, vl = hi_lo(w * v_row)
            kh, kl = hi_lo(k_row)
            a = jnp.where(row < 2, vh, jnp.where(row < 4, vl, 0.0)).astype(BF16)
            b = jnp.where(row < 4, jnp.where(row % 2 == 0, kh, kl), 0.0).astype(BF16)
            c_out_ref[i, h] = g * cmat + _dot_tn(a, b)
            n_out_ref[rsel, hs] = g * n_row + w * k_row
            y_ref[rsel, hs] = _head_norm_gate(hh, o_ref[rsel, hs], ng_ref[:, hs])


def _mlstm_step(z_small, row_blk, conv0, c0, n0, m0, conv_w, conv_b, wq_bf, wk_bf, norm_g, bb=4):
    rows = c0.shape[0]
    full = lambda a: pl.BlockSpec(a.shape, lambda i: (0,) * a.ndim)
    conv0_t = conv0.transpose(1, 0, 2)
    q, k, sc = pl.pallas_call(
        _mlstm_step_a_body,
        grid=(1,),
        in_specs=[pl.BlockSpec((rows, D_ML), lambda i: (row_blk, 1)),
                  pl.BlockSpec((rows, LANES), lambda i: (row_blk, GATE_COL)),
                  full(conv0_t), full(m0), full(conv_w), full(conv_b), full(wq_bf), full(wk_bf)],
        out_specs=[pl.BlockSpec((rows, D_ML), lambda i: (0, 0)), pl.BlockSpec((rows, D_ML), lambda i: (0, 0)),
                   pl.BlockSpec((rows, LANES), lambda i: (0, 0))],
        out_shape=[jax.ShapeDtypeStruct((rows, D_ML), F32), jax.ShapeDtypeStruct((rows, D_ML), F32),
                   jax.ShapeDtypeStruct((rows, LANES), F32)],
        compiler_params=_cparams(("arbitrary",), 48),
        name="mlstm_step_a",
    )(z_small, z_small, conv0_t, m0, conv_w, conv_b, wq_bf, wk_bf)
    row_spec = pl.BlockSpec((rows, D_ML), lambda i: (0, 0))
    c_spec = pl.BlockSpec((bb, N_HEADS, DH, DH), lambda i: (i, 0, 0, 0))
    y, c_new, n_new = pl.pallas_call(
        functools.partial(_mlstm_step_b_body, bb),
        grid=(rows // bb,),
        in_specs=[pl.BlockSpec(memory_space=pltpu.SMEM), row_spec, row_spec, row_spec,
                  pl.BlockSpec((rows, D_ML), lambda i: (row_blk, 2)),
                  pl.BlockSpec((rows, D_ML), lambda i: (row_blk, 3)),
                  pl.BlockSpec((1, D_ML), lambda i: (0, 0)), c_spec],
        out_specs=[row_spec, c_spec, row_spec],
        out_shape=[jax.ShapeDtypeStruct((rows, D_ML), F32), jax.ShapeDtypeStruct(c0.shape, F32),
                   jax.ShapeDtypeStruct((rows, D_ML), F32)],
        compiler_params=_cparams(("arbitrary",), 56),
        name="mlstm_step_b",
    )(sc[:, :N_STEP_SCALARS].reshape(rows * N_STEP_SCALARS), q, k, n0, z_small, z_small, norm_g, c0)
    return y, c_new, n_new, sc[:, 3 * N_HEADS:4 * N_HEADS]


ROW_CH = D_MODEL // LANES


def _load_row_tiles(ref, row0, m, c):
    return ref[pl.ds(row0 * ROW_CH + c, m, stride=ROW_CH), :]


def _store_row_tiles(ref, row0, val):
    m = val.shape[0]
    for c in range(ROW_CH):
        ref[pl.ds(row0 * ROW_CH + c, m, stride=ROW_CH), :] = val[:, c * LANES:(c + 1) * LANES]


def _mix_out_body(xp_ref, xs_ref, ysp_ref, yss_ref, ymp_ref, yms_ref,
                  gin_ref, bin_ref, wout_ref, g1_ref, b1_ref, rwh_ref, rwl_ref, rb_ref,
                  h1_ref, e_ref, gate_ref):
    is_step = pl.program_id(0) == pl.num_programs(0) - 1
    x = jnp.where(is_step, xs_ref[...], xp_ref[...])
    ys = jnp.where(is_step, yss_ref[...], ysp_ref[...])
    ym = jnp.where(is_step, yms_ref[...], ymp_ref[...])
    hp = _layer_norm(x, gin_ref[...], bin_ref[...])
    ycat = jnp.concatenate([ys, ym], axis=1).astype(BF16)
    mix = jnp.dot(ycat, wout_ref[...], preferred_element_type=F32)
    h1 = _layer_norm(DEEPNORM_ALPHA * hp + mix, g1_ref[...], b1_ref[...])
    _store_row_tiles(h1_ref, 0, h1)
    xh = h1.astype(BF16)
    xl = (h1 - xh.astype(F32)).astype(BF16)
    logits = (jnp.dot(xh, rwh_ref[...], preferred_element_type=F32)
              + jnp.dot(xh, rwl_ref[...], preferred_element_type=F32)
              + jnp.dot(xl, rwh_ref[...], preferred_element_type=F32)) + rb_ref[...]
    rows = logits.shape[0]
    lane = lax.broadcasted_iota(jnp.int32, (rows, LANES), 1)
    logits = jnp.where(lane < N_EXPERTS, logits, NEG_INF)
    e_acc = jnp.zeros((rows, LANES), jnp.int32)
    v_acc = jnp.full((rows, LANES), NEG_INF, F32)
    for k in range(TOP_K):
        mx = jnp.max(logits, axis=1, keepdims=True)
        idx = jnp.min(jnp.where(logits == mx, lane, LANES), axis=1, keepdims=True)
        e_acc = jnp.where(lane == k, idx, e_acc)
        v_acc = jnp.where(lane == k, mx, v_acc)
        logits = jnp.where(lane == idx, NEG_INF, logits)
    p = jnp.exp(v_acc - jnp.max(v_acc, axis=1, keepdims=True))
    e_ref[...] = e_acc
    gate_ref[...] = p / jnp.sum(p, axis=1, keepdims=True)


def _mix_out(x_p, x_s, ys_p, ys_s, ym_p, ym_s, consts, tiles_per_seq):
    d = D_MODEL
    full = lambda a: pl.BlockSpec(a.shape, lambda i: (0,) * a.ndim)
    n_p = x_p.shape[0] // TILE_T
    n_tiles = n_p + 1
    rows = n_tiles * TILE_T

    def p_idx(i):
        return jnp.minimum(i, n_p - 1)

    def frame_idx(i):
        j = p_idx(i)
        return (j // tiles_per_seq) * (tiles_per_seq + 1) + j % tiles_per_seq + 1

    return pl.pallas_call(
        _mix_out_body,
        grid=(n_tiles,),
        in_specs=[pl.BlockSpec((TILE_T, d), lambda i: (p_idx(i), 0)),
                  pl.BlockSpec((TILE_T, d), lambda i: (0, 0)),
                  pl.BlockSpec((TILE_T, D_S5), lambda i: (frame_idx(i), 0)),
                  pl.BlockSpec((TILE_T, D_S5), lambda i: (0, 0)),
                  pl.BlockSpec((TILE_T, D_ML), lambda i: (frame_idx(i), 0)),
                  pl.BlockSpec((TILE_T, D_ML), lambda i: (0, 0))] + [full(a) for a in consts],
        out_specs=[pl.BlockSpec((TILE_T * ROW_CH, LANES), lambda i: (i, 0)),
                   pl.BlockSpec((TILE_T, LANES), lambda i: (i, 0)),
                   pl.BlockSpec((TILE_T, LANES), lambda i: (i, 0))],
        out_shape=[jax.ShapeDtypeStruct((rows * ROW_CH, LANES), F32), jax.ShapeDtypeStruct((rows, LANES), jnp.int32),
                   jax.ShapeDtypeStruct((rows, LANES), F32)],
        compiler_params=_cparams(("arbitrary",), 48),
        name="mix_out",
    )(x_p, x_s, ys_p, ys_s, ym_p, ym_s, *consts)


MOE_BLK = 128
MOE_TM = 1536
MOE_TF = 256
MOE_SUB_MAX = 6
DMA_UNROLL = 8


def _invert_rows_body(dest_ref, src_ref):
    def zero(i, c):
        for q in range(DMA_UNROLL):
            src_ref[i * DMA_UNROLL + q] = 0
        return c
    lax.fori_loop(0, src_ref.shape[0] // DMA_UNROLL, zero, 0)

    def put(i, c):
        for q in range(DMA_UNROLL):
            src_ref[dest_ref[i * DMA_UNROLL + q]] = i * (DMA_UNROLL // TOP_K) + q // TOP_K
        return c
    lax.fori_loop(0, dest_ref.shape[0] // DMA_UNROLL, put, 0)


def _invert_rows(dest_flat, rows_pad):
    return pl.pallas_call(
        _invert_rows_body,
        in_specs=[pl.BlockSpec(memory_space=pltpu.SMEM)],
        out_specs=pl.BlockSpec(memory_space=pltpu.SMEM),
        out_shape=jax.ShapeDtypeStruct((rows_pad,), jnp.int32),
        name="invert_rows",
    )(dest_flat)


def _moe_routing(top_e, n_tok):
    n_pairs = n_tok * TOP_K
    rows_pad = n_pairs + N_EXPERTS * MOE_BLK
    n_pass = N_EXPERTS + rows_pad // MOE_TM
    onehot = (top_e[:, :, None] == jnp.arange(N_EXPERTS, dtype=jnp.int32)).astype(jnp.int32).sum(1)
    incl = jnp.cumsum(onehot, axis=0)
    counts = incl[-1]
    rank = jnp.take_along_axis(incl - onehot, top_e, axis=1)
    padded = (counts + MOE_BLK - 1) // MOE_BLK * MOE_BLK
    pad_end = jnp.cumsum(padded)
    pad_start = pad_end - padded
    dest = pad_start[top_e] + rank
    src = _invert_rows(dest.reshape(-1).astype(jnp.int32), rows_pad)
    passes_e = (padded + MOE_TM - 1) // MOE_TM
    pass_end = jnp.cumsum(passes_e)
    u = jnp.arange(n_pass, dtype=jnp.int32)
    e_u = jnp.minimum(jnp.searchsorted(pass_end, u, side="right"), N_EXPERTS - 1).astype(jnp.int32)
    j_u = u - (pass_end - passes_e)[e_u]
    rem = padded[e_u] - j_u * MOE_TM
    nblk = jnp.where(u < pass_end[-1], jnp.clip(rem, 0, MOE_TM) // MOE_BLK, 0).astype(jnp.int32)
    blk0 = ((pad_start[e_u] + j_u * MOE_TM) // MOE_BLK).astype(jnp.int32)
    blk0 = jnp.where(nblk > 0, blk0, 0)
    last_e = e_u[jnp.maximum(pass_end[-1] - 1, 0)]
    e_u = jnp.where(nblk > 0, e_u, last_e)
    nblk = jnp.concatenate([nblk, (pad_end[-1:] // MOE_BLK).astype(jnp.int32)])
    return dest.astype(jnp.int32), src, e_u, blk0, nblk


def _moe_ffn_body(e_ref, blk0_ref, nblk_ref, src_ref, h1_ref, wg_ref, wu_ref, bg_ref, bu_ref, wd_ref, bd_ref,
                  ybuf_ref, xg_ref, xb_ref, acc_ref, w1_ref, w2_ref, gsem, osem):
    u = pl.program_id(0)
    f = pl.program_id(1)
    nblk = nblk_ref[u]
    row0 = blk0_ref[u] * MOE_BLK
    blk_rt = MOE_BLK * ROW_CH

    def gather_copy(r):
        return pltpu.make_async_copy(h1_ref.at[pl.ds(src_ref[row0 + r] * ROW_CH, ROW_CH)],
                                     xg_ref.at[pl.ds(r * ROW_CH, ROW_CH)], gsem)

    def out_copy(b):
        rs = pl.ds(pl.multiple_of(b * blk_rt, blk_rt), blk_rt)
        dst = pl.ds(pl.multiple_of((row0 + b * MOE_BLK) * ROW_CH, blk_rt), blk_rt)
        return pltpu.make_async_copy(xg_ref.at[rs], ybuf_ref.at[dst], osem)

    @pl.when(jnp.logical_and(nblk > 0, f == 0))
    def _():
        def start(i, c):
            for q in range(DMA_UNROLL):
                gather_copy(i * DMA_UNROLL + q).start()
            return c
        lax.fori_loop(0, nblk * (MOE_BLK // DMA_UNROLL), start, 0)

        def wait(i, c):
            for q in range(DMA_UNROLL):
                gather_copy(i * DMA_UNROLL + q).wait()
            return c
        lax.fori_loop(0, nblk * (MOE_BLK // DMA_UNROLL), wait, 0)

        def cast(b, c):
            rs = pl.ds(pl.multiple_of(b * MOE_BLK, MOE_BLK), MOE_BLK)
            for ch in range(ROW_CH):
                xb_ref[rs, ch * LANES:(ch + 1) * LANES] = _load_row_tiles(
                    xg_ref, b * MOE_BLK, MOE_BLK, ch).astype(BF16)
            acc_ref[rs, :] = jnp.zeros((MOE_BLK, acc_ref.shape[1]), F32)
            return c
        lax.fori_loop(0, nblk, cast, 0)

    @pl.when(nblk > 0)
    def _():
        w1_ref[:, :MOE_TF] = wg_ref[0].astype(BF16)
        w1_ref[:, MOE_TF:] = wu_ref[0].astype(BF16)
        w2_ref[...] = wd_ref[0].astype(BF16)
        bg = bg_ref[0]
        bu = bu_ref[0]

        def ffn_rows(r0, m):
            rs = pl.ds(pl.multiple_of(r0, MOE_BLK), m)
            h = jnp.dot(xb_ref[rs, :], w1_ref[...], preferred_element_type=F32)
            x_glu = jnp.minimum(h[:, :MOE_TF] + bg, SWIGLU_LIMIT)
            x_lin = jnp.clip(h[:, MOE_TF:] + bu, -SWIGLU_LIMIT, SWIGLU_LIMIT)
            act = x_glu * jax.nn.sigmoid(SWIGLU_ALPHA * x_glu) * (x_lin + 1.0)
            acc_ref[rs, :] += jnp.dot(act.astype(BF16), w2_ref[...], preferred_element_type=F32)

        n_sub = (nblk + (MOE_SUB_MAX - 1)) // MOE_SUB_MAX
        q = nblk // n_sub
        n_hi = nblk - q * n_sub

        def sub(i, blk):
            sz = q + (i < n_hi).astype(jnp.int32)
            for s in range(1, MOE_SUB_MAX + 1):
                @pl.when(sz == s)
                def _(s=s):
                    ffn_rows(blk * MOE_BLK, s * MOE_BLK)
            return blk + sz
        lax.fori_loop(0, n_sub, sub, 0)

    @pl.when(jnp.logical_and(nblk > 0, f == pl.num_programs(1) - 1))
    def _():
        def emit(b, c):
            rs = pl.ds(pl.multiple_of(b * MOE_BLK, MOE_BLK), MOE_BLK)
            _store_row_tiles(xg_ref, b * MOE_BLK, acc_ref[rs, :] + bd_ref[0])
            out_copy(b).start()
            return c
        lax.fori_loop(0, nblk, emit, 0)

        def wait(b, c):
            out_copy(b).wait()
            return c
        lax.fori_loop(0, nblk, wait, 0)

    @pl.when(jnp.logical_and(u == pl.num_programs(0) - 1, f == pl.num_programs(1) - 1))
    def _():
        used = nblk_ref[pl.num_programs(0)]
        total = ybuf_ref.shape[0] // blk_rt
        xg_ref[0:blk_rt, :] = jnp.zeros((blk_rt, LANES), F32)

        def zero_copy(b):
            dst = pl.ds(pl.multiple_of(b * blk_rt, blk_rt), blk_rt)
            return pltpu.make_async_copy(xg_ref.at[0:blk_rt], ybuf_ref.at[dst], osem)

        def start(b, c):
            zero_copy(b).start()
            return c
        lax.fori_loop(used, total, start, 0)

        def wait(b, c):
            zero_copy(b).wait()
            return c
        lax.fori_loop(used, total, wait, 0)


def _moe_ffn(h1, src, e_u, blk0, nblk, w_gu, b_gu, w_dn, b_dn, rows_pad):
    d = D_MODEL
    n_pass = e_u.shape[0]
    n_f = D_FF // MOE_TF
    last_f = n_f - 1

    def fsel(u, f, nblk_ref):
        return jnp.where(nblk_ref[u] > 0, f, last_f)

    grid_spec = pltpu.PrefetchScalarGridSpec(
        num_scalar_prefetch=4,
        grid=(n_pass, n_f),
        in_specs=[
            pl.BlockSpec(memory_space=pl.ANY),
            pl.BlockSpec((1, d, MOE_TF), lambda u, f, e, b0, nb, s: (e[u], 0, fsel(u, f, nb))),
            pl.BlockSpec((1, d, MOE_TF), lambda u, f, e, b0, nb, s: (e[u], 0, n_f + fsel(u, f, nb))),
            pl.BlockSpec((1, 1, MOE_TF), lambda u, f, e, b0, nb, s: (e[u], 0, fsel(u, f, nb))),
            pl.BlockSpec((1, 1, MOE_TF), lambda u, f, e, b0, nb, s: (e[u], 0, n_f + fsel(u, f, nb))),
            pl.BlockSpec((1, MOE_TF, d), lambda u, f, e, b0, nb, s: (e[u], fsel(u, f, nb), 0)),
            pl.BlockSpec((1, 1, d), lambda u, f, e, b0, nb, s: (e[u], 0, 0)),
        ],
        out_specs=pl.BlockSpec(memory_space=pl.ANY),
        scratch_shapes=[pltpu.VMEM((MOE_TM * ROW_CH, LANES), F32), pltpu.VMEM((MOE_TM, d), BF16),
                        pltpu.VMEM((MOE_TM, d), F32), pltpu.VMEM((d, 2 * MOE_TF), BF16), pltpu.VMEM((MOE_TF, d), BF16),
                        pltpu.SemaphoreType.DMA(()), pltpu.SemaphoreType.DMA(())],
    )
    return pl.pallas_call(
        _moe_ffn_body,
        grid_spec=grid_spec,
        out_shape=jax.ShapeDtypeStruct((rows_pad * ROW_CH, LANES), F32),
        compiler_params=_cparams(("arbitrary", "arbitrary"), 60),
        name="moe_ffn",
    )(e_u, blk0, nblk, src, h1, w_gu, w_gu, b_gu.reshape(N_EXPERTS, 1, 2 * D_FF), b_gu.reshape(N_EXPERTS, 1, 2 * D_FF),
      w_dn, b_dn.reshape(N_EXPERTS, 1, d))


def _moe_combine_body(dest_ref, h1_ref, gate_ref, g2_ref, b2_ref, ybuf_ref, outp_ref, outs_ref, buf_ref, sem):
    i = pl.program_id(0)
    n_pairs = TILE_T * TOP_K
    slot = i % 2

    def copy(tile, sl, j, q):
        k = q % TOP_K
        r = j * (DMA_UNROLL // TOP_K) + q // TOP_K
        src_row = dest_ref[tile * n_pairs + j * DMA_UNROLL + q]
        return pltpu.make_async_copy(ybuf_ref.at[pl.ds(src_row * ROW_CH, ROW_CH)],
                                     buf_ref.at[sl, pl.ds((k * TILE_T + r) * ROW_CH, ROW_CH)], sem.at[sl])

    def issue(tile, sl):
        def body(j, c):
            for q in range(DMA_UNROLL):
                copy(tile, sl, j, q).start()
            return c
        lax.fori_loop(0, n_pairs // DMA_UNROLL, body, 0)

    @pl.when(i == 0)
    def _():
        issue(0, 0)

    @pl.when(i + 1 < pl.num_programs(0))
    def _():
        issue(i + 1, 1 - slot)

    def wait(j, c):
        for q in range(DMA_UNROLL):
            copy(i, slot, j, q).wait()
        return c
    lax.fori_loop(0, n_pairs // DMA_UNROLL, wait, 0)

    gate = gate_ref[...]
    cols = []
    for c in range(ROW_CH):
        fc = None
        for k in range(TOP_K):
            v = buf_ref[slot, pl.ds(k * TILE_T * ROW_CH + c, TILE_T, stride=ROW_CH), :] * gate[:, k:k + 1]
            fc = v if fc is None else fc + v
        cols.append(DEEPNORM_ALPHA * _load_row_tiles(h1_ref, 0, TILE_T, c) + fc)
    out = _layer_norm(jnp.concatenate(cols, axis=1), g2_ref[...], b2_ref[...])
    is_step = i == pl.num_programs(0) - 1

    @pl.when(jnp.logical_not(is_step))
    def _():
        outp_ref[...] = out

    @pl.when(is_step)
    def _():
        outs_ref[...] = out


def _moe_combine(dest, h1, gates, g2, b2, ybuf):
    d = D_MODEL
    n_tok = h1.shape[0] // ROW_CH
    n_tiles = n_tok // TILE_T
    grid_spec = pltpu.PrefetchScalarGridSpec(
        num_scalar_prefetch=1,
        grid=(n_tiles,),
        in_specs=[pl.BlockSpec((TILE_T * ROW_CH, LANES), lambda i, s: (i, 0)),
                  pl.BlockSpec((TILE_T, LANES), lambda i, s: (i, 0)),
                  pl.BlockSpec((1, d), lambda i, s: (0, 0)),
                  pl.BlockSpec((1, d), lambda i, s: (0, 0)),
                  pl.BlockSpec(memory_space=pl.ANY)],
        out_specs=[pl.BlockSpec((TILE_T, d), lambda i, s: (jnp.minimum(i, n_tiles - 2), 0)),
                   pl.BlockSpec((TILE_T, d), lambda i, s: (0, 0))],
        scratch_shapes=[pltpu.VMEM((2, TOP_K * TILE_T * ROW_CH, LANES), F32), pltpu.SemaphoreType.DMA((2,))],
    )
    return pl.pallas_call(
        _moe_combine_body,
        grid_spec=grid_spec,
        out_shape=[jax.ShapeDtypeStruct((n_tok - TILE_T, d), F32), jax.ShapeDtypeStruct((TILE_T, d), F32)],
        compiler_params=_cparams(("arbitrary",), 48),
        name="moe_combine",
    )(dest.reshape(-1), h1, gates, g2, b2, ybuf)


def kernel(x_prompt, x_sample, state_s5_re, state_s5_im, state_mlstm_c, state_mlstm_n, state_mlstm_m, state_mlstm_conv, meta_tokens, ln_in_g, ln_in_b, w_in, b_in, s5_a_re, s5_a_im, s5_log_dt, s5_b_re, s5_b_im, s5_c_re, s5_c_im, s5_d, s5_w_glu, mlstm_conv_w, mlstm_conv_b, mlstm_wq, mlstm_wk, mlstm_norm_g, w_out, ln1_g, ln1_b, router_w, router_b, w_gate_up, b_gate_up, w_down, b_down, ln2_g, ln2_b):
    bsz, seq, d = x_prompt.shape
    dec_b = x_sample.shape[0]
    n_pad = TILE_T - N_META
    x_small = jnp.concatenate([jnp.zeros((n_pad, d), F32), meta_tokens, x_sample.reshape(dec_b, d)], axis=0)
    w_in_p = jnp.pad(w_in[0], ((0, 0), (0, N_IN_PAD - N_IN))).astype(BF16)
    b_in_p = jnp.pad(b_in[0], (0, N_IN_PAD - N_IN)).reshape(1, N_IN_PAD)
    g_in = ln_in_g.reshape(1, d)
    bb_in = ln_in_b.reshape(1, d)
    z_p = _inproj(x_prompt.reshape(bsz * seq, d), g_in, bb_in, w_in_p, b_in_p, 512, 1408)
    z_s = _inproj(x_small, g_in, bb_in, w_in_p, b_in_p, 256, 1408)
    tabs = _s5_prep(s5_a_re[0], s5_a_im[0], s5_log_dt[0], s5_b_re[0], s5_b_im[0])
    bd_b, bd_c = _s5_block_diag(tabs[6], tabs[7], s5_c_re[0], s5_c_im[0])
    wglu_bf = s5_w_glu[0].astype(BF16)
    d_skip = s5_d[0].reshape(1, D_S5)
    n_tiles = seq // TILE_T + 1
    zero_state = jnp.zeros((bsz, 1, N_STATE), F32)
    y_s5_p, s5r_p, s5i_p = _s5_seq(z_s, z_p, zero_state, zero_state, tabs[:6], bd_b, bd_c, d_skip, wglu_bf,
                                   bsz, n_tiles, n_pad)
    y_s5_s, s5r_s, s5i_s = _s5_step(z_s, 1, state_s5_re[0].reshape(dec_b, N_STATE),
                                    state_s5_im[0].reshape(dec_b, N_STATE), tabs[:6], bd_b, bd_c, d_skip, wglu_bf)

    conv_w = mlstm_conv_w[0]
    conv_b = mlstm_conv_b[0].reshape(1, D_ML)
    wq_bf = mlstm_wq[0].astype(BF16)
    wk_bf = mlstm_wk[0].astype(BF16)
    norm_g = mlstm_norm_g[0].reshape(1, D_ML)
    y_ml_p, c_p, n_p, m_p = _mlstm_seq(z_s, z_p, conv_w, conv_b, wq_bf, wk_bf, norm_g, bsz, n_tiles, n_pad)
    conv0 = state_mlstm_conv[0]
    y_ml_s, c_s, n_s, m_s = _mlstm_step(z_s, 1, conv0, state_mlstm_c[0], state_mlstm_n[0].reshape(dec_b, D_ML),
                                        state_mlstm_m[0], conv_w, conv_b, wq_bf, wk_bf, norm_g)

    rw = jnp.pad(router_w[0], ((0, 0), (0, LANES - N_EXPERTS)))
    rw_hi = rw.astype(BF16)
    rw_lo = (rw - rw_hi.astype(F32)).astype(BF16)
    rb = jnp.pad(router_b[0], (0, LANES - N_EXPERTS)).reshape(1, LANES)
    consts = (g_in, bb_in, w_out[0].astype(BF16), ln1_g[0].reshape(1, d), ln1_b[0].reshape(1, d), rw_hi, rw_lo, rb)
    h1, top_e, gates = _mix_out(x_prompt.reshape(bsz * seq, d), x_sample.reshape(dec_b, d),
                                y_s5_p, y_s5_s, y_ml_p, y_ml_s, consts, seq // TILE_T)
    n_tok = h1.shape[0] // ROW_CH
    dest, src, e_u, blk0, nblk = _moe_routing(top_e[:, :TOP_K], n_tok)
    ybuf = _moe_ffn(h1, src, e_u, blk0, nblk, w_gate_up[0], b_gate_up[0], w_down[0], b_down[0], src.shape[0])
    out_p, out_s = _moe_combine(dest, h1, gates, ln2_g[0].reshape(1, d), ln2_b[0].reshape(1, d), ybuf)

    y_prompt = out_p.reshape(bsz, seq, d)
    y_sample = out_s.reshape(dec_b, 1, d)
    xm_p = z_p.reshape(bsz, seq, N_IN_PAD)[:, seq - (CONV_W - 1):, D_S5:D_S5 + D_ML]
    xm_s = z_s[TILE_T:, D_S5:D_S5 + D_ML]
    conv_s = jnp.concatenate([conv0[:, 1:], xm_s[:, None, :]], axis=1)
    return (y_prompt, y_sample,
            s5r_p.reshape(1, bsz, N_GROUPS, S5_STATE), s5i_p.reshape(1, bsz, N_GROUPS, S5_STATE),
            c_p[None], n_p.reshape(1, bsz, N_HEADS, DH), m_p[None, :, :N_HEADS, 0], xm_p[None],
            s5r_s.reshape(1, dec_b, N_GROUPS, S5_STATE), s5i_s.reshape(1, dec_b, N_GROUPS, S5_STATE),
            c_s[None], n_s.reshape(1, dec_b, N_HEADS, DH), m_s[None], conv_s[None])
```

```python
import functools
import math

import jax
import jax.numpy as jnp
from jax import lax
from jax.experimental import pallas as pl
from jax.experimental.pallas import tpu as pltpu

F32 = jnp.float32
BF16 = jnp.bfloat16

D_MODEL = 2048
N_META = 16
D_S5 = 1024
D_ML = 1024
S5_CH = 16
N_GROUPS = 64
S5_STATE = 64
N_STATE = N_GROUPS * S5_STATE
N_HEADS = 4
DH = 256
CONV_W = 4
N_EXPERTS = 32
TOP_K = 4
D_FF = 2048
SWIGLU_LIMIT = 7.0
SWIGLU_ALPHA = 1.702
LN_EPS = 1e-5
DEEPNORM_ALPHA = 2.0 ** 0.25
N_IN = D_S5 + 3 * D_ML + 2 * N_HEADS

LANES = 128
SUBLANES = 8
MXU_DIM = 256
MXU_ROWS = 16

TILE_T = 128
SEG_LEN = TILE_T // SUBLANES
N_IN_PAD = 33 * LANES
GATE_COL = 4 * D_S5 // LANES
S5_KCH = D_S5 // MXU_DIM
S5_SCH = N_STATE // S5_KCH
SCAN_LW = 256


def _cparams(sem, vmem_mb=None):
    kw = dict(dimension_semantics=sem)
    if vmem_mb is not None:
        kw["vmem_limit_bytes"] = vmem_mb * 1024 * 1024
    return pltpu.CompilerParams(**kw)


def _layer_norm(x, g, b):
    mu = jnp.mean(x, axis=-1, keepdims=True)
    xc = x - mu
    var = jnp.mean(xc * xc, axis=-1, keepdims=True)
    return xc * lax.rsqrt(var + LN_EPS) * g + b


def _inproj_body(x_ref, g_ref, b_ref, w_ref, bias_ref, z_ref, hn_ref):
    @pl.when(pl.program_id(1) == 0)
    def _():
        hn_ref[...] = _layer_norm(x_ref[...], g_ref[...], b_ref[...]).astype(BF16)

    z_ref[...] = jnp.dot(hn_ref[...], w_ref[...], preferred_element_type=F32) + bias_ref[...]


def _inproj(x, g, b, w, bias, tm, tn):
    rows, d = x.shape
    n = w.shape[1]
    return pl.pallas_call(
        _inproj_body,
        grid=(rows // tm, n // tn),
        in_specs=[
            pl.BlockSpec((tm, d), lambda i, j: (i, 0)),
            pl.BlockSpec((1, d), lambda i, j: (0, 0)),
            pl.BlockSpec((1, d), lambda i, j: (0, 0)),
            pl.BlockSpec((d, tn), lambda i, j: (0, j)),
            pl.BlockSpec((1, tn), lambda i, j: (0, j)),
        ],
        out_specs=pl.BlockSpec((tm, tn), lambda i, j: (i, j)),
        out_shape=jax.ShapeDtypeStruct((rows, n), F32),
        scratch_shapes=[pltpu.VMEM((tm, d), BF16)],
        compiler_params=_cparams(("arbitrary", "arbitrary"), 48),
        name="inproj",
    )(x, g, b, w, bias)


def _cmul(ar, ai, br, bi):
    return ar * br - ai * bi, ar * bi + ai * br


def _s5_prep_body(are_ref, aim_ref, dt_ref, bre_ref, bim_ref,
                  pre_ref, pim_ref, hre_ref, him_ref, qre_ref, qim_ref, bbre_ref, bbim_ref):
    lr = are_ref[...]
    li = aim_ref[...]
    dt = jnp.exp(dt_ref[...])
    mag = jnp.exp(lr * dt)
    ar = mag * jnp.cos(li * dt)
    ai = mag * jnp.sin(li * dt)
    nr = ar - 1.0
    ni = ai
    den = lr * lr + li * li
    cr = (nr * lr + ni * li) / den
    ci = (ni * lr - nr * li) / den
    bbre_ref[...] = cr * bre_ref[...] - ci * bim_ref[...]
    bbim_ref[...] = cr * bim_ref[...] + ci * bre_ref[...]
    pr, pi = ar, ai
    for j in range(SEG_LEN):
        pre_ref[j:j + 1, :] = pr
        pim_ref[j:j + 1, :] = pi
        if j + 1 < SEG_LEN:
            pr, pi = _cmul(pr, pi, ar, ai)
    row = lax.broadcasted_iota(jnp.int32, (SUBLANES, N_STATE), 0)
    kr, ki = pr, pi
    for idx, k in enumerate((1, 2, 4)):
        hre_ref[idx * 8:(idx + 1) * 8, :] = jnp.where(row >= k, jnp.broadcast_to(kr, (SUBLANES, N_STATE)), 0.0)
        him_ref[idx * 8:(idx + 1) * 8, :] = jnp.where(row >= k, jnp.broadcast_to(ki, (SUBLANES, N_STATE)), 0.0)
        kr, ki = _cmul(kr, ki, kr, ki)
    qr, qi = pr, pi
    for s in range(SUBLANES):
        qre_ref[s:s + 1, :] = qr
        qim_ref[s:s + 1, :] = qi
        if s + 1 < SUBLANES:
            qr, qi = _cmul(qr, qi, pr, pi)


def _s5_prep(a_re, a_im, log_dt, b_re, b_im):
    n = N_STATE
    are = a_re.reshape(1, n)
    aim = a_im.reshape(1, n)
    dtl = jnp.broadcast_to(log_dt[:, None], (N_GROUPS, S5_STATE)).reshape(1, n)
    bre = b_re.transpose(2, 0, 1).reshape(S5_CH, n)
    bim = b_im.transpose(2, 0, 1).reshape(S5_CH, n)
    shp = lambda r: jax.ShapeDtypeStruct((r, n), F32)
    return pl.pallas_call(
        _s5_prep_body,
        out_shape=(shp(SEG_LEN), shp(SEG_LEN), shp(24), shp(24), shp(8), shp(8), shp(S5_CH), shp(S5_CH)),
        name="s5_prep",
    )(are, aim, dtl, bre, bim)


def _s5_block_diag(bb_re, bb_im, c_re, c_im):
    gpc = 16
    eye = jnp.eye(gpc, dtype=bool)

    def bd_in(bb):
        t = bb.reshape(S5_CH, S5_KCH, gpc, S5_STATE).transpose(1, 2, 0, 3)
        t = jnp.where(eye[None, :, None, :, None], t[:, :, :, None, :], 0.0)
        return t.reshape(S5_KCH, gpc * S5_CH, gpc * S5_STATE)

    def bd_out(c):
        t = c.reshape(S5_KCH, gpc, S5_CH, S5_STATE).transpose(0, 1, 3, 2)
        t = jnp.where(eye[None, :, None, :, None], t[:, :, :, None, :], 0.0)
        return t.reshape(S5_KCH, gpc * S5_STATE, gpc * S5_CH)

    bd_b = jnp.concatenate([bd_in(bb_re), bd_in(bb_im)], axis=2).astype(BF16)
    bd_c = jnp.concatenate([bd_out(c_re), bd_out(-c_im)], axis=1).astype(BF16)
    return bd_b, bd_c


def _gelu_glu(y, wglu_ref):
    y = 0.5 * y * (1.0 + lax.erf(y * math.sqrt(0.5)))
    gate = jnp.dot(y.astype(BF16), wglu_ref[...], preferred_element_type=F32)
    return y * jax.nn.sigmoid(gate)


def _s5_in_proj(u_bf, bdb_ref, bur_ref, bui_ref):
    for c in range(S5_KCH):
        r = jnp.dot(u_bf[:, c * MXU_DIM:(c + 1) * MXU_DIM], bdb_ref[c], preferred_element_type=F32)
        bur_ref[:, c * S5_SCH:(c + 1) * S5_SCH] = r[:, :S5_SCH]
        bui_ref[:, c * S5_SCH:(c + 1) * S5_SCH] = r[:, S5_SCH:]


def _s5_out_proj(xr_ref, xi_ref, bdc_ref):
    ys = []
    for c in range(S5_KCH):
        xr = xr_ref[:, c * S5_SCH:(c + 1) * S5_SCH].astype(BF16)
        xi = xi_ref[:, c * S5_SCH:(c + 1) * S5_SCH].astype(BF16)
        ys.append(jnp.dot(xr, bdc_ref[c, :S5_SCH, :], preferred_element_type=F32)
                  + jnp.dot(xi, bdc_ref[c, S5_SCH:, :], preferred_element_type=F32))
    return jnp.concatenate(ys, axis=1)


def _s5_seq_body(n_pad, us_ref, up_ref, x0r_ref, x0i_ref, bdb_ref, bdc_ref, pre_ref, pim_ref,
                 hre_ref, him_ref, qre_ref, qim_ref, d_ref, wglu_ref,
                 y_ref, xr_out_ref, xi_out_ref, bur_ref, bui_ref, cr_ref, ci_ref, perm_ref):
    t = pl.program_id(1)
    n = N_STATE

    @pl.when(t == 0)
    def _():
        cr_ref[...] = jnp.broadcast_to(x0r_ref[0], (SUBLANES, n))
        ci_ref[...] = jnp.broadcast_to(x0i_ref[0], (SUBLANES, n))

    n_lc = D_S5 // LANES

    def load_perm(ref):
        for c in range(n_lc):
            perm_ref[c] = ref[:, c * LANES:(c + 1) * LANES]
        return jnp.concatenate(
            [jnp.concatenate([perm_ref[c, pl.ds(j, SUBLANES, stride=SEG_LEN), :] for c in range(n_lc)], axis=1)
             for j in range(SEG_LEN)], axis=0)

    prow = lax.broadcasted_iota(jnp.int32, (TILE_T, 1), 0)
    time = (prow % SUBLANES) * SEG_LEN + prow // SUBLANES
    u_first = jnp.where(time >= n_pad, load_perm(us_ref), 0.0)
    u = jnp.where(t == 0, u_first, load_perm(up_ref))
    _s5_in_proj(u.astype(BF16), bdb_ref, bur_ref, bui_ref)

    row8 = lax.broadcasted_iota(jnp.int32, (SUBLANES, SCAN_LW), 0)

    def scan_lanes(lc, carry):
        ls = pl.ds(pl.multiple_of(lc * SCAN_LW, SCAN_LW), SCAN_LW)
        bc = lambda ref, j: jnp.broadcast_to(ref[j:j + 1, ls], (SUBLANES, SCAN_LW))
        ar, ai = bc(pre_ref, 0), bc(pim_ref, 0)
        xr = jnp.zeros((SUBLANES, SCAN_LW), F32)
        xi = jnp.zeros((SUBLANES, SCAN_LW), F32)
        for j in range(SEG_LEN):
            rs = slice(j * SUBLANES, (j + 1) * SUBLANES)
            nr = ar * xr - ai * xi + bur_ref[rs, ls]
            ni = ar * xi + ai * xr + bui_ref[rs, ls]
            xr, xi = nr, ni
            bur_ref[rs, ls] = xr
            bui_ref[rs, ls] = xi
        er, ei = xr, xi
        for idx, k in enumerate((1, 2, 4)):
            sr = pltpu.roll(er, k, axis=0)
            si = pltpu.roll(ei, k, axis=0)
            hr = hre_ref[idx * 8:(idx + 1) * 8, ls]
            hi = him_ref[idx * 8:(idx + 1) * 8, ls]
            er, ei = er + (hr * sr - hi * si), ei + (hr * si + hi * sr)
        cpr = cr_ref[:, ls]
        cpi = ci_ref[:, ls]
        qr = qre_ref[:, ls]
        qi = qim_ref[:, ls]
        er, ei = er + (qr * cpr - qi * cpi), ei + (qr * cpi + qi * cpr)
        inr = jnp.where(row8 == 0, cpr, pltpu.roll(er, 1, axis=0))
        ini = jnp.where(row8 == 0, cpi, pltpu.roll(ei, 1, axis=0))
        cr_ref[:, ls] = jnp.broadcast_to(er[SUBLANES - 1:SUBLANES, :], (SUBLANES, SCAN_LW))
        ci_ref[:, ls] = jnp.broadcast_to(ei[SUBLANES - 1:SUBLANES, :], (SUBLANES, SCAN_LW))
        for j in range(SEG_LEN):
            rs = slice(j * SUBLANES, (j + 1) * SUBLANES)
            pr, pi = bc(pre_ref, j), bc(pim_ref, j)
            bur_ref[rs, ls] = bur_ref[rs, ls] + (pr * inr - pi * ini)
            bui_ref[rs, ls] = bui_ref[rs, ls] + (pr * ini + pi * inr)
        return carry

    lax.fori_loop(0, n // SCAN_LW, scan_lanes, 0)

    y = _s5_out_proj(bur_ref, bui_ref, bdc_ref) + d_ref[...] * u
    y = _gelu_glu(y, wglu_ref)
    for j in range(SEG_LEN):
        for c in range(n_lc):
            perm_ref[c, pl.ds(j, SUBLANES, stride=SEG_LEN), :] = y[j * SUBLANES:(j + 1) * SUBLANES,
                                                                   c * LANES:(c + 1) * LANES]
    for c in range(n_lc):
        y_ref[:, c * LANES:(c + 1) * LANES] = perm_ref[c]

    @pl.when(t == pl.num_programs(1) - 1)
    def _():
        xr_out_ref[0] = cr_ref[0:1, :]
        xi_out_ref[0] = ci_ref[0:1, :]


def _s5_seq(z_small, z_seq, x0r, x0i, tabs, bd_b, bd_c, d_skip, wglu_bf, n_batch, n_tiles, n_pad):
    pre, pim, hre, him, qre, qim = tabs
    n = N_STATE
    full = lambda a: pl.BlockSpec(a.shape, lambda b, t: (0,) * a.ndim)
    per_seq = n_tiles - 1
    return pl.pallas_call(
        functools.partial(_s5_seq_body, n_pad),
        grid=(n_batch, n_tiles),
        in_specs=[
            pl.BlockSpec((TILE_T, D_S5), lambda b, t: (0, 0)),
            pl.BlockSpec((TILE_T, D_S5), lambda b, t: (b * per_seq + jnp.maximum(t - 1, 0), 0)),
            pl.BlockSpec((1, 1, n), lambda b, t: (b, 0, 0)),
            pl.BlockSpec((1, 1, n), lambda b, t: (b, 0, 0)),
            full(bd_b), full(bd_c), full(pre), full(pim), full(hre), full(him), full(qre), full(qim),
            full(d_skip), full(wglu_bf),
        ],
        out_specs=[
            pl.BlockSpec((TILE_T, D_S5), lambda b, t: (b * n_tiles + t, 0)),
            pl.BlockSpec((1, 1, n), lambda b, t: (b, 0, 0)),
            pl.BlockSpec((1, 1, n), lambda b, t: (b, 0, 0)),
        ],
        out_shape=[
            jax.ShapeDtypeStruct((n_batch * n_tiles * TILE_T, D_S5), F32),
            jax.ShapeDtypeStruct((n_batch, 1, n), F32),
            jax.ShapeDtypeStruct((n_batch, 1, n), F32),
        ],
        scratch_shapes=[pltpu.VMEM((TILE_T, n), F32), pltpu.VMEM((TILE_T, n), F32),
                        pltpu.VMEM((SUBLANES, n), F32), pltpu.VMEM((SUBLANES, n), F32),
                        pltpu.VMEM((D_S5 // LANES, TILE_T, LANES), F32)],
        compiler_params=_cparams(("arbitrary", "arbitrary"), 48),
        name="s5_seq",
    )(z_small, z_seq, x0r, x0i, bd_b, bd_c, pre, pim, hre, him, qre, qim, d_skip, wglu_bf)


def _s5_step_body(u_ref, x0r_ref, x0i_ref, bdb_ref, bdc_ref, pre_ref, pim_ref, d_ref, wglu_ref,
                  y_ref, xr_ref, xi_ref):
    u = u_ref[...]
    _s5_in_proj(u.astype(BF16), bdb_ref, xr_ref, xi_ref)
    ar = pre_ref[0:1, :]
    ai = pim_ref[0:1, :]
    x0r = x0r_ref[...]
    x0i = x0i_ref[...]
    xr_ref[...] = xr_ref[...] + (ar * x0r - ai * x0i)
    xi_ref[...] = xi_ref[...] + (ar * x0i + ai * x0r)
    y = _s5_out_proj(xr_ref, xi_ref, bdc_ref) + d_ref[...] * u
    y_ref[...] = _gelu_glu(y, wglu_ref)


def _s5_step(z_small, row_blk, x0r, x0i, tabs, bd_b, bd_c, d_skip, wglu_bf):
    pre, pim = tabs[0], tabs[1]
    rows = x0r.shape[0]
    n = N_STATE
    full = lambda a: pl.BlockSpec(a.shape, lambda i: (0,) * a.ndim)
    return pl.pallas_call(
        _s5_step_body,
        grid=(1,),
        in_specs=[pl.BlockSpec((rows, D_S5), lambda i: (row_blk, 0)), full(x0r), full(x0i),
                  full(bd_b), full(bd_c), full(pre), full(pim), full(d_skip), full(wglu_bf)],
        out_specs=[pl.BlockSpec((rows, D_S5), lambda i: (0, 0)),
                   pl.BlockSpec((rows, n), lambda i: (0, 0)), pl.BlockSpec((rows, n), lambda i: (0, 0))],
        out_shape=[jax.ShapeDtypeStruct((rows, D_S5), F32), jax.ShapeDtypeStruct((rows, n), F32),
                   jax.ShapeDtypeStruct((rows, n), F32)],
        compiler_params=_cparams(("arbitrary",), 48),
        name="s5_step",
    )(z_small, x0r, x0i, bd_b, bd_c, pre, pim, d_skip, wglu_bf)


NEG_INF = float("-inf")
N_STEP_SCALARS = 5 * N_HEADS


def _log_sigmoid(x):
    return jnp.minimum(x, 0.0) - jnp.log1p(jnp.exp(-jnp.abs(x)))


def _split3(x):
    hi = x.astype(BF16)
    r1 = x - hi.astype(F32)
    mid = r1.astype(BF16)
    lo = (r1 - mid.astype(F32)).astype(BF16)
    return hi, mid, lo


def _head_norm_gate(h, o, g):
    mu = jnp.mean(h, axis=-1, keepdims=True)
    hc = h - mu
    var = jnp.mean(hc * hc, axis=-1, keepdims=True)
    return jax.nn.sigmoid(o) * (hc * lax.rsqrt(var + LN_EPS) * g)


def _dot_nt(a, b):
    return lax.dot_general(a, b, (((1,), (1,)), ((), ())), preferred_element_type=F32)


def _dot_tn(a, b):
    return lax.dot_general(a, b, (((0,), (0,)), ((), ())), preferred_element_type=F32)


def _mlstm_seq_body(n_pad, sxm_ref, sv_ref, so_ref, sg_ref, pxm_ref, pv_ref, po_ref, pg_ref,
                    cw_ref, cb_ref, wq_ref, wk_ref, ng_ref,
                    y_ref, c_out_ref, n_out_ref, m_out_ref, c_ref, n_ref, m_ref, prev_ref):
    t = pl.program_id(1)
    L = TILE_T

    @pl.when(t == 0)
    def _():
        c_ref[...] = jnp.zeros_like(c_ref)
        n_ref[...] = jnp.zeros_like(n_ref)
        m_ref[...] = jnp.zeros_like(m_ref)
        prev_ref[...] = jnp.zeros_like(prev_ref)

    first = t == 0
    row = lax.broadcasted_iota(jnp.int32, (L, 1), 0)
    valid = jnp.logical_or(jnp.logical_not(first), row >= n_pad)
    xm = jnp.where(valid, jnp.where(first, sxm_ref[...], pxm_ref[...]), 0.0)
    v = jnp.where(first, sv_ref[...], pv_ref[...])
    o = jnp.where(first, so_ref[...], po_ref[...])
    gt = jnp.where(first, sg_ref[...], pg_ref[...])

    prev = prev_ref[...]

    def shifted(j):
        if j == 0:
            return xm
        return pltpu.roll(jnp.where(row >= L - j, prev, xm), j, axis=0)

    xc = cb_ref[...]
    for j in range(CONV_W):
        xc = xc + shifted(CONV_W - 1 - j) * cw_ref[j:j + 1, :]
    prev_ref[...] = xm
    xc = xc * jax.nn.sigmoid(xc)

    ig = jnp.where(valid, gt, NEG_INF)
    lf = jnp.where(valid, _log_sigmoid(gt), 0.0)
    ti = lax.broadcasted_iota(jnp.int32, (L, L), 0)
    si = lax.broadcasted_iota(jnp.int32, (L, L), 1)
    causal = si <= ti
    tri = jnp.where(causal, 1.0, 0.0).astype(BF16)
    bc = sum(jnp.dot(tri, p, preferred_element_type=F32) for p in _split3(lf))
    ig_t = ig.T
    bc_t = bc.T

    for h in range(N_HEADS):
        hs = slice(h * DH, (h + 1) * DH)
        b_col = bc[:, N_HEADS + h:N_HEADS + h + 1]
        b_row = bc_t[N_HEADS + h:N_HEADS + h + 1, :]
        ig_row = ig_t[h:h + 1, :]
        ig_col = ig[:, h:h + 1]
        m_prev = m_ref[h:h + 1, 0:1]
        dlog = jnp.where(causal, b_col - b_row + ig_row, NEG_INF)
        inter = b_col + m_prev
        m_t = jnp.maximum(jnp.max(dlog, axis=1, keepdims=True), inter)
        w = jnp.exp(dlog - m_t)
        g = jnp.exp(inter - m_t)
        xh = xc[:, hs].astype(BF16)
        q = jnp.dot(xh, wq_ref[h], preferred_element_type=F32)
        k = jnp.dot(xh, wk_ref[h], preferred_element_type=F32) * (DH ** -0.5)
        qb = q.astype(BF16)
        kb = k.astype(BF16)
        s = _dot_nt(qb, kb) * w
        vh = v[:, hs]
        cmat = c_ref[h]
        n_row = n_ref[h]
        num = jnp.dot(s.astype(BF16), vh.astype(BF16), preferred_element_type=F32) \
            + g * _dot_nt(qb, cmat.astype(BF16))
        den = jnp.sum(s, axis=1, keepdims=True) + g * jnp.sum(q * n_row, axis=1, keepdims=True)
        hh = num / jnp.maximum(jnp.abs(den), jnp.exp(-m_t))
        b_last = b_col[L - 1:L, :]
        wlog = b_last - b_col + ig_col
        m_new = jnp.maximum(b_last + m_prev, jnp.max(wlog, axis=0, keepdims=True))
        w_end = jnp.exp(wlog - m_new)
        g_end = jnp.exp(b_last + m_prev - m_new)
        c_ref[h] = g_end * cmat + _dot_tn((vh * w_end).astype(BF16), kb)
        n_ref[h] = g_end * n_row + jnp.sum(w_end * k, axis=0, keepdims=True)
        m_ref[h:h + 1, :] = jnp.broadcast_to(m_new, (1, LANES))
        y_ref[:, hs] = _head_norm_gate(hh, o[:, hs], ng_ref[:, hs])

    @pl.when(t == pl.num_programs(1) - 1)
    def _():
        c_out_ref[0] = c_ref[...]
        n_out_ref[0] = n_ref[...]
        m_out_ref[0] = m_ref[...]


def _mlstm_seq(z_small, z_seq, conv_w, conv_b, wq_bf, wk_bf, norm_g, n_batch, n_tiles, n_pad):
    per_seq = n_tiles - 1
    full = lambda a: pl.BlockSpec(a.shape, lambda b, t: (0,) * a.ndim)
    nb = D_ML // LANES

    def small(col, width):
        return pl.BlockSpec((TILE_T, width), lambda b, t: (0, col))

    def seq(col, width):
        return pl.BlockSpec((TILE_T, width), lambda b, t: (b * per_seq + jnp.maximum(t - 1, 0), col))

    return pl.pallas_call(
        functools.partial(_mlstm_seq_body, n_pad),
        grid=(n_batch, n_tiles),
        in_specs=[small(1, D_ML), small(2, D_ML), small(3, D_ML), small(GATE_COL, LANES),
                  seq(1, D_ML), seq(2, D_ML), seq(3, D_ML), seq(GATE_COL, LANES),
                  full(conv_w), full(conv_b), full(wq_bf), full(wk_bf), full(norm_g)],
        out_specs=[
            pl.BlockSpec((TILE_T, D_ML), lambda b, t: (b * n_tiles + t, 0)),
            pl.BlockSpec((1, N_HEADS, DH, DH), lambda b, t: (b, 0, 0, 0)),
            pl.BlockSpec((1, N_HEADS, 1, DH), lambda b, t: (b, 0, 0, 0)),
            pl.BlockSpec((1, SUBLANES, LANES), lambda b, t: (b, 0, 0)),
        ],
        out_shape=[
            jax.ShapeDtypeStruct((n_batch * n_tiles * TILE_T, D_ML), F32),
            jax.ShapeDtypeStruct((n_batch, N_HEADS, DH, DH), F32),
            jax.ShapeDtypeStruct((n_batch, N_HEADS, 1, DH), F32),
            jax.ShapeDtypeStruct((n_batch, SUBLANES, LANES), F32),
        ],
        scratch_shapes=[pltpu.VMEM((N_HEADS, DH, DH), F32), pltpu.VMEM((N_HEADS, 1, DH), F32),
                        pltpu.VMEM((SUBLANES, LANES), F32), pltpu.VMEM((TILE_T, D_ML), F32)],
        compiler_params=_cparams(("arbitrary", "arbitrary"), 48),
        name="mlstm_seq",
    )(z_small, z_small, z_small, z_small, z_seq, z_seq, z_seq, z_seq, conv_w, conv_b, wq_bf, wk_bf, norm_g)


def _mlstm_step_a_body(xm_ref, g_ref, conv0_ref, m0_ref, cw_ref, cb_ref, wq_ref, wk_ref,
                       q_ref, k_ref, sc_ref):
    xc = cb_ref[...]
    for j in range(CONV_W - 1):
        xc = xc + conv0_ref[j] * cw_ref[j:j + 1, :]
    xc = xc + xm_ref[...] * cw_ref[CONV_W - 1:CONV_W, :]
    xc = xc * jax.nn.sigmoid(xc)
    gt = g_ref[...]
    ig = gt[:, 0:N_HEADS]
    lf = _log_sigmoid(gt[:, N_HEADS:2 * N_HEADS])
    inter = lf + m0_ref[...]
    m_t = jnp.maximum(ig, inter)
    w = jnp.exp(ig - m_t)
    g = jnp.exp(inter - m_t)
    qks = []
    for h in range(N_HEADS):
        hs = slice(h * DH, (h + 1) * DH)
        xh = xc[:, hs].astype(BF16)
        q = jnp.dot(xh, wq_ref[h], preferred_element_type=F32)
        k = jnp.dot(xh, wk_ref[h], preferred_element_type=F32) * (DH ** -0.5)
        q_ref[:, hs] = q
        k_ref[:, hs] = k
        qks.append(jnp.sum(q * k, axis=1, keepdims=True))
    s = jnp.concatenate(qks, axis=1) * w
    rows = s.shape[0]
    sc_ref[...] = jnp.concatenate(
        [s, w, g, m_t, jnp.exp(-m_t), jnp.zeros((rows, LANES - 5 * N_HEADS), F32)], axis=1)


def _mlstm_step_b_body(bb, sc_ref, q_ref, k_ref, n_ref, v_ref, o_ref, ng_ref, c_ref,
                       y_ref, c_out_ref, n_out_ref):
    i0 = pl.program_id(0) * bb
    row = lax.broadcasted_iota(jnp.int32, (MXU_ROWS, DH), 0)

    def hi_lo(x):
        hi = x.astype(BF16).astype(F32)
        return jnp.broadcast_to(hi, (MXU_ROWS, DH)), jnp.broadcast_to(x - hi, (MXU_ROWS, DH))

    for i in range(bb):
        for h in range(N_HEADS):
            hs = slice(h * DH, (h + 1) * DH)
            base = (i0 + i) * N_STEP_SCALARS
            s = sc_ref[base + h]
            w = sc_ref[base + N_HEADS + h]
            g = sc_ref[base + 2 * N_HEADS + h]
            em = sc_ref[base + 4 * N_HEADS + h]
            rsel = pl.ds(i0 + i, 1)
            q_row = q_ref[rsel, hs]
            k_row = k_ref[rsel, hs]
            n_row = n_ref[rsel, hs]
            v_row = v_ref[rsel, hs]
            cmat = c_ref[i, h]
            qh, ql = hi_lo(q_row)
            qmat = jnp.where(row == 0, qh, jnp.where(row == 1, ql, 0.0)).astype(BF16)
            cq = _dot_nt(qmat, cmat.astype(BF16))
            num = s * v_row + g * (cq[0:1, :] + cq[1:2, :])
            den = s + g * jnp.sum(n_row * q_row, axis=1, keepdims=True)
            hh = num / jnp.maximum(jnp.abs(den), em)
            vh, vl = hi_lo(w * v_row)
            kh, kl = hi_lo(k_row)
            a = jnp.where(row < 2, vh, jnp.where(row < 4, vl, 0.0)).astype(BF16)
            b = jnp.where(row < 4, jnp.where(row % 2 == 0, kh, kl), 0.0).astype(BF16)
            c_out_ref[i, h] = g * cmat + _dot_tn(a, b)
            n_out_ref[rsel, hs] = g * n_row + w * k_row
            y_ref[rsel, hs] = _head_norm_gate(hh, o_ref[rsel, hs], ng_ref[:, hs])


def _mlstm_step(z_small, row_blk, conv0, c0, n0, m0, conv_w, conv_b, wq_bf, wk_bf, norm_g, bb=4):
    rows = c0.shape[0]
    full = lambda a: pl.BlockSpec(a.shape, lambda i: (0,) * a.ndim)
    conv0_t = conv0.transpose(1, 0, 2)
    q, k, sc = pl.pallas_call(
        _mlstm_step_a_body,
        grid=(1,),
        in_specs=[pl.BlockSpec((rows, D_ML), lambda i: (row_blk, 1)),
                  pl.BlockSpec((rows, LANES), lambda i: (row_blk, GATE_COL)),
                  full(conv0_t), full(m0), full(conv_w), full(conv_b), full(wq_bf), full(wk_bf)],
        out_specs=[pl.BlockSpec((rows, D_ML), lambda i: (0, 0)), pl.BlockSpec((rows, D_ML), lambda i: (0, 0)),
                   pl.BlockSpec((rows, LANES), lambda i: (0, 0))],
        out_shape=[jax.ShapeDtypeStruct((rows, D_ML), F32), jax.ShapeDtypeStruct((rows, D_ML), F32),
                   jax.ShapeDtypeStruct((rows, LANES), F32)],
        compiler_params=_cparams(("arbitrary",), 48),
        name="mlstm_step_a",
    )(z_small, z_small, conv0_t, m0, conv_w, conv_b, wq_bf, wk_bf)
    row_spec = pl.BlockSpec((rows, D_ML), lambda i: (0, 0))
    c_spec = pl.BlockSpec((bb, N_HEADS, DH, DH), lambda i: (i, 0, 0, 0))
    y, c_new, n_new = pl.pallas_call(
        functools.partial(_mlstm_step_b_body, bb),
        grid=(rows // bb,),
        in_specs=[pl.BlockSpec(memory_space=pltpu.SMEM), row_spec, row_spec, row_spec,
                  pl.BlockSpec((rows, D_ML), lambda i: (row_blk, 2)),
                  pl.BlockSpec((rows, D_ML), lambda i: (row_blk, 3)),
                  pl.BlockSpec((1, D_ML), lambda i: (0, 0)), c_spec],
        out_specs=[row_spec, c_spec, row_spec],
        out_shape=[jax.ShapeDtypeStruct((rows, D_ML), F32), jax.ShapeDtypeStruct(c0.shape, F32),
                   jax.ShapeDtypeStruct((rows, D_ML), F32)],
        compiler_params=_cparams(("arbitrary",), 56),
        name="mlstm_step_b",
    )(sc[:, :N_STEP_SCALARS].reshape(rows * N_STEP_SCALARS), q, k, n0, z_small, z_small, norm_g, c0)
    return y, c_new, n_new, sc[:, 3 * N_HEADS:4 * N_HEADS]


ROW_CH = D_MODEL // LANES


def _load_row_tiles(ref, row0, m, c):
    return ref[pl.ds(row0 * ROW_CH + c, m, stride=ROW_CH), :]


def _store_row_tiles(ref, row0, val):
    m = val.shape[0]
    for c in range(ROW_CH):
        ref[pl.ds(row0 * ROW_CH + c, m, stride=ROW_CH), :] = val[:, c * LANES:(c + 1) * LANES]


def _mix_out_body(xp_ref, xs_ref, ysp0_ref, ysp1_ref, yss_ref, ymp0_ref, ymp1_ref, yms_ref,
                  gin_ref, bin_ref, wout_ref, g1_ref, b1_ref, rwh_ref, rwl_ref, rb_ref,
                  h1_ref, e_ref, gate_ref):
    is_step = pl.program_id(0) == pl.num_programs(0) - 1
    consts = (gin_ref, bin_ref, wout_ref, g1_ref, b1_ref, rwh_ref, rwl_ref, rb_ref)
    outs = (h1_ref, e_ref, gate_ref)
    two = lambda a, b: jnp.concatenate([a[...], b[...]], axis=0)

    @pl.when(jnp.logical_not(is_step))
    def _():
        _mix_out_tile(xp_ref[...], two(ysp0_ref, ysp1_ref), two(ymp0_ref, ymp1_ref), consts, outs)

    @pl.when(is_step)
    def _():
        _mix_out_tile(two(xs_ref, xs_ref), two(yss_ref, yss_ref), two(yms_ref, yms_ref), consts, outs)


def _mix_out_tile(x, ys, ym, consts, outs):
    gin_ref, bin_ref, wout_ref, g1_ref, b1_ref, rwh_ref, rwl_ref, rb_ref = consts
    h1_ref, e_ref, gate_ref = outs
    hp = _layer_norm(x, gin_ref[...], bin_ref[...])
    ycat = jnp.concatenate([ys, ym], axis=1).astype(BF16)
    mix = jnp.dot(ycat, wout_ref[...], preferred_element_type=F32)
    h1 = _layer_norm(DEEPNORM_ALPHA * hp + mix, g1_ref[...], b1_ref[...])
    _store_row_tiles(h1_ref, 0, h1)
    xh = h1.astype(BF16)
    xl = (h1 - xh.astype(F32)).astype(BF16)
    logits = (jnp.dot(xh, rwh_ref[...], preferred_element_type=F32)
              + jnp.dot(xh, rwl_ref[...], preferred_element_type=F32)
              + jnp.dot(xl, rwh_ref[...], preferred_element_type=F32)) + rb_ref[...]
    rows = logits.shape[0]
    lane = lax.broadcasted_iota(jnp.int32, (rows, LANES), 1)
    logits = jnp.where(lane < N_EXPERTS, logits, NEG_INF)
    e_acc = jnp.zeros((rows, LANES), jnp.int32)
    v_acc = jnp.full((rows, LANES), NEG_INF, F32)
    for k in range(TOP_K):
        mx = jnp.max(logits, axis=1, keepdims=True)
        idx = jnp.min(jnp.where(logits == mx, lane, LANES), axis=1, keepdims=True)
        e_acc = jnp.where(lane == k, idx, e_acc)
        v_acc = jnp.where(lane == k, mx, v_acc)
        logits = jnp.where(lane == idx, NEG_INF, logits)
    p = jnp.exp(v_acc - jnp.max(v_acc, axis=1, keepdims=True))
    e_ref[...] = e_acc
    gate_ref[...] = p / jnp.sum(p, axis=1, keepdims=True)


def _mix_out(x_p, x_s, ys_p, ys_s, ym_p, ym_s, consts, tiles_per_seq):
    d = D_MODEL
    full = lambda a: pl.BlockSpec(a.shape, lambda i: (0,) * a.ndim)
    tm = 2 * TILE_T
    n_p = x_p.shape[0] // tm
    n_tiles = n_p + 1
    rows = n_tiles * tm

    def p_idx(i):
        return jnp.minimum(i, n_p - 1)

    def frame_idx(i, half):
        j = 2 * p_idx(i) + half
        return (j // tiles_per_seq) * (tiles_per_seq + 1) + j % tiles_per_seq + 1

    frame = lambda width, half: pl.BlockSpec((TILE_T, width), lambda i: (frame_idx(i, half), 0))
    first = lambda width: pl.BlockSpec((TILE_T, width), lambda i: (0, 0))
    return pl.pallas_call(
        _mix_out_body,
        grid=(n_tiles,),
        in_specs=[pl.BlockSpec((tm, d), lambda i: (p_idx(i), 0)), first(d),
                  frame(D_S5, 0), frame(D_S5, 1), first(D_S5),
                  frame(D_ML, 0), frame(D_ML, 1), first(D_ML)] + [full(a) for a in consts],
        out_specs=[pl.BlockSpec((tm * ROW_CH, LANES), lambda i: (i, 0)),
                   pl.BlockSpec((tm, LANES), lambda i: (i, 0)),
                   pl.BlockSpec((tm, LANES), lambda i: (i, 0))],
        out_shape=[jax.ShapeDtypeStruct((rows * ROW_CH, LANES), F32), jax.ShapeDtypeStruct((rows, LANES), jnp.int32),
                   jax.ShapeDtypeStruct((rows, LANES), F32)],
        compiler_params=_cparams(("arbitrary",), 56),
        name="mix_out",
    )(x_p, x_s, ys_p, ys_p, ys_s, ym_p, ym_p, ym_s, *consts)


MOE_BLK = 128
MOE_TM = 1536
MOE_TF = 256
MOE_SUB_MAX = 6
DMA_UNROLL = 8


def _invert_rows_body(dest_ref, src_ref):
    def zero(i, c):
        for q in range(DMA_UNROLL):
            src_ref[i * DMA_UNROLL + q] = 0
        return c
    lax.fori_loop(0, src_ref.shape[0] // DMA_UNROLL, zero, 0)

    def put(i, c):
        for q in range(DMA_UNROLL):
            src_ref[dest_ref[i * DMA_UNROLL + q]] = i * (DMA_UNROLL // TOP_K) + q // TOP_K
        return c
    lax.fori_loop(0, dest_ref.shape[0] // DMA_UNROLL, put, 0)


def _invert_rows(dest_flat, rows_pad):
    return pl.pallas_call(
        _invert_rows_body,
        in_specs=[pl.BlockSpec(memory_space=pltpu.SMEM)],
        out_specs=pl.BlockSpec(memory_space=pltpu.SMEM),
        out_shape=jax.ShapeDtypeStruct((rows_pad,), jnp.int32),
        name="invert_rows",
    )(dest_flat)


def _moe_routing(top_e, n_tok):
    n_pairs = n_tok * TOP_K
    rows_pad = n_pairs + N_EXPERTS * MOE_BLK
    n_pass = N_EXPERTS + rows_pad // MOE_TM
    onehot = (top_e[:, :, None] == jnp.arange(N_EXPERTS, dtype=jnp.int32)).astype(jnp.int32).sum(1)
    incl = jnp.cumsum(onehot, axis=0)
    counts = incl[-1]
    rank = jnp.take_along_axis(incl - onehot, top_e, axis=1)
    padded = (counts + MOE_BLK - 1) // MOE_BLK * MOE_BLK
    pad_end = jnp.cumsum(padded)
    pad_start = pad_end - padded
    dest = pad_start[top_e] + rank
    src = _invert_rows(dest.reshape(-1).astype(jnp.int32), rows_pad)
    passes_e = (padded + MOE_TM - 1) // MOE_TM
    pass_end = jnp.cumsum(passes_e)
    u = jnp.arange(n_pass, dtype=jnp.int32)
    e_u = jnp.minimum(jnp.searchsorted(pass_end, u, side="right"), N_EXPERTS - 1).astype(jnp.int32)
    j_u = u - (pass_end - passes_e)[e_u]
    rem = padded[e_u] - j_u * MOE_TM
    nblk = jnp.where(u < pass_end[-1], jnp.clip(rem, 0, MOE_TM) // MOE_BLK, 0).astype(jnp.int32)
    blk0 = ((pad_start[e_u] + j_u * MOE_TM) // MOE_BLK).astype(jnp.int32)
    blk0 = jnp.where(nblk > 0, blk0, 0)
    last_e = e_u[jnp.maximum(pass_end[-1] - 1, 0)]
    e_u = jnp.where(nblk > 0, e_u, last_e)
    nblk = jnp.concatenate([nblk, (pad_end[-1:] // MOE_BLK).astype(jnp.int32)])
    return dest.astype(jnp.int32), src, e_u, blk0, nblk


def _moe_ffn_body(e_ref, blk0_ref, nblk_ref, src_ref, h1_ref, wg_ref, wu_ref, bg_ref, bu_ref, wd_ref, bd_ref,
                  ybuf_ref, xg_ref, xb_ref, acc_ref, w1_ref, w2_ref, stage_ref, gsem, osem):
    u = pl.program_id(0)
    f = pl.program_id(1)
    n_pass = pl.num_programs(0)
    nblk = nblk_ref[u]
    row0 = blk0_ref[u] * MOE_BLK
    blk_rt = MOE_BLK * ROW_CH

    def gather_copy(p, r):
        row = blk0_ref[p] * MOE_BLK + r
        return pltpu.make_async_copy(h1_ref.at[pl.ds(src_ref[row] * ROW_CH, ROW_CH)],
                                     xg_ref.at[pl.ds(r * ROW_CH, ROW_CH)], gsem)

    def issue_gather(p):
        def start(i, c):
            for q in range(DMA_UNROLL):
                gather_copy(p, i * DMA_UNROLL + q).start()
            return c
        lax.fori_loop(0, nblk_ref[p] * (MOE_BLK // DMA_UNROLL), start, 0)

    def out_copy(b):
        dst = pl.ds(pl.multiple_of((row0 + b * MOE_BLK) * ROW_CH, blk_rt), blk_rt)
        return pltpu.make_async_copy(stage_ref.at[b % 2], ybuf_ref.at[dst], osem.at[b % 2])

    @pl.when(jnp.logical_and(nblk > 0, f == 0))
    def _():
        @pl.when(u == 0)
        def _():
            issue_gather(u)

        def wait(i, c):
            for q in range(DMA_UNROLL):
                gather_copy(u, i * DMA_UNROLL + q).wait()
            return c
        lax.fori_loop(0, nblk * (MOE_BLK // DMA_UNROLL), wait, 0)

        def cast(b, c):
            rs = pl.ds(pl.multiple_of(b * MOE_BLK, MOE_BLK), MOE_BLK)
            for ch in range(ROW_CH):
                xb_ref[rs, ch * LANES:(ch + 1) * LANES] = _load_row_tiles(
                    xg_ref, b * MOE_BLK, MOE_BLK, ch).astype(BF16)
            acc_ref[rs, :] = jnp.zeros((MOE_BLK, acc_ref.shape[1]), F32)
            return c
        lax.fori_loop(0, nblk, cast, 0)

    nxt = jnp.minimum(u + 1, n_pass - 1)

    @pl.when(jnp.logical_and(jnp.logical_and(nblk > 0, f == 1),
                             jnp.logical_and(u + 1 < n_pass, nblk_ref[nxt] > 0)))
    def _():
        issue_gather(nxt)

    @pl.when(nblk > 0)
    def _():
        w1_ref[:, :MOE_TF] = wg_ref[0].astype(BF16)
        w1_ref[:, MOE_TF:] = wu_ref[0].astype(BF16)
        w2_ref[...] = wd_ref[0].astype(BF16)
        bg = bg_ref[0]
        bu = bu_ref[0]

        def ffn_rows(r0, m):
            rs = pl.ds(pl.multiple_of(r0, MOE_BLK), m)
            h = jnp.dot(xb_ref[rs, :], w1_ref[...], preferred_element_type=F32)
            x_glu = jnp.minimum(h[:, :MOE_TF] + bg, SWIGLU_LIMIT)
            x_lin = jnp.clip(h[:, MOE_TF:] + bu, -SWIGLU_LIMIT, SWIGLU_LIMIT)
            act = x_glu * jax.nn.sigmoid(SWIGLU_ALPHA * x_glu) * (x_lin + 1.0)
            acc_ref[rs, :] += jnp.dot(act.astype(BF16), w2_ref[...], preferred_element_type=F32)

        n_sub = (nblk + (MOE_SUB_MAX - 1)) // MOE_SUB_MAX
        q = nblk // n_sub
        n_hi = nblk - q * n_sub

        def sub(i, blk):
            sz = q + (i < n_hi).astype(jnp.int32)
            for s in range(1, MOE_SUB_MAX + 1):
                @pl.when(sz == s)
                def _(s=s):
                    ffn_rows(blk * MOE_BLK, s * MOE_BLK)
            return blk + sz
        lax.fori_loop(0, n_sub, sub, 0)

    @pl.when(jnp.logical_and(nblk > 0, f == pl.num_programs(1) - 1))
    def _():
        def emit(b, c):
            @pl.when(b >= 2)
            def _():
                out_copy(b - 2).wait()
            rs = pl.ds(pl.multiple_of(b * MOE_BLK, MOE_BLK), MOE_BLK)
            val = acc_ref[rs, :] + bd_ref[0]
            for ch in range(ROW_CH):
                stage_ref[b % 2, pl.ds(ch, MOE_BLK, stride=ROW_CH), :] = val[:, ch * LANES:(ch + 1) * LANES]
            out_copy(b).start()
            return c
        lax.fori_loop(0, nblk, emit, 0)

        @pl.when(nblk >= 2)
        def _():
            out_copy(nblk - 2).wait()
        out_copy(nblk - 1).wait()

    @pl.when(jnp.logical_and(u == pl.num_programs(0) - 1, f == pl.num_programs(1) - 1))
    def _():
        used = nblk_ref[pl.num_programs(0)]
        total = ybuf_ref.shape[0] // blk_rt
        stage_ref[0] = jnp.zeros((blk_rt, LANES), F32)

        def zero_copy(b):
            dst = pl.ds(pl.multiple_of(b * blk_rt, blk_rt), blk_rt)
            return pltpu.make_async_copy(stage_ref.at[0], ybuf_ref.at[dst], osem.at[0])

        def start(b, c):
            zero_copy(b).start()
            return c
        lax.fori_loop(used, total, start, 0)

        def wait(b, c):
            zero_copy(b).wait()
            return c
        lax.fori_loop(used, total, wait, 0)


def _moe_ffn(h1, src, e_u, blk0, nblk, w_gu, b_gu, w_dn, b_dn, rows_pad):
    d = D_MODEL
    n_pass = e_u.shape[0]
    n_f = D_FF // MOE_TF
    last_f = n_f - 1

    def fsel(u, f, nblk_ref):
        return jnp.where(nblk_ref[u] > 0, f, last_f)

    grid_spec = pltpu.PrefetchScalarGridSpec(
        num_scalar_prefetch=4,
        grid=(n_pass, n_f),
        in_specs=[
            pl.BlockSpec(memory_space=pl.ANY),
            pl.BlockSpec((1, d, MOE_TF), lambda u, f, e, b0, nb, s: (e[u], 0, fsel(u, f, nb))),
            pl.BlockSpec((1, d, MOE_TF), lambda u, f, e, b0, nb, s: (e[u], 0, n_f + fsel(u, f, nb))),
            pl.BlockSpec((1, 1, MOE_TF), lambda u, f, e, b0, nb, s: (e[u], 0, fsel(u, f, nb))),
            pl.BlockSpec((1, 1, MOE_TF), lambda u, f, e, b0, nb, s: (e[u], 0, n_f + fsel(u, f, nb))),
            pl.BlockSpec((1, MOE_TF, d), lambda u, f, e, b0, nb, s: (e[u], fsel(u, f, nb), 0)),
            pl.BlockSpec((1, 1, d), lambda u, f, e, b0, nb, s: (e[u], 0, 0)),
        ],
        out_specs=pl.BlockSpec(memory_space=pl.ANY),
        scratch_shapes=[pltpu.VMEM((MOE_TM * ROW_CH, LANES), F32), pltpu.VMEM((MOE_TM, d), BF16),
                        pltpu.VMEM((MOE_TM, d), F32), pltpu.VMEM((d, 2 * MOE_TF), BF16), pltpu.VMEM((MOE_TF, d), BF16),
                        pltpu.VMEM((2, MOE_BLK * ROW_CH, LANES), F32),
                        pltpu.SemaphoreType.DMA(()), pltpu.SemaphoreType.DMA((2,))],
    )
    return pl.pallas_call(
        _moe_ffn_body,
        grid_spec=grid_spec,
        out_shape=jax.ShapeDtypeStruct((rows_pad * ROW_CH, LANES), F32),
        compiler_params=_cparams(("arbitrary", "arbitrary"), 60),
        name="moe_ffn",
    )(e_u, blk0, nblk, src, h1, w_gu, w_gu, b_gu.reshape(N_EXPERTS, 1, 2 * D_FF), b_gu.reshape(N_EXPERTS, 1, 2 * D_FF),
      w_dn, b_dn.reshape(N_EXPERTS, 1, d))


def _moe_combine_body(dest_ref, h1_ref, gate_ref, g2_ref, b2_ref, ybuf_ref, outp_ref, outs_ref,
                      buf_ref, sem, pre_ref):
    i = pl.program_id(0)
    n_pairs = TILE_T * TOP_K
    slot = i % 2

    def copy(tile, sl, j, q):
        k = q % TOP_K
        r = j * (DMA_UNROLL // TOP_K) + q // TOP_K
        src_row = dest_ref[tile * n_pairs + j * DMA_UNROLL + q]
        return pltpu.make_async_copy(ybuf_ref.at[pl.ds(src_row * ROW_CH, ROW_CH)],
                                     buf_ref.at[sl, k, r // SUBLANES, pl.ds(0, ROW_CH), r % SUBLANES], sem.at[sl])

    def issue(tile, sl):
        def body(j, c):
            for q in range(DMA_UNROLL):
                copy(tile, sl, j, q).start()
            return c
        lax.fori_loop(0, n_pairs // DMA_UNROLL, body, 0)

    @pl.when(i == 0)
    def _():
        issue(0, 0)

    @pl.when(i + 1 < pl.num_programs(0))
    def _():
        issue(i + 1, 1 - slot)

    def wait(j, c):
        for q in range(DMA_UNROLL):
            copy(i, slot, j, q).wait()
        return c
    lax.fori_loop(0, n_pairs // DMA_UNROLL, wait, 0)

    gate = gate_ref[...]
    cs = lambda c: slice(c * LANES, (c + 1) * LANES)
    part = jnp.zeros((TILE_T, LANES), F32)
    gate_b = [jnp.broadcast_to(gate[:, k:k + 1], (TILE_T, LANES)) for k in range(TOP_K)]
    for c in range(ROW_CH):
        fc = None
        for k in range(TOP_K):
            v = buf_ref[slot, k, :, c].reshape(TILE_T, LANES) * gate_b[k]
            fc = v if fc is None else fc + v
        pre = DEEPNORM_ALPHA * _load_row_tiles(h1_ref, 0, TILE_T, c) + fc
        pre_ref[:, cs(c)] = pre
        part = part + pre
    mu = jnp.sum(part, axis=1, keepdims=True) * (1.0 / D_MODEL)
    part = jnp.zeros((TILE_T, LANES), F32)
    for c in range(ROW_CH):
        dlt = pre_ref[:, cs(c)] - mu
        part = part + dlt * dlt
    rstd = lax.rsqrt(jnp.sum(part, axis=1, keepdims=True) * (1.0 / D_MODEL) + LN_EPS)
    is_step = i == pl.num_programs(0) - 1

    def write(out_ref):
        for c in range(ROW_CH):
            out_ref[:, cs(c)] = (pre_ref[:, cs(c)] - mu) * rstd * g2_ref[:, cs(c)] + b2_ref[:, cs(c)]

    @pl.when(jnp.logical_not(is_step))
    def _():
        write(outp_ref)

    @pl.when(is_step)
    def _():
        write(outs_ref)


def _moe_combine(dest, h1, gates, g2, b2, ybuf, n_tok):
    d = D_MODEL
    n_tiles = n_tok // TILE_T
    grid_spec = pltpu.PrefetchScalarGridSpec(
        num_scalar_prefetch=1,
        grid=(n_tiles,),
        in_specs=[pl.BlockSpec((TILE_T * ROW_CH, LANES), lambda i, s: (i, 0)),
                  pl.BlockSpec((TILE_T, LANES), lambda i, s: (i, 0)),
                  pl.BlockSpec((1, d), lambda i, s: (0, 0)),
                  pl.BlockSpec((1, d), lambda i, s: (0, 0)),
                  pl.BlockSpec(memory_space=pl.ANY)],
        out_specs=[pl.BlockSpec((TILE_T, d), lambda i, s: (jnp.minimum(i, n_tiles - 2), 0)),
                   pl.BlockSpec((TILE_T, d), lambda i, s: (0, 0))],
        scratch_shapes=[pltpu.VMEM((2, TOP_K, TILE_T // SUBLANES, ROW_CH, SUBLANES, LANES), F32),
                        pltpu.SemaphoreType.DMA((2,)),
                        pltpu.VMEM((TILE_T, d), F32)],
    )
    return pl.pallas_call(
        _moe_combine_body,
        grid_spec=grid_spec,
        out_shape=[jax.ShapeDtypeStruct((n_tok - TILE_T, d), F32), jax.ShapeDtypeStruct((TILE_T, d), F32)],
        compiler_params=_cparams(("arbitrary",), 48),
        name="moe_combine",
    )(dest.reshape(-1), h1, gates, g2, b2, ybuf)


def kernel(x_prompt, x_sample, state_s5_re, state_s5_im, state_mlstm_c, state_mlstm_n, state_mlstm_m, state_mlstm_conv, meta_tokens, ln_in_g, ln_in_b, w_in, b_in, s5_a_re, s5_a_im, s5_log_dt, s5_b_re, s5_b_im, s5_c_re, s5_c_im, s5_d, s5_w_glu, mlstm_conv_w, mlstm_conv_b, mlstm_wq, mlstm_wk, mlstm_norm_g, w_out, ln1_g, ln1_b, router_w, router_b, w_gate_up, b_gate_up, w_down, b_down, ln2_g, ln2_b):
    bsz, seq, d = x_prompt.shape
    dec_b = x_sample.shape[0]
    n_pad = TILE_T - N_META
    x_small = jnp.concatenate([jnp.zeros((n_pad, d), F32), meta_tokens, x_sample.reshape(dec_b, d)], axis=0)
    w_in_p = jnp.pad(w_in[0], ((0, 0), (0, N_IN_PAD - N_IN))).astype(BF16)
    b_in_p = jnp.pad(b_in[0], (0, N_IN_PAD - N_IN)).reshape(1, N_IN_PAD)
    g_in = ln_in_g.reshape(1, d)
    bb_in = ln_in_b.reshape(1, d)
    z_p = _inproj(x_prompt.reshape(bsz * seq, d), g_in, bb_in, w_in_p, b_in_p, 512, 1408)
    z_s = _inproj(x_small, g_in, bb_in, w_in_p, b_in_p, 256, 1408)
    tabs = _s5_prep(s5_a_re[0], s5_a_im[0], s5_log_dt[0], s5_b_re[0], s5_b_im[0])
    bd_b, bd_c = _s5_block_diag(tabs[6], tabs[7], s5_c_re[0], s5_c_im[0])
    wglu_bf = s5_w_glu[0].astype(BF16)
    d_skip = s5_d[0].reshape(1, D_S5)
    n_tiles = seq // TILE_T + 1
    zero_state = jnp.zeros((bsz, 1, N_STATE), F32)
    y_s5_p, s5r_p, s5i_p = _s5_seq(z_s, z_p, zero_state, zero_state, tabs[:6], bd_b, bd_c, d_skip, wglu_bf,
                                   bsz, n_tiles, n_pad)
    y_s5_s, s5r_s, s5i_s = _s5_step(z_s, 1, state_s5_re[0].reshape(dec_b, N_STATE),
                                    state_s5_im[0].reshape(dec_b, N_STATE), tabs[:6], bd_b, bd_c, d_skip, wglu_bf)

    conv_w = mlstm_conv_w[0]
    conv_b = mlstm_conv_b[0].reshape(1, D_ML)
    wq_bf = mlstm_wq[0].astype(BF16)
    wk_bf = mlstm_wk[0].astype(BF16)
    norm_g = mlstm_norm_g[0].reshape(1, D_ML)
    y_ml_p, c_p, n_p, m_p = _mlstm_seq(z_s, z_p, conv_w, conv_b, wq_bf, wk_bf, norm_g, bsz, n_tiles, n_pad)
    conv0 = state_mlstm_conv[0]
    y_ml_s, c_s, n_s, m_s = _mlstm_step(z_s, 1, conv0, state_mlstm_c[0], state_mlstm_n[0].reshape(dec_b, D_ML),
                                        state_mlstm_m[0], conv_w, conv_b, wq_bf, wk_bf, norm_g)

    rw = jnp.pad(router_w[0], ((0, 0), (0, LANES - N_EXPERTS)))
    rw_hi = rw.astype(BF16)
    rw_lo = (rw - rw_hi.astype(F32)).astype(BF16)
    rb = jnp.pad(router_b[0], (0, LANES - N_EXPERTS)).reshape(1, LANES)
    consts = (g_in, bb_in, w_out[0].astype(BF16), ln1_g[0].reshape(1, d), ln1_b[0].reshape(1, d), rw_hi, rw_lo, rb)
    h1, top_e, gates = _mix_out(x_prompt.reshape(bsz * seq, d), x_sample.reshape(dec_b, d),
                                y_s5_p, y_s5_s, y_ml_p, y_ml_s, consts, seq // TILE_T)
    n_tok = bsz * seq + dec_b
    dest, src, e_u, blk0, nblk = _moe_routing(top_e[:n_tok, :TOP_K], n_tok)
    ybuf = _moe_ffn(h1, src, e_u, blk0, nblk, w_gate_up[0], b_gate_up[0], w_down[0], b_down[0], src.shape[0])
    out_p, out_s = _moe_combine(dest, h1, gates, ln2_g[0].reshape(1, d), ln2_b[0].reshape(1, d), ybuf, n_tok)

    y_prompt = out_p.reshape(bsz, seq, d)
    y_sample = out_s.reshape(dec_b, 1, d)
    xm_p = z_p.reshape(bsz, seq, N_IN_PAD)[:, seq - (CONV_W - 1):, D_S5:D_S5 + D_ML]
    xm_s = z_s[TILE_T:, D_S5:D_S5 + D_ML]
    conv_s = jnp.concatenate([conv0[:, 1:], xm_s[:, None, :]], axis=1)
    return (y_prompt, y_sample,
            s5r_p.reshape(1, bsz, N_GROUPS, S5_STATE), s5i_p.reshape(1, bsz, N_GROUPS, S5_STATE),
            c_p[None], n_p.reshape(1, bsz, N_HEADS, DH), m_p[None, :, :N_HEADS, 0], xm_p[None],
            s5r_s.reshape(1, dec_b, N_GROUPS, S5_STATE), s5i_s.reshape(1, dec_b, N_GROUPS, S5_STATE),
            c_s[None], n_s.reshape(1, dec_b, N_HEADS, DH), m_s[None], conv_s[None])
```

```python
import functools
import math

import jax
import jax.numpy as jnp
from jax import lax
from jax.experimental import pallas as pl
from jax.experimental.pallas import tpu as pltpu

F32 = jnp.float32
BF16 = jnp.bfloat16

D_MODEL = 2048
N_META = 16
D_S5 = 1024
D_ML = 1024
S5_CH = 16
N_GROUPS = 64
S5_STATE = 64
N_STATE = N_GROUPS * S5_STATE
N_HEADS = 4
DH = 256
CONV_W = 4
N_EXPERTS = 32
TOP_K = 4
D_FF = 2048
SWIGLU_LIMIT = 7.0
SWIGLU_ALPHA = 1.702
LN_EPS = 1e-5
DEEPNORM_ALPHA = 2.0 ** 0.25
N_IN = D_S5 + 3 * D_ML + 2 * N_HEADS

LANES = 128
SUBLANES = 8
MXU_DIM = 256
MXU_ROWS = 16

TILE_T = 128
SEG_LEN = TILE_T // SUBLANES
N_IN_PAD = 33 * LANES
GATE_COL = 4 * D_S5 // LANES
S5_KCH = D_S5 // MXU_DIM
S5_SCH = N_STATE // S5_KCH
SCAN_LW = 256


def _cparams(sem, vmem_mb=None):
    kw = dict(dimension_semantics=sem)
    if vmem_mb is not None:
        kw["vmem_limit_bytes"] = vmem_mb * 1024 * 1024
    return pltpu.CompilerParams(**kw)


def _layer_norm(x, g, b):
    mu = jnp.mean(x, axis=-1, keepdims=True)
    xc = x - mu
    var = jnp.mean(xc * xc, axis=-1, keepdims=True)
    return xc * lax.rsqrt(var + LN_EPS) * g + b


def _inproj_body(x_ref, g_ref, b_ref, w_ref, bias_ref, z_ref, hn_ref):
    @pl.when(pl.program_id(1) == 0)
    def _():
        hn_ref[...] = _layer_norm(x_ref[...], g_ref[...], b_ref[...]).astype(BF16)

    z_ref[...] = jnp.dot(hn_ref[...], w_ref[...], preferred_element_type=F32) + bias_ref[...]


def _inproj(x, g, b, w, bias, tm, tn):
    rows, d = x.shape
    n = w.shape[1]
    return pl.pallas_call(
        _inproj_body,
        grid=(rows // tm, n // tn),
        in_specs=[
            pl.BlockSpec((tm, d), lambda i, j: (i, 0)),
            pl.BlockSpec((1, d), lambda i, j: (0, 0)),
            pl.BlockSpec((1, d), lambda i, j: (0, 0)),
            pl.BlockSpec((d, tn), lambda i, j: (0, j)),
            pl.BlockSpec((1, tn), lambda i, j: (0, j)),
        ],
        out_specs=pl.BlockSpec((tm, tn), lambda i, j: (i, j)),
        out_shape=jax.ShapeDtypeStruct((rows, n), F32),
        scratch_shapes=[pltpu.VMEM((tm, d), BF16)],
        compiler_params=_cparams(("arbitrary", "arbitrary"), 48),
        name="inproj",
    )(x, g, b, w, bias)


def _cmul(ar, ai, br, bi):
    return ar * br - ai * bi, ar * bi + ai * br


def _s5_prep_body(are_ref, aim_ref, dt_ref, bre_ref, bim_ref,
                  pre_ref, pim_ref, hre_ref, him_ref, qre_ref, qim_ref, bbre_ref, bbim_ref):
    lr = are_ref[...]
    li = aim_ref[...]
    dt = jnp.exp(dt_ref[...])
    mag = jnp.exp(lr * dt)
    ar = mag * jnp.cos(li * dt)
    ai = mag * jnp.sin(li * dt)
    nr = ar - 1.0
    ni = ai
    den = lr * lr + li * li
    cr = (nr * lr + ni * li) / den
    ci = (ni * lr - nr * li) / den
    bbre_ref[...] = cr * bre_ref[...] - ci * bim_ref[...]
    bbim_ref[...] = cr * bim_ref[...] + ci * bre_ref[...]
    pr, pi = ar, ai
    for j in range(SEG_LEN):
        pre_ref[j:j + 1, :] = pr
        pim_ref[j:j + 1, :] = pi
        if j + 1 < SEG_LEN:
            pr, pi = _cmul(pr, pi, ar, ai)
    row = lax.broadcasted_iota(jnp.int32, (SUBLANES, N_STATE), 0)
    kr, ki = pr, pi
    for idx, k in enumerate((1, 2, 4)):
        hre_ref[idx * 8:(idx + 1) * 8, :] = jnp.where(row >= k, jnp.broadcast_to(kr, (SUBLANES, N_STATE)), 0.0)
        him_ref[idx * 8:(idx + 1) * 8, :] = jnp.where(row >= k, jnp.broadcast_to(ki, (SUBLANES, N_STATE)), 0.0)
        kr, ki = _cmul(kr, ki, kr, ki)
    qr, qi = pr, pi
    for s in range(SUBLANES):
        qre_ref[s:s + 1, :] = qr
        qim_ref[s:s + 1, :] = qi
        if s + 1 < SUBLANES:
            qr, qi = _cmul(qr, qi, pr, pi)


def _s5_prep(a_re, a_im, log_dt, b_re, b_im):
    n = N_STATE
    are = a_re.reshape(1, n)
    aim = a_im.reshape(1, n)
    dtl = jnp.broadcast_to(log_dt[:, None], (N_GROUPS, S5_STATE)).reshape(1, n)
    bre = b_re.transpose(2, 0, 1).reshape(S5_CH, n)
    bim = b_im.transpose(2, 0, 1).reshape(S5_CH, n)
    shp = lambda r: jax.ShapeDtypeStruct((r, n), F32)
    return pl.pallas_call(
        _s5_prep_body,
        out_shape=(shp(SEG_LEN), shp(SEG_LEN), shp(24), shp(24), shp(8), shp(8), shp(S5_CH), shp(S5_CH)),
        name="s5_prep",
    )(are, aim, dtl, bre, bim)


def _s5_block_diag(bb_re, bb_im, c_re, c_im):
    gpc = 16
    eye = jnp.eye(gpc, dtype=bool)

    def bd_in(bb):
        t = bb.reshape(S5_CH, S5_KCH, gpc, S5_STATE).transpose(1, 2, 0, 3)
        t = jnp.where(eye[None, :, None, :, None], t[:, :, :, None, :], 0.0)
        return t.reshape(S5_KCH, gpc * S5_CH, gpc * S5_STATE)

    def bd_out(c):
        t = c.reshape(S5_KCH, gpc, S5_CH, S5_STATE).transpose(0, 1, 3, 2)
        t = jnp.where(eye[None, :, None, :, None], t[:, :, :, None, :], 0.0)
        return t.reshape(S5_KCH, gpc * S5_STATE, gpc * S5_CH)

    bd_b = jnp.concatenate([bd_in(bb_re), bd_in(bb_im)], axis=2).astype(BF16)
    bd_c = jnp.concatenate([bd_out(c_re), bd_out(-c_im)], axis=1).astype(BF16)
    return bd_b, bd_c


def _gelu_glu(y, wglu_ref):
    y = 0.5 * y * (1.0 + lax.erf(y * math.sqrt(0.5)))
    gate = jnp.dot(y.astype(BF16), wglu_ref[...], preferred_element_type=F32)
    return y * jax.nn.sigmoid(gate)


def _s5_in_proj(u_bf, bdb_ref, bur_ref, bui_ref):
    for c in range(S5_KCH):
        r = jnp.dot(u_bf[:, c * MXU_DIM:(c + 1) * MXU_DIM], bdb_ref[c], preferred_element_type=F32)
        bur_ref[:, c * S5_SCH:(c + 1) * S5_SCH] = r[:, :S5_SCH]
        bui_ref[:, c * S5_SCH:(c + 1) * S5_SCH] = r[:, S5_SCH:]


def _s5_out_proj(xr_ref, xi_ref, bdc_ref):
    ys = []
    for c in range(S5_KCH):
        xr = xr_ref[:, c * S5_SCH:(c + 1) * S5_SCH].astype(BF16)
        xi = xi_ref[:, c * S5_SCH:(c + 1) * S5_SCH].astype(BF16)
        ys.append(jnp.dot(xr, bdc_ref[c, :S5_SCH, :], preferred_element_type=F32)
                  + jnp.dot(xi, bdc_ref[c, S5_SCH:, :], preferred_element_type=F32))
    return jnp.concatenate(ys, axis=1)


def _s5_seq_body(n_pad, us_ref, up_ref, x0r_ref, x0i_ref, bdb_ref, bdc_ref, pre_ref, pim_ref,
                 hre_ref, him_ref, qre_ref, qim_ref, d_ref, wglu_ref,
                 y_ref, xr_out_ref, xi_out_ref, bur_ref, bui_ref, cr_ref, ci_ref, perm_ref):
    t = pl.program_id(1)
    n = N_STATE

    @pl.when(t == 0)
    def _():
        cr_ref[...] = jnp.broadcast_to(x0r_ref[0], (SUBLANES, n))
        ci_ref[...] = jnp.broadcast_to(x0i_ref[0], (SUBLANES, n))

    n_lc = D_S5 // LANES

    def load_perm(ref):
        for c in range(n_lc):
            perm_ref[c] = ref[:, c * LANES:(c + 1) * LANES]
        return jnp.concatenate(
            [jnp.concatenate([perm_ref[c, pl.ds(j, SUBLANES, stride=SEG_LEN), :] for c in range(n_lc)], axis=1)
             for j in range(SEG_LEN)], axis=0)

    prow = lax.broadcasted_iota(jnp.int32, (TILE_T, 1), 0)
    time = (prow % SUBLANES) * SEG_LEN + prow // SUBLANES
    u_first = jnp.where(time >= n_pad, load_perm(us_ref), 0.0)
    u = jnp.where(t == 0, u_first, load_perm(up_ref))
    _s5_in_proj(u.astype(BF16), bdb_ref, bur_ref, bui_ref)

    row8 = lax.broadcasted_iota(jnp.int32, (SUBLANES, SCAN_LW), 0)

    def scan_lanes(lc, carry):
        ls = pl.ds(pl.multiple_of(lc * SCAN_LW, SCAN_LW), SCAN_LW)
        bc = lambda ref, j: jnp.broadcast_to(ref[j:j + 1, ls], (SUBLANES, SCAN_LW))
        ar, ai = bc(pre_ref, 0), bc(pim_ref, 0)
        xr = jnp.zeros((SUBLANES, SCAN_LW), F32)
        xi = jnp.zeros((SUBLANES, SCAN_LW), F32)
        for j in range(SEG_LEN):
            rs = slice(j * SUBLANES, (j + 1) * SUBLANES)
            nr = ar * xr - ai * xi + bur_ref[rs, ls]
            ni = ar * xi + ai * xr + bui_ref[rs, ls]
            xr, xi = nr, ni
            bur_ref[rs, ls] = xr
            bui_ref[rs, ls] = xi
        er, ei = xr, xi
        for idx, k in enumerate((1, 2, 4)):
            sr = pltpu.roll(er, k, axis=0)
            si = pltpu.roll(ei, k, axis=0)
            hr = hre_ref[idx * 8:(idx + 1) * 8, ls]
            hi = him_ref[idx * 8:(idx + 1) * 8, ls]
            er, ei = er + (hr * sr - hi * si), ei + (hr * si + hi * sr)
        cpr = cr_ref[:, ls]
        cpi = ci_ref[:, ls]
        qr = qre_ref[:, ls]
        qi = qim_ref[:, ls]
        er, ei = er + (qr * cpr - qi * cpi), ei + (qr * cpi + qi * cpr)
        inr = jnp.where(row8 == 0, cpr, pltpu.roll(er, 1, axis=0))
        ini = jnp.where(row8 == 0, cpi, pltpu.roll(ei, 1, axis=0))
        cr_ref[:, ls] = jnp.broadcast_to(er[SUBLANES - 1:SUBLANES, :], (SUBLANES, SCAN_LW))
        ci_ref[:, ls] = jnp.broadcast_to(ei[SUBLANES - 1:SUBLANES, :], (SUBLANES, SCAN_LW))
        for j in range(SEG_LEN):
            rs = slice(j * SUBLANES, (j + 1) * SUBLANES)
            pr, pi = bc(pre_ref, j), bc(pim_ref, j)
            bur_ref[rs, ls] = bur_ref[rs, ls] + (pr * inr - pi * ini)
            bui_ref[rs, ls] = bui_ref[rs, ls] + (pr * ini + pi * inr)
        return carry

    lax.fori_loop(0, n // SCAN_LW, scan_lanes, 0)

    y = _s5_out_proj(bur_ref, bui_ref, bdc_ref) + d_ref[...] * u
    y = _gelu_glu(y, wglu_ref)
    for j in range(SEG_LEN):
        for c in range(n_lc):
            perm_ref[c, pl.ds(j, SUBLANES, stride=SEG_LEN), :] = y[j * SUBLANES:(j + 1) * SUBLANES,
                                                                   c * LANES:(c + 1) * LANES]
    for c in range(n_lc):
        y_ref[:, c * LANES:(c + 1) * LANES] = perm_ref[c]

    @pl.when(t == pl.num_programs(1) - 1)
    def _():
        xr_out_ref[0] = cr_ref[0:1, :]
        xi_out_ref[0] = ci_ref[0:1, :]


def _s5_seq(z_small, z_seq, x0r, x0i, tabs, bd_b, bd_c, d_skip, wglu_bf, n_batch, n_tiles, n_pad):
    pre, pim, hre, him, qre, qim = tabs
    n = N_STATE
    full = lambda a: pl.BlockSpec(a.shape, lambda b, t: (0,) * a.ndim)
    per_seq = n_tiles - 1
    return pl.pallas_call(
        functools.partial(_s5_seq_body, n_pad),
        grid=(n_batch, n_tiles),
        in_specs=[
            pl.BlockSpec((TILE_T, D_S5), lambda b, t: (0, 0)),
            pl.BlockSpec((TILE_T, D_S5), lambda b, t: (b * per_seq + jnp.maximum(t - 1, 0), 0)),
            pl.BlockSpec((1, 1, n), lambda b, t: (b, 0, 0)),
            pl.BlockSpec((1, 1, n), lambda b, t: (b, 0, 0)),
            full(bd_b), full(bd_c), full(pre), full(pim), full(hre), full(him), full(qre), full(qim),
            full(d_skip), full(wglu_bf),
        ],
        out_specs=[
            pl.BlockSpec((TILE_T, D_S5), lambda b, t: (b * n_tiles + t, 0)),
            pl.BlockSpec((1, 1, n), lambda b, t: (b, 0, 0)),
            pl.BlockSpec((1, 1, n), lambda b, t: (b, 0, 0)),
        ],
        out_shape=[
            jax.ShapeDtypeStruct((n_batch * n_tiles * TILE_T, D_S5), F32),
            jax.ShapeDtypeStruct((n_batch, 1, n), F32),
            jax.ShapeDtypeStruct((n_batch, 1, n), F32),
        ],
        scratch_shapes=[pltpu.VMEM((TILE_T, n), F32), pltpu.VMEM((TILE_T, n), F32),
                        pltpu.VMEM((SUBLANES, n), F32), pltpu.VMEM((SUBLANES, n), F32),
                        pltpu.VMEM((D_S5 // LANES, TILE_T, LANES), F32)],
        compiler_params=_cparams(("arbitrary", "arbitrary"), 48),
        name="s5_seq",
    )(z_small, z_seq, x0r, x0i, bd_b, bd_c, pre, pim, hre, him, qre, qim, d_skip, wglu_bf)


def _s5_step_body(u_ref, x0r_ref, x0i_ref, bdb_ref, bdc_ref, pre_ref, pim_ref, d_ref, wglu_ref,
                  y_ref, xr_ref, xi_ref):
    u = u_ref[...]
    _s5_in_proj(u.astype(BF16), bdb_ref, xr_ref, xi_ref)
    ar = pre_ref[0:1, :]
    ai = pim_ref[0:1, :]
    x0r = x0r_ref[...]
    x0i = x0i_ref[...]
    xr_ref[...] = xr_ref[...] + (ar * x0r - ai * x0i)
    xi_ref[...] = xi_ref[...] + (ar * x0i + ai * x0r)
    y = _s5_out_proj(xr_ref, xi_ref, bdc_ref) + d_ref[...] * u
    y_ref[...] = _gelu_glu(y, wglu_ref)


def _s5_step(z_small, row_blk, x0r, x0i, tabs, bd_b, bd_c, d_skip, wglu_bf):
    pre, pim = tabs[0], tabs[1]
    rows = x0r.shape[0]
    n = N_STATE
    full = lambda a: pl.BlockSpec(a.shape, lambda i: (0,) * a.ndim)
    return pl.pallas_call(
        _s5_step_body,
        grid=(1,),
        in_specs=[pl.BlockSpec((rows, D_S5), lambda i: (row_blk, 0)), full(x0r), full(x0i),
                  full(bd_b), full(bd_c), full(pre), full(pim), full(d_skip), full(wglu_bf)],
        out_specs=[pl.BlockSpec((rows, D_S5), lambda i: (0, 0)),
                   pl.BlockSpec((rows, n), lambda i: (0, 0)), pl.BlockSpec((rows, n), lambda i: (0, 0))],
        out_shape=[jax.ShapeDtypeStruct((rows, D_S5), F32), jax.ShapeDtypeStruct((rows, n), F32),
                   jax.ShapeDtypeStruct((rows, n), F32)],
        compiler_params=_cparams(("arbitrary",), 48),
        name="s5_step",
    )(z_small, x0r, x0i, bd_b, bd_c, pre, pim, d_skip, wglu_bf)


NEG_INF = float("-inf")
N_STEP_SCALARS = 5 * N_HEADS


def _log_sigmoid(x):
    return jnp.minimum(x, 0.0) - jnp.log1p(jnp.exp(-jnp.abs(x)))


def _split3(x):
    hi = x.astype(BF16)
    r1 = x - hi.astype(F32)
    mid = r1.astype(BF16)
    lo = (r1 - mid.astype(F32)).astype(BF16)
    return hi, mid, lo


def _head_norm_gate(h, o, g):
    mu = jnp.mean(h, axis=-1, keepdims=True)
    hc = h - mu
    var = jnp.mean(hc * hc, axis=-1, keepdims=True)
    return jax.nn.sigmoid(o) * (hc * lax.rsqrt(var + LN_EPS) * g)


def _dot_nt(a, b):
    return lax.dot_general(a, b, (((1,), (1,)), ((), ())), preferred_element_type=F32)


def _dot_tn(a, b):
    return lax.dot_general(a, b, (((0,), (0,)), ((), ())), preferred_element_type=F32)


def _mlstm_seq_body(n_pad, sxm_ref, sv_ref, so_ref, sg_ref, pxm_ref, pv_ref, po_ref, pg_ref,
                    cw_ref, cb_ref, wq_ref, wk_ref, ng_ref,
                    y_ref, c_out_ref, n_out_ref, m_out_ref, c_ref, n_ref, m_ref, prev_ref):
    t = pl.program_id(1)
    L = TILE_T

    @pl.when(t == 0)
    def _():
        c_ref[...] = jnp.zeros_like(c_ref)
        n_ref[...] = jnp.zeros_like(n_ref)
        m_ref[...] = jnp.zeros_like(m_ref)
        prev_ref[...] = jnp.zeros_like(prev_ref)

    first = t == 0
    row = lax.broadcasted_iota(jnp.int32, (L, 1), 0)
    valid = jnp.logical_or(jnp.logical_not(first), row >= n_pad)
    xm = jnp.where(valid, jnp.where(first, sxm_ref[...], pxm_ref[...]), 0.0)
    v = jnp.where(first, sv_ref[...], pv_ref[...])
    o = jnp.where(first, so_ref[...], po_ref[...])
    gt = jnp.where(first, sg_ref[...], pg_ref[...])

    prev = prev_ref[...]

    def shifted(j):
        if j == 0:
            return xm
        return pltpu.roll(jnp.where(row >= L - j, prev, xm), j, axis=0)

    xc = cb_ref[...]
    for j in range(CONV_W):
        xc = xc + shifted(CONV_W - 1 - j) * cw_ref[j:j + 1, :]
    prev_ref[...] = xm
    xc = xc * jax.nn.sigmoid(xc)

    ig = jnp.where(valid, gt, NEG_INF)
    lf = jnp.where(valid, _log_sigmoid(gt), 0.0)
    ti = lax.broadcasted_iota(jnp.int32, (L, L), 0)
    si = lax.broadcasted_iota(jnp.int32, (L, L), 1)
    causal = si <= ti
    tri = jnp.where(causal, 1.0, 0.0).astype(BF16)
    bc = sum(jnp.dot(tri, p, preferred_element_type=F32) for p in _split3(lf))
    ig_t = ig.T
    bc_t = bc.T

    for h in range(N_HEADS):
        hs = slice(h * DH, (h + 1) * DH)
        b_col = bc[:, N_HEADS + h:N_HEADS + h + 1]
        b_row = bc_t[N_HEADS + h:N_HEADS + h + 1, :]
        ig_row = ig_t[h:h + 1, :]
        ig_col = ig[:, h:h + 1]
        m_prev = m_ref[h:h + 1, 0:1]
        dlog = jnp.where(causal, b_col - b_row + ig_row, NEG_INF)
        inter = b_col + m_prev
        m_t = jnp.maximum(jnp.max(dlog, axis=1, keepdims=True), inter)
        w = jnp.exp(dlog - m_t)
        g = jnp.exp(inter - m_t)
        xh = xc[:, hs].astype(BF16)
        q = jnp.dot(xh, wq_ref[h], preferred_element_type=F32)
        k = jnp.dot(xh, wk_ref[h], preferred_element_type=F32) * (DH ** -0.5)
        qb = q.astype(BF16)
        kb = k.astype(BF16)
        s = _dot_nt(qb, kb) * w
        vh = v[:, hs]
        cmat = c_ref[h]
        n_row = n_ref[h]
        num = jnp.dot(s.astype(BF16), vh.astype(BF16), preferred_element_type=F32) \
            + g * _dot_nt(qb, cmat.astype(BF16))
        den = jnp.sum(s, axis=1, keepdims=True) + g * jnp.sum(q * n_row, axis=1, keepdims=True)
        hh = num / jnp.maximum(jnp.abs(den), jnp.exp(-m_t))
        b_last = b_col[L - 1:L, :]
        wlog = b_last - b_col + ig_col
        m_new = jnp.maximum(b_last + m_prev, jnp.max(wlog, axis=0, keepdims=True))
        w_end = jnp.exp(wlog - m_new)
        g_end = jnp.exp(b_last + m_prev - m_new)
        c_ref[h] = g_end * cmat + _dot_tn((vh * w_end).astype(BF16), kb)
        n_ref[h] = g_end * n_row + jnp.sum(w_end * k, axis=0, keepdims=True)
        m_ref[h:h + 1, :] = jnp.broadcast_to(m_new, (1, LANES))
        y_ref[:, hs] = _head_norm_gate(hh, o[:, hs], ng_ref[:, hs])

    @pl.when(t == pl.num_programs(1) - 1)
    def _():
        c_out_ref[0] = c_ref[...]
        n_out_ref[0] = n_ref[...]
        m_out_ref[0] = m_ref[...]


def _mlstm_seq(z_small, z_seq, conv_w, conv_b, wq_bf, wk_bf, norm_g, n_batch, n_tiles, n_pad):
    per_seq = n_tiles - 1
    full = lambda a: pl.BlockSpec(a.shape, lambda b, t: (0,) * a.ndim)
    nb = D_ML // LANES

    def small(col, width):
        return pl.BlockSpec((TILE_T, width), lambda b, t: (0, col))

    def seq(col, width):
        return pl.BlockSpec((TILE_T, width), lambda b, t: (b * per_seq + jnp.maximum(t - 1, 0), col))

    return pl.pallas_call(
        functools.partial(_mlstm_seq_body, n_pad),
        grid=(n_batch, n_tiles),
        in_specs=[small(1, D_ML), small(2, D_ML), small(3, D_ML), small(GATE_COL, LANES),
                  seq(1, D_ML), seq(2, D_ML), seq(3, D_ML), seq(GATE_COL, LANES),
                  full(conv_w), full(conv_b), full(wq_bf), full(wk_bf), full(norm_g)],
        out_specs=[
            pl.BlockSpec((TILE_T, D_ML), lambda b, t: (b * n_tiles + t, 0)),
            pl.BlockSpec((1, N_HEADS, DH, DH), lambda b, t: (b, 0, 0, 0)),
            pl.BlockSpec((1, N_HEADS, 1, DH), lambda b, t: (b, 0, 0, 0)),
            pl.BlockSpec((1, SUBLANES, LANES), lambda b, t: (b, 0, 0)),
        ],
        out_shape=[
            jax.ShapeDtypeStruct((n_batch * n_tiles * TILE_T, D_ML), F32),
            jax.ShapeDtypeStruct((n_batch, N_HEADS, DH, DH), F32),
            jax.ShapeDtypeStruct((n_batch, N_HEADS, 1, DH), F32),
            jax.ShapeDtypeStruct((n_batch, SUBLANES, LANES), F32),
        ],
        scratch_shapes=[pltpu.VMEM((N_HEADS, DH, DH), F32), pltpu.VMEM((N_HEADS, 1, DH), F32),
                        pltpu.VMEM((SUBLANES, LANES), F32), pltpu.VMEM((TILE_T, D_ML), F32)],
        compiler_params=_cparams(("arbitrary", "arbitrary"), 48),
        name="mlstm_seq",
    )(z_small, z_small, z_small, z_small, z_seq, z_seq, z_seq, z_seq, conv_w, conv_b, wq_bf, wk_bf, norm_g)


def _mlstm_step_a_body(xm_ref, g_ref, conv0_ref, m0_ref, cw_ref, cb_ref, wq_ref, wk_ref,
                       q_ref, k_ref, sc_ref):
    xc = cb_ref[...]
    for j in range(CONV_W - 1):
        xc = xc + conv0_ref[j] * cw_ref[j:j + 1, :]
    xc = xc + xm_ref[...] * cw_ref[CONV_W - 1:CONV_W, :]
    xc = xc * jax.nn.sigmoid(xc)
    gt = g_ref[...]
    ig = gt[:, 0:N_HEADS]
    lf = _log_sigmoid(gt[:, N_HEADS:2 * N_HEADS])
    inter = lf + m0_ref[...]
    m_t = jnp.maximum(ig, inter)
    w = jnp.exp(ig - m_t)
    g = jnp.exp(inter - m_t)
    qks = []
    for h in range(N_HEADS):
        hs = slice(h * DH, (h + 1) * DH)
        xh = xc[:, hs].astype(BF16)
        q = jnp.dot(xh, wq_ref[h], preferred_element_type=F32)
        k = jnp.dot(xh, wk_ref[h], preferred_element_type=F32) * (DH ** -0.5)
        q_ref[:, hs] = q
        k_ref[:, hs] = k
        qks.append(jnp.sum(q * k, axis=1, keepdims=True))
    s = jnp.concatenate(qks, axis=1) * w
    rows = s.shape[0]
    sc_ref[...] = jnp.concatenate(
        [s, w, g, m_t, jnp.exp(-m_t), jnp.zeros((rows, LANES - 5 * N_HEADS), F32)], axis=1)


def _mlstm_step_b_body(bb, sc_ref, q_ref, k_ref, n_ref, v_ref, o_ref, ng_ref, c_ref,
                       y_ref, c_out_ref, n_out_ref):
    i0 = pl.program_id(0) * bb
    row = lax.broadcasted_iota(jnp.int32, (MXU_ROWS, DH), 0)

    def hi_lo(x):
        hi = x.astype(BF16).astype(F32)
        return jnp.broadcast_to(hi, (MXU_ROWS, DH)), jnp.broadcast_to(x - hi, (MXU_ROWS, DH))

    for i in range(bb):
        for h in range(N_HEADS):
            hs = slice(h * DH, (h + 1) * DH)
            base = (i0 + i) * N_STEP_SCALARS
            s = sc_ref[base + h]
            w = sc_ref[base + N_HEADS + h]
            g = sc_ref[base + 2 * N_HEADS + h]
            em = sc_ref[base + 4 * N_HEADS + h]
            rsel = pl.ds(i0 + i, 1)
            q_row = q_ref[rsel, hs]
            k_row = k_ref[rsel, hs]
            n_row = n_ref[rsel, hs]
            v_row = v_ref[rsel, hs]
            cmat = c_ref[i, h]
            qh, ql = hi_lo(q_row)
            qmat = jnp.where(row == 0, qh, jnp.where(row == 1, ql, 0.0)).astype(BF16)
            cq = _dot_nt(qmat, cmat.astype(BF16))
            num = s * v_row + g * (cq[0:1, :] + cq[1:2, :])
            den = s + g * jnp.sum(n_row * q_row, axis=1, keepdims=True)
            hh = num / jnp.maximum(jnp.abs(den), em)
            vh, vl = hi_lo(w * v_row)
            kh, kl = hi_lo(k_row)
            a = jnp.where(row < 2, vh, jnp.where(row < 4, vl, 0.0)).astype(BF16)
            b = jnp.where(row < 4, jnp.where(row % 2 == 0, kh, kl), 0.0).astype(BF16)
            c_out_ref[i, h] = g * cmat + _dot_tn(a, b)
            n_out_ref[rsel, hs] = g * n_row + w * k_row
            y_ref[rsel, hs] = _head_norm_gate(hh, o_ref[rsel, hs], ng_ref[:, hs])


def _mlstm_step(z_small, row_blk, conv0, c0, n0, m0, conv_w, conv_b, wq_bf, wk_bf, norm_g, bb=4):
    rows = c0.shape[0]
    full = lambda a: pl.BlockSpec(a.shape, lambda i: (0,) * a.ndim)
    conv0_t = conv0.transpose(1, 0, 2)
    q, k, sc = pl.pallas_call(
        _mlstm_step_a_body,
        grid=(1,),
        in_specs=[pl.BlockSpec((rows, D_ML), lambda i: (row_blk, 1)),
                  pl.BlockSpec((rows, LANES), lambda i: (row_blk, GATE_COL)),
                  full(conv0_t), full(m0), full(conv_w), full(conv_b), full(wq_bf), full(wk_bf)],
        out_specs=[pl.BlockSpec((rows, D_ML), lambda i: (0, 0)), pl.BlockSpec((rows, D_ML), lambda i: (0, 0)),
                   pl.BlockSpec((rows, LANES), lambda i: (0, 0))],
        out_shape=[jax.ShapeDtypeStruct((rows, D_ML), F32), jax.ShapeDtypeStruct((rows, D_ML), F32),
                   jax.ShapeDtypeStruct((rows, LANES), F32)],
        compiler_params=_cparams(("arbitrary",), 48),
        name="mlstm_step_a",
    )(z_small, z_small, conv0_t, m0, conv_w, conv_b, wq_bf, wk_bf)
    row_spec = pl.BlockSpec((rows, D_ML), lambda i: (0, 0))
    c_spec = pl.BlockSpec((bb, N_HEADS, DH, DH), lambda i: (i, 0, 0, 0))
    y, c_new, n_new = pl.pallas_call(
        functools.partial(_mlstm_step_b_body, bb),
        grid=(rows // bb,),
        in_specs=[pl.BlockSpec(memory_space=pltpu.SMEM), row_spec, row_spec, row_spec,
                  pl.BlockSpec((rows, D_ML), lambda i: (row_blk, 2)),
                  pl.BlockSpec((rows, D_ML), lambda i: (row_blk, 3)),
                  pl.BlockSpec((1, D_ML), lambda i: (0, 0)), c_spec],
        out_specs=[row_spec, c_spec, row_spec],
        out_shape=[jax.ShapeDtypeStruct((rows, D_ML), F32), jax.ShapeDtypeStruct(c0.shape, F32),
                   jax.ShapeDtypeStruct((rows, D_ML), F32)],
        compiler_params=_cparams(("arbitrary",), 56),
        name="mlstm_step_b",
    )(sc[:, :N_STEP_SCALARS].reshape(rows * N_STEP_SCALARS), q, k, n0, z_small, z_small, norm_g, c0)
    return y, c_new, n_new, sc[:, 3 * N_HEADS:4 * N_HEADS]


ROW_CH = D_MODEL // LANES


def _load_row_tiles(ref, row0, m, c):
    return ref[pl.ds(row0 * ROW_CH + c, m, stride=ROW_CH), :]


def _store_row_tiles(ref, row0, val):
    m = val.shape[0]
    for c in range(ROW_CH):
        ref[pl.ds(row0 * ROW_CH + c, m, stride=ROW_CH), :] = val[:, c * LANES:(c + 1) * LANES]


def _mix_out_body(xp_ref, xs_ref, ysp0_ref, ysp1_ref, yss_ref, ymp0_ref, ymp1_ref, yms_ref,
                  gin_ref, bin_ref, wout_ref, g1_ref, b1_ref, rwh_ref, rwl_ref, rb_ref,
                  h1_ref, e_ref, gate_ref):
    is_step = pl.program_id(0) == pl.num_programs(0) - 1
    consts = (gin_ref, bin_ref, wout_ref, g1_ref, b1_ref, rwh_ref, rwl_ref, rb_ref)
    outs = (h1_ref, e_ref, gate_ref)
    two = lambda a, b: jnp.concatenate([a[...], b[...]], axis=0)

    @pl.when(jnp.logical_not(is_step))
    def _():
        _mix_out_tile(xp_ref[...], two(ysp0_ref, ysp1_ref), two(ymp0_ref, ymp1_ref), consts, outs)

    @pl.when(is_step)
    def _():
        _mix_out_tile(two(xs_ref, xs_ref), two(yss_ref, yss_ref), two(yms_ref, yms_ref), consts, outs)


def _mix_out_tile(x, ys, ym, consts, outs):
    gin_ref, bin_ref, wout_ref, g1_ref, b1_ref, rwh_ref, rwl_ref, rb_ref = consts
    h1_ref, e_ref, gate_ref = outs
    hp = _layer_norm(x, gin_ref[...], bin_ref[...])
    ycat = jnp.concatenate([ys, ym], axis=1).astype(BF16)
    mix = jnp.dot(ycat, wout_ref[...], preferred_element_type=F32)
    h1 = _layer_norm(DEEPNORM_ALPHA * hp + mix, g1_ref[...], b1_ref[...])
    _store_row_tiles(h1_ref, 0, h1)
    xh = h1.astype(BF16)
    xl = (h1 - xh.astype(F32)).astype(BF16)
    logits = (jnp.dot(xh, rwh_ref[...], preferred_element_type=F32)
              + jnp.dot(xh, rwl_ref[...], preferred_element_type=F32)
              + jnp.dot(xl, rwh_ref[...], preferred_element_type=F32)) + rb_ref[...]
    rows = logits.shape[0]
    lane = lax.broadcasted_iota(jnp.int32, (rows, LANES), 1)
    logits = jnp.where(lane < N_EXPERTS, logits, NEG_INF)
    e_acc = jnp.zeros((rows, LANES), jnp.int32)
    v_acc = jnp.full((rows, LANES), NEG_INF, F32)
    for k in range(TOP_K):
        mx = jnp.max(logits, axis=1, keepdims=True)
        idx = jnp.min(jnp.where(logits == mx, lane, LANES), axis=1, keepdims=True)
        e_acc = jnp.where(lane == k, idx, e_acc)
        v_acc = jnp.where(lane == k, mx, v_acc)
        logits = jnp.where(lane == idx, NEG_INF, logits)
    p = jnp.exp(v_acc - jnp.max(v_acc, axis=1, keepdims=True))
    e_ref[...] = e_acc
    gate_ref[...] = p / jnp.sum(p, axis=1, keepdims=True)


def _mix_out(x_p, x_s, ys_p, ys_s, ym_p, ym_s, consts, tiles_per_seq):
    d = D_MODEL
    full = lambda a: pl.BlockSpec(a.shape, lambda i: (0,) * a.ndim)
    tm = 2 * TILE_T
    n_p = x_p.shape[0] // tm
    n_tiles = n_p + 1
    rows = n_tiles * tm

    def p_idx(i):
        return jnp.minimum(i, n_p - 1)

    def frame_idx(i, half):
        j = 2 * p_idx(i) + half
        return (j // tiles_per_seq) * (tiles_per_seq + 1) + j % tiles_per_seq + 1

    frame = lambda width, half: pl.BlockSpec((TILE_T, width), lambda i: (frame_idx(i, half), 0))
    first = lambda width: pl.BlockSpec((TILE_T, width), lambda i: (0, 0))
    return pl.pallas_call(
        _mix_out_body,
        grid=(n_tiles,),
        in_specs=[pl.BlockSpec((tm, d), lambda i: (p_idx(i), 0)), first(d),
                  frame(D_S5, 0), frame(D_S5, 1), first(D_S5),
                  frame(D_ML, 0), frame(D_ML, 1), first(D_ML)] + [full(a) for a in consts],
        out_specs=[pl.BlockSpec((tm * ROW_CH, LANES), lambda i: (i, 0)),
                   pl.BlockSpec((tm, LANES), lambda i: (i, 0)),
                   pl.BlockSpec((tm, LANES), lambda i: (i, 0))],
        out_shape=[jax.ShapeDtypeStruct((rows * ROW_CH, LANES), F32), jax.ShapeDtypeStruct((rows, LANES), jnp.int32),
                   jax.ShapeDtypeStruct((rows, LANES), F32)],
        compiler_params=_cparams(("arbitrary",), 56),
        name="mix_out",
    )(x_p, x_s, ys_p, ys_p, ys_s, ym_p, ym_p, ym_s, *consts)


MOE_BLK = 128
MOE_TM = 1536
MOE_TF = 256
MOE_SUB_MAX = 12
DMA_UNROLL = 8


def _invert_rows_body(dest_ref, src_ref):
    def zero(i, c):
        for q in range(DMA_UNROLL):
            src_ref[i * DMA_UNROLL + q] = 0
        return c
    lax.fori_loop(0, src_ref.shape[0] // DMA_UNROLL, zero, 0)

    def put(i, c):
        for q in range(DMA_UNROLL):
            src_ref[dest_ref[i * DMA_UNROLL + q]] = i * (DMA_UNROLL // TOP_K) + q // TOP_K
        return c
    lax.fori_loop(0, dest_ref.shape[0] // DMA_UNROLL, put, 0)


def _invert_rows(dest_flat, rows_pad):
    return pl.pallas_call(
        _invert_rows_body,
        in_specs=[pl.BlockSpec(memory_space=pltpu.SMEM)],
        out_specs=pl.BlockSpec(memory_space=pltpu.SMEM),
        out_shape=jax.ShapeDtypeStruct((rows_pad,), jnp.int32),
        name="invert_rows",
    )(dest_flat)


def _moe_routing(top_e, n_tok):
    n_pairs = n_tok * TOP_K
    rows_pad = n_pairs + N_EXPERTS * MOE_BLK
    n_pass = N_EXPERTS + rows_pad // MOE_TM
    onehot = (top_e[:, :, None] == jnp.arange(N_EXPERTS, dtype=jnp.int32)).astype(jnp.int32).sum(1)
    incl = jnp.cumsum(onehot, axis=0)
    counts = incl[-1]
    rank = jnp.take_along_axis(incl - onehot, top_e, axis=1)
    padded = (counts + MOE_BLK - 1) // MOE_BLK * MOE_BLK
    pad_end = jnp.cumsum(padded)
    pad_start = pad_end - padded
    dest = pad_start[top_e] + rank
    src = _invert_rows(dest.reshape(-1).astype(jnp.int32), rows_pad)
    passes_e = (padded + MOE_TM - 1) // MOE_TM
    pass_end = jnp.cumsum(passes_e)
    u = jnp.arange(n_pass, dtype=jnp.int32)
    e_u = jnp.minimum(jnp.searchsorted(pass_end, u, side="right"), N_EXPERTS - 1).astype(jnp.int32)
    j_u = u - (pass_end - passes_e)[e_u]
    rem = padded[e_u] - j_u * MOE_TM
    nblk = jnp.where(u < pass_end[-1], jnp.clip(rem, 0, MOE_TM) // MOE_BLK, 0).astype(jnp.int32)
    blk0 = ((pad_start[e_u] + j_u * MOE_TM) // MOE_BLK).astype(jnp.int32)
    blk0 = jnp.where(nblk > 0, blk0, 0)
    last_e = e_u[jnp.maximum(pass_end[-1] - 1, 0)]
    e_u = jnp.where(nblk > 0, e_u, last_e)
    nblk = jnp.concatenate([nblk, (pad_end[-1:] // MOE_BLK).astype(jnp.int32)])
    return dest.astype(jnp.int32), src, e_u, blk0, nblk


def _moe_ffn_body(e_ref, blk0_ref, nblk_ref, src_ref, h1_ref, wg_ref, wu_ref, bg_ref, bu_ref, wd_ref, bd_ref,
                  ybuf_ref, xg_ref, xb_ref, acc_ref, w1_ref, w2_ref, stage_ref, gsem, osem):
    u = pl.program_id(0)
    f = pl.program_id(1)
    n_pass = pl.num_programs(0)
    nblk = nblk_ref[u]
    row0 = blk0_ref[u] * MOE_BLK
    blk_rt = MOE_BLK * ROW_CH

    def for_all_rows(p, fn):
        base = blk0_ref[p] * MOE_BLK

        def body(rb, c):
            for s in range(SUBLANES):
                tok = src_ref[base + rb * SUBLANES + s]
                fn(pltpu.make_async_copy(h1_ref.at[pl.ds(tok * ROW_CH, ROW_CH)],
                                         xg_ref.at[rb, pl.ds(0, ROW_CH), s], gsem))
            return c
        lax.fori_loop(0, nblk_ref[p] * (MOE_BLK // SUBLANES), body, 0)

    def issue_gather(p):
        for_all_rows(p, lambda cp: cp.start())

    def out_copy(b):
        dst = pl.ds(pl.multiple_of((row0 + b * MOE_BLK) * ROW_CH, blk_rt), blk_rt)
        return pltpu.make_async_copy(stage_ref.at[b % 2], ybuf_ref.at[dst], osem.at[b % 2])

    @pl.when(jnp.logical_and(nblk > 0, f == 0))
    def _():
        @pl.when(u == 0)
        def _():
            issue_gather(u)

        for_all_rows(u, lambda cp: cp.wait())

        def cast(b, c):
            rs = pl.ds(pl.multiple_of(b * MOE_BLK, MOE_BLK), MOE_BLK)
            gs = pl.ds(pl.multiple_of(b * (MOE_BLK // SUBLANES), MOE_BLK // SUBLANES), MOE_BLK // SUBLANES)
            for ch in range(ROW_CH):
                xb_ref[rs, ch * LANES:(ch + 1) * LANES] = xg_ref[gs, ch].reshape(MOE_BLK, LANES).astype(BF16)
            acc_ref[rs, :] = jnp.zeros((MOE_BLK, acc_ref.shape[1]), F32)
            return c
        lax.fori_loop(0, nblk, cast, 0)

    nxt = jnp.minimum(u + 1, n_pass - 1)

    @pl.when(jnp.logical_and(jnp.logical_and(nblk > 0, f == 1),
                             jnp.logical_and(u + 1 < n_pass, nblk_ref[nxt] > 0)))
    def _():
        issue_gather(nxt)

    @pl.when(nblk > 0)
    def _():
        w1_ref[:, :MOE_TF] = wg_ref[0].astype(BF16)
        w1_ref[:, MOE_TF:] = wu_ref[0].astype(BF16)
        w2_ref[...] = wd_ref[0].astype(BF16)
        bg = bg_ref[0]
        bu = bu_ref[0]

        def ffn_rows(r0, m):
            rs = pl.ds(pl.multiple_of(r0, MOE_BLK), m)
            h = jnp.dot(xb_ref[rs, :], w1_ref[...], preferred_element_type=F32)
            x_glu = jnp.minimum(h[:, :MOE_TF] + bg, SWIGLU_LIMIT)
            x_lin = jnp.clip(h[:, MOE_TF:] + bu, -SWIGLU_LIMIT, SWIGLU_LIMIT)
            act = x_glu * jax.nn.sigmoid(SWIGLU_ALPHA * x_glu) * (x_lin + 1.0)
            acc_ref[rs, :] += jnp.dot(act.astype(BF16), w2_ref[...], preferred_element_type=F32)

        n_sub = (nblk + (MOE_SUB_MAX - 1)) // MOE_SUB_MAX
        q = nblk // n_sub
        n_hi = nblk - q * n_sub

        def sub(i, blk):
            sz = q + (i < n_hi).astype(jnp.int32)
            for s in range(1, MOE_SUB_MAX + 1):
                @pl.when(sz == s)
                def _(s=s):
                    ffn_rows(blk * MOE_BLK, s * MOE_BLK)
            return blk + sz
        lax.fori_loop(0, n_sub, sub, 0)

    @pl.when(jnp.logical_and(nblk > 0, f == pl.num_programs(1) - 1))
    def _():
        def emit(b, c):
            @pl.when(b >= 2)
            def _():
                out_copy(b - 2).wait()
            rs = pl.ds(pl.multiple_of(b * MOE_BLK, MOE_BLK), MOE_BLK)
            val = acc_ref[rs, :] + bd_ref[0]
            for ch in range(ROW_CH):
                stage_ref[b % 2, pl.ds(ch, MOE_BLK, stride=ROW_CH), :] = val[:, ch * LANES:(ch + 1) * LANES]
            out_copy(b).start()
            return c
        lax.fori_loop(0, nblk, emit, 0)

        @pl.when(nblk >= 2)
        def _():
            out_copy(nblk - 2).wait()
        out_copy(nblk - 1).wait()

    @pl.when(jnp.logical_and(u == pl.num_programs(0) - 1, f == pl.num_programs(1) - 1))
    def _():
        used = nblk_ref[pl.num_programs(0)]
        total = ybuf_ref.shape[0] // blk_rt
        stage_ref[0] = jnp.zeros((blk_rt, LANES), F32)

        def zero_copy(b):
            dst = pl.ds(pl.multiple_of(b * blk_rt, blk_rt), blk_rt)
            return pltpu.make_async_copy(stage_ref.at[0], ybuf_ref.at[dst], osem.at[0])

        def start(b, c):
            zero_copy(b).start()
            return c
        lax.fori_loop(used, total, start, 0)

        def wait(b, c):
            zero_copy(b).wait()
            return c
        lax.fori_loop(used, total, wait, 0)


def _moe_ffn(h1, src, e_u, blk0, nblk, w_gu, b_gu, w_dn, b_dn, rows_pad):
    d = D_MODEL
    n_pass = e_u.shape[0]
    n_f = D_FF // MOE_TF
    last_f = n_f - 1

    def fsel(u, f, nblk_ref):
        return jnp.where(nblk_ref[u] > 0, f, last_f)

    grid_spec = pltpu.PrefetchScalarGridSpec(
        num_scalar_prefetch=4,
        grid=(n_pass, n_f),
        in_specs=[
            pl.BlockSpec(memory_space=pl.ANY),
            pl.BlockSpec((1, d, MOE_TF), lambda u, f, e, b0, nb, s: (e[u], 0, fsel(u, f, nb))),
            pl.BlockSpec((1, d, MOE_TF), lambda u, f, e, b0, nb, s: (e[u], 0, n_f + fsel(u, f, nb))),
            pl.BlockSpec((1, 1, MOE_TF), lambda u, f, e, b0, nb, s: (e[u], 0, fsel(u, f, nb))),
            pl.BlockSpec((1, 1, MOE_TF), lambda u, f, e, b0, nb, s: (e[u], 0, n_f + fsel(u, f, nb))),
            pl.BlockSpec((1, MOE_TF, d), lambda u, f, e, b0, nb, s: (e[u], fsel(u, f, nb), 0)),
            pl.BlockSpec((1, 1, d), lambda u, f, e, b0, nb, s: (e[u], 0, 0)),
        ],
        out_specs=pl.BlockSpec(memory_space=pl.ANY),
        scratch_shapes=[pltpu.VMEM((MOE_TM // SUBLANES, ROW_CH, SUBLANES, LANES), F32), pltpu.VMEM((MOE_TM, d), BF16),
                        pltpu.VMEM((MOE_TM, d), F32), pltpu.VMEM((d, 2 * MOE_TF), BF16), pltpu.VMEM((MOE_TF, d), BF16),
                        pltpu.VMEM((2, MOE_BLK * ROW_CH, LANES), F32),
                        pltpu.SemaphoreType.DMA(()), pltpu.SemaphoreType.DMA((2,))],
    )
    return pl.pallas_call(
        _moe_ffn_body,
        grid_spec=grid_spec,
        out_shape=jax.ShapeDtypeStruct((rows_pad * ROW_CH, LANES), F32),
        compiler_params=_cparams(("arbitrary", "arbitrary"), 60),
        name="moe_ffn",
    )(e_u, blk0, nblk, src, h1, w_gu, w_gu, b_gu.reshape(N_EXPERTS, 1, 2 * D_FF), b_gu.reshape(N_EXPERTS, 1, 2 * D_FF),
      w_dn, b_dn.reshape(N_EXPERTS, 1, d))


def _moe_combine_body(dest_ref, h1_ref, gate_ref, g2_ref, b2_ref, ybuf_ref, outp_ref, outs_ref,
                      buf_ref, sem, pre_ref):
    i = pl.program_id(0)
    n_pairs = TILE_T * TOP_K
    slot = i % 2

    def copy(tile, sl, rb, s, k):
        src_row = dest_ref[tile * n_pairs + rb * (SUBLANES * TOP_K) + (s * TOP_K + k)]
        return pltpu.make_async_copy(ybuf_ref.at[pl.ds(src_row * ROW_CH, ROW_CH)],
                                     buf_ref.at[sl, k, rb, pl.ds(0, ROW_CH), s], sem.at[sl])

    def for_all_rows(tile, sl, fn):
        def body(rb, c):
            for s in range(SUBLANES):
                for k in range(TOP_K):
                    fn(copy(tile, sl, rb, s, k))
            return c
        lax.fori_loop(0, TILE_T // SUBLANES, body, 0)

    def issue(tile, sl):
        for_all_rows(tile, sl, lambda cp: cp.start())

    @pl.when(i == 0)
    def _():
        issue(0, 0)

    @pl.when(i + 1 < pl.num_programs(0))
    def _():
        issue(i + 1, 1 - slot)

    for_all_rows(i, slot, lambda cp: cp.wait())

    gate = gate_ref[...]
    cs = lambda c: slice(c * LANES, (c + 1) * LANES)
    part = jnp.zeros((TILE_T, LANES), F32)
    gate_b = [jnp.broadcast_to(gate[:, k:k + 1], (TILE_T, LANES)) for k in range(TOP_K)]
    for c in range(ROW_CH):
        fc = None
        for k in range(TOP_K):
            v = buf_ref[slot, k, :, c].reshape(TILE_T, LANES) * gate_b[k]
            fc = v if fc is None else fc + v
        pre = DEEPNORM_ALPHA * _load_row_tiles(h1_ref, 0, TILE_T, c) + fc
        pre_ref[:, cs(c)] = pre
        part = part + pre
    mu = jnp.sum(part, axis=1, keepdims=True) * (1.0 / D_MODEL)
    part = jnp.zeros((TILE_T, LANES), F32)
    for c in range(ROW_CH):
        dlt = pre_ref[:, cs(c)] - mu
        part = part + dlt * dlt
    rstd = lax.rsqrt(jnp.sum(part, axis=1, keepdims=True) * (1.0 / D_MODEL) + LN_EPS)
    is_step = i == pl.num_programs(0) - 1

    def write(out_ref):
        for c in range(ROW_CH):
            out_ref[:, cs(c)] = (pre_ref[:, cs(c)] - mu) * rstd * g2_ref[:, cs(c)] + b2_ref[:, cs(c)]

    @pl.when(jnp.logical_not(is_step))
    def _():
        write(outp_ref)

    @pl.when(is_step)
    def _():
        write(outs_ref)


def _moe_combine(dest, h1, gates, g2, b2, ybuf, n_tok):
    d = D_MODEL
    n_tiles = n_tok // TILE_T
    grid_spec = pltpu.PrefetchScalarGridSpec(
        num_scalar_prefetch=1,
        grid=(n_tiles,),
        in_specs=[pl.BlockSpec((TILE_T * ROW_CH, LANES), lambda i, s: (i, 0)),
                  pl.BlockSpec((TILE_T, LANES), lambda i, s: (i, 0)),
                  pl.BlockSpec((1, d), lambda i, s: (0, 0)),
                  pl.BlockSpec((1, d), lambda i, s: (0, 0)),
                  pl.BlockSpec(memory_space=pl.ANY)],
        out_specs=[pl.BlockSpec((TILE_T, d), lambda i, s: (jnp.minimum(i, n_tiles - 2), 0)),
                   pl.BlockSpec((TILE_T, d), lambda i, s: (0, 0))],
        scratch_shapes=[pltpu.VMEM((2, TOP_K, TILE_T // SUBLANES, ROW_CH, SUBLANES, LANES), F32),
                        pltpu.SemaphoreType.DMA((2,)),
                        pltpu.VMEM((TILE_T, d), F32)],
    )
    return pl.pallas_call(
        _moe_combine_body,
        grid_spec=grid_spec,
        out_shape=[jax.ShapeDtypeStruct((n_tok - TILE_T, d), F32), jax.ShapeDtypeStruct((TILE_T, d), F32)],
        compiler_params=_cparams(("arbitrary",), 48),
        name="moe_combine",
    )(dest.reshape(-1), h1, gates, g2, b2, ybuf)


def kernel(x_prompt, x_sample, state_s5_re, state_s5_im, state_mlstm_c, state_mlstm_n, state_mlstm_m, state_mlstm_conv, meta_tokens, ln_in_g, ln_in_b, w_in, b_in, s5_a_re, s5_a_im, s5_log_dt, s5_b_re, s5_b_im, s5_c_re, s5_c_im, s5_d, s5_w_glu, mlstm_conv_w, mlstm_conv_b, mlstm_wq, mlstm_wk, mlstm_norm_g, w_out, ln1_g, ln1_b, router_w, router_b, w_gate_up, b_gate_up, w_down, b_down, ln2_g, ln2_b):
    bsz, seq, d = x_prompt.shape
    dec_b = x_sample.shape[0]
    n_pad = TILE_T - N_META
    x_small = jnp.concatenate([jnp.zeros((n_pad, d), F32), meta_tokens, x_sample.reshape(dec_b, d)], axis=0)
    w_in_p = jnp.pad(w_in[0], ((0, 0), (0, N_IN_PAD - N_IN))).astype(BF16)
    b_in_p = jnp.pad(b_in[0], (0, N_IN_PAD - N_IN)).reshape(1, N_IN_PAD)
    g_in = ln_in_g.reshape(1, d)
    bb_in = ln_in_b.reshape(1, d)
    z_p = _inproj(x_prompt.reshape(bsz * seq, d), g_in, bb_in, w_in_p, b_in_p, 512, 1408)
    z_s = _inproj(x_small, g_in, bb_in, w_in_p, b_in_p, 256, 1408)
    tabs = _s5_prep(s5_a_re[0], s5_a_im[0], s5_log_dt[0], s5_b_re[0], s5_b_im[0])
    bd_b, bd_c = _s5_block_diag(tabs[6], tabs[7], s5_c_re[0], s5_c_im[0])
    wglu_bf = s5_w_glu[0].astype(BF16)
    d_skip = s5_d[0].reshape(1, D_S5)
    n_tiles = seq // TILE_T + 1
    zero_state = jnp.zeros((bsz, 1, N_STATE), F32)
    y_s5_p, s5r_p, s5i_p = _s5_seq(z_s, z_p, zero_state, zero_state, tabs[:6], bd_b, bd_c, d_skip, wglu_bf,
                                   bsz, n_tiles, n_pad)
    y_s5_s, s5r_s, s5i_s = _s5_step(z_s, 1, state_s5_re[0].reshape(dec_b, N_STATE),
                                    state_s5_im[0].reshape(dec_b, N_STATE), tabs[:6], bd_b, bd_c, d_skip, wglu_bf)

    conv_w = mlstm_conv_w[0]
    conv_b = mlstm_conv_b[0].reshape(1, D_ML)
    wq_bf = mlstm_wq[0].astype(BF16)
    wk_bf = mlstm_wk[0].astype(BF16)
    norm_g = mlstm_norm_g[0].reshape(1, D_ML)
    y_ml_p, c_p, n_p, m_p = _mlstm_seq(z_s, z_p, conv_w, conv_b, wq_bf, wk_bf, norm_g, bsz, n_tiles, n_pad)
    conv0 = state_mlstm_conv[0]
    y_ml_s, c_s, n_s, m_s = _mlstm_step(z_s, 1, conv0, state_mlstm_c[0], state_mlstm_n[0].reshape(dec_b, D_ML),
                                        state_mlstm_m[0], conv_w, conv_b, wq_bf, wk_bf, norm_g)

    rw = jnp.pad(router_w[0], ((0, 0), (0, LANES - N_EXPERTS)))
    rw_hi = rw.astype(BF16)
    rw_lo = (rw - rw_hi.astype(F32)).astype(BF16)
    rb = jnp.pad(router_b[0], (0, LANES - N_EXPERTS)).reshape(1, LANES)
    consts = (g_in, bb_in, w_out[0].astype(BF16), ln1_g[0].reshape(1, d), ln1_b[0].reshape(1, d), rw_hi, rw_lo, rb)
    h1, top_e, gates = _mix_out(x_prompt.reshape(bsz * seq, d), x_sample.reshape(dec_b, d),
                                y_s5_p, y_s5_s, y_ml_p, y_ml_s, consts, seq // TILE_T)
    n_tok = bsz * seq + dec_b
    dest, src, e_u, blk0, nblk = _moe_routing(top_e[:n_tok, :TOP_K], n_tok)
    ybuf = _moe_ffn(h1, src, e_u, blk0, nblk, w_gate_up[0], b_gate_up[0], w_down[0], b_down[0], src.shape[0])
    out_p, out_s = _moe_combine(dest, h1, gates, ln2_g[0].reshape(1, d), ln2_b[0].reshape(1, d), ybuf, n_tok)

    y_prompt = out_p.reshape(bsz, seq, d)
    y_sample = out_s.reshape(dec_b, 1, d)
    xm_p = z_p.reshape(bsz, seq, N_IN_PAD)[:, seq - (CONV_W - 1):, D_S5:D_S5 + D_ML]
    xm_s = z_s[TILE_T:, D_S5:D_S5 + D_ML]
    conv_s = jnp.concatenate([conv0[:, 1:], xm_s[:, None, :]], axis=1)
    return (y_prompt, y_sample,
            s5r_p.reshape(1, bsz, N_GROUPS, S5_STATE), s5i_p.reshape(1, bsz, N_GROUPS, S5_STATE),
            c_p[None], n_p.reshape(1, bsz, N_HEADS, DH), m_p[None, :, :N_HEADS, 0], xm_p[None],
            s5r_s.reshape(1, dec_b, N_GROUPS, S5_STATE), s5i_s.reshape(1, dec_b, N_GROUPS, S5_STATE),
            c_s[None], n_s.reshape(1, dec_b, N_HEADS, DH), m_s[None], conv_s[None])
```

```python
import functools
import math

import jax
import jax.numpy as jnp
from jax import lax
from jax.experimental import pallas as pl
from jax.experimental.pallas import tpu as pltpu

F32 = jnp.float32
BF16 = jnp.bfloat16

D_MODEL = 2048
N_META = 16
D_S5 = 1024
D_ML = 1024
S5_CH = 16
N_GROUPS = 64
S5_STATE = 64
N_STATE = N_GROUPS * S5_STATE
N_HEADS = 4
DH = 256
CONV_W = 4
N_EXPERTS = 32
TOP_K = 4
D_FF = 2048
SWIGLU_LIMIT = 7.0
SWIGLU_ALPHA = 1.702
LN_EPS = 1e-5
DEEPNORM_ALPHA = 2.0 ** 0.25
N_IN = D_S5 + 3 * D_ML + 2 * N_HEADS

LANES = 128
SUBLANES = 8
MXU_DIM = 256
MXU_ROWS = 16

TILE_T = 128
SEG_LEN = TILE_T // SUBLANES
N_IN_PAD = 33 * LANES
GATE_COL = 4 * D_S5 // LANES
S5_KCH = D_S5 // MXU_DIM
S5_SCH = N_STATE // S5_KCH
SCAN_LW = 256


def _cparams(sem, vmem_mb=None):
    kw = dict(dimension_semantics=sem)
    if vmem_mb is not None:
        kw["vmem_limit_bytes"] = vmem_mb * 1024 * 1024
    return pltpu.CompilerParams(**kw)


def _layer_norm(x, g, b):
    mu = jnp.mean(x, axis=-1, keepdims=True)
    xc = x - mu
    var = jnp.mean(xc * xc, axis=-1, keepdims=True)
    return xc * lax.rsqrt(var + LN_EPS) * g + b


def _inproj_body(x_ref, g_ref, b_ref, w_ref, bias_ref, z_ref, hn_ref):
    @pl.when(pl.program_id(1) == 0)
    def _():
        hn_ref[...] = _layer_norm(x_ref[...], g_ref[...], b_ref[...]).astype(BF16)

    z_ref[...] = jnp.dot(hn_ref[...], w_ref[...], preferred_element_type=F32) + bias_ref[...]


def _inproj(x, g, b, w, bias, tm, tn):
    rows, d = x.shape
    n = w.shape[1]
    return pl.pallas_call(
        _inproj_body,
        grid=(rows // tm, n // tn),
        in_specs=[
            pl.BlockSpec((tm, d), lambda i, j: (i, 0)),
            pl.BlockSpec((1, d), lambda i, j: (0, 0)),
            pl.BlockSpec((1, d), lambda i, j: (0, 0)),
            pl.BlockSpec((d, tn), lambda i, j: (0, j)),
            pl.BlockSpec((1, tn), lambda i, j: (0, j)),
        ],
        out_specs=pl.BlockSpec((tm, tn), lambda i, j: (i, j)),
        out_shape=jax.ShapeDtypeStruct((rows, n), F32),
        scratch_shapes=[pltpu.VMEM((tm, d), BF16)],
        compiler_params=_cparams(("arbitrary", "arbitrary"), 48),
        name="inproj",
    )(x, g, b, w, bias)


def _cmul(ar, ai, br, bi):
    return ar * br - ai * bi, ar * bi + ai * br


def _s5_prep_body(are_ref, aim_ref, dt_ref, bre_ref, bim_ref,
                  pre_ref, pim_ref, hre_ref, him_ref, qre_ref, qim_ref, bbre_ref, bbim_ref):
    lr = are_ref[...]
    li = aim_ref[...]
    dt = jnp.exp(dt_ref[...])
    mag = jnp.exp(lr * dt)
    ar = mag * jnp.cos(li * dt)
    ai = mag * jnp.sin(li * dt)
    nr = ar - 1.0
    ni = ai
    den = lr * lr + li * li
    cr = (nr * lr + ni * li) / den
    ci = (ni * lr - nr * li) / den
    bbre_ref[...] = cr * bre_ref[...] - ci * bim_ref[...]
    bbim_ref[...] = cr * bim_ref[...] + ci * bre_ref[...]
    pr, pi = ar, ai
    for j in range(SEG_LEN):
        pre_ref[j:j + 1, :] = pr
        pim_ref[j:j + 1, :] = pi
        if j + 1 < SEG_LEN:
            pr, pi = _cmul(pr, pi, ar, ai)
    row = lax.broadcasted_iota(jnp.int32, (SUBLANES, N_STATE), 0)
    kr, ki = pr, pi
    for idx, k in enumerate((1, 2, 4)):
        hre_ref[idx * 8:(idx + 1) * 8, :] = jnp.where(row >= k, jnp.broadcast_to(kr, (SUBLANES, N_STATE)), 0.0)
        him_ref[idx * 8:(idx + 1) * 8, :] = jnp.where(row >= k, jnp.broadcast_to(ki, (SUBLANES, N_STATE)), 0.0)
        kr, ki = _cmul(kr, ki, kr, ki)
    qr, qi = pr, pi
    for s in range(SUBLANES):
        qre_ref[s:s + 1, :] = qr
        qim_ref[s:s + 1, :] = qi
        if s + 1 < SUBLANES:
            qr, qi = _cmul(qr, qi, pr, pi)


def _s5_prep(a_re, a_im, log_dt, b_re, b_im):
    n = N_STATE
    are = a_re.reshape(1, n)
    aim = a_im.reshape(1, n)
    dtl = jnp.broadcast_to(log_dt[:, None], (N_GROUPS, S5_STATE)).reshape(1, n)
    bre = b_re.transpose(2, 0, 1).reshape(S5_CH, n)
    bim = b_im.transpose(2, 0, 1).reshape(S5_CH, n)
    shp = lambda r: jax.ShapeDtypeStruct((r, n), F32)
    return pl.pallas_call(
        _s5_prep_body,
        out_shape=(shp(SEG_LEN), shp(SEG_LEN), shp(24), shp(24), shp(8), shp(8), shp(S5_CH), shp(S5_CH)),
        name="s5_prep",
    )(are, aim, dtl, bre, bim)


def _s5_block_diag(bb_re, bb_im, c_re, c_im):
    gpc = 16
    eye = jnp.eye(gpc, dtype=bool)

    def bd_in(bb):
        t = bb.reshape(S5_CH, S5_KCH, gpc, S5_STATE).transpose(1, 2, 0, 3)
        t = jnp.where(eye[None, :, None, :, None], t[:, :, :, None, :], 0.0)
        return t.reshape(S5_KCH, gpc * S5_CH, gpc * S5_STATE)

    def bd_out(c):
        t = c.reshape(S5_KCH, gpc, S5_CH, S5_STATE).transpose(0, 1, 3, 2)
        t = jnp.where(eye[None, :, None, :, None], t[:, :, :, None, :], 0.0)
        return t.reshape(S5_KCH, gpc * S5_STATE, gpc * S5_CH)

    bd_b = jnp.concatenate([bd_in(bb_re), bd_in(bb_im)], axis=2).astype(BF16)
    bd_c = jnp.concatenate([bd_out(c_re), bd_out(-c_im)], axis=1).astype(BF16)
    return bd_b, bd_c


def _gelu_glu(y, wglu_ref):
    y = 0.5 * y * (1.0 + lax.erf(y * math.sqrt(0.5)))
    gate = jnp.dot(y.astype(BF16), wglu_ref[...], preferred_element_type=F32)
    return y * jax.nn.sigmoid(gate)


def _s5_in_proj(u_bf, bdb_ref, bur_ref, bui_ref):
    for c in range(S5_KCH):
        r = jnp.dot(u_bf[:, c * MXU_DIM:(c + 1) * MXU_DIM], bdb_ref[c], preferred_element_type=F32)
        bur_ref[:, c * S5_SCH:(c + 1) * S5_SCH] = r[:, :S5_SCH]
        bui_ref[:, c * S5_SCH:(c + 1) * S5_SCH] = r[:, S5_SCH:]


def _s5_out_proj(xr_ref, xi_ref, bdc_ref):
    ys = []
    for c in range(S5_KCH):
        xr = xr_ref[:, c * S5_SCH:(c + 1) * S5_SCH].astype(BF16)
        xi = xi_ref[:, c * S5_SCH:(c + 1) * S5_SCH].astype(BF16)
        ys.append(jnp.dot(xr, bdc_ref[c, :S5_SCH, :], preferred_element_type=F32)
                  + jnp.dot(xi, bdc_ref[c, S5_SCH:, :], preferred_element_type=F32))
    return jnp.concatenate(ys, axis=1)


def _s5_seq_body(n_pad, us_ref, up_ref, x0r_ref, x0i_ref, bdb_ref, bdc_ref, pre_ref, pim_ref,
                 hre_ref, him_ref, qre_ref, qim_ref, d_ref, wglu_ref,
                 y_ref, xr_out_ref, xi_out_ref, bur_ref, bui_ref, cr_ref, ci_ref, perm_ref):
    t = pl.program_id(1)
    n = N_STATE

    @pl.when(t == 0)
    def _():
        cr_ref[...] = jnp.broadcast_to(x0r_ref[0], (SUBLANES, n))
        ci_ref[...] = jnp.broadcast_to(x0i_ref[0], (SUBLANES, n))

    n_lc = D_S5 // LANES

    def load_perm(ref):
        for c in range(n_lc):
            perm_ref[c] = ref[:, c * LANES:(c + 1) * LANES]
        return jnp.concatenate(
            [jnp.concatenate([perm_ref[c, pl.ds(j, SUBLANES, stride=SEG_LEN), :] for c in range(n_lc)], axis=1)
             for j in range(SEG_LEN)], axis=0)

    prow = lax.broadcasted_iota(jnp.int32, (TILE_T, 1), 0)
    time = (prow % SUBLANES) * SEG_LEN + prow // SUBLANES
    u_first = jnp.where(time >= n_pad, load_perm(us_ref), 0.0)
    u = jnp.where(t == 0, u_first, load_perm(up_ref))
    _s5_in_proj(u.astype(BF16), bdb_ref, bur_ref, bui_ref)

    row8 = lax.broadcasted_iota(jnp.int32, (SUBLANES, SCAN_LW), 0)

    def scan_lanes(lc, carry):
        ls = pl.ds(pl.multiple_of(lc * SCAN_LW, SCAN_LW), SCAN_LW)
        bc = lambda ref, j: jnp.broadcast_to(ref[j:j + 1, ls], (SUBLANES, SCAN_LW))
        ar, ai = bc(pre_ref, 0), bc(pim_ref, 0)
        xr = jnp.zeros((SUBLANES, SCAN_LW), F32)
        xi = jnp.zeros((SUBLANES, SCAN_LW), F32)
        for j in range(SEG_LEN):
            rs = slice(j * SUBLANES, (j + 1) * SUBLANES)
            nr = ar * xr - ai * xi + bur_ref[rs, ls]
            ni = ar * xi + ai * xr + bui_ref[rs, ls]
            xr, xi = nr, ni
            bur_ref[rs, ls] = xr
            bui_ref[rs, ls] = xi
        er, ei = xr, xi
        for idx, k in enumerate((1, 2, 4)):
            sr = pltpu.roll(er, k, axis=0)
            si = pltpu.roll(ei, k, axis=0)
            hr = hre_ref[idx * 8:(idx + 1) * 8, ls]
            hi = him_ref[idx * 8:(idx + 1) * 8, ls]
            er, ei = er + (hr * sr - hi * si), ei + (hr * si + hi * sr)
        cpr = cr_ref[:, ls]
        cpi = ci_ref[:, ls]
        qr = qre_ref[:, ls]
        qi = qim_ref[:, ls]
        er, ei = er + (qr * cpr - qi * cpi), ei + (qr * cpi + qi * cpr)
        inr = jnp.where(row8 == 0, cpr, pltpu.roll(er, 1, axis=0))
        ini = jnp.where(row8 == 0, cpi, pltpu.roll(ei, 1, axis=0))
        cr_ref[:, ls] = jnp.broadcast_to(er[SUBLANES - 1:SUBLANES, :], (SUBLANES, SCAN_LW))
        ci_ref[:, ls] = jnp.broadcast_to(ei[SUBLANES - 1:SUBLANES, :], (SUBLANES, SCAN_LW))
        for j in range(SEG_LEN):
            rs = slice(j * SUBLANES, (j + 1) * SUBLANES)
            pr, pi = bc(pre_ref, j), bc(pim_ref, j)
            bur_ref[rs, ls] = bur_ref[rs, ls] + (pr * inr - pi * ini)
            bui_ref[rs, ls] = bui_ref[rs, ls] + (pr * ini + pi * inr)
        return carry

    lax.fori_loop(0, n // SCAN_LW, scan_lanes, 0)

    y = _s5_out_proj(bur_ref, bui_ref, bdc_ref) + d_ref[...] * u
    y = _gelu_glu(y, wglu_ref)
    for j in range(SEG_LEN):
        for c in range(n_lc):
            perm_ref[c, pl.ds(j, SUBLANES, stride=SEG_LEN), :] = y[j * SUBLANES:(j + 1) * SUBLANES,
                                                                   c * LANES:(c + 1) * LANES]
    for c in range(n_lc):
        y_ref[:, c * LANES:(c + 1) * LANES] = perm_ref[c]

    @pl.when(t == pl.num_programs(1) - 1)
    def _():
        xr_out_ref[0] = cr_ref[0:1, :]
        xi_out_ref[0] = ci_ref[0:1, :]


def _s5_seq(z_small, z_seq, x0r, x0i, tabs, bd_b, bd_c, d_skip, wglu_bf, n_batch, n_tiles, n_pad):
    pre, pim, hre, him, qre, qim = tabs
    n = N_STATE
    full = lambda a: pl.BlockSpec(a.shape, lambda b, t: (0,) * a.ndim)
    per_seq = n_tiles - 1
    return pl.pallas_call(
        functools.partial(_s5_seq_body, n_pad),
        grid=(n_batch, n_tiles),
        in_specs=[
            pl.BlockSpec((TILE_T, D_S5), lambda b, t: (0, 0)),
            pl.BlockSpec((TILE_T, D_S5), lambda b, t: (b * per_seq + jnp.maximum(t - 1, 0), 0)),
            pl.BlockSpec((1, 1, n), lambda b, t: (b, 0, 0)),
            pl.BlockSpec((1, 1, n), lambda b, t: (b, 0, 0)),
            full(bd_b), full(bd_c), full(pre), full(pim), full(hre), full(him), full(qre), full(qim),
            full(d_skip), full(wglu_bf),
        ],
        out_specs=[
            pl.BlockSpec((TILE_T, D_S5), lambda b, t: (b * n_tiles + t, 0)),
            pl.BlockSpec((1, 1, n), lambda b, t: (b, 0, 0)),
            pl.BlockSpec((1, 1, n), lambda b, t: (b, 0, 0)),
        ],
        out_shape=[
            jax.ShapeDtypeStruct((n_batch * n_tiles * TILE_T, D_S5), F32),
            jax.ShapeDtypeStruct((n_batch, 1, n), F32),
            jax.ShapeDtypeStruct((n_batch, 1, n), F32),
        ],
        scratch_shapes=[pltpu.VMEM((TILE_T, n), F32), pltpu.VMEM((TILE_T, n), F32),
                        pltpu.VMEM((SUBLANES, n), F32), pltpu.VMEM((SUBLANES, n), F32),
                        pltpu.VMEM((D_S5 // LANES, TILE_T, LANES), F32)],
        compiler_params=_cparams(("arbitrary", "arbitrary"), 48),
        name="s5_seq",
    )(z_small, z_seq, x0r, x0i, bd_b, bd_c, pre, pim, hre, him, qre, qim, d_skip, wglu_bf)


def _s5_step_body(u_ref, x0r_ref, x0i_ref, bdb_ref, bdc_ref, pre_ref, pim_ref, d_ref, wglu_ref,
                  y_ref, xr_ref, xi_ref):
    u = u_ref[...]
    _s5_in_proj(u.astype(BF16), bdb_ref, xr_ref, xi_ref)
    ar = pre_ref[0:1, :]
    ai = pim_ref[0:1, :]
    x0r = x0r_ref[...]
    x0i = x0i_ref[...]
    xr_ref[...] = xr_ref[...] + (ar * x0r - ai * x0i)
    xi_ref[...] = xi_ref[...] + (ar * x0i + ai * x0r)
    y = _s5_out_proj(xr_ref, xi_ref, bdc_ref) + d_ref[...] * u
    y_ref[...] = _gelu_glu(y, wglu_ref)


def _s5_step(z_small, row_blk, x0r, x0i, tabs, bd_b, bd_c, d_skip, wglu_bf):
    pre, pim = tabs[0], tabs[1]
    rows = x0r.shape[0]
    n = N_STATE
    full = lambda a: pl.BlockSpec(a.shape, lambda i: (0,) * a.ndim)
    return pl.pallas_call(
        _s5_step_body,
        grid=(1,),
        in_specs=[pl.BlockSpec((rows, D_S5), lambda i: (row_blk, 0)), full(x0r), full(x0i),
                  full(bd_b), full(bd_c), full(pre), full(pim), full(d_skip), full(wglu_bf)],
        out_specs=[pl.BlockSpec((rows, D_S5), lambda i: (0, 0)),
                   pl.BlockSpec((rows, n), lambda i: (0, 0)), pl.BlockSpec((rows, n), lambda i: (0, 0))],
        out_shape=[jax.ShapeDtypeStruct((rows, D_S5), F32), jax.ShapeDtypeStruct((rows, n), F32),
                   jax.ShapeDtypeStruct((rows, n), F32)],
        compiler_params=_cparams(("arbitrary",), 48),
        name="s5_step",
    )(z_small, x0r, x0i, bd_b, bd_c, pre, pim, d_skip, wglu_bf)


NEG_INF = float("-inf")
N_STEP_SCALARS = 5 * N_HEADS


def _log_sigmoid(x):
    return jnp.minimum(x, 0.0) - jnp.log1p(jnp.exp(-jnp.abs(x)))


def _split3(x):
    hi = x.astype(BF16)
    r1 = x - hi.astype(F32)
    mid = r1.astype(BF16)
    lo = (r1 - mid.astype(F32)).astype(BF16)
    return hi, mid, lo


def _head_norm_gate(h, o, g):
    mu = jnp.mean(h, axis=-1, keepdims=True)
    hc = h - mu
    var = jnp.mean(hc * hc, axis=-1, keepdims=True)
    return jax.nn.sigmoid(o) * (hc * lax.rsqrt(var + LN_EPS) * g)


def _dot_nt(a, b):
    return lax.dot_general(a, b, (((1,), (1,)), ((), ())), preferred_element_type=F32)


def _dot_tn(a, b):
    return lax.dot_general(a, b, (((0,), (0,)), ((), ())), preferred_element_type=F32)


def _mlstm_seq_body(n_pad, sxm_ref, sv_ref, so_ref, sg_ref, pxm_ref, pv_ref, po_ref, pg_ref,
                    cw_ref, cb_ref, wq_ref, wk_ref, ng_ref,
                    y_ref, c_out_ref, n_out_ref, m_out_ref, c_ref, n_ref, m_ref, prev_ref):
    t = pl.program_id(1)
    L = TILE_T

    @pl.when(t == 0)
    def _():
        c_ref[...] = jnp.zeros_like(c_ref)
        n_ref[...] = jnp.zeros_like(n_ref)
        m_ref[...] = jnp.zeros_like(m_ref)
        prev_ref[...] = jnp.zeros_like(prev_ref)

    first = t == 0
    row = lax.broadcasted_iota(jnp.int32, (L, 1), 0)
    valid = jnp.logical_or(jnp.logical_not(first), row >= n_pad)
    xm = jnp.where(valid, jnp.where(first, sxm_ref[...], pxm_ref[...]), 0.0)
    v = jnp.where(first, sv_ref[...], pv_ref[...])
    o = jnp.where(first, so_ref[...], po_ref[...])
    gt = jnp.where(first, sg_ref[...], pg_ref[...])

    prev = prev_ref[...]

    def shifted(j):
        if j == 0:
            return xm
        return pltpu.roll(jnp.where(row >= L - j, prev, xm), j, axis=0)

    xc = cb_ref[...]
    for j in range(CONV_W):
        xc = xc + shifted(CONV_W - 1 - j) * cw_ref[j:j + 1, :]
    prev_ref[...] = xm
    xc = xc * jax.nn.sigmoid(xc)

    ig = jnp.where(valid, gt, NEG_INF)
    lf = jnp.where(valid, _log_sigmoid(gt), 0.0)
    ti = lax.broadcasted_iota(jnp.int32, (L, L), 0)
    si = lax.broadcasted_iota(jnp.int32, (L, L), 1)
    causal = si <= ti
    tri = jnp.where(causal, 1.0, 0.0).astype(BF16)
    bc = sum(jnp.dot(tri, p, preferred_element_type=F32) for p in _split3(lf))
    ig_t = ig.T
    bc_t = bc.T

    for h in range(N_HEADS):
        hs = slice(h * DH, (h + 1) * DH)
        b_col = bc[:, N_HEADS + h:N_HEADS + h + 1]
        b_row = bc_t[N_HEADS + h:N_HEADS + h + 1, :]
        ig_row = ig_t[h:h + 1, :]
        ig_col = ig[:, h:h + 1]
        m_prev = m_ref[h:h + 1, 0:1]
        dlog = jnp.where(causal, b_col - b_row + ig_row, NEG_INF)
        inter = b_col + m_prev
        m_t = jnp.maximum(jnp.max(dlog, axis=1, keepdims=True), inter)
        w = jnp.exp(dlog - m_t)
        g = jnp.exp(inter - m_t)
        xh = xc[:, hs].astype(BF16)
        q = jnp.dot(xh, wq_ref[h], preferred_element_type=F32)
        k = jnp.dot(xh, wk_ref[h], preferred_element_type=F32) * (DH ** -0.5)
        qb = q.astype(BF16)
        kb = k.astype(BF16)
        s = _dot_nt(qb, kb) * w
        vh = v[:, hs]
        cmat = c_ref[h]
        n_row = n_ref[h]
        num = jnp.dot(s.astype(BF16), vh.astype(BF16), preferred_element_type=F32) \
            + g * _dot_nt(qb, cmat.astype(BF16))
        den = jnp.sum(s, axis=1, keepdims=True) + g * jnp.sum(q * n_row, axis=1, keepdims=True)
        hh = num / jnp.maximum(jnp.abs(den), jnp.exp(-m_t))
        b_last = b_col[L - 1:L, :]
        wlog = b_last - b_col + ig_col
        m_new = jnp.maximum(b_last + m_prev, jnp.max(wlog, axis=0, keepdims=True))
        w_end = jnp.exp(wlog - m_new)
        g_end = jnp.exp(b_last + m_prev - m_new)
        c_ref[h] = g_end * cmat + _dot_tn((vh * w_end).astype(BF16), kb)
        n_ref[h] = g_end * n_row + jnp.sum(w_end * k, axis=0, keepdims=True)
        m_ref[h:h + 1, :] = jnp.broadcast_to(m_new, (1, LANES))
        y_ref[:, hs] = _head_norm_gate(hh, o[:, hs], ng_ref[:, hs])

    @pl.when(t == pl.num_programs(1) - 1)
    def _():
        c_out_ref[0] = c_ref[...]
        n_out_ref[0] = n_ref[...]
        m_out_ref[0] = m_ref[...]


def _mlstm_seq(z_small, z_seq, conv_w, conv_b, wq_bf, wk_bf, norm_g, n_batch, n_tiles, n_pad):
    per_seq = n_tiles - 1
    full = lambda a: pl.BlockSpec(a.shape, lambda b, t: (0,) * a.ndim)
    nb = D_ML // LANES

    def small(col, width):
        return pl.BlockSpec((TILE_T, width), lambda b, t: (0, col))

    def seq(col, width):
        return pl.BlockSpec((TILE_T, width), lambda b, t: (b * per_seq + jnp.maximum(t - 1, 0), col))

    return pl.pallas_call(
        functools.partial(_mlstm_seq_body, n_pad),
        grid=(n_batch, n_tiles),
        in_specs=[small(1, D_ML), small(2, D_ML), small(3, D_ML), small(GATE_COL, LANES),
                  seq(1, D_ML), seq(2, D_ML), seq(3, D_ML), seq(GATE_COL, LANES),
                  full(conv_w), full(conv_b), full(wq_bf), full(wk_bf), full(norm_g)],
        out_specs=[
            pl.BlockSpec((TILE_T, D_ML), lambda b, t: (b * n_tiles + t, 0)),
            pl.BlockSpec((1, N_HEADS, DH, DH), lambda b, t: (b, 0, 0, 0)),
            pl.BlockSpec((1, N_HEADS, 1, DH), lambda b, t: (b, 0, 0, 0)),
            pl.BlockSpec((1, SUBLANES, LANES), lambda b, t: (b, 0, 0)),
        ],
        out_shape=[
            jax.ShapeDtypeStruct((n_batch * n_tiles * TILE_T, D_ML), F32),
            jax.ShapeDtypeStruct((n_batch, N_HEADS, DH, DH), F32),
            jax.ShapeDtypeStruct((n_batch, N_HEADS, 1, DH), F32),
            jax.ShapeDtypeStruct((n_batch, SUBLANES, LANES), F32),
        ],
        scratch_shapes=[pltpu.VMEM((N_HEADS, DH, DH), F32), pltpu.VMEM((N_HEADS, 1, DH), F32),
                        pltpu.VMEM((SUBLANES, LANES), F32), pltpu.VMEM((TILE_T, D_ML), F32)],
        compiler_params=_cparams(("arbitrary", "arbitrary"), 48),
        name="mlstm_seq",
    )(z_small, z_small, z_small, z_small, z_seq, z_seq, z_seq, z_seq, conv_w, conv_b, wq_bf, wk_bf, norm_g)


def _mlstm_step_a_body(xm_ref, g_ref, conv0_ref, m0_ref, cw_ref, cb_ref, wq_ref, wk_ref,
                       q_ref, k_ref, sc_ref):
    xc = cb_ref[...]
    for j in range(CONV_W - 1):
        xc = xc + conv0_ref[j] * cw_ref[j:j + 1, :]
    xc = xc + xm_ref[...] * cw_ref[CONV_W - 1:CONV_W, :]
    xc = xc * jax.nn.sigmoid(xc)
    gt = g_ref[...]
    ig = gt[:, 0:N_HEADS]
    lf = _log_sigmoid(gt[:, N_HEADS:2 * N_HEADS])
    inter = lf + m0_ref[...]
    m_t = jnp.maximum(ig, inter)
    w = jnp.exp(ig - m_t)
    g = jnp.exp(inter - m_t)
    qks = []
    for h in range(N_HEADS):
        hs = slice(h * DH, (h + 1) * DH)
        xh = xc[:, hs].astype(BF16)
        q = jnp.dot(xh, wq_ref[h], preferred_element_type=F32)
        k = jnp.dot(xh, wk_ref[h], preferred_element_type=F32) * (DH ** -0.5)
        q_ref[:, hs] = q
        k_ref[:, hs] = k
        qks.append(jnp.sum(q * k, axis=1, keepdims=True))
    s = jnp.concatenate(qks, axis=1) * w
    rows = s.shape[0]
    sc_ref[...] = jnp.concatenate(
        [s, w, g, m_t, jnp.exp(-m_t), jnp.zeros((rows, LANES - 5 * N_HEADS), F32)], axis=1)


def _mlstm_step_b_body(bb, sc_ref, q_ref, k_ref, n_ref, v_ref, o_ref, ng_ref, c_ref,
                       y_ref, c_out_ref, n_out_ref):
    i0 = pl.program_id(0) * bb
    row = lax.broadcasted_iota(jnp.int32, (MXU_ROWS, DH), 0)

    def hi_lo(x):
        hi = x.astype(BF16).astype(F32)
        return jnp.broadcast_to(hi, (MXU_ROWS, DH)), jnp.broadcast_to(x - hi, (MXU_ROWS, DH))

    for i in range(bb):
        for h in range(N_HEADS):
            hs = slice(h * DH, (h + 1) * DH)
            base = (i0 + i) * N_STEP_SCALARS
            s = sc_ref[base + h]
            w = sc_ref[base + N_HEADS + h]
            g = sc_ref[base + 2 * N_HEADS + h]
            em = sc_ref[base + 4 * N_HEADS + h]
            rsel = pl.ds(i0 + i, 1)
            q_row = q_ref[rsel, hs]
            k_row = k_ref[rsel, hs]
            n_row = n_ref[rsel, hs]
            v_row = v_ref[rsel, hs]
            cmat = c_ref[i, h]
            qh, ql = hi_lo(q_row)
            qmat = jnp.where(row == 0, qh, jnp.where(row == 1, ql, 0.0)).astype(BF16)
            cq = _dot_nt(qmat, cmat.astype(BF16))
            num = s * v_row + g * (cq[0:1, :] + cq[1:2, :])
            den = s + g * jnp.sum(n_row * q_row, axis=1, keepdims=True)
            hh = num / jnp.maximum(jnp.abs(den), em)
            vh, vl = hi_lo(w * v_row)
            kh, kl = hi_lo(k_row)
            a = jnp.where(row < 2, vh, jnp.where(row < 4, vl, 0.0)).astype(BF16)
            b = jnp.where(row < 4, jnp.where(row % 2 == 0, kh, kl), 0.0).astype(BF16)
            c_out_ref[i, h] = g * cmat + _dot_tn(a, b)
            n_out_ref[rsel, hs] = g * n_row + w * k_row
            y_ref[rsel, hs] = _head_norm_gate(hh, o_ref[rsel, hs], ng_ref[:, hs])


def _mlstm_step(z_small, row_blk, conv0, c0, n0, m0, conv_w, conv_b, wq_bf, wk_bf, norm_g, bb=4):
    rows = c0.shape[0]
    full = lambda a: pl.BlockSpec(a.shape, lambda i: (0,) * a.ndim)
    conv0_t = conv0.transpose(1, 0, 2)
    q, k, sc = pl.pallas_call(
        _mlstm_step_a_body,
        grid=(1,),
        in_specs=[pl.BlockSpec((rows, D_ML), lambda i: (row_blk, 1)),
                  pl.BlockSpec((rows, LANES), lambda i: (row_blk, GATE_COL)),
                  full(conv0_t), full(m0), full(conv_w), full(conv_b), full(wq_bf), full(wk_bf)],
        out_specs=[pl.BlockSpec((rows, D_ML), lambda i: (0, 0)), pl.BlockSpec((rows, D_ML), lambda i: (0, 0)),
                   pl.BlockSpec((rows, LANES), lambda i: (0, 0))],
        out_shape=[jax.ShapeDtypeStruct((rows, D_ML), F32), jax.ShapeDtypeStruct((rows, D_ML), F32),
                   jax.ShapeDtypeStruct((rows, LANES), F32)],
        compiler_params=_cparams(("arbitrary",), 48),
        name="mlstm_step_a",
    )(z_small, z_small, conv0_t, m0, conv_w, conv_b, wq_bf, wk_bf)
    row_spec = pl.BlockSpec((rows, D_ML), lambda i: (0, 0))
    c_spec = pl.BlockSpec((bb, N_HEADS, DH, DH), lambda i: (i, 0, 0, 0))
    y, c_new, n_new = pl.pallas_call(
        functools.partial(_mlstm_step_b_body, bb),
        grid=(rows // bb,),
        in_specs=[pl.BlockSpec(memory_space=pltpu.SMEM), row_spec, row_spec, row_spec,
                  pl.BlockSpec((rows, D_ML), lambda i: (row_blk, 2)),
                  pl.BlockSpec((rows, D_ML), lambda i: (row_blk, 3)),
                  pl.BlockSpec((1, D_ML), lambda i: (0, 0)), c_spec],
        out_specs=[row_spec, c_spec, row_spec],
        out_shape=[jax.ShapeDtypeStruct((rows, D_ML), F32), jax.ShapeDtypeStruct(c0.shape, F32),
                   jax.ShapeDtypeStruct((rows, D_ML), F32)],
        compiler_params=_cparams(("arbitrary",), 56),
        name="mlstm_step_b",
    )(sc[:, :N_STEP_SCALARS].reshape(rows * N_STEP_SCALARS), q, k, n0, z_small, z_small, norm_g, c0)
    return y, c_new, n_new, sc[:, 3 * N_HEADS:4 * N_HEADS]


ROW_CH = D_MODEL // LANES


def _load_row_tiles(ref, row0, m, c):
    return ref[pl.ds(row0 * ROW_CH + c, m, stride=ROW_CH), :]


def _store_row_tiles(ref, row0, val):
    m = val.shape[0]
    for c in range(ROW_CH):
        ref[pl.ds(row0 * ROW_CH + c, m, stride=ROW_CH), :] = val[:, c * LANES:(c + 1) * LANES]


def _mix_out_body(xp_ref, xs_ref, ysp0_ref, ysp1_ref, yss_ref, ymp0_ref, ymp1_ref, yms_ref,
                  gin_ref, bin_ref, wout_ref, g1_ref, b1_ref, rwh_ref, rwl_ref, rb_ref,
                  h1_ref, e_ref, gate_ref):
    is_step = pl.program_id(0) == pl.num_programs(0) - 1
    consts = (gin_ref, bin_ref, wout_ref, g1_ref, b1_ref, rwh_ref, rwl_ref, rb_ref)
    outs = (h1_ref, e_ref, gate_ref)
    two = lambda a, b: jnp.concatenate([a[...], b[...]], axis=0)

    @pl.when(jnp.logical_not(is_step))
    def _():
        _mix_out_tile(xp_ref[...], two(ysp0_ref, ysp1_ref), two(ymp0_ref, ymp1_ref), consts, outs)

    @pl.when(is_step)
    def _():
        _mix_out_tile(two(xs_ref, xs_ref), two(yss_ref, yss_ref), two(yms_ref, yms_ref), consts, outs)


def _mix_out_tile(x, ys, ym, consts, outs):
    gin_ref, bin_ref, wout_ref, g1_ref, b1_ref, rwh_ref, rwl_ref, rb_ref = consts
    h1_ref, e_ref, gate_ref = outs
    hp = _layer_norm(x, gin_ref[...], bin_ref[...])
    ycat = jnp.concatenate([ys, ym], axis=1).astype(BF16)
    mix = jnp.dot(ycat, wout_ref[...], preferred_element_type=F32)
    h1 = _layer_norm(DEEPNORM_ALPHA * hp + mix, g1_ref[...], b1_ref[...])
    _store_row_tiles(h1_ref, 0, h1)
    xh = h1.astype(BF16)
    xl = (h1 - xh.astype(F32)).astype(BF16)
    logits = (jnp.dot(xh, rwh_ref[...], preferred_element_type=F32)
              + jnp.dot(xh, rwl_ref[...], preferred_element_type=F32)
              + jnp.dot(xl, rwh_ref[...], preferred_element_type=F32)) + rb_ref[...]
    rows = logits.shape[0]
    lane = lax.broadcasted_iota(jnp.int32, (rows, LANES), 1)
    logits = jnp.where(lane < N_EXPERTS, logits, NEG_INF)
    e_acc = jnp.zeros((rows, LANES), jnp.int32)
    v_acc = jnp.full((rows, LANES), NEG_INF, F32)
    for k in range(TOP_K):
        mx = jnp.max(logits, axis=1, keepdims=True)
        idx = jnp.min(jnp.where(logits == mx, lane, LANES), axis=1, keepdims=True)
        e_acc = jnp.where(lane == k, idx, e_acc)
        v_acc = jnp.where(lane == k, mx, v_acc)
        logits = jnp.where(lane == idx, NEG_INF, logits)
    p = jnp.exp(v_acc - jnp.max(v_acc, axis=1, keepdims=True))
    e_ref[...] = e_acc
    gate_ref[...] = p / jnp.sum(p, axis=1, keepdims=True)


def _mix_out(x_p, x_s, ys_p, ys_s, ym_p, ym_s, consts, tiles_per_seq):
    d = D_MODEL
    full = lambda a: pl.BlockSpec(a.shape, lambda i: (0,) * a.ndim)
    tm = 2 * TILE_T
    n_p = x_p.shape[0] // tm
    n_tiles = n_p + 1
    rows = n_tiles * tm

    def p_idx(i):
        return jnp.minimum(i, n_p - 1)

    def frame_idx(i, half):
        j = 2 * p_idx(i) + half
        return (j // tiles_per_seq) * (tiles_per_seq + 1) + j % tiles_per_seq + 1

    frame = lambda width, half: pl.BlockSpec((TILE_T, width), lambda i: (frame_idx(i, half), 0))
    first = lambda width: pl.BlockSpec((TILE_T, width), lambda i: (0, 0))
    return pl.pallas_call(
        _mix_out_body,
        grid=(n_tiles,),
        in_specs=[pl.BlockSpec((tm, d), lambda i: (p_idx(i), 0)), first(d),
                  frame(D_S5, 0), frame(D_S5, 1), first(D_S5),
                  frame(D_ML, 0), frame(D_ML, 1), first(D_ML)] + [full(a) for a in consts],
        out_specs=[pl.BlockSpec((tm * ROW_CH, LANES), lambda i: (i, 0)),
                   pl.BlockSpec((tm, LANES), lambda i: (i, 0)),
                   pl.BlockSpec((tm, LANES), lambda i: (i, 0))],
        out_shape=[jax.ShapeDtypeStruct((rows * ROW_CH, LANES), F32), jax.ShapeDtypeStruct((rows, LANES), jnp.int32),
                   jax.ShapeDtypeStruct((rows, LANES), F32)],
        compiler_params=_cparams(("arbitrary",), 56),
        name="mix_out",
    )(x_p, x_s, ys_p, ys_p, ys_s, ym_p, ym_p, ym_s, *consts)


MOE_BLK = 128
MOE_TM = 1536
MOE_TF = 256
MOE_ONE_MIN = 6
DMA_UNROLL = 8


def _invert_rows_body(dest_ref, src_ref):
    def zero(i, c):
        for q in range(DMA_UNROLL):
            src_ref[i * DMA_UNROLL + q] = 0
        return c
    lax.fori_loop(0, src_ref.shape[0] // DMA_UNROLL, zero, 0)

    def put(i, c):
        for q in range(DMA_UNROLL):
            src_ref[dest_ref[i * DMA_UNROLL + q]] = i * (DMA_UNROLL // TOP_K) + q // TOP_K
        return c
    lax.fori_loop(0, dest_ref.shape[0] // DMA_UNROLL, put, 0)


def _invert_rows(dest_flat, rows_pad):
    return pl.pallas_call(
        _invert_rows_body,
        in_specs=[pl.BlockSpec(memory_space=pltpu.SMEM)],
        out_specs=pl.BlockSpec(memory_space=pltpu.SMEM),
        out_shape=jax.ShapeDtypeStruct((rows_pad,), jnp.int32),
        name="invert_rows",
    )(dest_flat)


def _moe_routing(top_e, n_tok):
    n_pairs = n_tok * TOP_K
    rows_pad = n_pairs + N_EXPERTS * MOE_BLK
    n_pass = N_EXPERTS + rows_pad // MOE_TM
    onehot = (top_e[:, :, None] == jnp.arange(N_EXPERTS, dtype=jnp.int32)).astype(jnp.int32).sum(1)
    incl = jnp.cumsum(onehot, axis=0)
    counts = incl[-1]
    rank = jnp.take_along_axis(incl - onehot, top_e, axis=1)
    padded = (counts + MOE_BLK - 1) // MOE_BLK * MOE_BLK
    pad_end = jnp.cumsum(padded)
    pad_start = pad_end - padded
    dest = pad_start[top_e] + rank
    src = _invert_rows(dest.reshape(-1).astype(jnp.int32), rows_pad)
    passes_e = (padded + MOE_TM - 1) // MOE_TM
    pass_end = jnp.cumsum(passes_e)
    u = jnp.arange(n_pass, dtype=jnp.int32)
    e_u = jnp.minimum(jnp.searchsorted(pass_end, u, side="right"), N_EXPERTS - 1).astype(jnp.int32)
    j_u = u - (pass_end - passes_e)[e_u]
    rem = padded[e_u] - j_u * MOE_TM
    nblk = jnp.where(u < pass_end[-1], jnp.clip(rem, 0, MOE_TM) // MOE_BLK, 0).astype(jnp.int32)
    blk0 = ((pad_start[e_u] + j_u * MOE_TM) // MOE_BLK).astype(jnp.int32)
    blk0 = jnp.where(nblk > 0, blk0, 0)
    last_e = e_u[jnp.maximum(pass_end[-1] - 1, 0)]
    e_u = jnp.where(nblk > 0, e_u, last_e)
    nblk = jnp.concatenate([nblk, (pad_end[-1:] // MOE_BLK).astype(jnp.int32)])
    return dest.astype(jnp.int32), src, e_u, blk0, nblk


def _moe_ffn_body(e_ref, blk0_ref, nblk_ref, src_ref, h1_ref, wgu_ref, wdn_ref, bgu_ref, bd_ref,
                  ybuf_ref, xg_ref, xb_ref, acc_ref, w32_a, w32_b, wb_a, wb_b, stage_ref, gsem, osem, wsem):
    u = pl.program_id(0)
    n_pass = pl.num_programs(0)
    n_f = D_FF // MOE_TF
    nblk = nblk_ref[u]
    row0 = blk0_ref[u] * MOE_BLK
    blk_rt = MOE_BLK * ROW_CH
    nxt = jnp.minimum(u + 1, n_pass - 1)
    nxt_active = jnp.logical_and(u + 1 < n_pass, nblk_ref[nxt] > 0)
    w32 = (w32_a, w32_b)
    wb = (wb_a, wb_b)

    def w_copies(p, f, par):
        e = e_ref[p]
        cols = pl.ds(pl.multiple_of(f * MOE_TF, MOE_TF), MOE_TF)
        cols_up = pl.ds(pl.multiple_of(D_FF + f * MOE_TF, MOE_TF), MOE_TF)
        g32, u32, d32 = w32[par]
        return (pltpu.make_async_copy(wgu_ref.at[e, :, cols], g32, wsem.at[par]),
                pltpu.make_async_copy(wgu_ref.at[e, :, cols_up], u32, wsem.at[par]),
                pltpu.make_async_copy(wdn_ref.at[e, cols, :], d32, wsem.at[par]))

    def cast_w(par):
        g32, u32, d32 = w32[par]
        w1, w2 = wb[par]
        w1[:, :MOE_TF] = g32[...].astype(BF16)
        w1[:, MOE_TF:] = u32[...].astype(BF16)
        w2[...] = d32[...].astype(BF16)

    def for_all_rows(p, fn):
        base = blk0_ref[p] * MOE_BLK

        def body(rb, c):
            for s in range(SUBLANES):
                tok = src_ref[base + rb * SUBLANES + s]
                fn(pltpu.make_async_copy(h1_ref.at[pl.ds(tok * ROW_CH, ROW_CH)],
                                         xg_ref.at[rb, pl.ds(0, ROW_CH), s], gsem))
            return c
        lax.fori_loop(0, nblk_ref[p] * (MOE_BLK // SUBLANES), body, 0)

    def issue_gather(p):
        for_all_rows(p, lambda cp: cp.start())

    def out_copy(b):
        dst = pl.ds(pl.multiple_of((row0 + b * MOE_BLK) * ROW_CH, blk_rt), blk_rt)
        return pltpu.make_async_copy(stage_ref.at[b % 2], ybuf_ref.at[dst], osem.at[b % 2])

    @pl.when(nblk > 0)
    def _():
        @pl.when(u == 0)
        def _():
            issue_gather(u)
            for cp in w_copies(u, 0, 0):
                cp.start()
            for cp in w_copies(u, 0, 0):
                cp.wait()
            cast_w(0)
            for cp in w_copies(u, 1, 1):
                cp.start()

        for_all_rows(u, lambda cp: cp.wait())

        def cast(b, c):
            rs = pl.ds(pl.multiple_of(b * MOE_BLK, MOE_BLK), MOE_BLK)
            gs = pl.ds(pl.multiple_of(b * (MOE_BLK // SUBLANES), MOE_BLK // SUBLANES), MOE_BLK // SUBLANES)
            for ch in range(ROW_CH):
                xb_ref[rs, ch * LANES:(ch + 1) * LANES] = xg_ref[gs, ch].reshape(MOE_BLK, LANES).astype(BF16)
            acc_ref[rs, :] = jnp.zeros((MOE_BLK, acc_ref.shape[1]), F32)
            return c
        lax.fori_loop(0, nblk, cast, 0)

        def ffn_rows(r0, m, f, par):
            w1, w2 = wb[par]
            bg = bgu_ref[0, :, pl.ds(pl.multiple_of(f * MOE_TF, MOE_TF), MOE_TF)]
            bu = bgu_ref[0, :, pl.ds(pl.multiple_of(D_FF + f * MOE_TF, MOE_TF), MOE_TF)]
            rs = pl.ds(pl.multiple_of(r0, MOE_BLK), m)
            h = jnp.dot(xb_ref[rs, :], w1[...], preferred_element_type=F32)
            x_glu = jnp.minimum(h[:, :MOE_TF] + bg, SWIGLU_LIMIT)
            x_lin = jnp.clip(h[:, MOE_TF:] + bu, -SWIGLU_LIMIT, SWIGLU_LIMIT)
            act = x_glu * jax.nn.sigmoid(SWIGLU_ALPHA * x_glu) * (x_lin + 1.0)
            acc_ref[rs, :] += jnp.dot(act.astype(BF16), w2[...], preferred_element_type=F32)

        def tile_step(f, par):
            in_pass1 = f + 1 < n_f
            in_pass2 = f + 2 < n_f

            @pl.when(jnp.logical_or(in_pass1, nxt_active))
            def _():
                for cp in w_copies(jnp.where(in_pass1, u, nxt), jnp.where(in_pass1, f + 1, 0), 1 - par):
                    cp.wait()

            @pl.when(jnp.logical_or(in_pass2, nxt_active))
            def _():
                for cp in w_copies(jnp.where(in_pass2, u, nxt), jnp.where(in_pass2, f + 2, f + 2 - n_f), par):
                    cp.start()

            @pl.when(jnp.logical_and(f == 1, nxt_active))
            def _():
                issue_gather(nxt)

            for m in range(MOE_ONE_MIN, MOE_TM // MOE_BLK + 1):
                @pl.when(nblk == m)
                def _(m=m):
                    ffn_rows(0, m * MOE_BLK, f, par)
                    cast_w(1 - par)

            @pl.when(nblk < MOE_ONE_MIN)
            def _():
                def one(b, c):
                    ffn_rows(b * MOE_BLK, MOE_BLK, f, par)
                    return c
                lax.fori_loop(0, nblk, one, 0)
                cast_w(1 - par)

        def tile_pair(f2, c):
            tile_step(2 * f2, 0)
            tile_step(2 * f2 + 1, 1)
            return c
        lax.fori_loop(0, n_f // 2, tile_pair, 0)

        def emit(b, c):
            @pl.when(b >= 2)
            def _():
                out_copy(b - 2).wait()
            rs = pl.ds(pl.multiple_of(b * MOE_BLK, MOE_BLK), MOE_BLK)
            val = acc_ref[rs, :] + bd_ref[0]
            for ch in range(ROW_CH):
                stage_ref[b % 2, pl.ds(ch, MOE_BLK, stride=ROW_CH), :] = val[:, ch * LANES:(ch + 1) * LANES]
            out_copy(b).start()
            return c
        lax.fori_loop(0, nblk, emit, 0)

        @pl.when(nblk >= 2)
        def _():
            out_copy(nblk - 2).wait()
        out_copy(nblk - 1).wait()

    @pl.when(u == n_pass - 1)
    def _():
        used = nblk_ref[n_pass]
        total = ybuf_ref.shape[0] // blk_rt
        stage_ref[0] = jnp.zeros((blk_rt, LANES), F32)

        def zero_copy(b):
            dst = pl.ds(pl.multiple_of(b * blk_rt, blk_rt), blk_rt)
            return pltpu.make_async_copy(stage_ref.at[0], ybuf_ref.at[dst], osem.at[0])

        def start(b, c):
            zero_copy(b).start()
            return c
        lax.fori_loop(used, total, start, 0)

        def wait(b, c):
            zero_copy(b).wait()
            return c
        lax.fori_loop(used, total, wait, 0)


def _moe_ffn(h1, src, e_u, blk0, nblk, w_gu, b_gu, w_dn, b_dn, rows_pad):
    d = D_MODEL
    n_pass = e_u.shape[0]
    w32_set = (pltpu.VMEM((d, MOE_TF), F32), pltpu.VMEM((d, MOE_TF), F32), pltpu.VMEM((MOE_TF, d), F32))
    wb_set = (pltpu.VMEM((d, 2 * MOE_TF), BF16), pltpu.VMEM((MOE_TF, d), BF16))
    grid_spec = pltpu.PrefetchScalarGridSpec(
        num_scalar_prefetch=4,
        grid=(n_pass,),
        in_specs=[
            pl.BlockSpec(memory_space=pl.ANY),
            pl.BlockSpec(memory_space=pl.ANY),
            pl.BlockSpec(memory_space=pl.ANY),
            pl.BlockSpec((1, 1, 2 * D_FF), lambda u, e, b0, nb, s: (e[u], 0, 0)),
            pl.BlockSpec((1, 1, d), lambda u, e, b0, nb, s: (e[u], 0, 0)),
        ],
        out_specs=pl.BlockSpec(memory_space=pl.ANY),
        scratch_shapes=[pltpu.VMEM((MOE_TM // SUBLANES, ROW_CH, SUBLANES, LANES), F32), pltpu.VMEM((MOE_TM, d), BF16),
                        pltpu.VMEM((MOE_TM, d), F32), w32_set, w32_set, wb_set, wb_set,
                        pltpu.VMEM((2, MOE_BLK * ROW_CH, LANES), F32),
                        pltpu.SemaphoreType.DMA(()), pltpu.SemaphoreType.DMA((2,)), pltpu.SemaphoreType.DMA((2,))],
    )
    return pl.pallas_call(
        _moe_ffn_body,
        grid_spec=grid_spec,
        out_shape=jax.ShapeDtypeStruct((rows_pad * ROW_CH, LANES), F32),
        compiler_params=_cparams(("arbitrary",), 60),
        name="moe_ffn",
    )(e_u, blk0, nblk, src, h1, w_gu, w_dn, b_gu.reshape(N_EXPERTS, 1, 2 * D_FF), b_dn.reshape(N_EXPERTS, 1, d))


def _moe_combine_body(dest_ref, h1_ref, gate_ref, g2_ref, b2_ref, ybuf_ref, outp_ref, outs_ref,
                      buf_ref, sem, pre_ref):
    i = pl.program_id(0)
    n_pairs = TILE_T * TOP_K
    slot = i % 2

    def copy(tile, sl, rb, s, k):
        src_row = dest_ref[tile * n_pairs + rb * (SUBLANES * TOP_K) + (s * TOP_K + k)]
        return pltpu.make_async_copy(ybuf_ref.at[pl.ds(src_row * ROW_CH, ROW_CH)],
                                     buf_ref.at[sl, k, rb, pl.ds(0, ROW_CH), s], sem.at[sl])

    def for_all_rows(tile, sl, fn):
        def body(rb, c):
            for s in range(SUBLANES):
                for k in range(TOP_K):
                    fn(copy(tile, sl, rb, s, k))
            return c
        lax.fori_loop(0, TILE_T // SUBLANES, body, 0)

    def issue(tile, sl):
        for_all_rows(tile, sl, lambda cp: cp.start())

    @pl.when(i == 0)
    def _():
        issue(0, 0)

    @pl.when(i + 1 < pl.num_programs(0))
    def _():
        issue(i + 1, 1 - slot)

    for_all_rows(i, slot, lambda cp: cp.wait())

    gate = gate_ref[...]
    cs = lambda c: slice(c * LANES, (c + 1) * LANES)
    part = jnp.zeros((TILE_T, LANES), F32)
    gate_b = [jnp.broadcast_to(gate[:, k:k + 1], (TILE_T, LANES)) for k in range(TOP_K)]
    for c in range(ROW_CH):
        fc = None
        for k in range(TOP_K):
            v = buf_ref[slot, k, :, c].reshape(TILE_T, LANES) * gate_b[k]
            fc = v if fc is None else fc + v
        pre = DEEPNORM_ALPHA * _load_row_tiles(h1_ref, 0, TILE_T, c) + fc
        pre_ref[:, cs(c)] = pre
        part = part + pre
    mu = jnp.sum(part, axis=1, keepdims=True) * (1.0 / D_MODEL)
    part = jnp.zeros((TILE_T, LANES), F32)
    for c in range(ROW_CH):
        dlt = pre_ref[:, cs(c)] - mu
        part = part + dlt * dlt
    rstd = lax.rsqrt(jnp.sum(part, axis=1, keepdims=True) * (1.0 / D_MODEL) + LN_EPS)
    is_step = i == pl.num_programs(0) - 1

    def write(out_ref):
        for c in range(ROW_CH):
            out_ref[:, cs(c)] = (pre_ref[:, cs(c)] - mu) * rstd * g2_ref[:, cs(c)] + b2_ref[:, cs(c)]

    @pl.when(jnp.logical_not(is_step))
    def _():
        write(outp_ref)

    @pl.when(is_step)
    def _():
        write(outs_ref)


def _moe_combine(dest, h1, gates, g2, b2, ybuf, n_tok):
    d = D_MODEL
    n_tiles = n_tok // TILE_T
    grid_spec = pltpu.PrefetchScalarGridSpec(
        num_scalar_prefetch=1,
        grid=(n_tiles,),
        in_specs=[pl.BlockSpec((TILE_T * ROW_CH, LANES), lambda i, s: (i, 0)),
                  pl.BlockSpec((TILE_T, LANES), lambda i, s: (i, 0)),
                  pl.BlockSpec((1, d), lambda i, s: (0, 0)),
                  pl.BlockSpec((1, d), lambda i, s: (0, 0)),
                  pl.BlockSpec(memory_space=pl.ANY)],
        out_specs=[pl.BlockSpec((TILE_T, d), lambda i, s: (jnp.minimum(i, n_tiles - 2), 0)),
                   pl.BlockSpec((TILE_T, d), lambda i, s: (0, 0))],
        scratch_shapes=[pltpu.VMEM((2, TOP_K, TILE_T // SUBLANES, ROW_CH, SUBLANES, LANES), F32),
                        pltpu.SemaphoreType.DMA((2,)),
                        pltpu.VMEM((TILE_T, d), F32)],
    )
    return pl.pallas_call(
        _moe_combine_body,
        grid_spec=grid_spec,
        out_shape=[jax.ShapeDtypeStruct((n_tok - TILE_T, d), F32), jax.ShapeDtypeStruct((TILE_T, d), F32)],
        compiler_params=_cparams(("arbitrary",), 48),
        name="moe_combine",
    )(dest.reshape(-1), h1, gates, g2, b2, ybuf)


def kernel(x_prompt, x_sample, state_s5_re, state_s5_im, state_mlstm_c, state_mlstm_n, state_mlstm_m, state_mlstm_conv, meta_tokens, ln_in_g, ln_in_b, w_in, b_in, s5_a_re, s5_a_im, s5_log_dt, s5_b_re, s5_b_im, s5_c_re, s5_c_im, s5_d, s5_w_glu, mlstm_conv_w, mlstm_conv_b, mlstm_wq, mlstm_wk, mlstm_norm_g, w_out, ln1_g, ln1_b, router_w, router_b, w_gate_up, b_gate_up, w_down, b_down, ln2_g, ln2_b):
    bsz, seq, d = x_prompt.shape
    dec_b = x_sample.shape[0]
    n_pad = TILE_T - N_META
    x_small = jnp.concatenate([jnp.zeros((n_pad, d), F32), meta_tokens, x_sample.reshape(dec_b, d)], axis=0)
    w_in_p = jnp.pad(w_in[0], ((0, 0), (0, N_IN_PAD - N_IN))).astype(BF16)
    b_in_p = jnp.pad(b_in[0], (0, N_IN_PAD - N_IN)).reshape(1, N_IN_PAD)
    g_in = ln_in_g.reshape(1, d)
    bb_in = ln_in_b.reshape(1, d)
    z_p = _inproj(x_prompt.reshape(bsz * seq, d), g_in, bb_in, w_in_p, b_in_p, 512, 1408)
    z_s = _inproj(x_small, g_in, bb_in, w_in_p, b_in_p, 256, 1408)
    tabs = _s5_prep(s5_a_re[0], s5_a_im[0], s5_log_dt[0], s5_b_re[0], s5_b_im[0])
    bd_b, bd_c = _s5_block_diag(tabs[6], tabs[7], s5_c_re[0], s5_c_im[0])
    wglu_bf = s5_w_glu[0].astype(BF16)
    d_skip = s5_d[0].reshape(1, D_S5)
    n_tiles = seq // TILE_T + 1
    zero_state = jnp.zeros((bsz, 1, N_STATE), F32)
    y_s5_p, s5r_p, s5i_p = _s5_seq(z_s, z_p, zero_state, zero_state, tabs[:6], bd_b, bd_c, d_skip, wglu_bf,
                                   bsz, n_tiles, n_pad)
    y_s5_s, s5r_s, s5i_s = _s5_step(z_s, 1, state_s5_re[0].reshape(dec_b, N_STATE),
                                    state_s5_im[0].reshape(dec_b, N_STATE), tabs[:6], bd_b, bd_c, d_skip, wglu_bf)

    conv_w = mlstm_conv_w[0]
    conv_b = mlstm_conv_b[0].reshape(1, D_ML)
    wq_bf = mlstm_wq[0].astype(BF16)
    wk_bf = mlstm_wk[0].astype(BF16)
    norm_g = mlstm_norm_g[0].reshape(1, D_ML)
    y_ml_p, c_p, n_p, m_p = _mlstm_seq(z_s, z_p, conv_w, conv_b, wq_bf, wk_bf, norm_g, bsz, n_tiles, n_pad)
    conv0 = state_mlstm_conv[0]
    y_ml_s, c_s, n_s, m_s = _mlstm_step(z_s, 1, conv0, state_mlstm_c[0], state_mlstm_n[0].reshape(dec_b, D_ML),
                                        state_mlstm_m[0], conv_w, conv_b, wq_bf, wk_bf, norm_g)

    rw = jnp.pad(router_w[0], ((0, 0), (0, LANES - N_EXPERTS)))
    rw_hi = rw.astype(BF16)
    rw_lo = (rw - rw_hi.astype(F32)).astype(BF16)
    rb = jnp.pad(router_b[0], (0, LANES - N_EXPERTS)).reshape(1, LANES)
    consts = (g_in, bb_in, w_out[0].astype(BF16), ln1_g[0].reshape(1, d), ln1_b[0].reshape(1, d), rw_hi, rw_lo, rb)
    h1, top_e, gates = _mix_out(x_prompt.reshape(bsz * seq, d), x_sample.reshape(dec_b, d),
                                y_s5_p, y_s5_s, y_ml_p, y_ml_s, consts, seq // TILE_T)
    n_tok = bsz * seq + dec_b
    dest, src, e_u, blk0, nblk = _moe_routing(top_e[:n_tok, :TOP_K], n_tok)
    ybuf = _moe_ffn(h1, src, e_u, blk0, nblk, w_gate_up[0], b_gate_up[0], w_down[0], b_down[0], src.shape[0])
    out_p, out_s = _moe_combine(dest, h1, gates, ln2_g[0].reshape(1, d), ln2_b[0].reshape(1, d), ybuf, n_tok)

    y_prompt = out_p.reshape(bsz, seq, d)
    y_sample = out_s.reshape(dec_b, 1, d)
    xm_p = z_p.reshape(bsz, seq, N_IN_PAD)[:, seq - (CONV_W - 1):, D_S5:D_S5 + D_ML]
    xm_s = z_s[TILE_T:, D_S5:D_S5 + D_ML]
    conv_s = jnp.concatenate([conv0[:, 1:], xm_s[:, None, :]], axis=1)
    return (y_prompt, y_sample,
            s5r_p.reshape(1, bsz, N_GROUPS, S5_STATE), s5i_p.reshape(1, bsz, N_GROUPS, S5_STATE),
            c_p[None], n_p.reshape(1, bsz, N_HEADS, DH), m_p[None, :, :N_HEADS, 0], xm_p[None],
            s5r_s.reshape(1, dec_b, N_GROUPS, S5_STATE), s5i_s.reshape(1, dec_b, N_GROUPS, S5_STATE),
            c_s[None], n_s.reshape(1, dec_b, N_HEADS, DH), m_s[None], conv_s[None])
```

```python
import functools
import math

import jax
import jax.numpy as jnp
from jax import lax
from jax.experimental import pallas as pl
from jax.experimental.pallas import tpu as pltpu

F32 = jnp.float32
BF16 = jnp.bfloat16

D_MODEL = 2048
N_META = 16
D_S5 = 1024
D_ML = 1024
S5_CH = 16
N_GROUPS = 64
S5_STATE = 64
N_STATE = N_GROUPS * S5_STATE
N_HEADS = 4
DH = 256
CONV_W = 4
N_EXPERTS = 32
TOP_K = 4
D_FF = 2048
SWIGLU_LIMIT = 7.0
SWIGLU_ALPHA = 1.702
LN_EPS = 1e-5
DEEPNORM_ALPHA = 2.0 ** 0.25
N_IN = D_S5 + 3 * D_ML + 2 * N_HEADS

LANES = 128
SUBLANES = 8
MXU_DIM = 256
MXU_ROWS = 16

TILE_T = 128
SEG_LEN = TILE_T // SUBLANES
N_IN_PAD = 33 * LANES
GATE_COL = 4 * D_S5 // LANES
S5_KCH = D_S5 // MXU_DIM
S5_SCH = N_STATE // S5_KCH
SCAN_LW = 256


def _cparams(sem, vmem_mb=None):
    kw = dict(dimension_semantics=sem)
    if vmem_mb is not None:
        kw["vmem_limit_bytes"] = vmem_mb * 1024 * 1024
    return pltpu.CompilerParams(**kw)


def _layer_norm(x, g, b):
    mu = jnp.mean(x, axis=-1, keepdims=True)
    xc = x - mu
    var = jnp.mean(xc * xc, axis=-1, keepdims=True)
    return xc * lax.rsqrt(var + LN_EPS) * g + b


def _inproj_body(x_ref, g_ref, b_ref, w_ref, bias_ref, z_ref, hn_ref):
    @pl.when(pl.program_id(1) == 0)
    def _():
        hn_ref[...] = _layer_norm(x_ref[...], g_ref[...], b_ref[...]).astype(BF16)

    z_ref[...] = jnp.dot(hn_ref[...], w_ref[...], preferred_element_type=F32) + bias_ref[...]


def _inproj(x, g, b, w, bias, tm, tn):
    rows, d = x.shape
    n = w.shape[1]
    return pl.pallas_call(
        _inproj_body,
        grid=(rows // tm, n // tn),
        in_specs=[
            pl.BlockSpec((tm, d), lambda i, j: (i, 0)),
            pl.BlockSpec((1, d), lambda i, j: (0, 0)),
            pl.BlockSpec((1, d), lambda i, j: (0, 0)),
            pl.BlockSpec((d, tn), lambda i, j: (0, j)),
            pl.BlockSpec((1, tn), lambda i, j: (0, j)),
        ],
        out_specs=pl.BlockSpec((tm, tn), lambda i, j: (i, j)),
        out_shape=jax.ShapeDtypeStruct((rows, n), F32),
        scratch_shapes=[pltpu.VMEM((tm, d), BF16)],
        compiler_params=_cparams(("arbitrary", "arbitrary"), 48),
        name="inproj",
    )(x, g, b, w, bias)


def _cmul(ar, ai, br, bi):
    return ar * br - ai * bi, ar * bi + ai * br


def _s5_prep_body(are_ref, aim_ref, dt_ref, bre_ref, bim_ref,
                  pre_ref, pim_ref, hre_ref, him_ref, qre_ref, qim_ref, bbre_ref, bbim_ref):
    lr = are_ref[...]
    li = aim_ref[...]
    dt = jnp.exp(dt_ref[...])
    mag = jnp.exp(lr * dt)
    ar = mag * jnp.cos(li * dt)
    ai = mag * jnp.sin(li * dt)
    nr = ar - 1.0
    ni = ai
    den = lr * lr + li * li
    cr = (nr * lr + ni * li) / den
    ci = (ni * lr - nr * li) / den
    bbre_ref[...] = cr * bre_ref[...] - ci * bim_ref[...]
    bbim_ref[...] = cr * bim_ref[...] + ci * bre_ref[...]
    pr, pi = ar, ai
    for j in range(SEG_LEN):
        pre_ref[j:j + 1, :] = pr
        pim_ref[j:j + 1, :] = pi
        if j + 1 < SEG_LEN:
            pr, pi = _cmul(pr, pi, ar, ai)
    row = lax.broadcasted_iota(jnp.int32, (SUBLANES, N_STATE), 0)
    kr, ki = pr, pi
    for idx, k in enumerate((1, 2, 4)):
        hre_ref[idx * 8:(idx + 1) * 8, :] = jnp.where(row >= k, jnp.broadcast_to(kr, (SUBLANES, N_STATE)), 0.0)
        him_ref[idx * 8:(idx + 1) * 8, :] = jnp.where(row >= k, jnp.broadcast_to(ki, (SUBLANES, N_STATE)), 0.0)
        kr, ki = _cmul(kr, ki, kr, ki)
    qr, qi = pr, pi
    for s in range(SUBLANES):
        qre_ref[s:s + 1, :] = qr
        qim_ref[s:s + 1, :] = qi
        if s + 1 < SUBLANES:
            qr, qi = _cmul(qr, qi, pr, pi)


def _s5_prep(a_re, a_im, log_dt, b_re, b_im):
    n = N_STATE
    are = a_re.reshape(1, n)
    aim = a_im.reshape(1, n)
    dtl = jnp.broadcast_to(log_dt[:, None], (N_GROUPS, S5_STATE)).reshape(1, n)
    bre = b_re.transpose(2, 0, 1).reshape(S5_CH, n)
    bim = b_im.transpose(2, 0, 1).reshape(S5_CH, n)
    shp = lambda r: jax.ShapeDtypeStruct((r, n), F32)
    return pl.pallas_call(
        _s5_prep_body,
        out_shape=(shp(SEG_LEN), shp(SEG_LEN), shp(24), shp(24), shp(8), shp(8), shp(S5_CH), shp(S5_CH)),
        name="s5_prep",
    )(are, aim, dtl, bre, bim)


def _s5_block_diag(bb_re, bb_im, c_re, c_im):
    gpc = 16
    eye = jnp.eye(gpc, dtype=bool)

    def bd_in(bb):
        t = bb.reshape(S5_CH, S5_KCH, gpc, S5_STATE).transpose(1, 2, 0, 3)
        t = jnp.where(eye[None, :, None, :, None], t[:, :, :, None, :], 0.0)
        return t.reshape(S5_KCH, gpc * S5_CH, gpc * S5_STATE)

    def bd_out(c):
        t = c.reshape(S5_KCH, gpc, S5_CH, S5_STATE).transpose(0, 1, 3, 2)
        t = jnp.where(eye[None, :, None, :, None], t[:, :, :, None, :], 0.0)
        return t.reshape(S5_KCH, gpc * S5_STATE, gpc * S5_CH)

    bd_b = jnp.concatenate([bd_in(bb_re), bd_in(bb_im)], axis=2).astype(BF16)
    bd_c = jnp.concatenate([bd_out(c_re), bd_out(-c_im)], axis=1).astype(BF16)
    return bd_b, bd_c


def _gelu_glu(y, wglu_ref):
    y = 0.5 * y * (1.0 + lax.erf(y * math.sqrt(0.5)))
    gate = jnp.dot(y.astype(BF16), wglu_ref[...], preferred_element_type=F32)
    return y * jax.nn.sigmoid(gate)


def _s5_in_proj(u_bf, bdb_ref, bur_ref, bui_ref):
    for c in range(S5_KCH):
        r = jnp.dot(u_bf[:, c * MXU_DIM:(c + 1) * MXU_DIM], bdb_ref[c], preferred_element_type=F32)
        bur_ref[:, c * S5_SCH:(c + 1) * S5_SCH] = r[:, :S5_SCH]
        bui_ref[:, c * S5_SCH:(c + 1) * S5_SCH] = r[:, S5_SCH:]


def _s5_out_proj(xr_ref, xi_ref, bdc_ref):
    ys = []
    for c in range(S5_KCH):
        xr = xr_ref[:, c * S5_SCH:(c + 1) * S5_SCH].astype(BF16)
        xi = xi_ref[:, c * S5_SCH:(c + 1) * S5_SCH].astype(BF16)
        ys.append(jnp.dot(xr, bdc_ref[c, :S5_SCH, :], preferred_element_type=F32)
                  + jnp.dot(xi, bdc_ref[c, S5_SCH:, :], preferred_element_type=F32))
    return jnp.concatenate(ys, axis=1)


def _s5_seq_body(n_pad, us_ref, up_ref, x0r_ref, x0i_ref, bdb_ref, bdc_ref, pre_ref, pim_ref,
                 hre_ref, him_ref, qre_ref, qim_ref, d_ref, wglu_ref,
                 y_ref, xr_out_ref, xi_out_ref, bur_ref, bui_ref, cr_ref, ci_ref, perm_ref):
    t = pl.program_id(1)
    n = N_STATE

    @pl.when(t == 0)
    def _():
        cr_ref[...] = jnp.broadcast_to(x0r_ref[0], (SUBLANES, n))
        ci_ref[...] = jnp.broadcast_to(x0i_ref[0], (SUBLANES, n))

    n_lc = D_S5 // LANES

    def load_perm(ref):
        for c in range(n_lc):
            perm_ref[c] = ref[:, c * LANES:(c + 1) * LANES]
        return jnp.concatenate(
            [jnp.concatenate([perm_ref[c, pl.ds(j, SUBLANES, stride=SEG_LEN), :] for c in range(n_lc)], axis=1)
             for j in range(SEG_LEN)], axis=0)

    prow = lax.broadcasted_iota(jnp.int32, (TILE_T, 1), 0)
    time = (prow % SUBLANES) * SEG_LEN + prow // SUBLANES
    u_first = jnp.where(time >= n_pad, load_perm(us_ref), 0.0)
    u = jnp.where(t == 0, u_first, load_perm(up_ref))
    _s5_in_proj(u.astype(BF16), bdb_ref, bur_ref, bui_ref)

    row8 = lax.broadcasted_iota(jnp.int32, (SUBLANES, SCAN_LW), 0)

    def scan_lanes(lc, carry):
        ls = pl.ds(pl.multiple_of(lc * SCAN_LW, SCAN_LW), SCAN_LW)
        bc = lambda ref, j: jnp.broadcast_to(ref[j:j + 1, ls], (SUBLANES, SCAN_LW))
        ar, ai = bc(pre_ref, 0), bc(pim_ref, 0)
        xr = jnp.zeros((SUBLANES, SCAN_LW), F32)
        xi = jnp.zeros((SUBLANES, SCAN_LW), F32)
        for j in range(SEG_LEN):
            rs = slice(j * SUBLANES, (j + 1) * SUBLANES)
            nr = ar * xr - ai * xi + bur_ref[rs, ls]
            ni = ar * xi + ai * xr + bui_ref[rs, ls]
            xr, xi = nr, ni
            bur_ref[rs, ls] = xr
            bui_ref[rs, ls] = xi
        er, ei = xr, xi
        for idx, k in enumerate((1, 2, 4)):
            sr = pltpu.roll(er, k, axis=0)
            si = pltpu.roll(ei, k, axis=0)
            hr = hre_ref[idx * 8:(idx + 1) * 8, ls]
            hi = him_ref[idx * 8:(idx + 1) * 8, ls]
            er, ei = er + (hr * sr - hi * si), ei + (hr * si + hi * sr)
        cpr = cr_ref[:, ls]
        cpi = ci_ref[:, ls]
        qr = qre_ref[:, ls]
        qi = qim_ref[:, ls]
        er, ei = er + (qr * cpr - qi * cpi), ei + (qr * cpi + qi * cpr)
        inr = jnp.where(row8 == 0, cpr, pltpu.roll(er, 1, axis=0))
        ini = jnp.where(row8 == 0, cpi, pltpu.roll(ei, 1, axis=0))
        cr_ref[:, ls] = jnp.broadcast_to(er[SUBLANES - 1:SUBLANES, :], (SUBLANES, SCAN_LW))
        ci_ref[:, ls] = jnp.broadcast_to(ei[SUBLANES - 1:SUBLANES, :], (SUBLANES, SCAN_LW))
        for j in range(SEG_LEN):
            rs = slice(j * SUBLANES, (j + 1) * SUBLANES)
            pr, pi = bc(pre_ref, j), bc(pim_ref, j)
            bur_ref[rs, ls] = bur_ref[rs, ls] + (pr * inr - pi * ini)
            bui_ref[rs, ls] = bui_ref[rs, ls] + (pr * ini + pi * inr)
        return carry

    lax.fori_loop(0, n // SCAN_LW, scan_lanes, 0)

    y = _s5_out_proj(bur_ref, bui_ref, bdc_ref) + d_ref[...] * u
    y = _gelu_glu(y, wglu_ref)
    for j in range(SEG_LEN):
        for c in range(n_lc):
            perm_ref[c, pl.ds(j, SUBLANES, stride=SEG_LEN), :] = y[j * SUBLANES:(j + 1) * SUBLANES,
                                                                   c * LANES:(c + 1) * LANES]
    for c in range(n_lc):
        y_ref[:, c * LANES:(c + 1) * LANES] = perm_ref[c]

    @pl.when(t == pl.num_programs(1) - 1)
    def _():
        xr_out_ref[0] = cr_ref[0:1, :]
        xi_out_ref[0] = ci_ref[0:1, :]


def _s5_seq(z_small, z_seq, x0r, x0i, tabs, bd_b, bd_c, d_skip, wglu_bf, n_batch, n_tiles, n_pad):
    pre, pim, hre, him, qre, qim = tabs
    n = N_STATE
    full = lambda a: pl.BlockSpec(a.shape, lambda b, t: (0,) * a.ndim)
    per_seq = n_tiles - 1
    return pl.pallas_call(
        functools.partial(_s5_seq_body, n_pad),
        grid=(n_batch, n_tiles),
        in_specs=[
            pl.BlockSpec((TILE_T, D_S5), lambda b, t: (0, 0)),
            pl.BlockSpec((TILE_T, D_S5), lambda b, t: (b * per_seq + jnp.maximum(t - 1, 0), 0)),
            pl.BlockSpec((1, 1, n), lambda b, t: (b, 0, 0)),
            pl.BlockSpec((1, 1, n), lambda b, t: (b, 0, 0)),
            full(bd_b), full(bd_c), full(pre), full(pim), full(hre), full(him), full(qre), full(qim),
            full(d_skip), full(wglu_bf),
        ],
        out_specs=[
            pl.BlockSpec((TILE_T, D_S5), lambda b, t: (b * n_tiles + t, 0)),
            pl.BlockSpec((1, 1, n), lambda b, t: (b, 0, 0)),
            pl.BlockSpec((1, 1, n), lambda b, t: (b, 0, 0)),
        ],
        out_shape=[
            jax.ShapeDtypeStruct((n_batch * n_tiles * TILE_T, D_S5), F32),
            jax.ShapeDtypeStruct((n_batch, 1, n), F32),
            jax.ShapeDtypeStruct((n_batch, 1, n), F32),
        ],
        scratch_shapes=[pltpu.VMEM((TILE_T, n), F32), pltpu.VMEM((TILE_T, n), F32),
                        pltpu.VMEM((SUBLANES, n), F32), pltpu.VMEM((SUBLANES, n), F32),
                        pltpu.VMEM((D_S5 // LANES, TILE_T, LANES), F32)],
        compiler_params=_cparams(("arbitrary", "arbitrary"), 48),
        name="s5_seq",
    )(z_small, z_seq, x0r, x0i, bd_b, bd_c, pre, pim, hre, him, qre, qim, d_skip, wglu_bf)


def _s5_step_body(u_ref, x0r_ref, x0i_ref, bdb_ref, bdc_ref, pre_ref, pim_ref, d_ref, wglu_ref,
                  y_ref, xr_ref, xi_ref):
    u = u_ref[...]
    _s5_in_proj(u.astype(BF16), bdb_ref, xr_ref, xi_ref)
    ar = pre_ref[0:1, :]
    ai = pim_ref[0:1, :]
    x0r = x0r_ref[...]
    x0i = x0i_ref[...]
    xr_ref[...] = xr_ref[...] + (ar * x0r - ai * x0i)
    xi_ref[...] = xi_ref[...] + (ar * x0i + ai * x0r)
    y = _s5_out_proj(xr_ref, xi_ref, bdc_ref) + d_ref[...] * u
    y_ref[...] = _gelu_glu(y, wglu_ref)


def _s5_step(z_small, row_blk, x0r, x0i, tabs, bd_b, bd_c, d_skip, wglu_bf):
    pre, pim = tabs[0], tabs[1]
    rows = x0r.shape[0]
    n = N_STATE
    full = lambda a: pl.BlockSpec(a.shape, lambda i: (0,) * a.ndim)
    return pl.pallas_call(
        _s5_step_body,
        grid=(1,),
        in_specs=[pl.BlockSpec((rows, D_S5), lambda i: (row_blk, 0)), full(x0r), full(x0i),
                  full(bd_b), full(bd_c), full(pre), full(pim), full(d_skip), full(wglu_bf)],
        out_specs=[pl.BlockSpec((rows, D_S5), lambda i: (0, 0)),
                   pl.BlockSpec((rows, n), lambda i: (0, 0)), pl.BlockSpec((rows, n), lambda i: (0, 0))],
        out_shape=[jax.ShapeDtypeStruct((rows, D_S5), F32), jax.ShapeDtypeStruct((rows, n), F32),
                   jax.ShapeDtypeStruct((rows, n), F32)],
        compiler_params=_cparams(("arbitrary",), 48),
        name="s5_step",
    )(z_small, x0r, x0i, bd_b, bd_c, pre, pim, d_skip, wglu_bf)


NEG_INF = float("-inf")
N_STEP_SCALARS = 5 * N_HEADS


def _log_sigmoid(x):
    return jnp.minimum(x, 0.0) - jnp.log1p(jnp.exp(-jnp.abs(x)))


def _split3(x):
    hi = x.astype(BF16)
    r1 = x - hi.astype(F32)
    mid = r1.astype(BF16)
    lo = (r1 - mid.astype(F32)).astype(BF16)
    return hi, mid, lo


def _head_norm_gate(h, o, g):
    mu = jnp.mean(h, axis=-1, keepdims=True)
    hc = h - mu
    var = jnp.mean(hc * hc, axis=-1, keepdims=True)
    return jax.nn.sigmoid(o) * (hc * lax.rsqrt(var + LN_EPS) * g)


def _dot_nt(a, b):
    return lax.dot_general(a, b, (((1,), (1,)), ((), ())), preferred_element_type=F32)


def _dot_tn(a, b):
    return lax.dot_general(a, b, (((0,), (0,)), ((), ())), preferred_element_type=F32)


def _mlstm_seq_body(n_pad, sxm_ref, sv_ref, so_ref, sg_ref, pxm_ref, pv_ref, po_ref, pg_ref,
                    cw_ref, cb_ref, wq_ref, wk_ref, ng_ref,
                    y_ref, c_out_ref, n_out_ref, m_out_ref, c_ref, n_ref, m_ref, prev_ref):
    t = pl.program_id(1)
    L = TILE_T

    @pl.when(t == 0)
    def _():
        c_ref[...] = jnp.zeros_like(c_ref)
        n_ref[...] = jnp.zeros_like(n_ref)
        m_ref[...] = jnp.zeros_like(m_ref)
        prev_ref[...] = jnp.zeros_like(prev_ref)

    first = t == 0
    row = lax.broadcasted_iota(jnp.int32, (L, 1), 0)
    valid = jnp.logical_or(jnp.logical_not(first), row >= n_pad)
    xm = jnp.where(valid, jnp.where(first, sxm_ref[...], pxm_ref[...]), 0.0)
    v = jnp.where(first, sv_ref[...], pv_ref[...])
    o = jnp.where(first, so_ref[...], po_ref[...])
    gt = jnp.where(first, sg_ref[...], pg_ref[...])

    prev = prev_ref[...]

    def shifted(j):
        if j == 0:
            return xm
        return pltpu.roll(jnp.where(row >= L - j, prev, xm), j, axis=0)

    xc = cb_ref[...]
    for j in range(CONV_W):
        xc = xc + shifted(CONV_W - 1 - j) * cw_ref[j:j + 1, :]
    prev_ref[...] = xm
    xc = xc * jax.nn.sigmoid(xc)

    ig = jnp.where(valid, gt, NEG_INF)
    lf = jnp.where(valid, _log_sigmoid(gt), 0.0)
    ti = lax.broadcasted_iota(jnp.int32, (L, L), 0)
    si = lax.broadcasted_iota(jnp.int32, (L, L), 1)
    causal = si <= ti
    tri = jnp.where(causal, 1.0, 0.0).astype(BF16)
    bc = sum(jnp.dot(tri, p, preferred_element_type=F32) for p in _split3(lf))
    ig_t = ig.T
    bc_t = bc.T

    for h in range(N_HEADS):
        hs = slice(h * DH, (h + 1) * DH)
        b_col = bc[:, N_HEADS + h:N_HEADS + h + 1]
        b_row = bc_t[N_HEADS + h:N_HEADS + h + 1, :]
        ig_row = ig_t[h:h + 1, :]
        ig_col = ig[:, h:h + 1]
        m_prev = m_ref[h:h + 1, 0:1]
        dlog = jnp.where(causal, b_col - b_row + ig_row, NEG_INF)
        inter = b_col + m_prev
        m_t = jnp.maximum(jnp.max(dlog, axis=1, keepdims=True), inter)
        w = jnp.exp(dlog - m_t)
        g = jnp.exp(inter - m_t)
        xh = xc[:, hs].astype(BF16)
        q = jnp.dot(xh, wq_ref[h], preferred_element_type=F32)
        k = jnp.dot(xh, wk_ref[h], preferred_element_type=F32) * (DH ** -0.5)
        qb = q.astype(BF16)
        kb = k.astype(BF16)
        s = _dot_nt(qb, kb) * w
        vh = v[:, hs]
        cmat = c_ref[h]
        n_row = n_ref[h]
        num = jnp.dot(s.astype(BF16), vh.astype(BF16), preferred_element_type=F32) \
            + g * _dot_nt(qb, cmat.astype(BF16))
        den = jnp.sum(s, axis=1, keepdims=True) + g * jnp.sum(q * n_row, axis=1, keepdims=True)
        hh = num / jnp.maximum(jnp.abs(den), jnp.exp(-m_t))
        b_last = b_col[L - 1:L, :]
        wlog = b_last - b_col + ig_col
        m_new = jnp.maximum(b_last + m_prev, jnp.max(wlog, axis=0, keepdims=True))
        w_end = jnp.exp(wlog - m_new)
        g_end = jnp.exp(b_last + m_prev - m_new)
        c_ref[h] = g_end * cmat + _dot_tn((vh * w_end).astype(BF16), kb)
        n_ref[h] = g_end * n_row + jnp.sum(w_end * k, axis=0, keepdims=True)
        m_ref[h:h + 1, :] = jnp.broadcast_to(m_new, (1, LANES))
        y_ref[:, hs] = _head_norm_gate(hh, o[:, hs], ng_ref[:, hs])

    @pl.when(t == pl.num_programs(1) - 1)
    def _():
        c_out_ref[0] = c_ref[...]
        n_out_ref[0] = n_ref[...]
        m_out_ref[0] = m_ref[...]


def _mlstm_seq(z_small, z_seq, conv_w, conv_b, wq_bf, wk_bf, norm_g, n_batch, n_tiles, n_pad):
    per_seq = n_tiles - 1
    full = lambda a: pl.BlockSpec(a.shape, lambda b, t: (0,) * a.ndim)
    nb = D_ML // LANES

    def small(col, width):
        return pl.BlockSpec((TILE_T, width), lambda b, t: (0, col))

    def seq(col, width):
        return pl.BlockSpec((TILE_T, width), lambda b, t: (b * per_seq + jnp.maximum(t - 1, 0), col))

    return pl.pallas_call(
        functools.partial(_mlstm_seq_body, n_pad),
        grid=(n_batch, n_tiles),
        in_specs=[small(1, D_ML), small(2, D_ML), small(3, D_ML), small(GATE_COL, LANES),
                  seq(1, D_ML), seq(2, D_ML), seq(3, D_ML), seq(GATE_COL, LANES),
                  full(conv_w), full(conv_b), full(wq_bf), full(wk_bf), full(norm_g)],
        out_specs=[
            pl.BlockSpec((TILE_T, D_ML), lambda b, t: (b * n_tiles + t, 0)),
            pl.BlockSpec((1, N_HEADS, DH, DH), lambda b, t: (b, 0, 0, 0)),
            pl.BlockSpec((1, N_HEADS, 1, DH), lambda b, t: (b, 0, 0, 0)),
            pl.BlockSpec((1, SUBLANES, LANES), lambda b, t: (b, 0, 0)),
        ],
        out_shape=[
            jax.ShapeDtypeStruct((n_batch * n_tiles * TILE_T, D_ML), F32),
            jax.ShapeDtypeStruct((n_batch, N_HEADS, DH, DH), F32),
            jax.ShapeDtypeStruct((n_batch, N_HEADS, 1, DH), F32),
            jax.ShapeDtypeStruct((n_batch, SUBLANES, LANES), F32),
        ],
        scratch_shapes=[pltpu.VMEM((N_HEADS, DH, DH), F32), pltpu.VMEM((N_HEADS, 1, DH), F32),
                        pltpu.VMEM((SUBLANES, LANES), F32), pltpu.VMEM((TILE_T, D_ML), F32)],
        compiler_params=_cparams(("arbitrary", "arbitrary"), 48),
        name="mlstm_seq",
    )(z_small, z_small, z_small, z_small, z_seq, z_seq, z_seq, z_seq, conv_w, conv_b, wq_bf, wk_bf, norm_g)


def _mlstm_step_a_body(xm_ref, g_ref, conv0_ref, m0_ref, cw_ref, cb_ref, wq_ref, wk_ref,
                       q_ref, k_ref, sc_ref):
    xc = cb_ref[...]
    for j in range(CONV_W - 1):
        xc = xc + conv0_ref[j] * cw_ref[j:j + 1, :]
    xc = xc + xm_ref[...] * cw_ref[CONV_W - 1:CONV_W, :]
    xc = xc * jax.nn.sigmoid(xc)
    gt = g_ref[...]
    ig = gt[:, 0:N_HEADS]
    lf = _log_sigmoid(gt[:, N_HEADS:2 * N_HEADS])
    inter = lf + m0_ref[...]
    m_t = jnp.maximum(ig, inter)
    w = jnp.exp(ig - m_t)
    g = jnp.exp(inter - m_t)
    qks = []
    for h in range(N_HEADS):
        hs = slice(h * DH, (h + 1) * DH)
        xh = xc[:, hs].astype(BF16)
        q = jnp.dot(xh, wq_ref[h], preferred_element_type=F32)
        k = jnp.dot(xh, wk_ref[h], preferred_element_type=F32) * (DH ** -0.5)
        q_ref[:, hs] = q
        k_ref[:, hs] = k
        qks.append(jnp.sum(q * k, axis=1, keepdims=True))
    s = jnp.concatenate(qks, axis=1) * w
    rows = s.shape[0]
    sc_ref[...] = jnp.concatenate(
        [s, w, g, m_t, jnp.exp(-m_t), jnp.zeros((rows, LANES - 5 * N_HEADS), F32)], axis=1)


def _mlstm_step_b_body(bb, sc_ref, q_ref, k_ref, n_ref, v_ref, o_ref, ng_ref, c_ref,
                       y_ref, c_out_ref, n_out_ref):
    i0 = pl.program_id(0) * bb
    row = lax.broadcasted_iota(jnp.int32, (MXU_ROWS, DH), 0)

    def hi_lo(x):
        hi = x.astype(BF16).astype(F32)
        return jnp.broadcast_to(hi, (MXU_ROWS, DH)), jnp.broadcast_to(x - hi, (MXU_ROWS, DH))

    for i in range(bb):
        for h in range(N_HEADS):
            hs = slice(h * DH, (h + 1) * DH)
            base = (i0 + i) * N_STEP_SCALARS
            s = sc_ref[base + h]
            w = sc_ref[base + N_HEADS + h]
            g = sc_ref[base + 2 * N_HEADS + h]
            em = sc_ref[base + 4 * N_HEADS + h]
            rsel = pl.ds(i0 + i, 1)
            q_row = q_ref[rsel, hs]
            k_row = k_ref[rsel, hs]
            n_row = n_ref[rsel, hs]
            v_row = v_ref[rsel, hs]
            cmat = c_ref[i, h]
            qh, ql = hi_lo(q_row)
            qmat = jnp.where(row == 0, qh, jnp.where(row == 1, ql, 0.0)).astype(BF16)
            cq = _dot_nt(qmat, cmat.astype(BF16))
            num = s * v_row + g * (cq[0:1, :] + cq[1:2, :])
            den = s + g * jnp.sum(n_row * q_row, axis=1, keepdims=True)
            hh = num / jnp.maximum(jnp.abs(den), em)
            vh, vl = hi_lo(w * v_row)
            kh, kl = hi_lo(k_row)
            a = jnp.where(row < 2, vh, jnp.where(row < 4, vl, 0.0)).astype(BF16)
            b = jnp.where(row < 4, jnp.where(row % 2 == 0, kh, kl), 0.0).astype(BF16)
            c_out_ref[i, h] = g * cmat + _dot_tn(a, b)
            n_out_ref[rsel, hs] = g * n_row + w * k_row
            y_ref[rsel, hs] = _head_norm_gate(hh, o_ref[rsel, hs], ng_ref[:, hs])


def _mlstm_step(z_small, row_blk, conv0, c0, n0, m0, conv_w, conv_b, wq_bf, wk_bf, norm_g, bb=4):
    rows = c0.shape[0]
    full = lambda a: pl.BlockSpec(a.shape, lambda i: (0,) * a.ndim)
    conv0_t = conv0.transpose(1, 0, 2)
    q, k, sc = pl.pallas_call(
        _mlstm_step_a_body,
        grid=(1,),
        in_specs=[pl.BlockSpec((rows, D_ML), lambda i: (row_blk, 1)),
                  pl.BlockSpec((rows, LANES), lambda i: (row_blk, GATE_COL)),
                  full(conv0_t), full(m0), full(conv_w), full(conv_b), full(wq_bf), full(wk_bf)],
        out_specs=[pl.BlockSpec((rows, D_ML), lambda i: (0, 0)), pl.BlockSpec((rows, D_ML), lambda i: (0, 0)),
                   pl.BlockSpec((rows, LANES), lambda i: (0, 0))],
        out_shape=[jax.ShapeDtypeStruct((rows, D_ML), F32), jax.ShapeDtypeStruct((rows, D_ML), F32),
                   jax.ShapeDtypeStruct((rows, LANES), F32)],
        compiler_params=_cparams(("arbitrary",), 48),
        name="mlstm_step_a",
    )(z_small, z_small, conv0_t, m0, conv_w, conv_b, wq_bf, wk_bf)
    row_spec = pl.BlockSpec((rows, D_ML), lambda i: (0, 0))
    c_spec = pl.BlockSpec((bb, N_HEADS, DH, DH), lambda i: (i, 0, 0, 0))
    y, c_new, n_new = pl.pallas_call(
        functools.partial(_mlstm_step_b_body, bb),
        grid=(rows // bb,),
        in_specs=[pl.BlockSpec(memory_space=pltpu.SMEM), row_spec, row_spec, row_spec,
                  pl.BlockSpec((rows, D_ML), lambda i: (row_blk, 2)),
                  pl.BlockSpec((rows, D_ML), lambda i: (row_blk, 3)),
                  pl.BlockSpec((1, D_ML), lambda i: (0, 0)), c_spec],
        out_specs=[row_spec, c_spec, row_spec],
        out_shape=[jax.ShapeDtypeStruct((rows, D_ML), F32), jax.ShapeDtypeStruct(c0.shape, F32),
                   jax.ShapeDtypeStruct((rows, D_ML), F32)],
        compiler_params=_cparams(("arbitrary",), 56),
        name="mlstm_step_b",
    )(sc[:, :N_STEP_SCALARS].reshape(rows * N_STEP_SCALARS), q, k, n0, z_small, z_small, norm_g, c0)
    return y, c_new, n_new, sc[:, 3 * N_HEADS:4 * N_HEADS]


ROW_CH = D_MODEL // LANES


def _load_row_tiles(ref, row0, m, c):
    return ref[pl.ds(row0 * ROW_CH + c, m, stride=ROW_CH), :]


def _store_row_tiles(ref, row0, val):
    m = val.shape[0]
    for c in range(ROW_CH):
        ref[pl.ds(row0 * ROW_CH + c, m, stride=ROW_CH), :] = val[:, c * LANES:(c + 1) * LANES]


def _mix_out_body(xp_ref, xs_ref, ysp0_ref, ysp1_ref, yss_ref, ymp0_ref, ymp1_ref, yms_ref,
                  gin_ref, bin_ref, wout_ref, g1_ref, b1_ref, rwh_ref, rwl_ref, rb_ref,
                  h1_ref, e_ref, gate_ref):
    is_step = pl.program_id(0) == pl.num_programs(0) - 1
    consts = (gin_ref, bin_ref, wout_ref, g1_ref, b1_ref, rwh_ref, rwl_ref, rb_ref)
    outs = (h1_ref, e_ref, gate_ref)
    two = lambda a, b: jnp.concatenate([a[...], b[...]], axis=0)

    @pl.when(jnp.logical_not(is_step))
    def _():
        _mix_out_tile(xp_ref[...], two(ysp0_ref, ysp1_ref), two(ymp0_ref, ymp1_ref), consts, outs)

    @pl.when(is_step)
    def _():
        _mix_out_tile(two(xs_ref, xs_ref), two(yss_ref, yss_ref), two(yms_ref, yms_ref), consts, outs)


def _mix_out_tile(x, ys, ym, consts, outs):
    gin_ref, bin_ref, wout_ref, g1_ref, b1_ref, rwh_ref, rwl_ref, rb_ref = consts
    h1_ref, e_ref, gate_ref = outs
    hp = _layer_norm(x, gin_ref[...], bin_ref[...])
    ycat = jnp.concatenate([ys, ym], axis=1).astype(BF16)
    mix = jnp.dot(ycat, wout_ref[...], preferred_element_type=F32)
    h1 = _layer_norm(DEEPNORM_ALPHA * hp + mix, g1_ref[...], b1_ref[...])
    _store_row_tiles(h1_ref, 0, h1)
    xh = h1.astype(BF16)
    xl = (h1 - xh.astype(F32)).astype(BF16)
    logits = (jnp.dot(xh, rwh_ref[...], preferred_element_type=F32)
              + jnp.dot(xh, rwl_ref[...], preferred_element_type=F32)
              + jnp.dot(xl, rwh_ref[...], preferred_element_type=F32)) + rb_ref[...]
    rows = logits.shape[0]
    lane = lax.broadcasted_iota(jnp.int32, (rows, LANES), 1)
    logits = jnp.where(lane < N_EXPERTS, logits, NEG_INF)
    e_acc = jnp.zeros((rows, LANES), jnp.int32)
    v_acc = jnp.full((rows, LANES), NEG_INF, F32)
    for k in range(TOP_K):
        mx = jnp.max(logits, axis=1, keepdims=True)
        idx = jnp.min(jnp.where(logits == mx, lane, LANES), axis=1, keepdims=True)
        e_acc = jnp.where(lane == k, idx, e_acc)
        v_acc = jnp.where(lane == k, mx, v_acc)
        logits = jnp.where(lane == idx, NEG_INF, logits)
    p = jnp.exp(v_acc - jnp.max(v_acc, axis=1, keepdims=True))
    e_ref[...] = e_acc
    gate_ref[...] = p / jnp.sum(p, axis=1, keepdims=True)


def _mix_out(x_p, x_s, ys_p, ys_s, ym_p, ym_s, consts, tiles_per_seq):
    d = D_MODEL
    full = lambda a: pl.BlockSpec(a.shape, lambda i: (0,) * a.ndim)
    tm = 2 * TILE_T
    n_p = x_p.shape[0] // tm
    n_tiles = n_p + 1
    rows = n_tiles * tm

    def p_idx(i):
        return jnp.minimum(i, n_p - 1)

    def frame_idx(i, half):
        j = 2 * p_idx(i) + half
        return (j // tiles_per_seq) * (tiles_per_seq + 1) + j % tiles_per_seq + 1

    frame = lambda width, half: pl.BlockSpec((TILE_T, width), lambda i: (frame_idx(i, half), 0))
    first = lambda width: pl.BlockSpec((TILE_T, width), lambda i: (0, 0))
    return pl.pallas_call(
        _mix_out_body,
        grid=(n_tiles,),
        in_specs=[pl.BlockSpec((tm, d), lambda i: (p_idx(i), 0)), first(d),
                  frame(D_S5, 0), frame(D_S5, 1), first(D_S5),
                  frame(D_ML, 0), frame(D_ML, 1), first(D_ML)] + [full(a) for a in consts],
        out_specs=[pl.BlockSpec((tm * ROW_CH, LANES), lambda i: (i, 0)),
                   pl.BlockSpec((tm, LANES), lambda i: (i, 0)),
                   pl.BlockSpec((tm, LANES), lambda i: (i, 0))],
        out_shape=[jax.ShapeDtypeStruct((rows * ROW_CH, LANES), F32), jax.ShapeDtypeStruct((rows, LANES), jnp.int32),
                   jax.ShapeDtypeStruct((rows, LANES), F32)],
        compiler_params=_cparams(("arbitrary",), 56),
        name="mix_out",
    )(x_p, x_s, ys_p, ys_p, ys_s, ym_p, ym_p, ym_s, *consts)


MOE_BLK = 128
MOE_TM = 1536
MOE_TF = 256
MOE_SUB_MAX = 5
DMA_UNROLL = 8


def _invert_rows_body(dest_ref, src_ref):
    def zero(i, c):
        for q in range(DMA_UNROLL):
            src_ref[i * DMA_UNROLL + q] = 0
        return c
    lax.fori_loop(0, src_ref.shape[0] // DMA_UNROLL, zero, 0)

    def put(i, c):
        for q in range(DMA_UNROLL):
            src_ref[dest_ref[i * DMA_UNROLL + q]] = i * (DMA_UNROLL // TOP_K) + q // TOP_K
        return c
    lax.fori_loop(0, dest_ref.shape[0] // DMA_UNROLL, put, 0)


def _invert_rows(dest_flat, rows_pad):
    return pl.pallas_call(
        _invert_rows_body,
        in_specs=[pl.BlockSpec(memory_space=pltpu.SMEM)],
        out_specs=pl.BlockSpec(memory_space=pltpu.SMEM),
        out_shape=jax.ShapeDtypeStruct((rows_pad,), jnp.int32),
        name="invert_rows",
    )(dest_flat)


def _moe_routing(top_e, n_tok):
    n_pairs = n_tok * TOP_K
    rows_pad = n_pairs + N_EXPERTS * MOE_BLK
    n_pass = N_EXPERTS + rows_pad // MOE_TM
    onehot = (top_e[:, :, None] == jnp.arange(N_EXPERTS, dtype=jnp.int32)).astype(jnp.int32).sum(1)
    incl = jnp.cumsum(onehot, axis=0)
    counts = incl[-1]
    rank = jnp.take_along_axis(incl - onehot, top_e, axis=1)
    padded = (counts + MOE_BLK - 1) // MOE_BLK * MOE_BLK
    pad_end = jnp.cumsum(padded)
    pad_start = pad_end - padded
    dest = pad_start[top_e] + rank
    src = _invert_rows(dest.reshape(-1).astype(jnp.int32), rows_pad)
    passes_e = (padded + MOE_TM - 1) // MOE_TM
    pass_end = jnp.cumsum(passes_e)
    u = jnp.arange(n_pass, dtype=jnp.int32)
    e_u = jnp.minimum(jnp.searchsorted(pass_end, u, side="right"), N_EXPERTS - 1).astype(jnp.int32)
    j_u = u - (pass_end - passes_e)[e_u]
    rem = padded[e_u] - j_u * MOE_TM
    nblk = jnp.where(u < pass_end[-1], jnp.clip(rem, 0, MOE_TM) // MOE_BLK, 0).astype(jnp.int32)
    blk0 = ((pad_start[e_u] + j_u * MOE_TM) // MOE_BLK).astype(jnp.int32)
    blk0 = jnp.where(nblk > 0, blk0, 0)
    last_e = e_u[jnp.maximum(pass_end[-1] - 1, 0)]
    e_u = jnp.where(nblk > 0, e_u, last_e)
    nblk = jnp.concatenate([nblk, (pad_end[-1:] // MOE_BLK).astype(jnp.int32)])
    return dest.astype(jnp.int32), src, e_u, blk0, nblk


def _moe_ffn_body(e_ref, blk0_ref, nblk_ref, src_ref, h1_ref, wgu_ref, wdn_ref, bgu_ref, bd_ref,
                  ybuf_ref, xg_ref, xb_ref, acc_ref, w32_a, w32_b, wb_a, wb_b, stage_ref, gsem, osem, wsem):
    u = pl.program_id(0)
    n_pass = pl.num_programs(0)
    n_f = D_FF // MOE_TF
    nblk = nblk_ref[u]
    row0 = blk0_ref[u] * MOE_BLK
    blk_rt = MOE_BLK * ROW_CH
    nxt = jnp.minimum(u + 1, n_pass - 1)
    nxt_active = jnp.logical_and(u + 1 < n_pass, nblk_ref[nxt] > 0)
    w32 = (w32_a, w32_b)
    wb = (wb_a, wb_b)

    def w_copies(p, f, par):
        e = e_ref[p]
        cols = pl.ds(pl.multiple_of(f * MOE_TF, MOE_TF), MOE_TF)
        cols_up = pl.ds(pl.multiple_of(D_FF + f * MOE_TF, MOE_TF), MOE_TF)
        g32, u32, d32 = w32[par]
        return (pltpu.make_async_copy(wgu_ref.at[e, :, cols], g32, wsem.at[par]),
                pltpu.make_async_copy(wgu_ref.at[e, :, cols_up], u32, wsem.at[par]),
                pltpu.make_async_copy(wdn_ref.at[e, cols, :], d32, wsem.at[par]))

    def cast_w_half(par, half):
        g32, u32, d32 = w32[par]
        w1, w2 = wb[par]
        r1 = pl.ds(pl.multiple_of(half * (D_MODEL // 2), D_MODEL // 2), D_MODEL // 2)
        r2 = pl.ds(pl.multiple_of(half * (MOE_TF // 2), MOE_TF // 2), MOE_TF // 2)
        w1[r1, :MOE_TF] = g32[r1, :].astype(BF16)
        w1[r1, MOE_TF:] = u32[r1, :].astype(BF16)
        w2[r2, :] = d32[r2, :].astype(BF16)

    def cast_w(par):
        cast_w_half(par, 0)
        cast_w_half(par, 1)

    def for_all_rows(p, fn):
        base = blk0_ref[p] * MOE_BLK

        def body(rb, c):
            for s in range(SUBLANES):
                tok = src_ref[base + rb * SUBLANES + s]
                fn(pltpu.make_async_copy(h1_ref.at[pl.ds(tok * ROW_CH, ROW_CH)],
                                         xg_ref.at[rb, pl.ds(0, ROW_CH), s], gsem))
            return c
        lax.fori_loop(0, nblk_ref[p] * (MOE_BLK // SUBLANES), body, 0)

    def issue_gather(p):
        for_all_rows(p, lambda cp: cp.start())

    def out_copy(b):
        dst = pl.ds(pl.multiple_of((row0 + b * MOE_BLK) * ROW_CH, blk_rt), blk_rt)
        return pltpu.make_async_copy(stage_ref.at[b % 2], ybuf_ref.at[dst], osem.at[b % 2])

    @pl.when(nblk > 0)
    def _():
        @pl.when(u == 0)
        def _():
            issue_gather(u)
            for cp in w_copies(u, 0, 0):
                cp.start()
            for cp in w_copies(u, 0, 0):
                cp.wait()
            cast_w(0)
            for cp in w_copies(u, 1, 1):
                cp.start()

        for_all_rows(u, lambda cp: cp.wait())

        def cast(b, c):
            rs = pl.ds(pl.multiple_of(b * MOE_BLK, MOE_BLK), MOE_BLK)
            gs = pl.ds(pl.multiple_of(b * (MOE_BLK // SUBLANES), MOE_BLK // SUBLANES), MOE_BLK // SUBLANES)
            for ch in range(ROW_CH):
                xb_ref[rs, ch * LANES:(ch + 1) * LANES] = xg_ref[gs, ch].reshape(MOE_BLK, LANES).astype(BF16)
            acc_ref[rs, :] = jnp.zeros((MOE_BLK, acc_ref.shape[1]), F32)
            return c
        lax.fori_loop(0, nblk, cast, 0)

        def ffn_rows(r0, m, f, par):
            w1, w2 = wb[par]
            bg = bgu_ref[0, :, pl.ds(pl.multiple_of(f * MOE_TF, MOE_TF), MOE_TF)]
            bu = bgu_ref[0, :, pl.ds(pl.multiple_of(D_FF + f * MOE_TF, MOE_TF), MOE_TF)]
            rs = pl.ds(pl.multiple_of(r0, MOE_BLK), m)
            h = jnp.dot(xb_ref[rs, :], w1[...], preferred_element_type=F32)
            x_glu = jnp.minimum(h[:, :MOE_TF] + bg, SWIGLU_LIMIT)
            x_lin = jnp.clip(h[:, MOE_TF:] + bu, -SWIGLU_LIMIT, SWIGLU_LIMIT)
            act = x_glu * jax.nn.sigmoid(SWIGLU_ALPHA * x_glu) * (x_lin + 1.0)
            acc_ref[rs, :] += jnp.dot(act.astype(BF16), w2[...], preferred_element_type=F32)

        def tile_step(f, par):
            in_pass1 = f + 1 < n_f
            in_pass2 = f + 2 < n_f

            @pl.when(jnp.logical_or(in_pass1, nxt_active))
            def _():
                for cp in w_copies(jnp.where(in_pass1, u, nxt), jnp.where(in_pass1, f + 1, 0), 1 - par):
                    cp.wait()

            @pl.when(jnp.logical_or(in_pass2, nxt_active))
            def _():
                for cp in w_copies(jnp.where(in_pass2, u, nxt), jnp.where(in_pass2, f + 2, f + 2 - n_f), par):
                    cp.start()

            @pl.when(jnp.logical_and(f == 1, nxt_active))
            def _():
                issue_gather(nxt)

            n_sub = (nblk + (MOE_SUB_MAX - 1)) // MOE_SUB_MAX
            q = nblk // n_sub
            n_hi = nblk - q * n_sub

            def sub(i, blk):
                sz = q + (i < n_hi).astype(jnp.int32)
                for m in range(1, MOE_SUB_MAX + 1):
                    @pl.when(sz == m)
                    def _(m=m):
                        ffn_rows(blk * MOE_BLK, m * MOE_BLK, f, par)
                        cast_w_half(1 - par, jnp.minimum(i, 1))
                return blk + sz
            lax.fori_loop(0, n_sub, sub, 0)

            @pl.when(n_sub == 1)
            def _():
                cast_w_half(1 - par, 1)

        def tile_pair(f2, c):
            tile_step(2 * f2, 0)
            tile_step(2 * f2 + 1, 1)
            return c
        lax.fori_loop(0, n_f // 2, tile_pair, 0)

        def emit(b, c):
            @pl.when(b >= 2)
            def _():
                out_copy(b - 2).wait()
            rs = pl.ds(pl.multiple_of(b * MOE_BLK, MOE_BLK), MOE_BLK)
            val = acc_ref[rs, :] + bd_ref[0]
            for ch in range(ROW_CH):
                stage_ref[b % 2, pl.ds(ch, MOE_BLK, stride=ROW_CH), :] = val[:, ch * LANES:(ch + 1) * LANES]
            out_copy(b).start()
            return c
        lax.fori_loop(0, nblk, emit, 0)

        @pl.when(nblk >= 2)
        def _():
            out_copy(nblk - 2).wait()
        out_copy(nblk - 1).wait()

    @pl.when(u == n_pass - 1)
    def _():
        used = nblk_ref[n_pass]
        total = ybuf_ref.shape[0] // blk_rt
        stage_ref[0] = jnp.zeros((blk_rt, LANES), F32)

        def zero_copy(b):
            dst = pl.ds(pl.multiple_of(b * blk_rt, blk_rt), blk_rt)
            return pltpu.make_async_copy(stage_ref.at[0], ybuf_ref.at[dst], osem.at[0])

        def start(b, c):
            zero_copy(b).start()
            return c
        lax.fori_loop(used, total, start, 0)

        def wait(b, c):
            zero_copy(b).wait()
            return c
        lax.fori_loop(used, total, wait, 0)


def _moe_ffn(h1, src, e_u, blk0, nblk, w_gu, b_gu, w_dn, b_dn, rows_pad):
    d = D_MODEL
    n_pass = e_u.shape[0]
    w32_set = (pltpu.VMEM((d, MOE_TF), F32), pltpu.VMEM((d, MOE_TF), F32), pltpu.VMEM((MOE_TF, d), F32))
    wb_set = (pltpu.VMEM((d, 2 * MOE_TF), BF16), pltpu.VMEM((MOE_TF, d), BF16))
    grid_spec = pltpu.PrefetchScalarGridSpec(
        num_scalar_prefetch=4,
        grid=(n_pass,),
        in_specs=[
            pl.BlockSpec(memory_space=pl.ANY),
            pl.BlockSpec(memory_space=pl.ANY),
            pl.BlockSpec(memory_space=pl.ANY),
            pl.BlockSpec((1, 1, 2 * D_FF), lambda u, e, b0, nb, s: (e[u], 0, 0)),
            pl.BlockSpec((1, 1, d), lambda u, e, b0, nb, s: (e[u], 0, 0)),
        ],
        out_specs=pl.BlockSpec(memory_space=pl.ANY),
        scratch_shapes=[pltpu.VMEM((MOE_TM // SUBLANES, ROW_CH, SUBLANES, LANES), F32), pltpu.VMEM((MOE_TM, d), BF16),
                        pltpu.VMEM((MOE_TM, d), F32), w32_set, w32_set, wb_set, wb_set,
                        pltpu.VMEM((2, MOE_BLK * ROW_CH, LANES), F32),
                        pltpu.SemaphoreType.DMA(()), pltpu.SemaphoreType.DMA((2,)), pltpu.SemaphoreType.DMA((2,))],
    )
    return pl.pallas_call(
        _moe_ffn_body,
        grid_spec=grid_spec,
        out_shape=jax.ShapeDtypeStruct((rows_pad * ROW_CH, LANES), F32),
        compiler_params=_cparams(("arbitrary",), 60),
        name="moe_ffn",
    )(e_u, blk0, nblk, src, h1, w_gu, w_dn, b_gu.reshape(N_EXPERTS, 1, 2 * D_FF), b_dn.reshape(N_EXPERTS, 1, d))


def _moe_combine_body(dest_ref, h1_ref, gate_ref, g2_ref, b2_ref, ybuf_ref, outp_ref, outs_ref,
                      buf_ref, sem, pre_ref):
    i = pl.program_id(0)
    n_pairs = TILE_T * TOP_K
    slot = i % 2

    def copy(tile, sl, rb, s, k):
        src_row = dest_ref[tile * n_pairs + rb * (SUBLANES * TOP_K) + (s * TOP_K + k)]
        return pltpu.make_async_copy(ybuf_ref.at[pl.ds(src_row * ROW_CH, ROW_CH)],
                                     buf_ref.at[sl, k, rb, pl.ds(0, ROW_CH), s], sem.at[sl])

    def for_all_rows(tile, sl, fn):
        def body(rb, c):
            for s in range(SUBLANES):
                for k in range(TOP_K):
                    fn(copy(tile, sl, rb, s, k))
            return c
        lax.fori_loop(0, TILE_T // SUBLANES, body, 0)

    def issue(tile, sl):
        for_all_rows(tile, sl, lambda cp: cp.start())

    @pl.when(i == 0)
    def _():
        issue(0, 0)

    @pl.when(i + 1 < pl.num_programs(0))
    def _():
        issue(i + 1, 1 - slot)

    for_all_rows(i, slot, lambda cp: cp.wait())

    gate = gate_ref[...]
    cs = lambda c: slice(c * LANES, (c + 1) * LANES)
    part = jnp.zeros((TILE_T, LANES), F32)
    gate_b = [jnp.broadcast_to(gate[:, k:k + 1], (TILE_T, LANES)) for k in range(TOP_K)]
    for c in range(ROW_CH):
        fc = None
        for k in range(TOP_K):
            v = buf_ref[slot, k, :, c].reshape(TILE_T, LANES) * gate_b[k]
            fc = v if fc is None else fc + v
        pre = DEEPNORM_ALPHA * _load_row_tiles(h1_ref, 0, TILE_T, c) + fc
        pre_ref[:, cs(c)] = pre
        part = part + pre
    mu = jnp.sum(part, axis=1, keepdims=True) * (1.0 / D_MODEL)
    part = jnp.zeros((TILE_T, LANES), F32)
    for c in range(ROW_CH):
        dlt = pre_ref[:, cs(c)] - mu
        part = part + dlt * dlt
    rstd = lax.rsqrt(jnp.sum(part, axis=1, keepdims=True) * (1.0 / D_MODEL) + LN_EPS)
    is_step = i == pl.num_programs(0) - 1

    def write(out_ref):
        for c in range(ROW_CH):
            out_ref[:, cs(c)] = (pre_ref[:, cs(c)] - mu) * rstd * g2_ref[:, cs(c)] + b2_ref[:, cs(c)]

    @pl.when(jnp.logical_not(is_step))
    def _():
        write(outp_ref)

    @pl.when(is_step)
    def _():
        write(outs_ref)


def _moe_combine(dest, h1, gates, g2, b2, ybuf, n_tok):
    d = D_MODEL
    n_tiles = n_tok // TILE_T
    grid_spec = pltpu.PrefetchScalarGridSpec(
        num_scalar_prefetch=1,
        grid=(n_tiles,),
        in_specs=[pl.BlockSpec((TILE_T * ROW_CH, LANES), lambda i, s: (i, 0)),
                  pl.BlockSpec((TILE_T, LANES), lambda i, s: (i, 0)),
                  pl.BlockSpec((1, d), lambda i, s: (0, 0)),
                  pl.BlockSpec((1, d), lambda i, s: (0, 0)),
                  pl.BlockSpec(memory_space=pl.ANY)],
        out_specs=[pl.BlockSpec((TILE_T, d), lambda i, s: (jnp.minimum(i, n_tiles - 2), 0)),
                   pl.BlockSpec((TILE_T, d), lambda i, s: (0, 0))],
        scratch_shapes=[pltpu.VMEM((2, TOP_K, TILE_T // SUBLANES, ROW_CH, SUBLANES, LANES), F32),
                        pltpu.SemaphoreType.DMA((2,)),
                        pltpu.VMEM((TILE_T, d), F32)],
    )
    return pl.pallas_call(
        _moe_combine_body,
        grid_spec=grid_spec,
        out_shape=[jax.ShapeDtypeStruct((n_tok - TILE_T, d), F32), jax.ShapeDtypeStruct((TILE_T, d), F32)],
        compiler_params=_cparams(("arbitrary",), 48),
        name="moe_combine",
    )(dest.reshape(-1), h1, gates, g2, b2, ybuf)


def kernel(x_prompt, x_sample, state_s5_re, state_s5_im, state_mlstm_c, state_mlstm_n, state_mlstm_m, state_mlstm_conv, meta_tokens, ln_in_g, ln_in_b, w_in, b_in, s5_a_re, s5_a_im, s5_log_dt, s5_b_re, s5_b_im, s5_c_re, s5_c_im, s5_d, s5_w_glu, mlstm_conv_w, mlstm_conv_b, mlstm_wq, mlstm_wk, mlstm_norm_g, w_out, ln1_g, ln1_b, router_w, router_b, w_gate_up, b_gate_up, w_down, b_down, ln2_g, ln2_b):
    bsz, seq, d = x_prompt.shape
    dec_b = x_sample.shape[0]
    n_pad = TILE_T - N_META
    x_small = jnp.concatenate([jnp.zeros((n_pad, d), F32), meta_tokens, x_sample.reshape(dec_b, d)], axis=0)
    w_in_p = jnp.pad(w_in[0], ((0, 0), (0, N_IN_PAD - N_IN))).astype(BF16)
    b_in_p = jnp.pad(b_in[0], (0, N_IN_PAD - N_IN)).reshape(1, N_IN_PAD)
    g_in = ln_in_g.reshape(1, d)
    bb_in = ln_in_b.reshape(1, d)
    z_p = _inproj(x_prompt.reshape(bsz * seq, d), g_in, bb_in, w_in_p, b_in_p, 512, 1408)
    z_s = _inproj(x_small, g_in, bb_in, w_in_p, b_in_p, 256, 1408)
    tabs = _s5_prep(s5_a_re[0], s5_a_im[0], s5_log_dt[0], s5_b_re[0], s5_b_im[0])
    bd_b, bd_c = _s5_block_diag(tabs[6], tabs[7], s5_c_re[0], s5_c_im[0])
    wglu_bf = s5_w_glu[0].astype(BF16)
    d_skip = s5_d[0].reshape(1, D_S5)
    n_tiles = seq // TILE_T + 1
    zero_state = jnp.zeros((bsz, 1, N_STATE), F32)
    y_s5_p, s5r_p, s5i_p = _s5_seq(z_s, z_p, zero_state, zero_state, tabs[:6], bd_b, bd_c, d_skip, wglu_bf,
                                   bsz, n_tiles, n_pad)
    y_s5_s, s5r_s, s5i_s = _s5_step(z_s, 1, state_s5_re[0].reshape(dec_b, N_STATE),
                                    state_s5_im[0].reshape(dec_b, N_STATE), tabs[:6], bd_b, bd_c, d_skip, wglu_bf)

    conv_w = mlstm_conv_w[0]
    conv_b = mlstm_conv_b[0].reshape(1, D_ML)
    wq_bf = mlstm_wq[0].astype(BF16)
    wk_bf = mlstm_wk[0].astype(BF16)
    norm_g = mlstm_norm_g[0].reshape(1, D_ML)
    y_ml_p, c_p, n_p, m_p = _mlstm_seq(z_s, z_p, conv_w, conv_b, wq_bf, wk_bf, norm_g, bsz, n_tiles, n_pad)
    conv0 = state_mlstm_conv[0]
    y_ml_s, c_s, n_s, m_s = _mlstm_step(z_s, 1, conv0, state_mlstm_c[0], state_mlstm_n[0].reshape(dec_b, D_ML),
                                        state_mlstm_m[0], conv_w, conv_b, wq_bf, wk_bf, norm_g)

    rw = jnp.pad(router_w[0], ((0, 0), (0, LANES - N_EXPERTS)))
    rw_hi = rw.astype(BF16)
    rw_lo = (rw - rw_hi.astype(F32)).astype(BF16)
    rb = jnp.pad(router_b[0], (0, LANES - N_EXPERTS)).reshape(1, LANES)
    consts = (g_in, bb_in, w_out[0].astype(BF16), ln1_g[0].reshape(1, d), ln1_b[0].reshape(1, d), rw_hi, rw_lo, rb)
    h1, top_e, gates = _mix_out(x_prompt.reshape(bsz * seq, d), x_sample.reshape(dec_b, d),
                                y_s5_p, y_s5_s, y_ml_p, y_ml_s, consts, seq // TILE_T)
    n_tok = bsz * seq + dec_b
    dest, src, e_u, blk0, nblk = _moe_routing(top_e[:n_tok, :TOP_K], n_tok)
    ybuf = _moe_ffn(h1, src, e_u, blk0, nblk, w_gate_up[0], b_gate_up[0], w_down[0], b_down[0], src.shape[0])
    out_p, out_s = _moe_combine(dest, h1, gates, ln2_g[0].reshape(1, d), ln2_b[0].reshape(1, d), ybuf, n_tok)

    y_prompt = out_p.reshape(bsz, seq, d)
    y_sample = out_s.reshape(dec_b, 1, d)
    xm_p = z_p.reshape(bsz, seq, N_IN_PAD)[:, seq - (CONV_W - 1):, D_S5:D_S5 + D_ML]
    xm_s = z_s[TILE_T:, D_S5:D_S5 + D_ML]
    conv_s = jnp.concatenate([conv0[:, 1:], xm_s[:, None, :]], axis=1)
    return (y_prompt, y_sample,
            s5r_p.reshape(1, bsz, N_GROUPS, S5_STATE), s5i_p.reshape(1, bsz, N_GROUPS, S5_STATE),
            c_p[None], n_p.reshape(1, bsz, N_HEADS, DH), m_p[None, :, :N_HEADS, 0], xm_p[None],
            s5r_s.reshape(1, dec_b, N_GROUPS, S5_STATE), s5i_s.reshape(1, dec_b, N_GROUPS, S5_STATE),
            c_s[None], n_s.reshape(1, dec_b, N_HEADS, DH), m_s[None], conv_s[None])
```

```python
import functools
import math

import jax
import jax.numpy as jnp
from jax import lax
from jax.experimental import pallas as pl
from jax.experimental.pallas import tpu as pltpu

F32 = jnp.float32
BF16 = jnp.bfloat16

D_MODEL = 2048
N_META = 16
D_S5 = 1024
D_ML = 1024
S5_CH = 16
N_GROUPS = 64
S5_STATE = 64
N_STATE = N_GROUPS * S5_STATE
N_HEADS = 4
DH = 256
CONV_W = 4
N_EXPERTS = 32
TOP_K = 4
D_FF = 2048
SWIGLU_LIMIT = 7.0
SWIGLU_ALPHA = 1.702
LN_EPS = 1e-5
DEEPNORM_ALPHA = 2.0 ** 0.25
N_IN = D_S5 + 3 * D_ML + 2 * N_HEADS

LANES = 128
SUBLANES = 8
MXU_DIM = 256
MXU_ROWS = 16

TILE_T = 128
SEG_LEN = TILE_T // SUBLANES
N_IN_PAD = 33 * LANES
GATE_COL = 4 * D_S5 // LANES
S5_KCH = D_S5 // MXU_DIM
S5_SCH = N_STATE // S5_KCH
SCAN_LW = 256


def _cparams(sem, vmem_mb=None):
    kw = dict(dimension_semantics=sem)
    if vmem_mb is not None:
        kw["vmem_limit_bytes"] = vmem_mb * 1024 * 1024
    return pltpu.CompilerParams(**kw)


def _layer_norm(x, g, b):
    mu = jnp.mean(x, axis=-1, keepdims=True)
    xc = x - mu
    var = jnp.mean(xc * xc, axis=-1, keepdims=True)
    return xc * lax.rsqrt(var + LN_EPS) * g + b


def _inproj_body(x_ref, g_ref, b_ref, w_ref, bias_ref, z_ref, hn_ref):
    @pl.when(pl.program_id(1) == 0)
    def _():
        hn_ref[...] = _layer_norm(x_ref[...], g_ref[...], b_ref[...]).astype(BF16)

    z_ref[...] = jnp.dot(hn_ref[...], w_ref[...], preferred_element_type=F32) + bias_ref[...]


def _inproj(x, g, b, w, bias, tm, tn):
    rows, d = x.shape
    n = w.shape[1]
    return pl.pallas_call(
        _inproj_body,
        grid=(rows // tm, n // tn),
        in_specs=[
            pl.BlockSpec((tm, d), lambda i, j: (i, 0)),
            pl.BlockSpec((1, d), lambda i, j: (0, 0)),
            pl.BlockSpec((1, d), lambda i, j: (0, 0)),
            pl.BlockSpec((d, tn), lambda i, j: (0, j)),
            pl.BlockSpec((1, tn), lambda i, j: (0, j)),
        ],
        out_specs=pl.BlockSpec((tm, tn), lambda i, j: (i, j)),
        out_shape=jax.ShapeDtypeStruct((rows, n), F32),
        scratch_shapes=[pltpu.VMEM((tm, d), BF16)],
        compiler_params=_cparams(("arbitrary", "arbitrary"), 48),
        name="inproj",
    )(x, g, b, w, bias)


def _cmul(ar, ai, br, bi):
    return ar * br - ai * bi, ar * bi + ai * br


def _s5_prep_body(are_ref, aim_ref, dt_ref, bre_ref, bim_ref,
                  pre_ref, pim_ref, hre_ref, him_ref, qre_ref, qim_ref, bbre_ref, bbim_ref):
    lr = are_ref[...]
    li = aim_ref[...]
    dt = jnp.exp(dt_ref[...])
    mag = jnp.exp(lr * dt)
    ar = mag * jnp.cos(li * dt)
    ai = mag * jnp.sin(li * dt)
    nr = ar - 1.0
    ni = ai
    den = lr * lr + li * li
    cr = (nr * lr + ni * li) / den
    ci = (ni * lr - nr * li) / den
    bbre_ref[...] = cr * bre_ref[...] - ci * bim_ref[...]
    bbim_ref[...] = cr * bim_ref[...] + ci * bre_ref[...]
    pr, pi = ar, ai
    for j in range(SEG_LEN):
        pre_ref[j:j + 1, :] = pr
        pim_ref[j:j + 1, :] = pi
        if j + 1 < SEG_LEN:
            pr, pi = _cmul(pr, pi, ar, ai)
    row = lax.broadcasted_iota(jnp.int32, (SUBLANES, N_STATE), 0)
    kr, ki = pr, pi
    for idx, k in enumerate((1, 2, 4)):
        hre_ref[idx * 8:(idx + 1) * 8, :] = jnp.where(row >= k, jnp.broadcast_to(kr, (SUBLANES, N_STATE)), 0.0)
        him_ref[idx * 8:(idx + 1) * 8, :] = jnp.where(row >= k, jnp.broadcast_to(ki, (SUBLANES, N_STATE)), 0.0)
        kr, ki = _cmul(kr, ki, kr, ki)
    qr, qi = pr, pi
    for s in range(SUBLANES):
        qre_ref[s:s + 1, :] = qr
        qim_ref[s:s + 1, :] = qi
        if s + 1 < SUBLANES:
            qr, qi = _cmul(qr, qi, pr, pi)


def _s5_prep(a_re, a_im, log_dt, b_re, b_im):
    n = N_STATE
    are = a_re.reshape(1, n)
    aim = a_im.reshape(1, n)
    dtl = jnp.broadcast_to(log_dt[:, None], (N_GROUPS, S5_STATE)).reshape(1, n)
    bre = b_re.transpose(2, 0, 1).reshape(S5_CH, n)
    bim = b_im.transpose(2, 0, 1).reshape(S5_CH, n)
    shp = lambda r: jax.ShapeDtypeStruct((r, n), F32)
    return pl.pallas_call(
        _s5_prep_body,
        out_shape=(shp(SEG_LEN), shp(SEG_LEN), shp(24), shp(24), shp(8), shp(8), shp(S5_CH), shp(S5_CH)),
        name="s5_prep",
    )(are, aim, dtl, bre, bim)


def _s5_block_diag(bb_re, bb_im, c_re, c_im):
    gpc = 16
    eye = jnp.eye(gpc, dtype=bool)

    def bd_in(bb):
        t = bb.reshape(S5_CH, S5_KCH, gpc, S5_STATE).transpose(1, 2, 0, 3)
        t = jnp.where(eye[None, :, None, :, None], t[:, :, :, None, :], 0.0)
        return t.reshape(S5_KCH, gpc * S5_CH, gpc * S5_STATE)

    def bd_out(c):
        t = c.reshape(S5_KCH, gpc, S5_CH, S5_STATE).transpose(0, 1, 3, 2)
        t = jnp.where(eye[None, :, None, :, None], t[:, :, :, None, :], 0.0)
        return t.reshape(S5_KCH, gpc * S5_STATE, gpc * S5_CH)

    bd_b = jnp.concatenate([bd_in(bb_re), bd_in(bb_im)], axis=2).astype(BF16)
    bd_c = jnp.concatenate([bd_out(c_re), bd_out(-c_im)], axis=1).astype(BF16)
    return bd_b, bd_c


def _gelu_glu(y, wglu_ref):
    y = 0.5 * y * (1.0 + lax.erf(y * math.sqrt(0.5)))
    gate = jnp.dot(y.astype(BF16), wglu_ref[...], preferred_element_type=F32)
    return y * jax.nn.sigmoid(gate)


def _s5_in_proj(u_bf, bdb_ref, bur_ref, bui_ref):
    for c in range(S5_KCH):
        r = jnp.dot(u_bf[:, c * MXU_DIM:(c + 1) * MXU_DIM], bdb_ref[c], preferred_element_type=F32)
        bur_ref[:, c * S5_SCH:(c + 1) * S5_SCH] = r[:, :S5_SCH]
        bui_ref[:, c * S5_SCH:(c + 1) * S5_SCH] = r[:, S5_SCH:]


def _s5_out_proj(xr_ref, xi_ref, bdc_ref):
    ys = []
    for c in range(S5_KCH):
        xr = xr_ref[:, c * S5_SCH:(c + 1) * S5_SCH].astype(BF16)
        xi = xi_ref[:, c * S5_SCH:(c + 1) * S5_SCH].astype(BF16)
        ys.append(jnp.dot(xr, bdc_ref[c, :S5_SCH, :], preferred_element_type=F32)
                  + jnp.dot(xi, bdc_ref[c, S5_SCH:, :], preferred_element_type=F32))
    return jnp.concatenate(ys, axis=1)


def _s5_seq_body(n_pad, us_ref, up_ref, x0r_ref, x0i_ref, bdb_ref, bdc_ref, pre_ref, pim_ref,
                 hre_ref, him_ref, qre_ref, qim_ref, d_ref, wglu_ref,
                 y_ref, xr_out_ref, xi_out_ref, bur_ref, bui_ref, cr_ref, ci_ref, perm_ref):
    t = pl.program_id(1)
    n = N_STATE

    @pl.when(t == 0)
    def _():
        cr_ref[...] = jnp.broadcast_to(x0r_ref[0], (SUBLANES, n))
        ci_ref[...] = jnp.broadcast_to(x0i_ref[0], (SUBLANES, n))

    n_lc = D_S5 // LANES

    def load_perm(ref):
        for c in range(n_lc):
            perm_ref[c] = ref[:, c * LANES:(c + 1) * LANES]
        return jnp.concatenate(
            [jnp.concatenate([perm_ref[c, pl.ds(j, SUBLANES, stride=SEG_LEN), :] for c in range(n_lc)], axis=1)
             for j in range(SEG_LEN)], axis=0)

    prow = lax.broadcasted_iota(jnp.int32, (TILE_T, 1), 0)
    time = (prow % SUBLANES) * SEG_LEN + prow // SUBLANES
    u_first = jnp.where(time >= n_pad, load_perm(us_ref), 0.0)
    u = jnp.where(t == 0, u_first, load_perm(up_ref))
    _s5_in_proj(u.astype(BF16), bdb_ref, bur_ref, bui_ref)

    row8 = lax.broadcasted_iota(jnp.int32, (SUBLANES, SCAN_LW), 0)

    def scan_lanes(lc, carry):
        ls = pl.ds(pl.multiple_of(lc * SCAN_LW, SCAN_LW), SCAN_LW)
        bc = lambda ref, j: jnp.broadcast_to(ref[j:j + 1, ls], (SUBLANES, SCAN_LW))
        ar, ai = bc(pre_ref, 0), bc(pim_ref, 0)
        xr = jnp.zeros((SUBLANES, SCAN_LW), F32)
        xi = jnp.zeros((SUBLANES, SCAN_LW), F32)
        for j in range(SEG_LEN):
            rs = slice(j * SUBLANES, (j + 1) * SUBLANES)
            nr = ar * xr - ai * xi + bur_ref[rs, ls]
            ni = ar * xi + ai * xr + bui_ref[rs, ls]
            xr, xi = nr, ni
            bur_ref[rs, ls] = xr
            bui_ref[rs, ls] = xi
        er, ei = xr, xi
        for idx, k in enumerate((1, 2, 4)):
            sr = pltpu.roll(er, k, axis=0)
            si = pltpu.roll(ei, k, axis=0)
            hr = hre_ref[idx * 8:(idx + 1) * 8, ls]
            hi = him_ref[idx * 8:(idx + 1) * 8, ls]
            er, ei = er + (hr * sr - hi * si), ei + (hr * si + hi * sr)
        cpr = cr_ref[:, ls]
        cpi = ci_ref[:, ls]
        qr = qre_ref[:, ls]
        qi = qim_ref[:, ls]
        er, ei = er + (qr * cpr - qi * cpi), ei + (qr * cpi + qi * cpr)
        inr = jnp.where(row8 == 0, cpr, pltpu.roll(er, 1, axis=0))
        ini = jnp.where(row8 == 0, cpi, pltpu.roll(ei, 1, axis=0))
        cr_ref[:, ls] = jnp.broadcast_to(er[SUBLANES - 1:SUBLANES, :], (SUBLANES, SCAN_LW))
        ci_ref[:, ls] = jnp.broadcast_to(ei[SUBLANES - 1:SUBLANES, :], (SUBLANES, SCAN_LW))
        for j in range(SEG_LEN):
            rs = slice(j * SUBLANES, (j + 1) * SUBLANES)
            pr, pi = bc(pre_ref, j), bc(pim_ref, j)
            bur_ref[rs, ls] = bur_ref[rs, ls] + (pr * inr - pi * ini)
            bui_ref[rs, ls] = bui_ref[rs, ls] + (pr * ini + pi * inr)
        return carry

    lax.fori_loop(0, n // SCAN_LW, scan_lanes, 0)

    y = _s5_out_proj(bur_ref, bui_ref, bdc_ref) + d_ref[...] * u
    y = _gelu_glu(y, wglu_ref)
    for j in range(SEG_LEN):
        for c in range(n_lc):
            perm_ref[c, pl.ds(j, SUBLANES, stride=SEG_LEN), :] = y[j * SUBLANES:(j + 1) * SUBLANES,
                                                                   c * LANES:(c + 1) * LANES]
    for c in range(n_lc):
        y_ref[:, c * LANES:(c + 1) * LANES] = perm_ref[c]

    @pl.when(t == pl.num_programs(1) - 1)
    def _():
        xr_out_ref[0] = cr_ref[0:1, :]
        xi_out_ref[0] = ci_ref[0:1, :]


def _s5_seq(z_small, z_seq, x0r, x0i, tabs, bd_b, bd_c, d_skip, wglu_bf, n_batch, n_tiles, n_pad):
    pre, pim, hre, him, qre, qim = tabs
    n = N_STATE
    full = lambda a: pl.BlockSpec(a.shape, lambda b, t: (0,) * a.ndim)
    per_seq = n_tiles - 1
    return pl.pallas_call(
        functools.partial(_s5_seq_body, n_pad),
        grid=(n_batch, n_tiles),
        in_specs=[
            pl.BlockSpec((TILE_T, D_S5), lambda b, t: (0, 0)),
            pl.BlockSpec((TILE_T, D_S5), lambda b, t: (b * per_seq + jnp.maximum(t - 1, 0), 0)),
            pl.BlockSpec((1, 1, n), lambda b, t: (b, 0, 0)),
            pl.BlockSpec((1, 1, n), lambda b, t: (b, 0, 0)),
            full(bd_b), full(bd_c), full(pre), full(pim), full(hre), full(him), full(qre), full(qim),
            full(d_skip), full(wglu_bf),
        ],
        out_specs=[
            pl.BlockSpec((TILE_T, D_S5), lambda b, t: (b * n_tiles + t, 0)),
            pl.BlockSpec((1, 1, n), lambda b, t: (b, 0, 0)),
            pl.BlockSpec((1, 1, n), lambda b, t: (b, 0, 0)),
        ],
        out_shape=[
            jax.ShapeDtypeStruct((n_batch * n_tiles * TILE_T, D_S5), F32),
            jax.ShapeDtypeStruct((n_batch, 1, n), F32),
            jax.ShapeDtypeStruct((n_batch, 1, n), F32),
        ],
        scratch_shapes=[pltpu.VMEM((TILE_T, n), F32), pltpu.VMEM((TILE_T, n), F32),
                        pltpu.VMEM((SUBLANES, n), F32), pltpu.VMEM((SUBLANES, n), F32),
                        pltpu.VMEM((D_S5 // LANES, TILE_T, LANES), F32)],
        compiler_params=_cparams(("arbitrary", "arbitrary"), 48),
        name="s5_seq",
    )(z_small, z_seq, x0r, x0i, bd_b, bd_c, pre, pim, hre, him, qre, qim, d_skip, wglu_bf)


def _s5_step_body(u_ref, x0r_ref, x0i_ref, bdb_ref, bdc_ref, pre_ref, pim_ref, d_ref, wglu_ref,
                  y_ref, xr_ref, xi_ref):
    u = u_ref[...]
    _s5_in_proj(u.astype(BF16), bdb_ref, xr_ref, xi_ref)
    ar = pre_ref[0:1, :]
    ai = pim_ref[0:1, :]
    x0r = x0r_ref[...]
    x0i = x0i_ref[...]
    xr_ref[...] = xr_ref[...] + (ar * x0r - ai * x0i)
    xi_ref[...] = xi_ref[...] + (ar * x0i + ai * x0r)
    y = _s5_out_proj(xr_ref, xi_ref, bdc_ref) + d_ref[...] * u
    y_ref[...] = _gelu_glu(y, wglu_ref)


def _s5_step(z_small, row_blk, x0r, x0i, tabs, bd_b, bd_c, d_skip, wglu_bf):
    pre, pim = tabs[0], tabs[1]
    rows = x0r.shape[0]
    n = N_STATE
    full = lambda a: pl.BlockSpec(a.shape, lambda i: (0,) * a.ndim)
    return pl.pallas_call(
        _s5_step_body,
        grid=(1,),
        in_specs=[pl.BlockSpec((rows, D_S5), lambda i: (row_blk, 0)), full(x0r), full(x0i),
                  full(bd_b), full(bd_c), full(pre), full(pim), full(d_skip), full(wglu_bf)],
        out_specs=[pl.BlockSpec((rows, D_S5), lambda i: (0, 0)),
                   pl.BlockSpec((rows, n), lambda i: (0, 0)), pl.BlockSpec((rows, n), lambda i: (0, 0))],
        out_shape=[jax.ShapeDtypeStruct((rows, D_S5), F32), jax.ShapeDtypeStruct((rows, n), F32),
                   jax.ShapeDtypeStruct((rows, n), F32)],
        compiler_params=_cparams(("arbitrary",), 48),
        name="s5_step",
    )(z_small, x0r, x0i, bd_b, bd_c, pre, pim, d_skip, wglu_bf)


NEG_INF = float("-inf")
N_STEP_SCALARS = 5 * N_HEADS


def _log_sigmoid(x):
    return jnp.minimum(x, 0.0) - jnp.log1p(jnp.exp(-jnp.abs(x)))


def _split3(x):
    hi = x.astype(BF16)
    r1 = x - hi.astype(F32)
    mid = r1.astype(BF16)
    lo = (r1 - mid.astype(F32)).astype(BF16)
    return hi, mid, lo


def _head_norm_gate(h, o, g):
    mu = jnp.mean(h, axis=-1, keepdims=True)
    hc = h - mu
    var = jnp.mean(hc * hc, axis=-1, keepdims=True)
    return jax.nn.sigmoid(o) * (hc * lax.rsqrt(var + LN_EPS) * g)


def _dot_nt(a, b):
    return lax.dot_general(a, b, (((1,), (1,)), ((), ())), preferred_element_type=F32)


def _dot_tn(a, b):
    return lax.dot_general(a, b, (((0,), (0,)), ((), ())), preferred_element_type=F32)


def _mlstm_seq_body(n_pad, sxm_ref, sv_ref, so_ref, sg_ref, pxm_ref, pv_ref, po_ref, pg_ref,
                    cw_ref, cb_ref, wq_ref, wk_ref, ng_ref,
                    y_ref, c_out_ref, n_out_ref, m_out_ref, c_ref, n_ref, m_ref, prev_ref):
    t = pl.program_id(1)
    L = TILE_T

    @pl.when(t == 0)
    def _():
        c_ref[...] = jnp.zeros_like(c_ref)
        n_ref[...] = jnp.zeros_like(n_ref)
        m_ref[...] = jnp.zeros_like(m_ref)
        prev_ref[...] = jnp.zeros_like(prev_ref)

    first = t == 0
    row = lax.broadcasted_iota(jnp.int32, (L, 1), 0)
    valid = jnp.logical_or(jnp.logical_not(first), row >= n_pad)
    xm = jnp.where(valid, jnp.where(first, sxm_ref[...], pxm_ref[...]), 0.0)
    v = jnp.where(first, sv_ref[...], pv_ref[...])
    o = jnp.where(first, so_ref[...], po_ref[...])
    gt = jnp.where(first, sg_ref[...], pg_ref[...])

    prev = prev_ref[...]

    def shifted(j):
        if j == 0:
            return xm
        return pltpu.roll(jnp.where(row >= L - j, prev, xm), j, axis=0)

    xc = cb_ref[...]
    for j in range(CONV_W):
        xc = xc + shifted(CONV_W - 1 - j) * cw_ref[j:j + 1, :]
    prev_ref[...] = xm
    xc = xc * jax.nn.sigmoid(xc)

    ig = jnp.where(valid, gt, NEG_INF)
    lf = jnp.where(valid, _log_sigmoid(gt), 0.0)
    ti = lax.broadcasted_iota(jnp.int32, (L, L), 0)
    si = lax.broadcasted_iota(jnp.int32, (L, L), 1)
    causal = si <= ti
    tri = jnp.where(causal, 1.0, 0.0).astype(BF16)
    bc = sum(jnp.dot(tri, p, preferred_element_type=F32) for p in _split3(lf))
    ig_t = ig.T
    bc_t = bc.T

    for h in range(N_HEADS):
        hs = slice(h * DH, (h + 1) * DH)
        b_col = bc[:, N_HEADS + h:N_HEADS + h + 1]
        b_row = bc_t[N_HEADS + h:N_HEADS + h + 1, :]
        ig_row = ig_t[h:h + 1, :]
        ig_col = ig[:, h:h + 1]
        m_prev = m_ref[h:h + 1, 0:1]
        dlog = jnp.where(causal, b_col - b_row + ig_row, NEG_INF)
        inter = b_col + m_prev
        m_t = jnp.maximum(jnp.max(dlog, axis=1, keepdims=True), inter)
        w = jnp.exp(dlog - m_t)
        g = jnp.exp(inter - m_t)
        xh = xc[:, hs].astype(BF16)
        q = jnp.dot(xh, wq_ref[h], preferred_element_type=F32)
        k = jnp.dot(xh, wk_ref[h], preferred_element_type=F32) * (DH ** -0.5)
        qb = q.astype(BF16)
        kb = k.astype(BF16)
        s = _dot_nt(qb, kb) * w
        vh = v[:, hs]
        cmat = c_ref[h]
        n_row = n_ref[h]
        num = jnp.dot(s.astype(BF16), vh.astype(BF16), preferred_element_type=F32) \
            + g * _dot_nt(qb, cmat.astype(BF16))
        den = jnp.sum(s, axis=1, keepdims=True) + g * jnp.sum(q * n_row, axis=1, keepdims=True)
        hh = num / jnp.maximum(jnp.abs(den), jnp.exp(-m_t))
        b_last = b_col[L - 1:L, :]
        wlog = b_last - b_col + ig_col
        m_new = jnp.maximum(b_last + m_prev, jnp.max(wlog, axis=0, keepdims=True))
        w_end = jnp.exp(wlog - m_new)
        g_end = jnp.exp(b_last + m_prev - m_new)
        c_ref[h] = g_end * cmat + _dot_tn((vh * w_end).astype(BF16), kb)
        n_ref[h] = g_end * n_row + jnp.sum(w_end * k, axis=0, keepdims=True)
        m_ref[h:h + 1, :] = jnp.broadcast_to(m_new, (1, LANES))
        y_ref[:, hs] = _head_norm_gate(hh, o[:, hs], ng_ref[:, hs])

    @pl.when(t == pl.num_programs(1) - 1)
    def _():
        c_out_ref[0] = c_ref[...]
        n_out_ref[0] = n_ref[...]
        m_out_ref[0] = m_ref[...]


def _mlstm_seq(z_small, z_seq, conv_w, conv_b, wq_bf, wk_bf, norm_g, n_batch, n_tiles, n_pad):
    per_seq = n_tiles - 1
    full = lambda a: pl.BlockSpec(a.shape, lambda b, t: (0,) * a.ndim)
    nb = D_ML // LANES

    def small(col, width):
        return pl.BlockSpec((TILE_T, width), lambda b, t: (0, col))

    def seq(col, width):
        return pl.BlockSpec((TILE_T, width), lambda b, t: (b * per_seq + jnp.maximum(t - 1, 0), col))

    return pl.pallas_call(
        functools.partial(_mlstm_seq_body, n_pad),
        grid=(n_batch, n_tiles),
        in_specs=[small(1, D_ML), small(2, D_ML), small(3, D_ML), small(GATE_COL, LANES),
                  seq(1, D_ML), seq(2, D_ML), seq(3, D_ML), seq(GATE_COL, LANES),
                  full(conv_w), full(conv_b), full(wq_bf), full(wk_bf), full(norm_g)],
        out_specs=[
            pl.BlockSpec((TILE_T, D_ML), lambda b, t: (b * n_tiles + t, 0)),
            pl.BlockSpec((1, N_HEADS, DH, DH), lambda b, t: (b, 0, 0, 0)),
            pl.BlockSpec((1, N_HEADS, 1, DH), lambda b, t: (b, 0, 0, 0)),
            pl.BlockSpec((1, SUBLANES, LANES), lambda b, t: (b, 0, 0)),
        ],
        out_shape=[
            jax.ShapeDtypeStruct((n_batch * n_tiles * TILE_T, D_ML), F32),
            jax.ShapeDtypeStruct((n_batch, N_HEADS, DH, DH), F32),
            jax.ShapeDtypeStruct((n_batch, N_HEADS, 1, DH), F32),
            jax.ShapeDtypeStruct((n_batch, SUBLANES, LANES), F32),
        ],
        scratch_shapes=[pltpu.VMEM((N_HEADS, DH, DH), F32), pltpu.VMEM((N_HEADS, 1, DH), F32),
                        pltpu.VMEM((SUBLANES, LANES), F32), pltpu.VMEM((TILE_T, D_ML), F32)],
        compiler_params=_cparams(("arbitrary", "arbitrary"), 48),
        name="mlstm_seq",
    )(z_small, z_small, z_small, z_small, z_seq, z_seq, z_seq, z_seq, conv_w, conv_b, wq_bf, wk_bf, norm_g)


def _mlstm_step_a_body(xm_ref, g_ref, conv0_ref, m0_ref, cw_ref, cb_ref, wq_ref, wk_ref,
                       q_ref, k_ref, sc_ref):
    xc = cb_ref[...]
    for j in range(CONV_W - 1):
        xc = xc + conv0_ref[j] * cw_ref[j:j + 1, :]
    xc = xc + xm_ref[...] * cw_ref[CONV_W - 1:CONV_W, :]
    xc = xc * jax.nn.sigmoid(xc)
    gt = g_ref[...]
    ig = gt[:, 0:N_HEADS]
    lf = _log_sigmoid(gt[:, N_HEADS:2 * N_HEADS])
    inter = lf + m0_ref[...]
    m_t = jnp.maximum(ig, inter)
    w = jnp.exp(ig - m_t)
    g = jnp.exp(inter - m_t)
    qks = []
    for h in range(N_HEADS):
        hs = slice(h * DH, (h + 1) * DH)
        xh = xc[:, hs].astype(BF16)
        q = jnp.dot(xh, wq_ref[h], preferred_element_type=F32)
        k = jnp.dot(xh, wk_ref[h], preferred_element_type=F32) * (DH ** -0.5)
        q_ref[:, hs] = q
        k_ref[:, hs] = k
        qks.append(jnp.sum(q * k, axis=1, keepdims=True))
    s = jnp.concatenate(qks, axis=1) * w
    rows = s.shape[0]
    sc_ref[...] = jnp.concatenate(
        [s, w, g, m_t, jnp.exp(-m_t), jnp.zeros((rows, LANES - 5 * N_HEADS), F32)], axis=1)


def _mlstm_step_b_body(bb, sc_ref, q_ref, k_ref, n_ref, v_ref, o_ref, ng_ref, c_ref,
                       y_ref, c_out_ref, n_out_ref):
    i0 = pl.program_id(0) * bb
    row = lax.broadcasted_iota(jnp.int32, (MXU_ROWS, DH), 0)

    def hi_lo(x):
        hi = x.astype(BF16).astype(F32)
        return jnp.broadcast_to(hi, (MXU_ROWS, DH)), jnp.broadcast_to(x - hi, (MXU_ROWS, DH))

    for i in range(bb):
        for h in range(N_HEADS):
            hs = slice(h * DH, (h + 1) * DH)
            base = (i0 + i) * N_STEP_SCALARS
            s = sc_ref[base + h]
            w = sc_ref[base + N_HEADS + h]
            g = sc_ref[base + 2 * N_HEADS + h]
            em = sc_ref[base + 4 * N_HEADS + h]
            rsel = pl.ds(i0 + i, 1)
            q_row = q_ref[rsel, hs]
            k_row = k_ref[rsel, hs]
            n_row = n_ref[rsel, hs]
            v_row = v_ref[rsel, hs]
            cmat = c_ref[i, h]
            qh, ql = hi_lo(q_row)
            qmat = jnp.where(row == 0, qh, jnp.where(row == 1, ql, 0.0)).astype(BF16)
            cq = _dot_nt(qmat, cmat.astype(BF16))
            num = s * v_row + g * (cq[0:1, :] + cq[1:2, :])
            den = s + g * jnp.sum(n_row * q_row, axis=1, keepdims=True)
            hh = num / jnp.maximum(jnp.abs(den), em)
            vh, vl = hi_lo(w * v_row)
            kh, kl = hi_lo(k_row)
            a = jnp.where(row < 2, vh, jnp.where(row < 4, vl, 0.0)).astype(BF16)
            b = jnp.where(row < 4, jnp.where(row % 2 == 0, kh, kl), 0.0).astype(BF16)
            c_out_ref[i, h] = g * cmat + _dot_tn(a, b)
            n_out_ref[rsel, hs] = g * n_row + w * k_row
            y_ref[rsel, hs] = _head_norm_gate(hh, o_ref[rsel, hs], ng_ref[:, hs])


def _mlstm_step(z_small, row_blk, conv0, c0, n0, m0, conv_w, conv_b, wq_bf, wk_bf, norm_g, bb=4):
    rows = c0.shape[0]
    full = lambda a: pl.BlockSpec(a.shape, lambda i: (0,) * a.ndim)
    conv0_t = conv0.transpose(1, 0, 2)
    q, k, sc = pl.pallas_call(
        _mlstm_step_a_body,
        grid=(1,),
        in_specs=[pl.BlockSpec((rows, D_ML), lambda i: (row_blk, 1)),
                  pl.BlockSpec((rows, LANES), lambda i: (row_blk, GATE_COL)),
                  full(conv0_t), full(m0), full(conv_w), full(conv_b), full(wq_bf), full(wk_bf)],
        out_specs=[pl.BlockSpec((rows, D_ML), lambda i: (0, 0)), pl.BlockSpec((rows, D_ML), lambda i: (0, 0)),
                   pl.BlockSpec((rows, LANES), lambda i: (0, 0))],
        out_shape=[jax.ShapeDtypeStruct((rows, D_ML), F32), jax.ShapeDtypeStruct((rows, D_ML), F32),
                   jax.ShapeDtypeStruct((rows, LANES), F32)],
        compiler_params=_cparams(("arbitrary",), 48),
        name="mlstm_step_a",
    )(z_small, z_small, conv0_t, m0, conv_w, conv_b, wq_bf, wk_bf)
    row_spec = pl.BlockSpec((rows, D_ML), lambda i: (0, 0))
    c_spec = pl.BlockSpec((bb, N_HEADS, DH, DH), lambda i: (i, 0, 0, 0))
    y, c_new, n_new = pl.pallas_call(
        functools.partial(_mlstm_step_b_body, bb),
        grid=(rows // bb,),
        in_specs=[pl.BlockSpec(memory_space=pltpu.SMEM), row_spec, row_spec, row_spec,
                  pl.BlockSpec((rows, D_ML), lambda i: (row_blk, 2)),
                  pl.BlockSpec((rows, D_ML), lambda i: (row_blk, 3)),
                  pl.BlockSpec((1, D_ML), lambda i: (0, 0)), c_spec],
        out_specs=[row_spec, c_spec, row_spec],
        out_shape=[jax.ShapeDtypeStruct((rows, D_ML), F32), jax.ShapeDtypeStruct(c0.shape, F32),
                   jax.ShapeDtypeStruct((rows, D_ML), F32)],
        compiler_params=_cparams(("arbitrary",), 56),
        name="mlstm_step_b",
    )(sc[:, :N_STEP_SCALARS].reshape(rows * N_STEP_SCALARS), q, k, n0, z_small, z_small, norm_g, c0)
    return y, c_new, n_new, sc[:, 3 * N_HEADS:4 * N_HEADS]


ROW_CH = D_MODEL // LANES


def _load_row_tiles(ref, row0, m, c):
    return ref[pl.ds(row0 * ROW_CH + c, m, stride=ROW_CH), :]


def _store_row_tiles(ref, row0, val):
    m = val.shape[0]
    for c in range(ROW_CH):
        ref[pl.ds(row0 * ROW_CH + c, m, stride=ROW_CH), :] = val[:, c * LANES:(c + 1) * LANES]


def _mix_out_body(xp_ref, xs_ref, ysp0_ref, ysp1_ref, yss_ref, ymp0_ref, ymp1_ref, yms_ref,
                  gin_ref, bin_ref, wout_ref, g1_ref, b1_ref, rwh_ref, rwl_ref, rb_ref,
                  h1_ref, e_ref, gate_ref):
    is_step = pl.program_id(0) == pl.num_programs(0) - 1
    consts = (gin_ref, bin_ref, wout_ref, g1_ref, b1_ref, rwh_ref, rwl_ref, rb_ref)
    outs = (h1_ref, e_ref, gate_ref)
    two = lambda a, b: jnp.concatenate([a[...], b[...]], axis=0)

    @pl.when(jnp.logical_not(is_step))
    def _():
        _mix_out_tile(xp_ref[...], two(ysp0_ref, ysp1_ref), two(ymp0_ref, ymp1_ref), consts, outs)

    @pl.when(is_step)
    def _():
        _mix_out_tile(two(xs_ref, xs_ref), two(yss_ref, yss_ref), two(yms_ref, yms_ref), consts, outs)


def _mix_out_tile(x, ys, ym, consts, outs):
    gin_ref, bin_ref, wout_ref, g1_ref, b1_ref, rwh_ref, rwl_ref, rb_ref = consts
    h1_ref, e_ref, gate_ref = outs
    hp = _layer_norm(x, gin_ref[...], bin_ref[...])
    ycat = jnp.concatenate([ys, ym], axis=1).astype(BF16)
    mix = jnp.dot(ycat, wout_ref[...], preferred_element_type=F32)
    h1 = _layer_norm(DEEPNORM_ALPHA * hp + mix, g1_ref[...], b1_ref[...])
    _store_row_tiles(h1_ref, 0, h1)
    xh = h1.astype(BF16)
    xl = (h1 - xh.astype(F32)).astype(BF16)
    logits = (jnp.dot(xh, rwh_ref[...], preferred_element_type=F32)
              + jnp.dot(xh, rwl_ref[...], preferred_element_type=F32)
              + jnp.dot(xl, rwh_ref[...], preferred_element_type=F32)) + rb_ref[...]
    rows = logits.shape[0]
    lane = lax.broadcasted_iota(jnp.int32, (rows, LANES), 1)
    logits = jnp.where(lane < N_EXPERTS, logits, NEG_INF)
    e_acc = jnp.zeros((rows, LANES), jnp.int32)
    v_acc = jnp.full((rows, LANES), NEG_INF, F32)
    for k in range(TOP_K):
        mx = jnp.max(logits, axis=1, keepdims=True)
        idx = jnp.min(jnp.where(logits == mx, lane, LANES), axis=1, keepdims=True)
        e_acc = jnp.where(lane == k, idx, e_acc)
        v_acc = jnp.where(lane == k, mx, v_acc)
        logits = jnp.where(lane == idx, NEG_INF, logits)
    p = jnp.exp(v_acc - jnp.max(v_acc, axis=1, keepdims=True))
    e_ref[...] = e_acc
    gate_ref[...] = p / jnp.sum(p, axis=1, keepdims=True)


def _mix_out(x_p, x_s, ys_p, ys_s, ym_p, ym_s, consts, tiles_per_seq):
    d = D_MODEL
    full = lambda a: pl.BlockSpec(a.shape, lambda i: (0,) * a.ndim)
    tm = 2 * TILE_T
    n_p = x_p.shape[0] // tm
    n_tiles = n_p + 1
    rows = n_tiles * tm

    def p_idx(i):
        return jnp.minimum(i, n_p - 1)

    def frame_idx(i, half):
        j = 2 * p_idx(i) + half
        return (j // tiles_per_seq) * (tiles_per_seq + 1) + j % tiles_per_seq + 1

    frame = lambda width, half: pl.BlockSpec((TILE_T, width), lambda i: (frame_idx(i, half), 0))
    first = lambda width: pl.BlockSpec((TILE_T, width), lambda i: (0, 0))
    return pl.pallas_call(
        _mix_out_body,
        grid=(n_tiles,),
        in_specs=[pl.BlockSpec((tm, d), lambda i: (p_idx(i), 0)), first(d),
                  frame(D_S5, 0), frame(D_S5, 1), first(D_S5),
                  frame(D_ML, 0), frame(D_ML, 1), first(D_ML)] + [full(a) for a in consts],
        out_specs=[pl.BlockSpec((tm * ROW_CH, LANES), lambda i: (i, 0)),
                   pl.BlockSpec((tm, LANES), lambda i: (i, 0)),
                   pl.BlockSpec((tm, LANES), lambda i: (i, 0))],
        out_shape=[jax.ShapeDtypeStruct((rows * ROW_CH, LANES), F32), jax.ShapeDtypeStruct((rows, LANES), jnp.int32),
                   jax.ShapeDtypeStruct((rows, LANES), F32)],
        compiler_params=_cparams(("arbitrary",), 56),
        name="mix_out",
    )(x_p, x_s, ys_p, ys_p, ys_s, ym_p, ym_p, ym_s, *consts)


MOE_BLK = 128
MOE_TM = 1536
MOE_TF = 256
MOE_SUB_MAX = 5
DMA_UNROLL = 8
MOE_STAGES = 4
W_DMA_SPLIT = 4
SCALAR_UNROLL = 32


def _invert_rows_body(dest_ref, src_ref):
    def zero(i, c):
        for q in range(SCALAR_UNROLL):
            src_ref[i * SCALAR_UNROLL + q] = 0
        return c
    lax.fori_loop(0, src_ref.shape[0] // SCALAR_UNROLL, zero, 0)

    def put(i, c):
        for q in range(SCALAR_UNROLL):
            src_ref[dest_ref[i * SCALAR_UNROLL + q]] = i * (SCALAR_UNROLL // TOP_K) + q // TOP_K
        return c
    lax.fori_loop(0, dest_ref.shape[0] // SCALAR_UNROLL, put, 0)


def _invert_rows(dest_flat, rows_pad):
    return pl.pallas_call(
        _invert_rows_body,
        in_specs=[pl.BlockSpec(memory_space=pltpu.SMEM)],
        out_specs=pl.BlockSpec(memory_space=pltpu.SMEM),
        out_shape=jax.ShapeDtypeStruct((rows_pad,), jnp.int32),
        name="invert_rows",
    )(dest_flat)


def _moe_routing(top_e, n_tok):
    n_pairs = n_tok * TOP_K
    rows_pad = n_pairs + N_EXPERTS * MOE_BLK
    n_pass = N_EXPERTS + rows_pad // MOE_TM
    onehot = (top_e[:, :, None] == jnp.arange(N_EXPERTS, dtype=jnp.int32)).astype(jnp.int32).sum(1)
    incl = jnp.cumsum(onehot, axis=0)
    counts = incl[-1]
    rank = jnp.take_along_axis(incl - onehot, top_e, axis=1)
    padded = (counts + MOE_BLK - 1) // MOE_BLK * MOE_BLK
    pad_end = jnp.cumsum(padded)
    pad_start = pad_end - padded
    dest = pad_start[top_e] + rank
    src = _invert_rows(dest.reshape(-1).astype(jnp.int32), rows_pad)
    passes_e = (padded + MOE_TM - 1) // MOE_TM
    pass_end = jnp.cumsum(passes_e)
    u = jnp.arange(n_pass, dtype=jnp.int32)
    e_u = jnp.minimum(jnp.searchsorted(pass_end, u, side="right"), N_EXPERTS - 1).astype(jnp.int32)
    j_u = u - (pass_end - passes_e)[e_u]
    rem = padded[e_u] - j_u * MOE_TM
    nblk = jnp.where(u < pass_end[-1], jnp.clip(rem, 0, MOE_TM) // MOE_BLK, 0).astype(jnp.int32)
    blk0 = ((pad_start[e_u] + j_u * MOE_TM) // MOE_BLK).astype(jnp.int32)
    blk0 = jnp.where(nblk > 0, blk0, 0)
    last_e = e_u[jnp.maximum(pass_end[-1] - 1, 0)]
    e_u = jnp.where(nblk > 0, e_u, last_e)
    nblk = jnp.concatenate([nblk, (pad_end[-1:] // MOE_BLK).astype(jnp.int32)])
    return dest.astype(jnp.int32), src, e_u, blk0, nblk


def _moe_ffn_body(e_ref, blk0_ref, nblk_ref, src_ref, h1_ref, wgu_ref, wdn_ref, bgu_ref, bd_ref,
                  ybuf_ref, xg_ref, xb_ref, acc_ref, w32_a, w32_b, wb_a, wb_b, stage_ref, gsem, osem, wsem):
    u = pl.program_id(0)
    n_pass = pl.num_programs(0)
    n_f = D_FF // MOE_TF
    nblk = nblk_ref[u]
    row0 = blk0_ref[u] * MOE_BLK
    blk_rt = MOE_BLK * ROW_CH
    nxt = jnp.minimum(u + 1, n_pass - 1)
    nxt_active = jnp.logical_and(u + 1 < n_pass, nblk_ref[nxt] > 0)
    w32 = (w32_a, w32_b)
    wb = (wb_a, wb_b)

    def w_copies(p, f, par):
        e = e_ref[p]
        cols = pl.ds(pl.multiple_of(f * MOE_TF, MOE_TF), MOE_TF)
        cols_up = pl.ds(pl.multiple_of(D_FF + f * MOE_TF, MOE_TF), MOE_TF)
        g32, u32, d32 = w32[par]
        out = []
        for i in range(W_DMA_SPLIT):
            r1 = pl.ds(i * (D_MODEL // W_DMA_SPLIT), D_MODEL // W_DMA_SPLIT)
            r2 = pl.ds(i * (MOE_TF // W_DMA_SPLIT), MOE_TF // W_DMA_SPLIT)
            rows_dn = pl.ds(pl.multiple_of(f * MOE_TF + i * (MOE_TF // W_DMA_SPLIT), MOE_TF // W_DMA_SPLIT),
                            MOE_TF // W_DMA_SPLIT)
            out.append(pltpu.make_async_copy(wgu_ref.at[e, r1, cols], g32.at[r1], wsem.at[par]))
            out.append(pltpu.make_async_copy(wgu_ref.at[e, r1, cols_up], u32.at[r1], wsem.at[par]))
            out.append(pltpu.make_async_copy(wdn_ref.at[e, rows_dn, :], d32.at[r2], wsem.at[par]))
        return out

    def cast_w_half(par, half):
        g32, u32, d32 = w32[par]
        w1, w2 = wb[par]
        r1 = pl.ds(pl.multiple_of(half * (D_MODEL // 2), D_MODEL // 2), D_MODEL // 2)
        r2 = pl.ds(pl.multiple_of(half * (MOE_TF // 2), MOE_TF // 2), MOE_TF // 2)
        w1[r1, :MOE_TF] = g32[r1, :].astype(BF16)
        w1[r1, MOE_TF:] = u32[r1, :].astype(BF16)
        w2[r2, :] = d32[r2, :].astype(BF16)

    def cast_w(par):
        cast_w_half(par, 0)
        cast_w_half(par, 1)

    def for_all_rows(p, fn):
        base = blk0_ref[p] * MOE_BLK

        def body(rb, c):
            for s in range(SUBLANES):
                tok = src_ref[base + rb * SUBLANES + s]
                fn(pltpu.make_async_copy(h1_ref.at[pl.ds(tok * ROW_CH, ROW_CH)],
                                         xg_ref.at[rb, pl.ds(0, ROW_CH), s], gsem))
            return c
        lax.fori_loop(0, nblk_ref[p] * (MOE_BLK // SUBLANES), body, 0)

    def issue_gather(p):
        for_all_rows(p, lambda cp: cp.start())

    def out_copy(b):
        dst = pl.ds(pl.multiple_of((row0 + b * MOE_BLK) * ROW_CH, blk_rt), blk_rt)
        return pltpu.make_async_copy(stage_ref.at[b % MOE_STAGES], ybuf_ref.at[dst], osem.at[b % MOE_STAGES])

    @pl.when(nblk > 0)
    def _():
        @pl.when(u == 0)
        def _():
            issue_gather(u)
            for cp in w_copies(u, 0, 0):
                cp.start()
            for cp in w_copies(u, 0, 0):
                cp.wait()
            cast_w(0)
            for cp in w_copies(u, 1, 1):
                cp.start()

        for_all_rows(u, lambda cp: cp.wait())

        def cast(b, c):
            rs = pl.ds(pl.multiple_of(b * MOE_BLK, MOE_BLK), MOE_BLK)
            gs = pl.ds(pl.multiple_of(b * (MOE_BLK // SUBLANES), MOE_BLK // SUBLANES), MOE_BLK // SUBLANES)
            for ch in range(ROW_CH):
                xb_ref[rs, ch * LANES:(ch + 1) * LANES] = xg_ref[gs, ch].reshape(MOE_BLK, LANES).astype(BF16)
            acc_ref[rs, :] = jnp.zeros((MOE_BLK, acc_ref.shape[1]), F32)
            return c
        lax.fori_loop(0, nblk, cast, 0)

        def ffn_rows(r0, m, f, par):
            w1, w2 = wb[par]
            bg = bgu_ref[0, :, pl.ds(pl.multiple_of(f * MOE_TF, MOE_TF), MOE_TF)]
            bu = bgu_ref[0, :, pl.ds(pl.multiple_of(D_FF + f * MOE_TF, MOE_TF), MOE_TF)]
            rs = pl.ds(pl.multiple_of(r0, MOE_BLK), m)
            h = jnp.dot(xb_ref[rs, :], w1[...], preferred_element_type=F32)
            x_glu = jnp.minimum(h[:, :MOE_TF] + bg, SWIGLU_LIMIT)
            x_lin = jnp.clip(h[:, MOE_TF:] + bu, -SWIGLU_LIMIT, SWIGLU_LIMIT)
            act = x_glu * jax.nn.sigmoid(SWIGLU_ALPHA * x_glu) * (x_lin + 1.0)
            acc_ref[rs, :] += jnp.dot(act.astype(BF16), w2[...], preferred_element_type=F32)

        def tile_step(f, par):
            in_pass1 = f + 1 < n_f
            in_pass2 = f + 2 < n_f

            @pl.when(jnp.logical_or(in_pass1, nxt_active))
            def _():
                for cp in w_copies(jnp.where(in_pass1, u, nxt), jnp.where(in_pass1, f + 1, 0), 1 - par):
                    cp.wait()

            @pl.when(jnp.logical_or(in_pass2, nxt_active))
            def _():
                for cp in w_copies(jnp.where(in_pass2, u, nxt), jnp.where(in_pass2, f + 2, f + 2 - n_f), par):
                    cp.start()

            @pl.when(jnp.logical_and(f == 1, nxt_active))
            def _():
                issue_gather(nxt)

            n_sub = (nblk + (MOE_SUB_MAX - 1)) // MOE_SUB_MAX
            q = nblk // n_sub
            n_hi = nblk - q * n_sub

            def sub(i, blk):
                sz = q + (i < n_hi).astype(jnp.int32)
                for m in range(1, MOE_SUB_MAX + 1):
                    @pl.when(sz == m)
                    def _(m=m):
                        ffn_rows(blk * MOE_BLK, m * MOE_BLK, f, par)
                        cast_w_half(1 - par, jnp.minimum(i, 1))
                return blk + sz
            lax.fori_loop(0, n_sub, sub, 0)

            @pl.when(n_sub == 1)
            def _():
                cast_w_half(1 - par, 1)

        def tile_pair(f2, c):
            tile_step(2 * f2, 0)
            tile_step(2 * f2 + 1, 1)
            return c
        lax.fori_loop(0, n_f // 2, tile_pair, 0)

        def emit(b, c):
            @pl.when(b >= MOE_STAGES)
            def _():
                out_copy(b - MOE_STAGES).wait()
            rs = pl.ds(pl.multiple_of(b * MOE_BLK, MOE_BLK), MOE_BLK)
            val = acc_ref[rs, :] + bd_ref[0]
            for ch in range(ROW_CH):
                stage_ref[b % MOE_STAGES, pl.ds(ch, MOE_BLK, stride=ROW_CH), :] = val[:, ch * LANES:(ch + 1) * LANES]
            out_copy(b).start()
            return c
        lax.fori_loop(0, nblk, emit, 0)

        def drain(b, c):
            out_copy(b).wait()
            return c
        lax.fori_loop(jnp.maximum(nblk - MOE_STAGES, 0), nblk, drain, 0)

    @pl.when(u == n_pass - 1)
    def _():
        used = nblk_ref[n_pass]
        total = ybuf_ref.shape[0] // blk_rt
        stage_ref[0] = jnp.zeros((blk_rt, LANES), F32)

        def zero_copy(b):
            dst = pl.ds(pl.multiple_of(b * blk_rt, blk_rt), blk_rt)
            return pltpu.make_async_copy(stage_ref.at[0], ybuf_ref.at[dst], osem.at[0])

        def start(b, c):
            zero_copy(b).start()
            return c
        lax.fori_loop(used, total, start, 0)

        def wait(b, c):
            zero_copy(b).wait()
            return c
        lax.fori_loop(used, total, wait, 0)


def _moe_ffn(h1, src, e_u, blk0, nblk, w_gu, b_gu, w_dn, b_dn, rows_pad):
    d = D_MODEL
    n_pass = e_u.shape[0]
    w32_set = (pltpu.VMEM((d, MOE_TF), F32), pltpu.VMEM((d, MOE_TF), F32), pltpu.VMEM((MOE_TF, d), F32))
    wb_set = (pltpu.VMEM((d, 2 * MOE_TF), BF16), pltpu.VMEM((MOE_TF, d), BF16))
    grid_spec = pltpu.PrefetchScalarGridSpec(
        num_scalar_prefetch=4,
        grid=(n_pass,),
        in_specs=[
            pl.BlockSpec(memory_space=pl.ANY),
            pl.BlockSpec(memory_space=pl.ANY),
            pl.BlockSpec(memory_space=pl.ANY),
            pl.BlockSpec((1, 1, 2 * D_FF), lambda u, e, b0, nb, s: (e[u], 0, 0)),
            pl.BlockSpec((1, 1, d), lambda u, e, b0, nb, s: (e[u], 0, 0)),
        ],
        out_specs=pl.BlockSpec(memory_space=pl.ANY),
        scratch_shapes=[pltpu.VMEM((MOE_TM // SUBLANES, ROW_CH, SUBLANES, LANES), F32), pltpu.VMEM((MOE_TM, d), BF16),
                        pltpu.VMEM((MOE_TM, d), F32), w32_set, w32_set, wb_set, wb_set,
                        pltpu.VMEM((MOE_STAGES, MOE_BLK * ROW_CH, LANES), F32),
                        pltpu.SemaphoreType.DMA(()), pltpu.SemaphoreType.DMA((MOE_STAGES,)),
                        pltpu.SemaphoreType.DMA((2,))],
    )
    return pl.pallas_call(
        _moe_ffn_body,
        grid_spec=grid_spec,
        out_shape=jax.ShapeDtypeStruct((rows_pad * ROW_CH, LANES), F32),
        compiler_params=_cparams(("arbitrary",), 60),
        name="moe_ffn",
    )(e_u, blk0, nblk, src, h1, w_gu, w_dn, b_gu.reshape(N_EXPERTS, 1, 2 * D_FF), b_dn.reshape(N_EXPERTS, 1, d))


def _moe_combine_body(dest_ref, h1_ref, gate_ref, g2_ref, b2_ref, ybuf_ref, outp_ref, outs_ref,
                      buf_ref, sem, pre_ref):
    i = pl.program_id(0)
    n_pairs = TILE_T * TOP_K
    slot = i % 2

    def copy(tile, sl, rb, s, k):
        src_row = dest_ref[tile * n_pairs + rb * (SUBLANES * TOP_K) + (s * TOP_K + k)]
        return pltpu.make_async_copy(ybuf_ref.at[pl.ds(src_row * ROW_CH, ROW_CH)],
                                     buf_ref.at[sl, k, rb, pl.ds(0, ROW_CH), s], sem.at[sl])

    def for_all_rows(tile, sl, fn):
        def body(rb, c):
            for s in range(SUBLANES):
                for k in range(TOP_K):
                    fn(copy(tile, sl, rb, s, k))
            return c
        lax.fori_loop(0, TILE_T // SUBLANES, body, 0)

    def issue(tile, sl):
        for_all_rows(tile, sl, lambda cp: cp.start())

    @pl.when(i == 0)
    def _():
        issue(0, 0)

    @pl.when(i + 1 < pl.num_programs(0))
    def _():
        issue(i + 1, 1 - slot)

    for_all_rows(i, slot, lambda cp: cp.wait())

    gate = gate_ref[...]
    cs = lambda c: slice(c * LANES, (c + 1) * LANES)
    part = jnp.zeros((TILE_T, LANES), F32)
    gate_b = [jnp.broadcast_to(gate[:, k:k + 1], (TILE_T, LANES)) for k in range(TOP_K)]
    for c in range(ROW_CH):
        fc = None
        for k in range(TOP_K):
            v = buf_ref[slot, k, :, c].reshape(TILE_T, LANES) * gate_b[k]
            fc = v if fc is None else fc + v
        pre = DEEPNORM_ALPHA * _load_row_tiles(h1_ref, 0, TILE_T, c) + fc
        pre_ref[:, cs(c)] = pre
        part = part + pre
    mu = jnp.sum(part, axis=1, keepdims=True) * (1.0 / D_MODEL)
    part = jnp.zeros((TILE_T, LANES), F32)
    for c in range(ROW_CH):
        dlt = pre_ref[:, cs(c)] - mu
        part = part + dlt * dlt
    rstd = lax.rsqrt(jnp.sum(part, axis=1, keepdims=True) * (1.0 / D_MODEL) + LN_EPS)
    is_step = i == pl.num_programs(0) - 1

    def write(out_ref):
        for c in range(ROW_CH):
            out_ref[:, cs(c)] = (pre_ref[:, cs(c)] - mu) * rstd * g2_ref[:, cs(c)] + b2_ref[:, cs(c)]

    @pl.when(jnp.logical_not(is_step))
    def _():
        write(outp_ref)

    @pl.when(is_step)
    def _():
        write(outs_ref)


def _moe_combine(dest, h1, gates, g2, b2, ybuf, n_tok):
    d = D_MODEL
    n_tiles = n_tok // TILE_T
    grid_spec = pltpu.PrefetchScalarGridSpec(
        num_scalar_prefetch=1,
        grid=(n_tiles,),
        in_specs=[pl.BlockSpec((TILE_T * ROW_CH, LANES), lambda i, s: (i, 0)),
                  pl.BlockSpec((TILE_T, LANES), lambda i, s: (i, 0)),
                  pl.BlockSpec((1, d), lambda i, s: (0, 0)),
                  pl.BlockSpec((1, d), lambda i, s: (0, 0)),
                  pl.BlockSpec(memory_space=pl.ANY)],
        out_specs=[pl.BlockSpec((TILE_T, d), lambda i, s: (jnp.minimum(i, n_tiles - 2), 0)),
                   pl.BlockSpec((TILE_T, d), lambda i, s: (0, 0))],
        scratch_shapes=[pltpu.VMEM((2, TOP_K, TILE_T // SUBLANES, ROW_CH, SUBLANES, LANES), F32),
                        pltpu.SemaphoreType.DMA((2,)),
                        pltpu.VMEM((TILE_T, d), F32)],
    )
    return pl.pallas_call(
        _moe_combine_body,
        grid_spec=grid_spec,
        out_shape=[jax.ShapeDtypeStruct((n_tok - TILE_T, d), F32), jax.ShapeDtypeStruct((TILE_T, d), F32)],
        compiler_params=_cparams(("arbitrary",), 48),
        name="moe_combine",
    )(dest.reshape(-1), h1, gates, g2, b2, ybuf)


def kernel(x_prompt, x_sample, state_s5_re, state_s5_im, state_mlstm_c, state_mlstm_n, state_mlstm_m, state_mlstm_conv, meta_tokens, ln_in_g, ln_in_b, w_in, b_in, s5_a_re, s5_a_im, s5_log_dt, s5_b_re, s5_b_im, s5_c_re, s5_c_im, s5_d, s5_w_glu, mlstm_conv_w, mlstm_conv_b, mlstm_wq, mlstm_wk, mlstm_norm_g, w_out, ln1_g, ln1_b, router_w, router_b, w_gate_up, b_gate_up, w_down, b_down, ln2_g, ln2_b):
    bsz, seq, d = x_prompt.shape
    dec_b = x_sample.shape[0]
    n_pad = TILE_T - N_META
    x_small = jnp.concatenate([jnp.zeros((n_pad, d), F32), meta_tokens, x_sample.reshape(dec_b, d)], axis=0)
    w_in_p = jnp.pad(w_in[0], ((0, 0), (0, N_IN_PAD - N_IN))).astype(BF16)
    b_in_p = jnp.pad(b_in[0], (0, N_IN_PAD - N_IN)).reshape(1, N_IN_PAD)
    g_in = ln_in_g.reshape(1, d)
    bb_in = ln_in_b.reshape(1, d)
    z_p = _inproj(x_prompt.reshape(bsz * seq, d), g_in, bb_in, w_in_p, b_in_p, 512, 1408)
    z_s = _inproj(x_small, g_in, bb_in, w_in_p, b_in_p, 256, 1408)
    tabs = _s5_prep(s5_a_re[0], s5_a_im[0], s5_log_dt[0], s5_b_re[0], s5_b_im[0])
    bd_b, bd_c = _s5_block_diag(tabs[6], tabs[7], s5_c_re[0], s5_c_im[0])
    wglu_bf = s5_w_glu[0].astype(BF16)
    d_skip = s5_d[0].reshape(1, D_S5)
    n_tiles = seq // TILE_T + 1
    zero_state = jnp.zeros((bsz, 1, N_STATE), F32)
    y_s5_p, s5r_p, s5i_p = _s5_seq(z_s, z_p, zero_state, zero_state, tabs[:6], bd_b, bd_c, d_skip, wglu_bf,
                                   bsz, n_tiles, n_pad)
    y_s5_s, s5r_s, s5i_s = _s5_step(z_s, 1, state_s5_re[0].reshape(dec_b, N_STATE),
                                    state_s5_im[0].reshape(dec_b, N_STATE), tabs[:6], bd_b, bd_c, d_skip, wglu_bf)

    conv_w = mlstm_conv_w[0]
    conv_b = mlstm_conv_b[0].reshape(1, D_ML)
    wq_bf = mlstm_wq[0].astype(BF16)
    wk_bf = mlstm_wk[0].astype(BF16)
    norm_g = mlstm_norm_g[0].reshape(1, D_ML)
    y_ml_p, c_p, n_p, m_p = _mlstm_seq(z_s, z_p, conv_w, conv_b, wq_bf, wk_bf, norm_g, bsz, n_tiles, n_pad)
    conv0 = state_mlstm_conv[0]
    y_ml_s, c_s, n_s, m_s = _mlstm_step(z_s, 1, conv0, state_mlstm_c[0], state_mlstm_n[0].reshape(dec_b, D_ML),
                                        state_mlstm_m[0], conv_w, conv_b, wq_bf, wk_bf, norm_g)

    rw = jnp.pad(router_w[0], ((0, 0), (0, LANES - N_EXPERTS)))
    rw_hi = rw.astype(BF16)
    rw_lo = (rw - rw_hi.astype(F32)).astype(BF16)
    rb = jnp.pad(router_b[0], (0, LANES - N_EXPERTS)).reshape(1, LANES)
    consts = (g_in, bb_in, w_out[0].astype(BF16), ln1_g[0].reshape(1, d), ln1_b[0].reshape(1, d), rw_hi, rw_lo, rb)
    h1, top_e, gates = _mix_out(x_prompt.reshape(bsz * seq, d), x_sample.reshape(dec_b, d),
                                y_s5_p, y_s5_s, y_ml_p, y_ml_s, consts, seq // TILE_T)
    n_tok = bsz * seq + dec_b
    dest, src, e_u, blk0, nblk = _moe_routing(top_e[:n_tok, :TOP_K], n_tok)
    ybuf = _moe_ffn(h1, src, e_u, blk0, nblk, w_gate_up[0], b_gate_up[0], w_down[0], b_down[0], src.shape[0])
    out_p, out_s = _moe_combine(dest, h1, gates, ln2_g[0].reshape(1, d), ln2_b[0].reshape(1, d), ybuf, n_tok)

    y_prompt = out_p.reshape(bsz, seq, d)
    y_sample = out_s.reshape(dec_b, 1, d)
    xm_p = z_p.reshape(bsz, seq, N_IN_PAD)[:, seq - (CONV_W - 1):, D_S5:D_S5 + D_ML]
    xm_s = z_s[TILE_T:, D_S5:D_S5 + D_ML]
    conv_s = jnp.concatenate([conv0[:, 1:], xm_s[:, None, :]], axis=1)
    return (y_prompt, y_sample,
            s5r_p.reshape(1, bsz, N_GROUPS, S5_STATE), s5i_p.reshape(1, bsz, N_GROUPS, S5_STATE),
            c_p[None], n_p.reshape(1, bsz, N_HEADS, DH), m_p[None, :, :N_HEADS, 0], xm_p[None],
            s5r_s.reshape(1, dec_b, N_GROUPS, S5_STATE), s5i_s.reshape(1, dec_b, N_GROUPS, S5_STATE),
            c_s[None], n_s.reshape(1, dec_b, N_HEADS, DH), m_s[None], conv_s[None])
```

```python
import functools
import math

import jax
import jax.numpy as jnp
from jax import lax
from jax.experimental import pallas as pl
from jax.experimental.pallas import tpu as pltpu

F32 = jnp.float32
BF16 = jnp.bfloat16

D_MODEL = 2048
N_META = 16
D_S5 = 1024
D_ML = 1024
S5_CH = 16
N_GROUPS = 64
S5_STATE = 64
N_STATE = N_GROUPS * S5_STATE
N_HEADS = 4
DH = 256
CONV_W = 4
N_EXPERTS = 32
TOP_K = 4
D_FF = 2048
SWIGLU_LIMIT = 7.0
SWIGLU_ALPHA = 1.702
LN_EPS = 1e-5
DEEPNORM_ALPHA = 2.0 ** 0.25
N_IN = D_S5 + 3 * D_ML + 2 * N_HEADS

LANES = 128
SUBLANES = 8
MXU_DIM = 256
MXU_ROWS = 16

TILE_T = 128
SEG_LEN = TILE_T // SUBLANES
N_IN_PAD = 33 * LANES
GATE_COL = 4 * D_S5 // LANES
S5_KCH = D_S5 // MXU_DIM
S5_SCH = N_STATE // S5_KCH
SCAN_LW = 256


def _cparams(sem, vmem_mb=None):
    kw = dict(dimension_semantics=sem)
    if vmem_mb is not None:
        kw["vmem_limit_bytes"] = vmem_mb * 1024 * 1024
    return pltpu.CompilerParams(**kw)


def _layer_norm(x, g, b):
    mu = jnp.mean(x, axis=-1, keepdims=True)
    xc = x - mu
    var = jnp.mean(xc * xc, axis=-1, keepdims=True)
    return xc * lax.rsqrt(var + LN_EPS) * g + b


def _inproj_body(x_ref, g_ref, b_ref, w_ref, bias_ref, z_ref, hn_ref):
    @pl.when(pl.program_id(1) == 0)
    def _():
        hn_ref[...] = _layer_norm(x_ref[...], g_ref[...], b_ref[...]).astype(BF16)

    z_ref[...] = jnp.dot(hn_ref[...], w_ref[...], preferred_element_type=F32) + bias_ref[...]


def _inproj(x, g, b, w, bias, tm, tn):
    rows, d = x.shape
    n = w.shape[1]
    return pl.pallas_call(
        _inproj_body,
        grid=(rows // tm, n // tn),
        in_specs=[
            pl.BlockSpec((tm, d), lambda i, j: (i, 0)),
            pl.BlockSpec((1, d), lambda i, j: (0, 0)),
            pl.BlockSpec((1, d), lambda i, j: (0, 0)),
            pl.BlockSpec((d, tn), lambda i, j: (0, j)),
            pl.BlockSpec((1, tn), lambda i, j: (0, j)),
        ],
        out_specs=pl.BlockSpec((tm, tn), lambda i, j: (i, j)),
        out_shape=jax.ShapeDtypeStruct((rows, n), F32),
        scratch_shapes=[pltpu.VMEM((tm, d), BF16)],
        compiler_params=_cparams(("arbitrary", "arbitrary"), 48),
        name="inproj",
    )(x, g, b, w, bias)


def _cmul(ar, ai, br, bi):
    return ar * br - ai * bi, ar * bi + ai * br


def _s5_prep_body(are_ref, aim_ref, dt_ref, bre_ref, bim_ref,
                  pre_ref, pim_ref, hre_ref, him_ref, qre_ref, qim_ref, bbre_ref, bbim_ref):
    lr = are_ref[...]
    li = aim_ref[...]
    dt = jnp.exp(dt_ref[...])
    mag = jnp.exp(lr * dt)
    ar = mag * jnp.cos(li * dt)
    ai = mag * jnp.sin(li * dt)
    nr = ar - 1.0
    ni = ai
    den = lr * lr + li * li
    cr = (nr * lr + ni * li) / den
    ci = (ni * lr - nr * li) / den
    bbre_ref[...] = cr * bre_ref[...] - ci * bim_ref[...]
    bbim_ref[...] = cr * bim_ref[...] + ci * bre_ref[...]
    pr, pi = ar, ai
    for j in range(SEG_LEN):
        pre_ref[j:j + 1, :] = pr
        pim_ref[j:j + 1, :] = pi
        if j + 1 < SEG_LEN:
            pr, pi = _cmul(pr, pi, ar, ai)
    row = lax.broadcasted_iota(jnp.int32, (SUBLANES, N_STATE), 0)
    kr, ki = pr, pi
    for idx, k in enumerate((1, 2, 4)):
        hre_ref[idx * 8:(idx + 1) * 8, :] = jnp.where(row >= k, jnp.broadcast_to(kr, (SUBLANES, N_STATE)), 0.0)
        him_ref[idx * 8:(idx + 1) * 8, :] = jnp.where(row >= k, jnp.broadcast_to(ki, (SUBLANES, N_STATE)), 0.0)
        kr, ki = _cmul(kr, ki, kr, ki)
    qr, qi = pr, pi
    for s in range(SUBLANES):
        qre_ref[s:s + 1, :] = qr
        qim_ref[s:s + 1, :] = qi
        if s + 1 < SUBLANES:
            qr, qi = _cmul(qr, qi, pr, pi)


def _s5_prep(a_re, a_im, log_dt, b_re, b_im):
    n = N_STATE
    are = a_re.reshape(1, n)
    aim = a_im.reshape(1, n)
    dtl = jnp.broadcast_to(log_dt[:, None], (N_GROUPS, S5_STATE)).reshape(1, n)
    bre = b_re.transpose(2, 0, 1).reshape(S5_CH, n)
    bim = b_im.transpose(2, 0, 1).reshape(S5_CH, n)
    shp = lambda r: jax.ShapeDtypeStruct((r, n), F32)
    return pl.pallas_call(
        _s5_prep_body,
        out_shape=(shp(SEG_LEN), shp(SEG_LEN), shp(24), shp(24), shp(8), shp(8), shp(S5_CH), shp(S5_CH)),
        name="s5_prep",
    )(are, aim, dtl, bre, bim)


def _s5_block_diag(bb_re, bb_im, c_re, c_im):
    gpc = 16
    eye = jnp.eye(gpc, dtype=bool)

    def bd_in(bb):
        t = bb.reshape(S5_CH, S5_KCH, gpc, S5_STATE).transpose(1, 2, 0, 3)
        t = jnp.where(eye[None, :, None, :, None], t[:, :, :, None, :], 0.0)
        return t.reshape(S5_KCH, gpc * S5_CH, gpc * S5_STATE)

    def bd_out(c):
        t = c.reshape(S5_KCH, gpc, S5_CH, S5_STATE).transpose(0, 1, 3, 2)
        t = jnp.where(eye[None, :, None, :, None], t[:, :, :, None, :], 0.0)
        return t.reshape(S5_KCH, gpc * S5_STATE, gpc * S5_CH)

    bd_b = jnp.concatenate([bd_in(bb_re), bd_in(bb_im)], axis=2).astype(BF16)
    bd_c = jnp.concatenate([bd_out(c_re), bd_out(-c_im)], axis=1).astype(BF16)
    return bd_b, bd_c


def _gelu_glu(y, wglu_ref):
    y = 0.5 * y * (1.0 + lax.erf(y * math.sqrt(0.5)))
    gate = jnp.dot(y.astype(BF16), wglu_ref[...], preferred_element_type=F32)
    return y * jax.nn.sigmoid(gate)


def _s5_in_proj(u_bf, bdb_ref, bur_ref, bui_ref):
    for c in range(S5_KCH):
        r = jnp.dot(u_bf[:, c * MXU_DIM:(c + 1) * MXU_DIM], bdb_ref[c], preferred_element_type=F32)
        bur_ref[:, c * S5_SCH:(c + 1) * S5_SCH] = r[:, :S5_SCH]
        bui_ref[:, c * S5_SCH:(c + 1) * S5_SCH] = r[:, S5_SCH:]


def _s5_out_proj(xr_ref, xi_ref, bdc_ref):
    ys = []
    for c in range(S5_KCH):
        xr = xr_ref[:, c * S5_SCH:(c + 1) * S5_SCH].astype(BF16)
        xi = xi_ref[:, c * S5_SCH:(c + 1) * S5_SCH].astype(BF16)
        ys.append(jnp.dot(xr, bdc_ref[c, :S5_SCH, :], preferred_element_type=F32)
                  + jnp.dot(xi, bdc_ref[c, S5_SCH:, :], preferred_element_type=F32))
    return jnp.concatenate(ys, axis=1)


def _s5_seq_body(n_pad, us_ref, upa_ref, upb_ref, x0r_ref, x0i_ref, bdb_ref, bdc_ref, pre_ref, pim_ref,
                 hre_ref, him_ref, qre_ref, qim_ref, d_ref, wglu_ref,
                 y_ref, xr_out_ref, xi_out_ref, *scratch):
    t = pl.program_id(1)
    n = N_STATE
    n_lc = D_S5 // LANES
    scan_per_chunk = S5_SCH // SCAN_LW
    seqs = [dict(zip(("bur", "bui", "cr", "ci", "perm"), scratch[5 * i:5 * i + 5]), idx=i, up=up)
            for i, up in enumerate((upa_ref, upb_ref))]

    @pl.when(t == 0)
    def _():
        for sq in seqs:
            sq["cr"][...] = jnp.broadcast_to(x0r_ref[sq["idx"]], (SUBLANES, n))
            sq["ci"][...] = jnp.broadcast_to(x0i_ref[sq["idx"]], (SUBLANES, n))

    def load_perm(ref, perm_ref):
        for c in range(n_lc):
            perm_ref[c] = ref[:, c * LANES:(c + 1) * LANES]
        return jnp.concatenate(
            [jnp.concatenate([perm_ref[c, pl.ds(j, SUBLANES, stride=SEG_LEN), :] for c in range(n_lc)], axis=1)
             for j in range(SEG_LEN)], axis=0)

    prow = lax.broadcasted_iota(jnp.int32, (TILE_T, 1), 0)
    time = (prow % SUBLANES) * SEG_LEN + prow // SUBLANES
    row8 = lax.broadcasted_iota(jnp.int32, (SUBLANES, SCAN_LW), 0)

    def load_u(sq):
        u_first = jnp.where(time >= n_pad, load_perm(us_ref, sq["perm"]), 0.0)
        sq["u"] = jnp.where(t == 0, u_first, load_perm(sq["up"], sq["perm"]))
        sq["u_bf"] = sq["u"].astype(BF16)

    def in_proj(sq, c):
        r = jnp.dot(sq["u_bf"][:, c * MXU_DIM:(c + 1) * MXU_DIM], bdb_ref[c], preferred_element_type=F32)
        sq["bur"][:, c * S5_SCH:(c + 1) * S5_SCH] = r[:, :S5_SCH]
        sq["bui"][:, c * S5_SCH:(c + 1) * S5_SCH] = r[:, S5_SCH:]

    def out_proj(sq, c):
        xr = sq["bur"][:, c * S5_SCH:(c + 1) * S5_SCH].astype(BF16)
        xi = sq["bui"][:, c * S5_SCH:(c + 1) * S5_SCH].astype(BF16)
        sq.setdefault("y", []).append(
            jnp.dot(xr, bdc_ref[c, :S5_SCH, :], preferred_element_type=F32)
            + jnp.dot(xi, bdc_ref[c, S5_SCH:, :], preferred_element_type=F32))

    def finish(sq):
        y = jnp.concatenate(sq["y"], axis=1) + d_ref[...] * sq["u"]
        y = _gelu_glu(y, wglu_ref)
        perm_ref = sq["perm"]
        for j in range(SEG_LEN):
            for c in range(n_lc):
                perm_ref[c, pl.ds(j, SUBLANES, stride=SEG_LEN), :] = y[j * SUBLANES:(j + 1) * SUBLANES,
                                                                       c * LANES:(c + 1) * LANES]
        r0 = sq["idx"] * TILE_T
        for c in range(n_lc):
            y_ref[r0:r0 + TILE_T, c * LANES:(c + 1) * LANES] = perm_ref[c]

    def scan_lanes(sq, lc):
        bur_ref, bui_ref, cr_ref, ci_ref = sq["bur"], sq["bui"], sq["cr"], sq["ci"]
        ls = slice(lc * SCAN_LW, (lc + 1) * SCAN_LW)
        bc = lambda ref, j: jnp.broadcast_to(ref[j:j + 1, ls], (SUBLANES, SCAN_LW))
        ar, ai = bc(pre_ref, 0), bc(pim_ref, 0)
        xr = jnp.zeros((SUBLANES, SCAN_LW), F32)
        xi = jnp.zeros((SUBLANES, SCAN_LW), F32)
        for j in range(SEG_LEN):
            rs = slice(j * SUBLANES, (j + 1) * SUBLANES)
            nr = ar * xr - ai * xi + bur_ref[rs, ls]
            ni = ar * xi + ai * xr + bui_ref[rs, ls]
            xr, xi = nr, ni
            bur_ref[rs, ls] = xr
            bui_ref[rs, ls] = xi
        er, ei = xr, xi
        for idx, k in enumerate((1, 2, 4)):
            sr = pltpu.roll(er, k, axis=0)
            si = pltpu.roll(ei, k, axis=0)
            hr = hre_ref[idx * 8:(idx + 1) * 8, ls]
            hi = him_ref[idx * 8:(idx + 1) * 8, ls]
            er, ei = er + (hr * sr - hi * si), ei + (hr * si + hi * sr)
        cpr = cr_ref[:, ls]
        cpi = ci_ref[:, ls]
        qr = qre_ref[:, ls]
        qi = qim_ref[:, ls]
        er, ei = er + (qr * cpr - qi * cpi), ei + (qr * cpi + qi * cpr)
        inr = jnp.where(row8 == 0, cpr, pltpu.roll(er, 1, axis=0))
        ini = jnp.where(row8 == 0, cpi, pltpu.roll(ei, 1, axis=0))
        cr_ref[:, ls] = jnp.broadcast_to(er[SUBLANES - 1:SUBLANES, :], (SUBLANES, SCAN_LW))
        ci_ref[:, ls] = jnp.broadcast_to(ei[SUBLANES - 1:SUBLANES, :], (SUBLANES, SCAN_LW))
        for j in range(SEG_LEN):
            rs = slice(j * SUBLANES, (j + 1) * SUBLANES)
            pr, pi = bc(pre_ref, j), bc(pim_ref, j)
            bur_ref[rs, ls] = bur_ref[rs, ls] + (pr * inr - pi * ini)
            bui_ref[rs, ls] = bui_ref[rs, ls] + (pr * ini + pi * inr)

    def scan_chunk(sq, c):
        for k in range(scan_per_chunk):
            scan_lanes(sq, c * scan_per_chunk + k)

    sa, sb = seqs
    load_u(sa)
    load_u(sb)
    for c in range(S5_KCH):
        in_proj(sa, c)
    for c in range(S5_KCH):
        in_proj(sb, c)
        scan_chunk(sa, c)
    for c in range(S5_KCH):
        out_proj(sa, c)
        scan_chunk(sb, c)
    finish(sa)
    for c in range(S5_KCH):
        out_proj(sb, c)
    finish(sb)

    @pl.when(t == pl.num_programs(1) - 1)
    def _():
        for sq in seqs:
            xr_out_ref[sq["idx"]] = sq["cr"][0:1, :]
            xi_out_ref[sq["idx"]] = sq["ci"][0:1, :]


def _s5_seq(z_small, z_seq, x0r, x0i, tabs, bd_b, bd_c, d_skip, wglu_bf, n_batch, n_tiles, n_pad):
    pre, pim, hre, him, qre, qim = tabs
    n = N_STATE
    full = lambda a: pl.BlockSpec(a.shape, lambda b, t: (0,) * a.ndim)
    per_seq = n_tiles - 1
    seq_tile = lambda s: pl.BlockSpec((TILE_T, D_S5),
                                      lambda p, t: ((2 * p + s) * per_seq + jnp.maximum(t - 1, 0), 0))
    seq_scratch = [pltpu.VMEM((TILE_T, n), F32), pltpu.VMEM((TILE_T, n), F32),
                   pltpu.VMEM((SUBLANES, n), F32), pltpu.VMEM((SUBLANES, n), F32),
                   pltpu.VMEM((D_S5 // LANES, TILE_T, LANES), F32)]
    return pl.pallas_call(
        functools.partial(_s5_seq_body, n_pad),
        grid=(n_batch // 2, n_tiles),
        in_specs=[
            pl.BlockSpec((TILE_T, D_S5), lambda p, t: (0, 0)),
            seq_tile(0), seq_tile(1),
            pl.BlockSpec((2, 1, n), lambda p, t: (p, 0, 0)),
            pl.BlockSpec((2, 1, n), lambda p, t: (p, 0, 0)),
            full(bd_b), full(bd_c), full(pre), full(pim), full(hre), full(him), full(qre), full(qim),
            full(d_skip), full(wglu_bf),
        ],
        out_specs=[
            pl.BlockSpec((2 * TILE_T, D_S5), lambda p, t: (p * n_tiles + t, 0)),
            pl.BlockSpec((2, 1, n), lambda p, t: (p, 0, 0)),
            pl.BlockSpec((2, 1, n), lambda p, t: (p, 0, 0)),
        ],
        out_shape=[
            jax.ShapeDtypeStruct((n_batch * n_tiles * TILE_T, D_S5), F32),
            jax.ShapeDtypeStruct((n_batch, 1, n), F32),
            jax.ShapeDtypeStruct((n_batch, 1, n), F32),
        ],
        scratch_shapes=seq_scratch + seq_scratch,
        compiler_params=_cparams(("arbitrary", "arbitrary"), 48),
        name="s5_seq",
    )(z_small, z_seq, z_seq, x0r, x0i, bd_b, bd_c, pre, pim, hre, him, qre, qim, d_skip, wglu_bf)


def _s5_frame_block(b, tile, n_tiles):
    return ((b // 2) * n_tiles + tile) * 2 + b % 2


def _s5_step_body(u_ref, x0r_ref, x0i_ref, bdb_ref, bdc_ref, pre_ref, pim_ref, d_ref, wglu_ref,
                  y_ref, xr_ref, xi_ref):
    u = u_ref[...]
    _s5_in_proj(u.astype(BF16), bdb_ref, xr_ref, xi_ref)
    ar = pre_ref[0:1, :]
    ai = pim_ref[0:1, :]
    x0r = x0r_ref[...]
    x0i = x0i_ref[...]
    xr_ref[...] = xr_ref[...] + (ar * x0r - ai * x0i)
    xi_ref[...] = xi_ref[...] + (ar * x0i + ai * x0r)
    y = _s5_out_proj(xr_ref, xi_ref, bdc_ref) + d_ref[...] * u
    y_ref[...] = _gelu_glu(y, wglu_ref)


def _s5_step(z_small, row_blk, x0r, x0i, tabs, bd_b, bd_c, d_skip, wglu_bf):
    pre, pim = tabs[0], tabs[1]
    rows = x0r.shape[0]
    n = N_STATE
    full = lambda a: pl.BlockSpec(a.shape, lambda i: (0,) * a.ndim)
    return pl.pallas_call(
        _s5_step_body,
        grid=(1,),
        in_specs=[pl.BlockSpec((rows, D_S5), lambda i: (row_blk, 0)), full(x0r), full(x0i),
                  full(bd_b), full(bd_c), full(pre), full(pim), full(d_skip), full(wglu_bf)],
        out_specs=[pl.BlockSpec((rows, D_S5), lambda i: (0, 0)),
                   pl.BlockSpec((rows, n), lambda i: (0, 0)), pl.BlockSpec((rows, n), lambda i: (0, 0))],
        out_shape=[jax.ShapeDtypeStruct((rows, D_S5), F32), jax.ShapeDtypeStruct((rows, n), F32),
                   jax.ShapeDtypeStruct((rows, n), F32)],
        compiler_params=_cparams(("arbitrary",), 48),
        name="s5_step",
    )(z_small, x0r, x0i, bd_b, bd_c, pre, pim, d_skip, wglu_bf)


NEG_INF = float("-inf")
N_STEP_SCALARS = 5 * N_HEADS


def _log_sigmoid(x):
    return jnp.minimum(x, 0.0) - jnp.log1p(jnp.exp(-jnp.abs(x)))


def _split3(x):
    hi = x.astype(BF16)
    r1 = x - hi.astype(F32)
    mid = r1.astype(BF16)
    lo = (r1 - mid.astype(F32)).astype(BF16)
    return hi, mid, lo


def _head_norm_gate(h, o, g):
    mu = jnp.mean(h, axis=-1, keepdims=True)
    hc = h - mu
    var = jnp.mean(hc * hc, axis=-1, keepdims=True)
    return jax.nn.sigmoid(o) * (hc * lax.rsqrt(var + LN_EPS) * g)


def _dot_nt(a, b):
    return lax.dot_general(a, b, (((1,), (1,)), ((), ())), preferred_element_type=F32)


def _dot_tn(a, b):
    return lax.dot_general(a, b, (((0,), (0,)), ((), ())), preferred_element_type=F32)


def _mlstm_seq_body(n_pad, sxm_ref, sv_ref, so_ref, sg_ref, pxm_ref, pv_ref, po_ref, pg_ref,
                    cw_ref, cb_ref, wq_ref, wk_ref, ng_ref,
                    y_ref, c_out_ref, n_out_ref, m_out_ref, c_ref, n_ref, m_ref, prev_ref):
    t = pl.program_id(1)
    L = TILE_T

    @pl.when(t == 0)
    def _():
        c_ref[...] = jnp.zeros_like(c_ref)
        n_ref[...] = jnp.zeros_like(n_ref)
        m_ref[...] = jnp.zeros_like(m_ref)
        prev_ref[...] = jnp.zeros_like(prev_ref)

    first = t == 0
    row = lax.broadcasted_iota(jnp.int32, (L, 1), 0)
    valid = jnp.logical_or(jnp.logical_not(first), row >= n_pad)
    xm = jnp.where(valid, jnp.where(first, sxm_ref[...], pxm_ref[...]), 0.0)
    v = jnp.where(first, sv_ref[...], pv_ref[...])
    o = jnp.where(first, so_ref[...], po_ref[...])
    gt = jnp.where(first, sg_ref[...], pg_ref[...])

    prev = prev_ref[...]

    def shifted(j):
        if j == 0:
            return xm
        return pltpu.roll(jnp.where(row >= L - j, prev, xm), j, axis=0)

    xc = cb_ref[...]
    for j in range(CONV_W):
        xc = xc + shifted(CONV_W - 1 - j) * cw_ref[j:j + 1, :]
    prev_ref[...] = xm
    xc = xc * jax.nn.sigmoid(xc)

    ig = jnp.where(valid, gt, NEG_INF)
    lf = jnp.where(valid, _log_sigmoid(gt), 0.0)
    ti = lax.broadcasted_iota(jnp.int32, (L, L), 0)
    si = lax.broadcasted_iota(jnp.int32, (L, L), 1)
    causal = si <= ti
    tri = jnp.where(causal, 1.0, 0.0).astype(BF16)
    bc = sum(jnp.dot(tri, p, preferred_element_type=F32) for p in _split3(lf))
    ig_t = ig.T
    bc_t = bc.T

    for h in range(N_HEADS):
        hs = slice(h * DH, (h + 1) * DH)
        b_col = bc[:, N_HEADS + h:N_HEADS + h + 1]
        b_row = bc_t[N_HEADS + h:N_HEADS + h + 1, :]
        ig_row = ig_t[h:h + 1, :]
        ig_col = ig[:, h:h + 1]
        m_prev = m_ref[h:h + 1, 0:1]
        dlog = jnp.where(causal, b_col - b_row + ig_row, NEG_INF)
        inter = b_col + m_prev
        m_t = jnp.maximum(jnp.max(dlog, axis=1, keepdims=True), inter)
        w = jnp.exp(dlog - m_t)
        g = jnp.exp(inter - m_t)
        xh = xc[:, hs].astype(BF16)
        q = jnp.dot(xh, wq_ref[h], preferred_element_type=F32)
        k = jnp.dot(xh, wk_ref[h], preferred_element_type=F32) * (DH ** -0.5)
        qb = q.astype(BF16)
        kb = k.astype(BF16)
        s = _dot_nt(qb, kb) * w
        vh = v[:, hs]
        cmat = c_ref[h]
        n_row = n_ref[h]
        num = jnp.dot(s.astype(BF16), vh.astype(BF16), preferred_element_type=F32) \
            + g * _dot_nt(qb, cmat.astype(BF16))
        den = jnp.sum(s, axis=1, keepdims=True) + g * jnp.sum(q * n_row, axis=1, keepdims=True)
        hh = num / jnp.maximum(jnp.abs(den), jnp.exp(-m_t))
        b_last = b_col[L - 1:L, :]
        wlog = b_last - b_col + ig_col
        m_new = jnp.maximum(b_last + m_prev, jnp.max(wlog, axis=0, keepdims=True))
        w_end = jnp.exp(wlog - m_new)
        g_end = jnp.exp(b_last + m_prev - m_new)
        c_ref[h] = g_end * cmat + _dot_tn((vh * w_end).astype(BF16), kb)
        n_ref[h] = g_end * n_row + jnp.sum(w_end * k, axis=0, keepdims=True)
        m_ref[h:h + 1, :] = jnp.broadcast_to(m_new, (1, LANES))
        y_ref[:, hs] = _head_norm_gate(hh, o[:, hs], ng_ref[:, hs])

    @pl.when(t == pl.num_programs(1) - 1)
    def _():
        c_out_ref[0] = c_ref[...]
        n_out_ref[0] = n_ref[...]
        m_out_ref[0] = m_ref[...]


def _mlstm_seq(z_small, z_seq, conv_w, conv_b, wq_bf, wk_bf, norm_g, n_batch, n_tiles, n_pad):
    per_seq = n_tiles - 1
    full = lambda a: pl.BlockSpec(a.shape, lambda b, t: (0,) * a.ndim)
    nb = D_ML // LANES

    def small(col, width):
        return pl.BlockSpec((TILE_T, width), lambda b, t: (0, col))

    def seq(col, width):
        return pl.BlockSpec((TILE_T, width), lambda b, t: (b * per_seq + jnp.maximum(t - 1, 0), col))

    return pl.pallas_call(
        functools.partial(_mlstm_seq_body, n_pad),
        grid=(n_batch, n_tiles),
        in_specs=[small(1, D_ML), small(2, D_ML), small(3, D_ML), small(GATE_COL, LANES),
                  seq(1, D_ML), seq(2, D_ML), seq(3, D_ML), seq(GATE_COL, LANES),
                  full(conv_w), full(conv_b), full(wq_bf), full(wk_bf), full(norm_g)],
        out_specs=[
            pl.BlockSpec((TILE_T, D_ML), lambda b, t: (b * n_tiles + t, 0)),
            pl.BlockSpec((1, N_HEADS, DH, DH), lambda b, t: (b, 0, 0, 0)),
            pl.BlockSpec((1, N_HEADS, 1, DH), lambda b, t: (b, 0, 0, 0)),
            pl.BlockSpec((1, SUBLANES, LANES), lambda b, t: (b, 0, 0)),
        ],
        out_shape=[
            jax.ShapeDtypeStruct((n_batch * n_tiles * TILE_T, D_ML), F32),
            jax.ShapeDtypeStruct((n_batch, N_HEADS, DH, DH), F32),
            jax.ShapeDtypeStruct((n_batch, N_HEADS, 1, DH), F32),
            jax.ShapeDtypeStruct((n_batch, SUBLANES, LANES), F32),
        ],
        scratch_shapes=[pltpu.VMEM((N_HEADS, DH, DH), F32), pltpu.VMEM((N_HEADS, 1, DH), F32),
                        pltpu.VMEM((SUBLANES, LANES), F32), pltpu.VMEM((TILE_T, D_ML), F32)],
        compiler_params=_cparams(("arbitrary", "arbitrary"), 48),
        name="mlstm_seq",
    )(z_small, z_small, z_small, z_small, z_seq, z_seq, z_seq, z_seq, conv_w, conv_b, wq_bf, wk_bf, norm_g)


def _mlstm_step_a_body(xm_ref, g_ref, conv0_ref, m0_ref, cw_ref, cb_ref, wq_ref, wk_ref,
                       q_ref, k_ref, sc_ref):
    xc = cb_ref[...]
    for j in range(CONV_W - 1):
        xc = xc + conv0_ref[j] * cw_ref[j:j + 1, :]
    xc = xc + xm_ref[...] * cw_ref[CONV_W - 1:CONV_W, :]
    xc = xc * jax.nn.sigmoid(xc)
    gt = g_ref[...]
    ig = gt[:, 0:N_HEADS]
    lf = _log_sigmoid(gt[:, N_HEADS:2 * N_HEADS])
    inter = lf + m0_ref[...]
    m_t = jnp.maximum(ig, inter)
    w = jnp.exp(ig - m_t)
    g = jnp.exp(inter - m_t)
    qks = []
    for h in range(N_HEADS):
        hs = slice(h * DH, (h + 1) * DH)
        xh = xc[:, hs].astype(BF16)
        q = jnp.dot(xh, wq_ref[h], preferred_element_type=F32)
        k = jnp.dot(xh, wk_ref[h], preferred_element_type=F32) * (DH ** -0.5)
        q_ref[:, hs] = q
        k_ref[:, hs] = k
        qks.append(jnp.sum(q * k, axis=1, keepdims=True))
    s = jnp.concatenate(qks, axis=1) * w
    rows = s.shape[0]
    sc_ref[...] = jnp.concatenate(
        [s, w, g, m_t, jnp.exp(-m_t), jnp.zeros((rows, LANES - 5 * N_HEADS), F32)], axis=1)


def _mlstm_step_b_body(bb, sc_ref, q_ref, k_ref, n_ref, v_ref, o_ref, ng_ref, c_ref,
                       y_ref, c_out_ref, n_out_ref):
    i0 = pl.program_id(0) * bb
    row = lax.broadcasted_iota(jnp.int32, (MXU_ROWS, DH), 0)

    def hi_lo(x):
        hi = x.astype(BF16).astype(F32)
        return jnp.broadcast_to(hi, (MXU_ROWS, DH)), jnp.broadcast_to(x - hi, (MXU_ROWS, DH))

    for i in range(bb):
        for h in range(N_HEADS):
            hs = slice(h * DH, (h + 1) * DH)
            base = (i0 + i) * N_STEP_SCALARS
            s = sc_ref[base + h]
            w = sc_ref[base + N_HEADS + h]
            g = sc_ref[base + 2 * N_HEADS + h]
            em = sc_ref[base + 4 * N_HEADS + h]
            rsel = pl.ds(i0 + i, 1)
            q_row = q_ref[rsel, hs]
            k_row = k_ref[rsel, hs]
            n_row = n_ref[rsel, hs]
            v_row = v_ref[rsel, hs]
            cmat = c_ref[i, h]
            qh, ql = hi_lo(q_row)
            qmat = jnp.where(row == 0, qh, jnp.where(row == 1, ql, 0.0)).astype(BF16)
            cq = _dot_nt(qmat, cmat.astype(BF16))
            num = s * v_row + g * (cq[0:1, :] + cq[1:2, :])
            den = s + g * jnp.sum(n_row * q_row, axis=1, keepdims=True)
            hh = num / jnp.maximum(jnp.abs(den), em)
            vh, vl = hi_lo(w * v_row)
            kh, kl = hi_lo(k_row)
            a = jnp.where(row < 2, vh, jnp.where(row < 4, vl, 0.0)).astype(BF16)
            b = jnp.where(row < 4, jnp.where(row % 2 == 0, kh, kl), 0.0).astype(BF16)
            c_out_ref[i, h] = g * cmat + _dot_tn(a, b)
            n_out_ref[rsel, hs] = g * n_row + w * k_row
            y_ref[rsel, hs] = _head_norm_gate(hh, o_ref[rsel, hs], ng_ref[:, hs])


def _mlstm_step(z_small, row_blk, conv0, c0, n0, m0, conv_w, conv_b, wq_bf, wk_bf, norm_g, bb=4):
    rows = c0.shape[0]
    full = lambda a: pl.BlockSpec(a.shape, lambda i: (0,) * a.ndim)
    conv0_t = conv0.transpose(1, 0, 2)
    q, k, sc = pl.pallas_call(
        _mlstm_step_a_body,
        grid=(1,),
        in_specs=[pl.BlockSpec((rows, D_ML), lambda i: (row_blk, 1)),
                  pl.BlockSpec((rows, LANES), lambda i: (row_blk, GATE_COL)),
                  full(conv0_t), full(m0), full(conv_w), full(conv_b), full(wq_bf), full(wk_bf)],
        out_specs=[pl.BlockSpec((rows, D_ML), lambda i: (0, 0)), pl.BlockSpec((rows, D_ML), lambda i: (0, 0)),
                   pl.BlockSpec((rows, LANES), lambda i: (0, 0))],
        out_shape=[jax.ShapeDtypeStruct((rows, D_ML), F32), jax.ShapeDtypeStruct((rows, D_ML), F32),
                   jax.ShapeDtypeStruct((rows, LANES), F32)],
        compiler_params=_cparams(("arbitrary",), 48),
        name="mlstm_step_a",
    )(z_small, z_small, conv0_t, m0, conv_w, conv_b, wq_bf, wk_bf)
    row_spec = pl.BlockSpec((rows, D_ML), lambda i: (0, 0))
    c_spec = pl.BlockSpec((bb, N_HEADS, DH, DH), lambda i: (i, 0, 0, 0))
    y, c_new, n_new = pl.pallas_call(
        functools.partial(_mlstm_step_b_body, bb),
        grid=(rows // bb,),
        in_specs=[pl.BlockSpec(memory_space=pltpu.SMEM), row_spec, row_spec, row_spec,
                  pl.BlockSpec((rows, D_ML), lambda i: (row_blk, 2)),
                  pl.BlockSpec((rows, D_ML), lambda i: (row_blk, 3)),
                  pl.BlockSpec((1, D_ML), lambda i: (0, 0)), c_spec],
        out_specs=[row_spec, c_spec, row_spec],
        out_shape=[jax.ShapeDtypeStruct((rows, D_ML), F32), jax.ShapeDtypeStruct(c0.shape, F32),
                   jax.ShapeDtypeStruct((rows, D_ML), F32)],
        compiler_params=_cparams(("arbitrary",), 56),
        name="mlstm_step_b",
    )(sc[:, :N_STEP_SCALARS].reshape(rows * N_STEP_SCALARS), q, k, n0, z_small, z_small, norm_g, c0)
    return y, c_new, n_new, sc[:, 3 * N_HEADS:4 * N_HEADS]


ROW_CH = D_MODEL // LANES


def _load_row_tiles(ref, row0, m, c):
    return ref[pl.ds(row0 * ROW_CH + c, m, stride=ROW_CH), :]


def _store_row_tiles(ref, row0, val):
    m = val.shape[0]
    for c in range(ROW_CH):
        ref[pl.ds(row0 * ROW_CH + c, m, stride=ROW_CH), :] = val[:, c * LANES:(c + 1) * LANES]


def _mix_out_body(xp_ref, xs_ref, ysp0_ref, ysp1_ref, yss_ref, ymp0_ref, ymp1_ref, yms_ref,
                  gin_ref, bin_ref, wout_ref, g1_ref, b1_ref, rwh_ref, rwl_ref, rb_ref,
                  h1_ref, e_ref, gate_ref):
    is_step = pl.program_id(0) == pl.num_programs(0) - 1
    consts = (gin_ref, bin_ref, wout_ref, g1_ref, b1_ref, rwh_ref, rwl_ref, rb_ref)
    outs = (h1_ref, e_ref, gate_ref)
    two = lambda a, b: jnp.concatenate([a[...], b[...]], axis=0)

    @pl.when(jnp.logical_not(is_step))
    def _():
        _mix_out_tile(xp_ref[...], two(ysp0_ref, ysp1_ref), two(ymp0_ref, ymp1_ref), consts, outs)

    @pl.when(is_step)
    def _():
        _mix_out_tile(two(xs_ref, xs_ref), two(yss_ref, yss_ref), two(yms_ref, yms_ref), consts, outs)


def _mix_out_tile(x, ys, ym, consts, outs):
    gin_ref, bin_ref, wout_ref, g1_ref, b1_ref, rwh_ref, rwl_ref, rb_ref = consts
    h1_ref, e_ref, gate_ref = outs
    hp = _layer_norm(x, gin_ref[...], bin_ref[...])
    ycat = jnp.concatenate([ys, ym], axis=1).astype(BF16)
    mix = jnp.dot(ycat, wout_ref[...], preferred_element_type=F32)
    h1 = _layer_norm(DEEPNORM_ALPHA * hp + mix, g1_ref[...], b1_ref[...])
    _store_row_tiles(h1_ref, 0, h1)
    xh = h1.astype(BF16)
    xl = (h1 - xh.astype(F32)).astype(BF16)
    logits = (jnp.dot(xh, rwh_ref[...], preferred_element_type=F32)
              + jnp.dot(xh, rwl_ref[...], preferred_element_type=F32)
              + jnp.dot(xl, rwh_ref[...], preferred_element_type=F32)) + rb_ref[...]
    rows = logits.shape[0]
    lane = lax.broadcasted_iota(jnp.int32, (rows, LANES), 1)
    logits = jnp.where(lane < N_EXPERTS, logits, NEG_INF)
    e_acc = jnp.zeros((rows, LANES), jnp.int32)
    v_acc = jnp.full((rows, LANES), NEG_INF, F32)
    for k in range(TOP_K):
        mx = jnp.max(logits, axis=1, keepdims=True)
        idx = jnp.min(jnp.where(logits == mx, lane, LANES), axis=1, keepdims=True)
        e_acc = jnp.where(lane == k, idx, e_acc)
        v_acc = jnp.where(lane == k, mx, v_acc)
        logits = jnp.where(lane == idx, NEG_INF, logits)
    p = jnp.exp(v_acc - jnp.max(v_acc, axis=1, keepdims=True))
    e_ref[...] = e_acc
    gate_ref[...] = p / jnp.sum(p, axis=1, keepdims=True)


def _mix_out(x_p, x_s, ys_p, ys_s, ym_p, ym_s, consts, tiles_per_seq):
    d = D_MODEL
    full = lambda a: pl.BlockSpec(a.shape, lambda i: (0,) * a.ndim)
    tm = 2 * TILE_T
    n_p = x_p.shape[0] // tm
    n_tiles = n_p + 1
    rows = n_tiles * tm

    def p_idx(i):
        return jnp.minimum(i, n_p - 1)

    def frame_idx(i, half):
        j = 2 * p_idx(i) + half
        return (j // tiles_per_seq) * (tiles_per_seq + 1) + j % tiles_per_seq + 1

    def s5_idx(i, half):
        j = 2 * p_idx(i) + half
        return _s5_frame_block(j // tiles_per_seq, j % tiles_per_seq + 1, tiles_per_seq + 1)

    frame = lambda width, half: pl.BlockSpec((TILE_T, width), lambda i: (frame_idx(i, half), 0))
    frame_s5 = lambda half: pl.BlockSpec((TILE_T, D_S5), lambda i: (s5_idx(i, half), 0))
    first = lambda width: pl.BlockSpec((TILE_T, width), lambda i: (0, 0))
    return pl.pallas_call(
        _mix_out_body,
        grid=(n_tiles,),
        in_specs=[pl.BlockSpec((tm, d), lambda i: (p_idx(i), 0)), first(d),
                  frame_s5(0), frame_s5(1), first(D_S5),
                  frame(D_ML, 0), frame(D_ML, 1), first(D_ML)] + [full(a) for a in consts],
        out_specs=[pl.BlockSpec((tm * ROW_CH, LANES), lambda i: (i, 0)),
                   pl.BlockSpec((tm, LANES), lambda i: (i, 0)),
                   pl.BlockSpec((tm, LANES), lambda i: (i, 0))],
        out_shape=[jax.ShapeDtypeStruct((rows * ROW_CH, LANES), F32), jax.ShapeDtypeStruct((rows, LANES), jnp.int32),
                   jax.ShapeDtypeStruct((rows, LANES), F32)],
        compiler_params=_cparams(("arbitrary",), 56),
        name="mix_out",
    )(x_p, x_s, ys_p, ys_p, ys_s, ym_p, ym_p, ym_s, *consts)


MOE_BLK = 128
MOE_TM = 1536
MOE_TF = 256
MOE_SUB_MAX = 5
DMA_UNROLL = 8
MOE_STAGES = 4
W_DMA_SPLIT = 4
SCALAR_UNROLL = 32


def _invert_rows_body(dest_ref, src_ref):
    def zero(i, c):
        for q in range(SCALAR_UNROLL):
            src_ref[i * SCALAR_UNROLL + q] = 0
        return c
    lax.fori_loop(0, src_ref.shape[0] // SCALAR_UNROLL, zero, 0)

    def put(i, c):
        for q in range(SCALAR_UNROLL):
            src_ref[dest_ref[i * SCALAR_UNROLL + q]] = i * (SCALAR_UNROLL // TOP_K) + q // TOP_K
        return c
    lax.fori_loop(0, dest_ref.shape[0] // SCALAR_UNROLL, put, 0)


def _invert_rows(dest_flat, rows_pad):
    return pl.pallas_call(
        _invert_rows_body,
        in_specs=[pl.BlockSpec(memory_space=pltpu.SMEM)],
        out_specs=pl.BlockSpec(memory_space=pltpu.SMEM),
        out_shape=jax.ShapeDtypeStruct((rows_pad,), jnp.int32),
        name="invert_rows",
    )(dest_flat)


def _moe_routing(top_e, n_tok):
    n_pairs = n_tok * TOP_K
    rows_pad = n_pairs + N_EXPERTS * MOE_BLK
    n_pass = N_EXPERTS + rows_pad // MOE_TM
    onehot = (top_e[:, :, None] == jnp.arange(N_EXPERTS, dtype=jnp.int32)).astype(jnp.int32).sum(1)
    incl = jnp.cumsum(onehot, axis=0)
    counts = incl[-1]
    rank = jnp.take_along_axis(incl - onehot, top_e, axis=1)
    padded = (counts + MOE_BLK - 1) // MOE_BLK * MOE_BLK
    pad_end = jnp.cumsum(padded)
    pad_start = pad_end - padded
    dest = pad_start[top_e] + rank
    src = _invert_rows(dest.reshape(-1).astype(jnp.int32), rows_pad)
    passes_e = (padded + MOE_TM - 1) // MOE_TM
    pass_end = jnp.cumsum(passes_e)
    u = jnp.arange(n_pass, dtype=jnp.int32)
    e_u = jnp.minimum(jnp.searchsorted(pass_end, u, side="right"), N_EXPERTS - 1).astype(jnp.int32)
    j_u = u - (pass_end - passes_e)[e_u]
    rem = padded[e_u] - j_u * MOE_TM
    nblk = jnp.where(u < pass_end[-1], jnp.clip(rem, 0, MOE_TM) // MOE_BLK, 0).astype(jnp.int32)
    blk0 = ((pad_start[e_u] + j_u * MOE_TM) // MOE_BLK).astype(jnp.int32)
    blk0 = jnp.where(nblk > 0, blk0, 0)
    last_e = e_u[jnp.maximum(pass_end[-1] - 1, 0)]
    e_u = jnp.where(nblk > 0, e_u, last_e)
    nblk = jnp.concatenate([nblk, (pad_end[-1:] // MOE_BLK).astype(jnp.int32)])
    return dest.astype(jnp.int32), src, e_u, blk0, nblk


def _moe_ffn_body(e_ref, blk0_ref, nblk_ref, src_ref, h1_ref, wgu_ref, wdn_ref, bgu_ref, bd_ref,
                  ybuf_ref, xg_ref, xb_ref, acc_ref, w32_a, w32_b, wb_a, wb_b, stage_ref, gsem, osem, wsem):
    u = pl.program_id(0)
    n_pass = pl.num_programs(0)
    n_f = D_FF // MOE_TF
    nblk = nblk_ref[u]
    row0 = blk0_ref[u] * MOE_BLK
    blk_rt = MOE_BLK * ROW_CH
    nxt = jnp.minimum(u + 1, n_pass - 1)
    nxt_active = jnp.logical_and(u + 1 < n_pass, nblk_ref[nxt] > 0)
    w32 = (w32_a, w32_b)
    wb = (wb_a, wb_b)

    def w_copies(p, f, par):
        e = e_ref[p]
        cols = pl.ds(pl.multiple_of(f * MOE_TF, MOE_TF), MOE_TF)
        cols_up = pl.ds(pl.multiple_of(D_FF + f * MOE_TF, MOE_TF), MOE_TF)
        g32, u32, d32 = w32[par]
        out = []
        for i in range(W_DMA_SPLIT):
            r1 = pl.ds(i * (D_MODEL // W_DMA_SPLIT), D_MODEL // W_DMA_SPLIT)
            r2 = pl.ds(i * (MOE_TF // W_DMA_SPLIT), MOE_TF // W_DMA_SPLIT)
            rows_dn = pl.ds(pl.multiple_of(f * MOE_TF + i * (MOE_TF // W_DMA_SPLIT), MOE_TF // W_DMA_SPLIT),
                            MOE_TF // W_DMA_SPLIT)
            out.append(pltpu.make_async_copy(wgu_ref.at[e, r1, cols], g32.at[r1], wsem.at[par]))
            out.append(pltpu.make_async_copy(wgu_ref.at[e, r1, cols_up], u32.at[r1], wsem.at[par]))
            out.append(pltpu.make_async_copy(wdn_ref.at[e, rows_dn, :], d32.at[r2], wsem.at[par]))
        return out

    def cast_w_half(par, half):
        g32, u32, d32 = w32[par]
        w1, w2 = wb[par]
        r1 = pl.ds(pl.multiple_of(half * (D_MODEL // 2), D_MODEL // 2), D_MODEL // 2)
        r2 = pl.ds(pl.multiple_of(half * (MOE_TF // 2), MOE_TF // 2), MOE_TF // 2)
        w1[r1, :MOE_TF] = g32[r1, :].astype(BF16)
        w1[r1, MOE_TF:] = u32[r1, :].astype(BF16)
        w2[r2, :] = d32[r2, :].astype(BF16)

    def cast_w(par):
        cast_w_half(par, 0)
        cast_w_half(par, 1)

    def for_all_rows(p, fn):
        base = blk0_ref[p] * MOE_BLK

        def body(rb, c):
            for s in range(SUBLANES):
                tok = src_ref[base + rb * SUBLANES + s]
                fn(pltpu.make_async_copy(h1_ref.at[pl.ds(tok * ROW_CH, ROW_CH)],
                                         xg_ref.at[rb, pl.ds(0, ROW_CH), s], gsem))
            return c
        lax.fori_loop(0, nblk_ref[p] * (MOE_BLK // SUBLANES), body, 0)

    def issue_gather(p):
        for_all_rows(p, lambda cp: cp.start())

    def out_copy(b):
        dst = pl.ds(pl.multiple_of((row0 + b * MOE_BLK) * ROW_CH, blk_rt), blk_rt)
        return pltpu.make_async_copy(stage_ref.at[b % MOE_STAGES], ybuf_ref.at[dst], osem.at[b % MOE_STAGES])

    @pl.when(nblk > 0)
    def _():
        @pl.when(u == 0)
        def _():
            issue_gather(u)
            for cp in w_copies(u, 0, 0):
                cp.start()
            for cp in w_copies(u, 0, 0):
                cp.wait()
            cast_w(0)
            for cp in w_copies(u, 1, 1):
                cp.start()

        for_all_rows(u, lambda cp: cp.wait())

        def cast(b, c):
            rs = pl.ds(pl.multiple_of(b * MOE_BLK, MOE_BLK), MOE_BLK)
            gs = pl.ds(pl.multiple_of(b * (MOE_BLK // SUBLANES), MOE_BLK // SUBLANES), MOE_BLK // SUBLANES)
            for ch in range(ROW_CH):
                xb_ref[rs, ch * LANES:(ch + 1) * LANES] = xg_ref[gs, ch].reshape(MOE_BLK, LANES).astype(BF16)
            acc_ref[rs, :] = jnp.zeros((MOE_BLK, acc_ref.shape[1]), F32)
            return c
        lax.fori_loop(0, nblk, cast, 0)

        def ffn_rows(r0, m, f, par):
            w1, w2 = wb[par]
            bg = bgu_ref[0, :, pl.ds(pl.multiple_of(f * MOE_TF, MOE_TF), MOE_TF)]
            bu = bgu_ref[0, :, pl.ds(pl.multiple_of(D_FF + f * MOE_TF, MOE_TF), MOE_TF)]
            rs = pl.ds(pl.multiple_of(r0, MOE_BLK), m)
            h = jnp.dot(xb_ref[rs, :], w1[...], preferred_element_type=F32)
            x_glu = jnp.minimum(h[:, :MOE_TF] + bg, SWIGLU_LIMIT)
            x_lin = jnp.clip(h[:, MOE_TF:] + bu, -SWIGLU_LIMIT, SWIGLU_LIMIT)
            act = x_glu * jax.nn.sigmoid(SWIGLU_ALPHA * x_glu) * (x_lin + 1.0)
            acc_ref[rs, :] += jnp.dot(act.astype(BF16), w2[...], preferred_element_type=F32)

        def tile_step(f, par):
            in_pass1 = f + 1 < n_f
            in_pass2 = f + 2 < n_f

            @pl.when(jnp.logical_or(in_pass1, nxt_active))
            def _():
                for cp in w_copies(jnp.where(in_pass1, u, nxt), jnp.where(in_pass1, f + 1, 0), 1 - par):
                    cp.wait()

            @pl.when(jnp.logical_or(in_pass2, nxt_active))
            def _():
                for cp in w_copies(jnp.where(in_pass2, u, nxt), jnp.where(in_pass2, f + 2, f + 2 - n_f), par):
                    cp.start()

            @pl.when(jnp.logical_and(f == 1, nxt_active))
            def _():
                issue_gather(nxt)

            n_sub = (nblk + (MOE_SUB_MAX - 1)) // MOE_SUB_MAX
            q = nblk // n_sub
            n_hi = nblk - q * n_sub

            def sub(i, blk):
                sz = q + (i < n_hi).astype(jnp.int32)
                for m in range(1, MOE_SUB_MAX + 1):
                    @pl.when(sz == m)
                    def _(m=m):
                        ffn_rows(blk * MOE_BLK, m * MOE_BLK, f, par)
                        cast_w_half(1 - par, jnp.minimum(i, 1))
                return blk + sz
            lax.fori_loop(0, n_sub, sub, 0)

            @pl.when(n_sub == 1)
            def _():
                cast_w_half(1 - par, 1)

        def tile_pair(f2, c):
            tile_step(2 * f2, 0)
            tile_step(2 * f2 + 1, 1)
            return c
        lax.fori_loop(0, n_f // 2, tile_pair, 0)

        def emit(b, c):
            @pl.when(b >= MOE_STAGES)
            def _():
                out_copy(b - MOE_STAGES).wait()
            rs = pl.ds(pl.multiple_of(b * MOE_BLK, MOE_BLK), MOE_BLK)
            val = acc_ref[rs, :] + bd_ref[0]
            for ch in range(ROW_CH):
                stage_ref[b % MOE_STAGES, pl.ds(ch, MOE_BLK, stride=ROW_CH), :] = val[:, ch * LANES:(ch + 1) * LANES]
            out_copy(b).start()
            return c
        lax.fori_loop(0, nblk, emit, 0)

        def drain(b, c):
            out_copy(b).wait()
            return c
        lax.fori_loop(jnp.maximum(nblk - MOE_STAGES, 0), nblk, drain, 0)

    @pl.when(u == n_pass - 1)
    def _():
        used = nblk_ref[n_pass]
        total = ybuf_ref.shape[0] // blk_rt
        stage_ref[0] = jnp.zeros((blk_rt, LANES), F32)

        def zero_copy(b):
            dst = pl.ds(pl.multiple_of(b * blk_rt, blk_rt), blk_rt)
            return pltpu.make_async_copy(stage_ref.at[0], ybuf_ref.at[dst], osem.at[0])

        def start(b, c):
            zero_copy(b).start()
            return c
        lax.fori_loop(used, total, start, 0)

        def wait(b, c):
            zero_copy(b).wait()
            return c
        lax.fori_loop(used, total, wait, 0)


def _moe_ffn(h1, src, e_u, blk0, nblk, w_gu, b_gu, w_dn, b_dn, rows_pad):
    d = D_MODEL
    n_pass = e_u.shape[0]
    w32_set = (pltpu.VMEM((d, MOE_TF), F32), pltpu.VMEM((d, MOE_TF), F32), pltpu.VMEM((MOE_TF, d), F32))
    wb_set = (pltpu.VMEM((d, 2 * MOE_TF), BF16), pltpu.VMEM((MOE_TF, d), BF16))
    grid_spec = pltpu.PrefetchScalarGridSpec(
        num_scalar_prefetch=4,
        grid=(n_pass,),
        in_specs=[
            pl.BlockSpec(memory_space=pl.ANY),
            pl.BlockSpec(memory_space=pl.ANY),
            pl.BlockSpec(memory_space=pl.ANY),
            pl.BlockSpec((1, 1, 2 * D_FF), lambda u, e, b0, nb, s: (e[u], 0, 0)),
            pl.BlockSpec((1, 1, d), lambda u, e, b0, nb, s: (e[u], 0, 0)),
        ],
        out_specs=pl.BlockSpec(memory_space=pl.ANY),
        scratch_shapes=[pltpu.VMEM((MOE_TM // SUBLANES, ROW_CH, SUBLANES, LANES), F32), pltpu.VMEM((MOE_TM, d), BF16),
                        pltpu.VMEM((MOE_TM, d), F32), w32_set, w32_set, wb_set, wb_set,
                        pltpu.VMEM((MOE_STAGES, MOE_BLK * ROW_CH, LANES), F32),
                        pltpu.SemaphoreType.DMA(()), pltpu.SemaphoreType.DMA((MOE_STAGES,)),
                        pltpu.SemaphoreType.DMA((2,))],
    )
    return pl.pallas_call(
        _moe_ffn_body,
        grid_spec=grid_spec,
        out_shape=jax.ShapeDtypeStruct((rows_pad * ROW_CH, LANES), F32),
        compiler_params=_cparams(("arbitrary",), 60),
        name="moe_ffn",
    )(e_u, blk0, nblk, src, h1, w_gu, w_dn, b_gu.reshape(N_EXPERTS, 1, 2 * D_FF), b_dn.reshape(N_EXPERTS, 1, d))


def _moe_combine_body(dest_ref, h1_ref, gate_ref, g2_ref, b2_ref, ybuf_ref, outp_ref, outs_ref,
                      buf_ref, sem, pre_ref):
    i = pl.program_id(0)
    n_pairs = TILE_T * TOP_K
    slot = i % 2

    def copy(tile, sl, rb, s, k):
        src_row = dest_ref[tile * n_pairs + rb * (SUBLANES * TOP_K) + (s * TOP_K + k)]
        return pltpu.make_async_copy(ybuf_ref.at[pl.ds(src_row * ROW_CH, ROW_CH)],
                                     buf_ref.at[sl, k, rb, pl.ds(0, ROW_CH), s], sem.at[sl])

    def for_all_rows(tile, sl, fn):
        def body(rb, c):
            for s in range(SUBLANES):
                for k in range(TOP_K):
                    fn(copy(tile, sl, rb, s, k))
            return c
        lax.fori_loop(0, TILE_T // SUBLANES, body, 0)

    def issue(tile, sl):
        for_all_rows(tile, sl, lambda cp: cp.start())

    @pl.when(i == 0)
    def _():
        issue(0, 0)

    @pl.when(i + 1 < pl.num_programs(0))
    def _():
        issue(i + 1, 1 - slot)

    for_all_rows(i, slot, lambda cp: cp.wait())

    gate = gate_ref[...]
    cs = lambda c: slice(c * LANES, (c + 1) * LANES)
    part = jnp.zeros((TILE_T, LANES), F32)
    gate_b = [jnp.broadcast_to(gate[:, k:k + 1], (TILE_T, LANES)) for k in range(TOP_K)]
    for c in range(ROW_CH):
        fc = None
        for k in range(TOP_K):
            v = buf_ref[slot, k, :, c].reshape(TILE_T, LANES) * gate_b[k]
            fc = v if fc is None else fc + v
        pre = DEEPNORM_ALPHA * _load_row_tiles(h1_ref, 0, TILE_T, c) + fc
        pre_ref[:, cs(c)] = pre
        part = part + pre
    mu = jnp.sum(part, axis=1, keepdims=True) * (1.0 / D_MODEL)
    part = jnp.zeros((TILE_T, LANES), F32)
    for c in range(ROW_CH):
        dlt = pre_ref[:, cs(c)] - mu
        part = part + dlt * dlt
    rstd = lax.rsqrt(jnp.sum(part, axis=1, keepdims=True) * (1.0 / D_MODEL) + LN_EPS)
    is_step = i == pl.num_programs(0) - 1

    def write(out_ref):
        for c in range(ROW_CH):
            out_ref[:, cs(c)] = (pre_ref[:, cs(c)] - mu) * rstd * g2_ref[:, cs(c)] + b2_ref[:, cs(c)]

    @pl.when(jnp.logical_not(is_step))
    def _():
        write(outp_ref)

    @pl.when(is_step)
    def _():
        write(outs_ref)


def _moe_combine(dest, h1, gates, g2, b2, ybuf, n_tok):
    d = D_MODEL
    n_tiles = n_tok // TILE_T
    grid_spec = pltpu.PrefetchScalarGridSpec(
        num_scalar_prefetch=1,
        grid=(n_tiles,),
        in_specs=[pl.BlockSpec((TILE_T * ROW_CH, LANES), lambda i, s: (i, 0)),
                  pl.BlockSpec((TILE_T, LANES), lambda i, s: (i, 0)),
                  pl.BlockSpec((1, d), lambda i, s: (0, 0)),
                  pl.BlockSpec((1, d), lambda i, s: (0, 0)),
                  pl.BlockSpec(memory_space=pl.ANY)],
        out_specs=[pl.BlockSpec((TILE_T, d), lambda i, s: (jnp.minimum(i, n_tiles - 2), 0)),
                   pl.BlockSpec((TILE_T, d), lambda i, s: (0, 0))],
        scratch_shapes=[pltpu.VMEM((2, TOP_K, TILE_T // SUBLANES, ROW_CH, SUBLANES, LANES), F32),
                        pltpu.SemaphoreType.DMA((2,)),
                        pltpu.VMEM((TILE_T, d), F32)],
    )
    return pl.pallas_call(
        _moe_combine_body,
        grid_spec=grid_spec,
        out_shape=[jax.ShapeDtypeStruct((n_tok - TILE_T, d), F32), jax.ShapeDtypeStruct((TILE_T, d), F32)],
        compiler_params=_cparams(("arbitrary",), 48),
        name="moe_combine",
    )(dest.reshape(-1), h1, gates, g2, b2, ybuf)


def kernel(x_prompt, x_sample, state_s5_re, state_s5_im, state_mlstm_c, state_mlstm_n, state_mlstm_m, state_mlstm_conv, meta_tokens, ln_in_g, ln_in_b, w_in, b_in, s5_a_re, s5_a_im, s5_log_dt, s5_b_re, s5_b_im, s5_c_re, s5_c_im, s5_d, s5_w_glu, mlstm_conv_w, mlstm_conv_b, mlstm_wq, mlstm_wk, mlstm_norm_g, w_out, ln1_g, ln1_b, router_w, router_b, w_gate_up, b_gate_up, w_down, b_down, ln2_g, ln2_b):
    bsz, seq, d = x_prompt.shape
    dec_b = x_sample.shape[0]
    n_pad = TILE_T - N_META
    x_small = jnp.concatenate([jnp.zeros((n_pad, d), F32), meta_tokens, x_sample.reshape(dec_b, d)], axis=0)
    w_in_p = jnp.pad(w_in[0], ((0, 0), (0, N_IN_PAD - N_IN))).astype(BF16)
    b_in_p = jnp.pad(b_in[0], (0, N_IN_PAD - N_IN)).reshape(1, N_IN_PAD)
    g_in = ln_in_g.reshape(1, d)
    bb_in = ln_in_b.reshape(1, d)
    z_p = _inproj(x_prompt.reshape(bsz * seq, d), g_in, bb_in, w_in_p, b_in_p, 512, 1408)
    z_s = _inproj(x_small, g_in, bb_in, w_in_p, b_in_p, 256, 1408)
    tabs = _s5_prep(s5_a_re[0], s5_a_im[0], s5_log_dt[0], s5_b_re[0], s5_b_im[0])
    bd_b, bd_c = _s5_block_diag(tabs[6], tabs[7], s5_c_re[0], s5_c_im[0])
    wglu_bf = s5_w_glu[0].astype(BF16)
    d_skip = s5_d[0].reshape(1, D_S5)
    n_tiles = seq // TILE_T + 1
    zero_state = jnp.zeros((bsz, 1, N_STATE), F32)
    y_s5_p, s5r_p, s5i_p = _s5_seq(z_s, z_p, zero_state, zero_state, tabs[:6], bd_b, bd_c, d_skip, wglu_bf,
                                   bsz, n_tiles, n_pad)
    y_s5_s, s5r_s, s5i_s = _s5_step(z_s, 1, state_s5_re[0].reshape(dec_b, N_STATE),
                                    state_s5_im[0].reshape(dec_b, N_STATE), tabs[:6], bd_b, bd_c, d_skip, wglu_bf)

    conv_w = mlstm_conv_w[0]
    conv_b = mlstm_conv_b[0].reshape(1, D_ML)
    wq_bf = mlstm_wq[0].astype(BF16)
    wk_bf = mlstm_wk[0].astype(BF16)
    norm_g = mlstm_norm_g[0].reshape(1, D_ML)
    y_ml_p, c_p, n_p, m_p = _mlstm_seq(z_s, z_p, conv_w, conv_b, wq_bf, wk_bf, norm_g, bsz, n_tiles, n_pad)
    conv0 = state_mlstm_conv[0]
    y_ml_s, c_s, n_s, m_s = _mlstm_step(z_s, 1, conv0, state_mlstm_c[0], state_mlstm_n[0].reshape(dec_b, D_ML),
                                        state_mlstm_m[0], conv_w, conv_b, wq_bf, wk_bf, norm_g)

    rw = jnp.pad(router_w[0], ((0, 0), (0, LANES - N_EXPERTS)))
    rw_hi = rw.astype(BF16)
    rw_lo = (rw - rw_hi.astype(F32)).astype(BF16)
    rb = jnp.pad(router_b[0], (0, LANES - N_EXPERTS)).reshape(1, LANES)
    consts = (g_in, bb_in, w_out[0].astype(BF16), ln1_g[0].reshape(1, d), ln1_b[0].reshape(1, d), rw_hi, rw_lo, rb)
    h1, top_e, gates = _mix_out(x_prompt.reshape(bsz * seq, d), x_sample.reshape(dec_b, d),
                                y_s5_p, y_s5_s, y_ml_p, y_ml_s, consts, seq // TILE_T)
    n_tok = bsz * seq + dec_b
    dest, src, e_u, blk0, nblk = _moe_routing(top_e[:n_tok, :TOP_K], n_tok)
    ybuf = _moe_ffn(h1, src, e_u, blk0, nblk, w_gate_up[0], b_gate_up[0], w_down[0], b_down[0], src.shape[0])
    out_p, out_s = _moe_combine(dest, h1, gates, ln2_g[0].reshape(1, d), ln2_b[0].reshape(1, d), ybuf, n_tok)

    y_prompt = out_p.reshape(bsz, seq, d)
    y_sample = out_s.reshape(dec_b, 1, d)
    xm_p = z_p.reshape(bsz, seq, N_IN_PAD)[:, seq - (CONV_W - 1):, D_S5:D_S5 + D_ML]
    xm_s = z_s[TILE_T:, D_S5:D_S5 + D_ML]
    conv_s = jnp.concatenate([conv0[:, 1:], xm_s[:, None, :]], axis=1)
    return (y_prompt, y_sample,
            s5r_p.reshape(1, bsz, N_GROUPS, S5_STATE), s5i_p.reshape(1, bsz, N_GROUPS, S5_STATE),
            c_p[None], n_p.reshape(1, bsz, N_HEADS, DH), m_p[None, :, :N_HEADS, 0], xm_p[None],
            s5r_s.reshape(1, dec_b, N_GROUPS, S5_STATE), s5i_s.reshape(1, dec_b, N_GROUPS, S5_STATE),
            c_s[None], n_s.reshape(1, dec_b, N_HEADS, DH), m_s[None], conv_s[None])
```

```python
import functools
import math

import jax
import jax.numpy as jnp
from jax import lax
from jax.experimental import pallas as pl
from jax.experimental.pallas import tpu as pltpu

F32 = jnp.float32
BF16 = jnp.bfloat16

D_MODEL = 2048
N_META = 16
D_S5 = 1024
D_ML = 1024
S5_CH = 16
N_GROUPS = 64
S5_STATE = 64
N_STATE = N_GROUPS * S5_STATE
N_HEADS = 4
DH = 256
CONV_W = 4
N_EXPERTS = 32
TOP_K = 4
D_FF = 2048
SWIGLU_LIMIT = 7.0
SWIGLU_ALPHA = 1.702
LN_EPS = 1e-5
DEEPNORM_ALPHA = 2.0 ** 0.25
N_IN = D_S5 + 3 * D_ML + 2 * N_HEADS

LANES = 128
SUBLANES = 8
MXU_DIM = 256
MXU_ROWS = 16

TILE_T = 128
SEG_LEN = TILE_T // SUBLANES
N_IN_PAD = 33 * LANES
GATE_COL = 4 * D_S5 // LANES
S5_KCH = D_S5 // MXU_DIM
S5_SCH = N_STATE // S5_KCH
SCAN_LW = 256


def _cparams(sem, vmem_mb=None):
    kw = dict(dimension_semantics=sem)
    if vmem_mb is not None:
        kw["vmem_limit_bytes"] = vmem_mb * 1024 * 1024
    return pltpu.CompilerParams(**kw)


def _layer_norm(x, g, b):
    mu = jnp.mean(x, axis=-1, keepdims=True)
    xc = x - mu
    var = jnp.mean(xc * xc, axis=-1, keepdims=True)
    return xc * lax.rsqrt(var + LN_EPS) * g + b


def _inproj_body(x_ref, g_ref, b_ref, w_ref, bias_ref, z_ref, hn_ref):
    @pl.when(pl.program_id(1) == 0)
    def _():
        hn_ref[...] = _layer_norm(x_ref[...], g_ref[...], b_ref[...]).astype(BF16)

    z_ref[...] = jnp.dot(hn_ref[...], w_ref[...], preferred_element_type=F32) + bias_ref[...]


def _inproj(x, g, b, w, bias, tm, tn):
    rows, d = x.shape
    n = w.shape[1]
    return pl.pallas_call(
        _inproj_body,
        grid=(rows // tm, n // tn),
        in_specs=[
            pl.BlockSpec((tm, d), lambda i, j: (i, 0)),
            pl.BlockSpec((1, d), lambda i, j: (0, 0)),
            pl.BlockSpec((1, d), lambda i, j: (0, 0)),
            pl.BlockSpec((d, tn), lambda i, j: (0, j)),
            pl.BlockSpec((1, tn), lambda i, j: (0, j)),
        ],
        out_specs=pl.BlockSpec((tm, tn), lambda i, j: (i, j)),
        out_shape=jax.ShapeDtypeStruct((rows, n), F32),
        scratch_shapes=[pltpu.VMEM((tm, d), BF16)],
        compiler_params=_cparams(("arbitrary", "arbitrary"), 48),
        name="inproj",
    )(x, g, b, w, bias)


def _cmul(ar, ai, br, bi):
    return ar * br - ai * bi, ar * bi + ai * br


def _s5_prep_body(are_ref, aim_ref, dt_ref, bre_ref, bim_ref,
                  pre_ref, pim_ref, hre_ref, him_ref, qre_ref, qim_ref, bbre_ref, bbim_ref):
    lr = are_ref[...]
    li = aim_ref[...]
    dt = jnp.exp(dt_ref[...])
    mag = jnp.exp(lr * dt)
    ar = mag * jnp.cos(li * dt)
    ai = mag * jnp.sin(li * dt)
    nr = ar - 1.0
    ni = ai
    den = lr * lr + li * li
    cr = (nr * lr + ni * li) / den
    ci = (ni * lr - nr * li) / den
    bbre_ref[...] = cr * bre_ref[...] - ci * bim_ref[...]
    bbim_ref[...] = cr * bim_ref[...] + ci * bre_ref[...]
    pr, pi = ar, ai
    for j in range(SEG_LEN):
        pre_ref[j:j + 1, :] = pr
        pim_ref[j:j + 1, :] = pi
        if j + 1 < SEG_LEN:
            pr, pi = _cmul(pr, pi, ar, ai)
    row = lax.broadcasted_iota(jnp.int32, (SUBLANES, N_STATE), 0)
    kr, ki = pr, pi
    for idx, k in enumerate((1, 2, 4)):
        hre_ref[idx * 8:(idx + 1) * 8, :] = jnp.where(row >= k, jnp.broadcast_to(kr, (SUBLANES, N_STATE)), 0.0)
        him_ref[idx * 8:(idx + 1) * 8, :] = jnp.where(row >= k, jnp.broadcast_to(ki, (SUBLANES, N_STATE)), 0.0)
        kr, ki = _cmul(kr, ki, kr, ki)
    qr, qi = pr, pi
    for s in range(SUBLANES):
        qre_ref[s:s + 1, :] = qr
        qim_ref[s:s + 1, :] = qi
        if s + 1 < SUBLANES:
            qr, qi = _cmul(qr, qi, pr, pi)


def _s5_prep(a_re, a_im, log_dt, b_re, b_im):
    n = N_STATE
    are = a_re.reshape(1, n)
    aim = a_im.reshape(1, n)
    dtl = jnp.broadcast_to(log_dt[:, None], (N_GROUPS, S5_STATE)).reshape(1, n)
    bre = b_re.transpose(2, 0, 1).reshape(S5_CH, n)
    bim = b_im.transpose(2, 0, 1).reshape(S5_CH, n)
    shp = lambda r: jax.ShapeDtypeStruct((r, n), F32)
    return pl.pallas_call(
        _s5_prep_body,
        out_shape=(shp(SEG_LEN), shp(SEG_LEN), shp(24), shp(24), shp(8), shp(8), shp(S5_CH), shp(S5_CH)),
        name="s5_prep",
    )(are, aim, dtl, bre, bim)


def _s5_block_diag(bb_re, bb_im, c_re, c_im):
    gpc = 16
    eye = jnp.eye(gpc, dtype=bool)

    def bd_in(bb):
        t = bb.reshape(S5_CH, S5_KCH, gpc, S5_STATE).transpose(1, 2, 0, 3)
        t = jnp.where(eye[None, :, None, :, None], t[:, :, :, None, :], 0.0)
        return t.reshape(S5_KCH, gpc * S5_CH, gpc * S5_STATE)

    def bd_out(c):
        t = c.reshape(S5_KCH, gpc, S5_CH, S5_STATE).transpose(0, 1, 3, 2)
        t = jnp.where(eye[None, :, None, :, None], t[:, :, :, None, :], 0.0)
        return t.reshape(S5_KCH, gpc * S5_STATE, gpc * S5_CH)

    bd_b = jnp.concatenate([bd_in(bb_re), bd_in(bb_im)], axis=2).astype(BF16)
    bd_c = jnp.concatenate([bd_out(c_re), bd_out(-c_im)], axis=1).astype(BF16)
    return bd_b, bd_c


def _gelu_glu(y, wglu_ref):
    y = 0.5 * y * (1.0 + lax.erf(y * math.sqrt(0.5)))
    gate = jnp.dot(y.astype(BF16), wglu_ref[...], preferred_element_type=F32)
    return y * jax.nn.sigmoid(gate)


def _s5_in_proj(u_bf, bdb_ref, bur_ref, bui_ref):
    for c in range(S5_KCH):
        r = jnp.dot(u_bf[:, c * MXU_DIM:(c + 1) * MXU_DIM], bdb_ref[c], preferred_element_type=F32)
        bur_ref[:, c * S5_SCH:(c + 1) * S5_SCH] = r[:, :S5_SCH]
        bui_ref[:, c * S5_SCH:(c + 1) * S5_SCH] = r[:, S5_SCH:]


def _s5_out_proj(xr_ref, xi_ref, bdc_ref):
    ys = []
    for c in range(S5_KCH):
        xr = xr_ref[:, c * S5_SCH:(c + 1) * S5_SCH].astype(BF16)
        xi = xi_ref[:, c * S5_SCH:(c + 1) * S5_SCH].astype(BF16)
        ys.append(jnp.dot(xr, bdc_ref[c, :S5_SCH, :], preferred_element_type=F32)
                  + jnp.dot(xi, bdc_ref[c, S5_SCH:, :], preferred_element_type=F32))
    return jnp.concatenate(ys, axis=1)


def _s5_seq_body(n_pad, us_ref, upa_ref, upb_ref, x0r_ref, x0i_ref, bdb_ref, bdc_ref, pre_ref, pim_ref,
                 hre_ref, him_ref, qre_ref, qim_ref, d_ref, wglu_ref,
                 y_ref, xr_out_ref, xi_out_ref, *scratch):
    t = pl.program_id(1)
    n = N_STATE
    n_lc = D_S5 // LANES
    scan_per_chunk = S5_SCH // SCAN_LW
    seqs = [dict(zip(("bur", "bui", "cr", "ci", "perm"), scratch[5 * i:5 * i + 5]), idx=i, up=up)
            for i, up in enumerate((upa_ref, upb_ref))]

    @pl.when(t == 0)
    def _():
        for sq in seqs:
            sq["cr"][...] = jnp.broadcast_to(x0r_ref[sq["idx"]], (SUBLANES, n))
            sq["ci"][...] = jnp.broadcast_to(x0i_ref[sq["idx"]], (SUBLANES, n))

    def load_perm(ref, perm_ref):
        for c in range(n_lc):
            perm_ref[c] = ref[:, c * LANES:(c + 1) * LANES]
        return jnp.concatenate(
            [jnp.concatenate([perm_ref[c, pl.ds(j, SUBLANES, stride=SEG_LEN), :] for c in range(n_lc)], axis=1)
             for j in range(SEG_LEN)], axis=0)

    prow = lax.broadcasted_iota(jnp.int32, (TILE_T, 1), 0)
    time = (prow % SUBLANES) * SEG_LEN + prow // SUBLANES
    row8 = lax.broadcasted_iota(jnp.int32, (SUBLANES, SCAN_LW), 0)

    def load_u(sq):
        u_first = jnp.where(time >= n_pad, load_perm(us_ref, sq["perm"]), 0.0)
        sq["u"] = jnp.where(t == 0, u_first, load_perm(sq["up"], sq["perm"]))
        sq["u_bf"] = sq["u"].astype(BF16)

    def in_proj(sq, c):
        r = jnp.dot(sq["u_bf"][:, c * MXU_DIM:(c + 1) * MXU_DIM], bdb_ref[c], preferred_element_type=F32)
        sq["bur"][:, c * S5_SCH:(c + 1) * S5_SCH] = r[:, :S5_SCH]
        sq["bui"][:, c * S5_SCH:(c + 1) * S5_SCH] = r[:, S5_SCH:]

    def out_proj(sq, c):
        xr = sq["bur"][:, c * S5_SCH:(c + 1) * S5_SCH].astype(BF16)
        xi = sq["bui"][:, c * S5_SCH:(c + 1) * S5_SCH].astype(BF16)
        sq.setdefault("y", []).append(
            jnp.dot(xr, bdc_ref[c, :S5_SCH, :], preferred_element_type=F32)
            + jnp.dot(xi, bdc_ref[c, S5_SCH:, :], preferred_element_type=F32))

    def finish(sq):
        y = jnp.concatenate(sq["y"], axis=1) + d_ref[...] * sq["u"]
        y = _gelu_glu(y, wglu_ref)
        perm_ref = sq["perm"]
        for j in range(SEG_LEN):
            for c in range(n_lc):
                perm_ref[c, pl.ds(j, SUBLANES, stride=SEG_LEN), :] = y[j * SUBLANES:(j + 1) * SUBLANES,
                                                                       c * LANES:(c + 1) * LANES]
        r0 = sq["idx"] * TILE_T
        for c in range(n_lc):
            y_ref[r0:r0 + TILE_T, c * LANES:(c + 1) * LANES] = perm_ref[c]

    def scan_lanes(sq, lc):
        bur_ref, bui_ref, cr_ref, ci_ref = sq["bur"], sq["bui"], sq["cr"], sq["ci"]
        ls = slice(lc * SCAN_LW, (lc + 1) * SCAN_LW)
        bc = lambda ref, j: jnp.broadcast_to(ref[j:j + 1, ls], (SUBLANES, SCAN_LW))
        ar, ai = bc(pre_ref, 0), bc(pim_ref, 0)
        xr = jnp.zeros((SUBLANES, SCAN_LW), F32)
        xi = jnp.zeros((SUBLANES, SCAN_LW), F32)
        for j in range(SEG_LEN):
            rs = slice(j * SUBLANES, (j + 1) * SUBLANES)
            nr = ar * xr - ai * xi + bur_ref[rs, ls]
            ni = ar * xi + ai * xr + bui_ref[rs, ls]
            xr, xi = nr, ni
            bur_ref[rs, ls] = xr
            bui_ref[rs, ls] = xi
        er, ei = xr, xi
        for idx, k in enumerate((1, 2, 4)):
            sr = pltpu.roll(er, k, axis=0)
            si = pltpu.roll(ei, k, axis=0)
            hr = hre_ref[idx * 8:(idx + 1) * 8, ls]
            hi = him_ref[idx * 8:(idx + 1) * 8, ls]
            er, ei = er + (hr * sr - hi * si), ei + (hr * si + hi * sr)
        cpr = cr_ref[:, ls]
        cpi = ci_ref[:, ls]
        qr = qre_ref[:, ls]
        qi = qim_ref[:, ls]
        er, ei = er + (qr * cpr - qi * cpi), ei + (qr * cpi + qi * cpr)
        inr = jnp.where(row8 == 0, cpr, pltpu.roll(er, 1, axis=0))
        ini = jnp.where(row8 == 0, cpi, pltpu.roll(ei, 1, axis=0))
        cr_ref[:, ls] = jnp.broadcast_to(er[SUBLANES - 1:SUBLANES, :], (SUBLANES, SCAN_LW))
        ci_ref[:, ls] = jnp.broadcast_to(ei[SUBLANES - 1:SUBLANES, :], (SUBLANES, SCAN_LW))
        for j in range(SEG_LEN):
            rs = slice(j * SUBLANES, (j + 1) * SUBLANES)
            pr, pi = bc(pre_ref, j), bc(pim_ref, j)
            bur_ref[rs, ls] = bur_ref[rs, ls] + (pr * inr - pi * ini)
            bui_ref[rs, ls] = bui_ref[rs, ls] + (pr * ini + pi * inr)

    def scan_chunk(sq, c):
        for k in range(scan_per_chunk):
            scan_lanes(sq, c * scan_per_chunk + k)

    sa, sb = seqs
    load_u(sa)
    load_u(sb)
    for c in range(S5_KCH):
        in_proj(sa, c)
    for c in range(S5_KCH):
        in_proj(sb, c)
        scan_chunk(sa, c)
    for c in range(S5_KCH):
        out_proj(sa, c)
        scan_chunk(sb, c)
    finish(sa)
    for c in range(S5_KCH):
        out_proj(sb, c)
    finish(sb)

    @pl.when(t == pl.num_programs(1) - 1)
    def _():
        for sq in seqs:
            xr_out_ref[sq["idx"]] = sq["cr"][0:1, :]
            xi_out_ref[sq["idx"]] = sq["ci"][0:1, :]


def _s5_seq(z_small, z_seq, x0r, x0i, tabs, bd_b, bd_c, d_skip, wglu_bf, n_batch, n_tiles, n_pad):
    pre, pim, hre, him, qre, qim = tabs
    n = N_STATE
    full = lambda a: pl.BlockSpec(a.shape, lambda b, t: (0,) * a.ndim)
    per_seq = n_tiles - 1
    seq_tile = lambda s: pl.BlockSpec((TILE_T, D_S5),
                                      lambda p, t: ((2 * p + s) * per_seq + jnp.maximum(t - 1, 0), 0))
    seq_scratch = [pltpu.VMEM((TILE_T, n), F32), pltpu.VMEM((TILE_T, n), F32),
                   pltpu.VMEM((SUBLANES, n), F32), pltpu.VMEM((SUBLANES, n), F32),
                   pltpu.VMEM((D_S5 // LANES, TILE_T, LANES), F32)]
    return pl.pallas_call(
        functools.partial(_s5_seq_body, n_pad),
        grid=(n_batch // 2, n_tiles),
        in_specs=[
            pl.BlockSpec((TILE_T, D_S5), lambda p, t: (0, 0)),
            seq_tile(0), seq_tile(1),
            pl.BlockSpec((2, 1, n), lambda p, t: (p, 0, 0)),
            pl.BlockSpec((2, 1, n), lambda p, t: (p, 0, 0)),
            full(bd_b), full(bd_c), full(pre), full(pim), full(hre), full(him), full(qre), full(qim),
            full(d_skip), full(wglu_bf),
        ],
        out_specs=[
            pl.BlockSpec((2 * TILE_T, D_S5), lambda p, t: (p * n_tiles + t, 0)),
            pl.BlockSpec((2, 1, n), lambda p, t: (p, 0, 0)),
            pl.BlockSpec((2, 1, n), lambda p, t: (p, 0, 0)),
        ],
        out_shape=[
            jax.ShapeDtypeStruct((n_batch * n_tiles * TILE_T, D_S5), F32),
            jax.ShapeDtypeStruct((n_batch, 1, n), F32),
            jax.ShapeDtypeStruct((n_batch, 1, n), F32),
        ],
        scratch_shapes=seq_scratch + seq_scratch,
        compiler_params=_cparams(("arbitrary", "arbitrary"), 48),
        name="s5_seq",
    )(z_small, z_seq, z_seq, x0r, x0i, bd_b, bd_c, pre, pim, hre, him, qre, qim, d_skip, wglu_bf)


def _s5_frame_block(b, tile, n_tiles):
    return ((b // 2) * n_tiles + tile) * 2 + b % 2


def _s5_step_body(u_ref, x0r_ref, x0i_ref, bdb_ref, bdc_ref, pre_ref, pim_ref, d_ref, wglu_ref,
                  y_ref, xr_ref, xi_ref):
    u = u_ref[...]
    _s5_in_proj(u.astype(BF16), bdb_ref, xr_ref, xi_ref)
    ar = pre_ref[0:1, :]
    ai = pim_ref[0:1, :]
    x0r = x0r_ref[...]
    x0i = x0i_ref[...]
    xr_ref[...] = xr_ref[...] + (ar * x0r - ai * x0i)
    xi_ref[...] = xi_ref[...] + (ar * x0i + ai * x0r)
    y = _s5_out_proj(xr_ref, xi_ref, bdc_ref) + d_ref[...] * u
    y_ref[...] = _gelu_glu(y, wglu_ref)


def _s5_step(z_small, row_blk, x0r, x0i, tabs, bd_b, bd_c, d_skip, wglu_bf):
    pre, pim = tabs[0], tabs[1]
    rows = x0r.shape[0]
    n = N_STATE
    full = lambda a: pl.BlockSpec(a.shape, lambda i: (0,) * a.ndim)
    return pl.pallas_call(
        _s5_step_body,
        grid=(1,),
        in_specs=[pl.BlockSpec((rows, D_S5), lambda i: (row_blk, 0)), full(x0r), full(x0i),
                  full(bd_b), full(bd_c), full(pre), full(pim), full(d_skip), full(wglu_bf)],
        out_specs=[pl.BlockSpec((rows, D_S5), lambda i: (0, 0)),
                   pl.BlockSpec((rows, n), lambda i: (0, 0)), pl.BlockSpec((rows, n), lambda i: (0, 0))],
        out_shape=[jax.ShapeDtypeStruct((rows, D_S5), F32), jax.ShapeDtypeStruct((rows, n), F32),
                   jax.ShapeDtypeStruct((rows, n), F32)],
        compiler_params=_cparams(("arbitrary",), 48),
        name="s5_step",
    )(z_small, x0r, x0i, bd_b, bd_c, pre, pim, d_skip, wglu_bf)


NEG_INF = float("-inf")
N_STEP_SCALARS = 5 * N_HEADS


def _log_sigmoid(x):
    return jnp.minimum(x, 0.0) - jnp.log1p(jnp.exp(-jnp.abs(x)))


def _split3(x):
    hi = x.astype(BF16)
    r1 = x - hi.astype(F32)
    mid = r1.astype(BF16)
    lo = (r1 - mid.astype(F32)).astype(BF16)
    return hi, mid, lo


def _head_norm_gate(h, o, g):
    mu = jnp.mean(h, axis=-1, keepdims=True)
    hc = h - mu
    var = jnp.mean(hc * hc, axis=-1, keepdims=True)
    return jax.nn.sigmoid(o) * (hc * lax.rsqrt(var + LN_EPS) * g)


def _dot_nt(a, b):
    return lax.dot_general(a, b, (((1,), (1,)), ((), ())), preferred_element_type=F32)


def _dot_tn(a, b):
    return lax.dot_general(a, b, (((0,), (0,)), ((), ())), preferred_element_type=F32)


def _mlstm_seq_body(n_pad, sxm_ref, sv_ref, so_ref, sg_ref, pxm_ref, pv_ref, po_ref, pg_ref,
                    cw_ref, cb_ref, wq_ref, wk_ref, ng_ref,
                    y_ref, c_out_ref, n_out_ref, m_out_ref, c_ref, n_ref, m_ref, prev_ref):
    t = pl.program_id(1)
    L = TILE_T

    @pl.when(t == 0)
    def _():
        c_ref[...] = jnp.zeros_like(c_ref)
        n_ref[...] = jnp.zeros_like(n_ref)
        m_ref[...] = jnp.zeros_like(m_ref)
        prev_ref[...] = jnp.zeros_like(prev_ref)

    first = t == 0
    row = lax.broadcasted_iota(jnp.int32, (L, 1), 0)
    valid = jnp.logical_or(jnp.logical_not(first), row >= n_pad)
    xm = jnp.where(valid, jnp.where(first, sxm_ref[...], pxm_ref[...]), 0.0)
    v = jnp.where(first, sv_ref[...], pv_ref[...])
    o = jnp.where(first, so_ref[...], po_ref[...])
    gt = jnp.where(first, sg_ref[...], pg_ref[...])

    prev = prev_ref[...]

    def shifted(j):
        if j == 0:
            return xm
        return pltpu.roll(jnp.where(row >= L - j, prev, xm), j, axis=0)

    xc = cb_ref[...]
    for j in range(CONV_W):
        xc = xc + shifted(CONV_W - 1 - j) * cw_ref[j:j + 1, :]
    prev_ref[...] = xm
    xc = xc * jax.nn.sigmoid(xc)

    ig = jnp.where(valid, gt, NEG_INF)
    lf = jnp.where(valid, _log_sigmoid(gt), 0.0)
    ti = lax.broadcasted_iota(jnp.int32, (L, L), 0)
    si = lax.broadcasted_iota(jnp.int32, (L, L), 1)
    causal = si <= ti
    tri = jnp.where(causal, 1.0, 0.0).astype(BF16)
    bc = sum(jnp.dot(tri, p, preferred_element_type=F32) for p in _split3(lf))
    ig_t = ig.T
    bc_t = bc.T

    for h in range(N_HEADS):
        hs = slice(h * DH, (h + 1) * DH)
        b_col = bc[:, N_HEADS + h:N_HEADS + h + 1]
        b_row = bc_t[N_HEADS + h:N_HEADS + h + 1, :]
        ig_row = ig_t[h:h + 1, :]
        ig_col = ig[:, h:h + 1]
        m_prev = m_ref[h:h + 1, 0:1]
        dlog = jnp.where(causal, b_col - b_row + ig_row, NEG_INF)
        inter = b_col + m_prev
        m_t = jnp.maximum(jnp.max(dlog, axis=1, keepdims=True), inter)
        w = jnp.exp(dlog - m_t)
        g = jnp.exp(inter - m_t)
        xh = xc[:, hs].astype(BF16)
        q = jnp.dot(xh, wq_ref[h], preferred_element_type=F32)
        k = jnp.dot(xh, wk_ref[h], preferred_element_type=F32) * (DH ** -0.5)
        qb = q.astype(BF16)
        kb = k.astype(BF16)
        s = _dot_nt(qb, kb) * w
        vh = v[:, hs]
        cmat = c_ref[h]
        n_row = n_ref[h]
        num = jnp.dot(s.astype(BF16), vh.astype(BF16), preferred_element_type=F32) \
            + g * _dot_nt(qb, cmat.astype(BF16))
        den = jnp.sum(s, axis=1, keepdims=True) + g * jnp.sum(q * n_row, axis=1, keepdims=True)
        hh = num / jnp.maximum(jnp.abs(den), jnp.exp(-m_t))
        b_last = b_col[L - 1:L, :]
        wlog = b_last - b_col + ig_col
        m_new = jnp.maximum(b_last + m_prev, jnp.max(wlog, axis=0, keepdims=True))
        w_end = jnp.exp(wlog - m_new)
        g_end = jnp.exp(b_last + m_prev - m_new)
        c_ref[h] = g_end * cmat + _dot_tn((vh * w_end).astype(BF16), kb)
        n_ref[h] = g_end * n_row + jnp.sum(w_end * k, axis=0, keepdims=True)
        m_ref[h:h + 1, :] = jnp.broadcast_to(m_new, (1, LANES))
        y_ref[:, hs] = _head_norm_gate(hh, o[:, hs], ng_ref[:, hs])

    @pl.when(t == pl.num_programs(1) - 1)
    def _():
        c_out_ref[0] = c_ref[...]
        n_out_ref[0] = n_ref[...]
        m_out_ref[0] = m_ref[...]


def _mlstm_seq(z_small, z_seq, conv_w, conv_b, wq_bf, wk_bf, norm_g, n_batch, n_tiles, n_pad):
    per_seq = n_tiles - 1
    full = lambda a: pl.BlockSpec(a.shape, lambda b, t: (0,) * a.ndim)
    nb = D_ML // LANES

    def small(col, width):
        return pl.BlockSpec((TILE_T, width), lambda b, t: (0, col))

    def seq(col, width):
        return pl.BlockSpec((TILE_T, width), lambda b, t: (b * per_seq + jnp.maximum(t - 1, 0), col))

    return pl.pallas_call(
        functools.partial(_mlstm_seq_body, n_pad),
        grid=(n_batch, n_tiles),
        in_specs=[small(1, D_ML), small(2, D_ML), small(3, D_ML), small(GATE_COL, LANES),
                  seq(1, D_ML), seq(2, D_ML), seq(3, D_ML), seq(GATE_COL, LANES),
                  full(conv_w), full(conv_b), full(wq_bf), full(wk_bf), full(norm_g)],
        out_specs=[
            pl.BlockSpec((TILE_T, D_ML), lambda b, t: (b * n_tiles + t, 0)),
            pl.BlockSpec((1, N_HEADS, DH, DH), lambda b, t: (b, 0, 0, 0)),
            pl.BlockSpec((1, N_HEADS, 1, DH), lambda b, t: (b, 0, 0, 0)),
            pl.BlockSpec((1, SUBLANES, LANES), lambda b, t: (b, 0, 0)),
        ],
        out_shape=[
            jax.ShapeDtypeStruct((n_batch * n_tiles * TILE_T, D_ML), F32),
            jax.ShapeDtypeStruct((n_batch, N_HEADS, DH, DH), F32),
            jax.ShapeDtypeStruct((n_batch, N_HEADS, 1, DH), F32),
            jax.ShapeDtypeStruct((n_batch, SUBLANES, LANES), F32),
        ],
        scratch_shapes=[pltpu.VMEM((N_HEADS, DH, DH), F32), pltpu.VMEM((N_HEADS, 1, DH), F32),
                        pltpu.VMEM((SUBLANES, LANES), F32), pltpu.VMEM((TILE_T, D_ML), F32)],
        compiler_params=_cparams(("arbitrary", "arbitrary"), 48),
        name="mlstm_seq",
    )(z_small, z_small, z_small, z_small, z_seq, z_seq, z_seq, z_seq, conv_w, conv_b, wq_bf, wk_bf, norm_g)


def _mlstm_step_a_body(xm_ref, g_ref, conv0_ref, m0_ref, cw_ref, cb_ref, wq_ref, wk_ref,
                       q_ref, k_ref, sc_ref):
    xc = cb_ref[...]
    for j in range(CONV_W - 1):
        xc = xc + conv0_ref[j] * cw_ref[j:j + 1, :]
    xc = xc + xm_ref[...] * cw_ref[CONV_W - 1:CONV_W, :]
    xc = xc * jax.nn.sigmoid(xc)
    gt = g_ref[...]
    ig = gt[:, 0:N_HEADS]
    lf = _log_sigmoid(gt[:, N_HEADS:2 * N_HEADS])
    inter = lf + m0_ref[...]
    m_t = jnp.maximum(ig, inter)
    w = jnp.exp(ig - m_t)
    g = jnp.exp(inter - m_t)
    qks = []
    for h in range(N_HEADS):
        hs = slice(h * DH, (h + 1) * DH)
        xh = xc[:, hs].astype(BF16)
        q = jnp.dot(xh, wq_ref[h], preferred_element_type=F32)
        k = jnp.dot(xh, wk_ref[h], preferred_element_type=F32) * (DH ** -0.5)
        q_ref[:, hs] = q
        k_ref[:, hs] = k
        qks.append(jnp.sum(q * k, axis=1, keepdims=True))
    s = jnp.concatenate(qks, axis=1) * w
    rows = s.shape[0]
    sc_ref[...] = jnp.concatenate(
        [s, w, g, m_t, jnp.exp(-m_t), jnp.zeros((rows, LANES - 5 * N_HEADS), F32)], axis=1)


def _mlstm_step_b_body(bb, sc_ref, q_ref, k_ref, n_ref, v_ref, o_ref, ng_ref, c_ref,
                       y_ref, c_out_ref, n_out_ref):
    i0 = pl.program_id(0) * bb
    row = lax.broadcasted_iota(jnp.int32, (MXU_ROWS, DH), 0)

    def hi_lo(x):
        hi = x.astype(BF16).astype(F32)
        return jnp.broadcast_to(hi, (MXU_ROWS, DH)), jnp.broadcast_to(x - hi, (MXU_ROWS, DH))

    for i in range(bb):
        for h in range(N_HEADS):
            hs = slice(h * DH, (h + 1) * DH)
            base = (i0 + i) * N_STEP_SCALARS
            s = sc_ref[base + h]
            w = sc_ref[base + N_HEADS + h]
            g = sc_ref[base + 2 * N_HEADS + h]
            em = sc_ref[base + 4 * N_HEADS + h]
            rsel = pl.ds(i0 + i, 1)
            q_row = q_ref[rsel, hs]
            k_row = k_ref[rsel, hs]
            n_row = n_ref[rsel, hs]
            v_row = v_ref[rsel, hs]
            cmat = c_ref[i, h]
            qh, ql = hi_lo(q_row)
            qmat = jnp.where(row == 0, qh, jnp.where(row == 1, ql, 0.0)).astype(BF16)
            cq = _dot_nt(qmat, cmat.astype(BF16))
            num = s * v_row + g * (cq[0:1, :] + cq[1:2, :])
            den = s + g * jnp.sum(n_row * q_row, axis=1, keepdims=True)
            hh = num / jnp.maximum(jnp.abs(den), em)
            vh, vl = hi_lo(w * v_row)
            kh, kl = hi_lo(k_row)
            a = jnp.where(row < 2, vh, jnp.where(row < 4, vl, 0.0)).astype(BF16)
            b = jnp.where(row < 4, jnp.where(row % 2 == 0, kh, kl), 0.0).astype(BF16)
            c_out_ref[i, h] = g * cmat + _dot_tn(a, b)
            n_out_ref[rsel, hs] = g * n_row + w * k_row
            y_ref[rsel, hs] = _head_norm_gate(hh, o_ref[rsel, hs], ng_ref[:, hs])


def _mlstm_step(z_small, row_blk, conv0, c0, n0, m0, conv_w, conv_b, wq_bf, wk_bf, norm_g, bb=4):
    rows = c0.shape[0]
    full = lambda a: pl.BlockSpec(a.shape, lambda i: (0,) * a.ndim)
    conv0_t = conv0.transpose(1, 0, 2)
    q, k, sc = pl.pallas_call(
        _mlstm_step_a_body,
        grid=(1,),
        in_specs=[pl.BlockSpec((rows, D_ML), lambda i: (row_blk, 1)),
                  pl.BlockSpec((rows, LANES), lambda i: (row_blk, GATE_COL)),
                  full(conv0_t), full(m0), full(conv_w), full(conv_b), full(wq_bf), full(wk_bf)],
        out_specs=[pl.BlockSpec((rows, D_ML), lambda i: (0, 0)), pl.BlockSpec((rows, D_ML), lambda i: (0, 0)),
                   pl.BlockSpec((rows, LANES), lambda i: (0, 0))],
        out_shape=[jax.ShapeDtypeStruct((rows, D_ML), F32), jax.ShapeDtypeStruct((rows, D_ML), F32),
                   jax.ShapeDtypeStruct((rows, LANES), F32)],
        compiler_params=_cparams(("arbitrary",), 48),
        name="mlstm_step_a",
    )(z_small, z_small, conv0_t, m0, conv_w, conv_b, wq_bf, wk_bf)
    row_spec = pl.BlockSpec((rows, D_ML), lambda i: (0, 0))
    c_spec = pl.BlockSpec((bb, N_HEADS, DH, DH), lambda i: (i, 0, 0, 0))
    y, c_new, n_new = pl.pallas_call(
        functools.partial(_mlstm_step_b_body, bb),
        grid=(rows // bb,),
        in_specs=[pl.BlockSpec(memory_space=pltpu.SMEM), row_spec, row_spec, row_spec,
                  pl.BlockSpec((rows, D_ML), lambda i: (row_blk, 2)),
                  pl.BlockSpec((rows, D_ML), lambda i: (row_blk, 3)),
                  pl.BlockSpec((1, D_ML), lambda i: (0, 0)), c_spec],
        out_specs=[row_spec, c_spec, row_spec],
        out_shape=[jax.ShapeDtypeStruct((rows, D_ML), F32), jax.ShapeDtypeStruct(c0.shape, F32),
                   jax.ShapeDtypeStruct((rows, D_ML), F32)],
        compiler_params=_cparams(("arbitrary",), 56),
        name="mlstm_step_b",
    )(sc[:, :N_STEP_SCALARS].reshape(rows * N_STEP_SCALARS), q, k, n0, z_small, z_small, norm_g, c0)
    return y, c_new, n_new, sc[:, 3 * N_HEADS:4 * N_HEADS]


ROW_CH = D_MODEL // LANES


def _load_row_tiles(ref, row0, m, c):
    return ref[pl.ds(row0 * ROW_CH + c, m, stride=ROW_CH), :]


def _store_row_tiles(ref, row0, val):
    m = val.shape[0]
    for c in range(ROW_CH):
        ref[pl.ds(row0 * ROW_CH + c, m, stride=ROW_CH), :] = val[:, c * LANES:(c + 1) * LANES]


def _mix_out_body(xp_ref, xs_ref, ysp0_ref, ysp1_ref, yss_ref, ymp0_ref, ymp1_ref, yms_ref,
                  gin_ref, bin_ref, wout_ref, g1_ref, b1_ref, rwh_ref, rwl_ref, rb_ref,
                  h1_ref, e_ref, gate_ref):
    is_step = pl.program_id(0) == pl.num_programs(0) - 1
    consts = (gin_ref, bin_ref, wout_ref, g1_ref, b1_ref, rwh_ref, rwl_ref, rb_ref)
    outs = (h1_ref, e_ref, gate_ref)
    two = lambda a, b: jnp.concatenate([a[...], b[...]], axis=0)

    @pl.when(jnp.logical_not(is_step))
    def _():
        _mix_out_tile(xp_ref[...], two(ysp0_ref, ysp1_ref), two(ymp0_ref, ymp1_ref), consts, outs)

    @pl.when(is_step)
    def _():
        _mix_out_tile(two(xs_ref, xs_ref), two(yss_ref, yss_ref), two(yms_ref, yms_ref), consts, outs)


def _mix_out_tile(x, ys, ym, consts, outs):
    gin_ref, bin_ref, wout_ref, g1_ref, b1_ref, rwh_ref, rwl_ref, rb_ref = consts
    h1_ref, e_ref, gate_ref = outs
    hp = _layer_norm(x, gin_ref[...], bin_ref[...])
    ycat = jnp.concatenate([ys, ym], axis=1).astype(BF16)
    mix = jnp.dot(ycat, wout_ref[...], preferred_element_type=F32)
    h1 = _layer_norm(DEEPNORM_ALPHA * hp + mix, g1_ref[...], b1_ref[...])
    _store_row_tiles(h1_ref, 0, h1)
    xh = h1.astype(BF16)
    xl = (h1 - xh.astype(F32)).astype(BF16)
    logits = (jnp.dot(xh, rwh_ref[...], preferred_element_type=F32)
              + jnp.dot(xh, rwl_ref[...], preferred_element_type=F32)
              + jnp.dot(xl, rwh_ref[...], preferred_element_type=F32)) + rb_ref[...]
    rows = logits.shape[0]
    lane = lax.broadcasted_iota(jnp.int32, (rows, LANES), 1)
    logits = jnp.where(lane < N_EXPERTS, logits, NEG_INF)
    e_acc = jnp.zeros((rows, LANES), jnp.int32)
    v_acc = jnp.full((rows, LANES), NEG_INF, F32)
    for k in range(TOP_K):
        mx = jnp.max(logits, axis=1, keepdims=True)
        idx = jnp.min(jnp.where(logits == mx, lane, LANES), axis=1, keepdims=True)
        e_acc = jnp.where(lane == k, idx, e_acc)
        v_acc = jnp.where(lane == k, mx, v_acc)
        logits = jnp.where(lane == idx, NEG_INF, logits)
    p = jnp.exp(v_acc - jnp.max(v_acc, axis=1, keepdims=True))
    e_ref[...] = e_acc
    gate_ref[...] = p / jnp.sum(p, axis=1, keepdims=True)


def _mix_out(x_p, x_s, ys_p, ys_s, ym_p, ym_s, consts, tiles_per_seq):
    d = D_MODEL
    full = lambda a: pl.BlockSpec(a.shape, lambda i: (0,) * a.ndim)
    tm = 2 * TILE_T
    n_p = x_p.shape[0] // tm
    n_tiles = n_p + 1
    rows = n_tiles * tm

    def p_idx(i):
        return jnp.minimum(i, n_p - 1)

    def frame_idx(i, half):
        j = 2 * p_idx(i) + half
        return (j // tiles_per_seq) * (tiles_per_seq + 1) + j % tiles_per_seq + 1

    def s5_idx(i, half):
        j = 2 * p_idx(i) + half
        return _s5_frame_block(j // tiles_per_seq, j % tiles_per_seq + 1, tiles_per_seq + 1)

    frame = lambda width, half: pl.BlockSpec((TILE_T, width), lambda i: (frame_idx(i, half), 0))
    frame_s5 = lambda half: pl.BlockSpec((TILE_T, D_S5), lambda i: (s5_idx(i, half), 0))
    first = lambda width: pl.BlockSpec((TILE_T, width), lambda i: (0, 0))
    return pl.pallas_call(
        _mix_out_body,
        grid=(n_tiles,),
        in_specs=[pl.BlockSpec((tm, d), lambda i: (p_idx(i), 0)), first(d),
                  frame_s5(0), frame_s5(1), first(D_S5),
                  frame(D_ML, 0), frame(D_ML, 1), first(D_ML)] + [full(a) for a in consts],
        out_specs=[pl.BlockSpec((tm * ROW_CH, LANES), lambda i: (i, 0)),
                   pl.BlockSpec((tm, LANES), lambda i: (i, 0)),
                   pl.BlockSpec((tm, LANES), lambda i: (i, 0))],
        out_shape=[jax.ShapeDtypeStruct((rows * ROW_CH, LANES), F32), jax.ShapeDtypeStruct((rows, LANES), jnp.int32),
                   jax.ShapeDtypeStruct((rows, LANES), F32)],
        compiler_params=_cparams(("arbitrary",), 56),
        name="mix_out",
    )(x_p, x_s, ys_p, ys_p, ys_s, ym_p, ym_p, ym_s, *consts)


MOE_BLK = 128
MOE_TM = 1536
MOE_TF = 256
MOE_SUB_MAX = 5
DMA_UNROLL = 8
MOE_STAGES = 4
W_DMA_SPLIT = 4
SCALAR_UNROLL = 32


def _invert_rows_body(dest_ref, src_ref):
    def zero(i, c):
        for q in range(SCALAR_UNROLL):
            src_ref[i * SCALAR_UNROLL + q] = 0
        return c
    lax.fori_loop(0, src_ref.shape[0] // SCALAR_UNROLL, zero, 0)

    def put(i, c):
        for q in range(SCALAR_UNROLL):
            src_ref[dest_ref[i * SCALAR_UNROLL + q]] = i * (SCALAR_UNROLL // TOP_K) + q // TOP_K
        return c
    lax.fori_loop(0, dest_ref.shape[0] // SCALAR_UNROLL, put, 0)


def _invert_rows(dest_flat, rows_pad):
    return pl.pallas_call(
        _invert_rows_body,
        in_specs=[pl.BlockSpec(memory_space=pltpu.SMEM)],
        out_specs=pl.BlockSpec(memory_space=pltpu.SMEM),
        out_shape=jax.ShapeDtypeStruct((rows_pad,), jnp.int32),
        name="invert_rows",
    )(dest_flat)


def _moe_routing(top_e, n_tok):
    n_pairs = n_tok * TOP_K
    rows_pad = n_pairs + N_EXPERTS * MOE_BLK
    n_pass = N_EXPERTS + rows_pad // MOE_TM
    onehot = (top_e[:, :, None] == jnp.arange(N_EXPERTS, dtype=jnp.int32)).astype(jnp.int32).sum(1)
    incl = jnp.cumsum(onehot, axis=0)
    counts = incl[-1]
    rank = jnp.take_along_axis(incl - onehot, top_e, axis=1)
    padded = (counts + MOE_BLK - 1) // MOE_BLK * MOE_BLK
    pad_end = jnp.cumsum(padded)
    pad_start = pad_end - padded
    dest = pad_start[top_e] + rank
    src = _invert_rows(dest.reshape(-1).astype(jnp.int32), rows_pad)
    passes_e = (padded + MOE_TM - 1) // MOE_TM
    pass_end = jnp.cumsum(passes_e)
    u = jnp.arange(n_pass, dtype=jnp.int32)
    e_u = jnp.minimum(jnp.searchsorted(pass_end, u, side="right"), N_EXPERTS - 1).astype(jnp.int32)
    j_u = u - (pass_end - passes_e)[e_u]
    rem = padded[e_u] - j_u * MOE_TM
    nblk = jnp.where(u < pass_end[-1], jnp.clip(rem, 0, MOE_TM) // MOE_BLK, 0).astype(jnp.int32)
    blk0 = ((pad_start[e_u] + j_u * MOE_TM) // MOE_BLK).astype(jnp.int32)
    blk0 = jnp.where(nblk > 0, blk0, 0)
    last_e = e_u[jnp.maximum(pass_end[-1] - 1, 0)]
    e_u = jnp.where(nblk > 0, e_u, last_e)
    nblk = jnp.concatenate([nblk, (pad_end[-1:] // MOE_BLK).astype(jnp.int32)])
    return dest.astype(jnp.int32), src, e_u, blk0, nblk


def _moe_ffn_body(e_ref, blk0_ref, nblk_ref, src_ref, h1_ref, wgu_ref, wdn_ref, bgu_ref, bd_ref,
                  ybuf_ref, xg_ref, xb_ref, acc_ref, w32_a, w32_b, wb_a, wb_b, stage_ref, gsem, osem, wsem):
    u = pl.program_id(0)
    n_pass = pl.num_programs(0)
    n_f = D_FF // MOE_TF
    nblk = nblk_ref[u]
    row0 = blk0_ref[u] * MOE_BLK
    blk_rt = MOE_BLK * ROW_CH
    nxt = jnp.minimum(u + 1, n_pass - 1)
    nxt_active = jnp.logical_and(u + 1 < n_pass, nblk_ref[nxt] > 0)
    w32 = (w32_a, w32_b)
    wb = (wb_a, wb_b)

    def w_copies(p, f, par):
        e = e_ref[p]
        cols = pl.ds(pl.multiple_of(f * MOE_TF, MOE_TF), MOE_TF)
        cols_up = pl.ds(pl.multiple_of(D_FF + f * MOE_TF, MOE_TF), MOE_TF)
        g32, u32, d32 = w32[par]
        out = []
        for i in range(W_DMA_SPLIT):
            r1 = pl.ds(i * (D_MODEL // W_DMA_SPLIT), D_MODEL // W_DMA_SPLIT)
            r2 = pl.ds(i * (MOE_TF // W_DMA_SPLIT), MOE_TF // W_DMA_SPLIT)
            rows_dn = pl.ds(pl.multiple_of(f * MOE_TF + i * (MOE_TF // W_DMA_SPLIT), MOE_TF // W_DMA_SPLIT),
                            MOE_TF // W_DMA_SPLIT)
            out.append(pltpu.make_async_copy(wgu_ref.at[e, r1, cols], g32.at[r1], wsem.at[par]))
            out.append(pltpu.make_async_copy(wgu_ref.at[e, r1, cols_up], u32.at[r1], wsem.at[par]))
            out.append(pltpu.make_async_copy(wdn_ref.at[e, rows_dn, :], d32.at[r2], wsem.at[par]))
        return out

    def cast_w_half(par, half):
        g32, u32, d32 = w32[par]
        w1, w2 = wb[par]
        r1 = pl.ds(pl.multiple_of(half * (D_MODEL // 2), D_MODEL // 2), D_MODEL // 2)
        r2 = pl.ds(pl.multiple_of(half * (MOE_TF // 2), MOE_TF // 2), MOE_TF // 2)
        w1[r1, :MOE_TF] = g32[r1, :].astype(BF16)
        w1[r1, MOE_TF:] = u32[r1, :].astype(BF16)
        w2[r2, :] = d32[r2, :].astype(BF16)

    def cast_w(par):
        cast_w_half(par, 0)
        cast_w_half(par, 1)

    def for_all_rows(p, fn):
        base = blk0_ref[p] * MOE_BLK

        def body(rb, c):
            for s in range(SUBLANES):
                tok = src_ref[base + rb * SUBLANES + s]
                fn(pltpu.make_async_copy(h1_ref.at[pl.ds(tok * ROW_CH, ROW_CH)],
                                         xg_ref.at[rb, pl.ds(0, ROW_CH), s], gsem))
            return c
        lax.fori_loop(0, nblk_ref[p] * (MOE_BLK // SUBLANES), body, 0)

    def issue_gather(p):
        for_all_rows(p, lambda cp: cp.start())

    def out_copy(b):
        dst = pl.ds(pl.multiple_of((row0 + b * MOE_BLK) * ROW_CH, blk_rt), blk_rt)
        return pltpu.make_async_copy(stage_ref.at[b % MOE_STAGES], ybuf_ref.at[dst], osem.at[b % MOE_STAGES])

    @pl.when(nblk > 0)
    def _():
        @pl.when(u == 0)
        def _():
            issue_gather(u)
            for cp in w_copies(u, 0, 0):
                cp.start()
            for cp in w_copies(u, 0, 0):
                cp.wait()
            cast_w(0)
            for cp in w_copies(u, 1, 1):
                cp.start()

        row_wait = pltpu.make_async_copy(h1_ref.at[pl.ds(0, ROW_CH)], xg_ref.at[0, pl.ds(0, ROW_CH), 0], gsem)

        def wait_rows(rb, c):
            for _ in range(SUBLANES):
                row_wait.wait()
            return c
        lax.fori_loop(0, nblk * (MOE_BLK // SUBLANES), wait_rows, 0)

        def cast(b, c):
            rs = pl.ds(pl.multiple_of(b * MOE_BLK, MOE_BLK), MOE_BLK)
            gs = pl.ds(pl.multiple_of(b * (MOE_BLK // SUBLANES), MOE_BLK // SUBLANES), MOE_BLK // SUBLANES)
            for ch in range(ROW_CH):
                xb_ref[rs, ch * LANES:(ch + 1) * LANES] = xg_ref[gs, ch].reshape(MOE_BLK, LANES).astype(BF16)
            acc_ref[rs, :] = jnp.zeros((MOE_BLK, acc_ref.shape[1]), F32)
            return c
        lax.fori_loop(0, nblk, cast, 0)

        def ffn_rows(r0, m, f, par):
            w1, w2 = wb[par]
            bg = bgu_ref[0, :, pl.ds(pl.multiple_of(f * MOE_TF, MOE_TF), MOE_TF)]
            bu = bgu_ref[0, :, pl.ds(pl.multiple_of(D_FF + f * MOE_TF, MOE_TF), MOE_TF)]
            rs = pl.ds(pl.multiple_of(r0, MOE_BLK), m)
            h = jnp.dot(xb_ref[rs, :], w1[...], preferred_element_type=F32)
            x_glu = jnp.minimum(h[:, :MOE_TF] + bg, SWIGLU_LIMIT)
            x_lin = jnp.clip(h[:, MOE_TF:] + bu, -SWIGLU_LIMIT, SWIGLU_LIMIT)
            act = x_glu * jax.nn.sigmoid(SWIGLU_ALPHA * x_glu) * (x_lin + 1.0)
            acc_ref[rs, :] += jnp.dot(act.astype(BF16), w2[...], preferred_element_type=F32)

        def tile_step(f, par):
            in_pass1 = f + 1 < n_f
            in_pass2 = f + 2 < n_f

            @pl.when(jnp.logical_or(in_pass1, nxt_active))
            def _():
                for cp in w_copies(jnp.where(in_pass1, u, nxt), jnp.where(in_pass1, f + 1, 0), 1 - par):
                    cp.wait()

            @pl.when(jnp.logical_or(in_pass2, nxt_active))
            def _():
                for cp in w_copies(jnp.where(in_pass2, u, nxt), jnp.where(in_pass2, f + 2, f + 2 - n_f), par):
                    cp.start()

            @pl.when(jnp.logical_and(f == 1, nxt_active))
            def _():
                issue_gather(nxt)

            n_sub = (nblk + (MOE_SUB_MAX - 1)) // MOE_SUB_MAX
            q = nblk // n_sub
            n_hi = nblk - q * n_sub

            def sub(i, blk):
                sz = q + (i < n_hi).astype(jnp.int32)
                for m in range(1, MOE_SUB_MAX + 1):
                    @pl.when(sz == m)
                    def _(m=m):
                        ffn_rows(blk * MOE_BLK, m * MOE_BLK, f, par)
                        cast_w_half(1 - par, jnp.minimum(i, 1))
                return blk + sz
            lax.fori_loop(0, n_sub, sub, 0)

            @pl.when(n_sub == 1)
            def _():
                cast_w_half(1 - par, 1)

        def tile_pair(f2, c):
            tile_step(2 * f2, 0)
            tile_step(2 * f2 + 1, 1)
            return c
        lax.fori_loop(0, n_f // 2, tile_pair, 0)

        def emit(b, c):
            @pl.when(b >= MOE_STAGES)
            def _():
                out_copy(b - MOE_STAGES).wait()
            rs = pl.ds(pl.multiple_of(b * MOE_BLK, MOE_BLK), MOE_BLK)
            val = acc_ref[rs, :] + bd_ref[0]
            for ch in range(ROW_CH):
                stage_ref[b % MOE_STAGES, pl.ds(ch, MOE_BLK, stride=ROW_CH), :] = val[:, ch * LANES:(ch + 1) * LANES]
            out_copy(b).start()
            return c
        lax.fori_loop(0, nblk, emit, 0)

        def drain(b, c):
            out_copy(b).wait()
            return c
        lax.fori_loop(jnp.maximum(nblk - MOE_STAGES, 0), nblk, drain, 0)

    @pl.when(u == n_pass - 1)
    def _():
        used = nblk_ref[n_pass]
        total = ybuf_ref.shape[0] // blk_rt
        stage_ref[0] = jnp.zeros((blk_rt, LANES), F32)

        def zero_copy(b):
            dst = pl.ds(pl.multiple_of(b * blk_rt, blk_rt), blk_rt)
            return pltpu.make_async_copy(stage_ref.at[0], ybuf_ref.at[dst], osem.at[0])

        def start(b, c):
            zero_copy(b).start()
            return c
        lax.fori_loop(used, total, start, 0)

        def wait(b, c):
            zero_copy(b).wait()
            return c
        lax.fori_loop(used, total, wait, 0)


def _moe_ffn(h1, src, e_u, blk0, nblk, w_gu, b_gu, w_dn, b_dn, rows_pad):
    d = D_MODEL
    n_pass = e_u.shape[0]
    w32_set = (pltpu.VMEM((d, MOE_TF), F32), pltpu.VMEM((d, MOE_TF), F32), pltpu.VMEM((MOE_TF, d), F32))
    wb_set = (pltpu.VMEM((d, 2 * MOE_TF), BF16), pltpu.VMEM((MOE_TF, d), BF16))
    grid_spec = pltpu.PrefetchScalarGridSpec(
        num_scalar_prefetch=4,
        grid=(n_pass,),
        in_specs=[
            pl.BlockSpec(memory_space=pl.ANY),
            pl.BlockSpec(memory_space=pl.ANY),
            pl.BlockSpec(memory_space=pl.ANY),
            pl.BlockSpec((1, 1, 2 * D_FF), lambda u, e, b0, nb, s: (e[u], 0, 0)),
            pl.BlockSpec((1, 1, d), lambda u, e, b0, nb, s: (e[u], 0, 0)),
        ],
        out_specs=pl.BlockSpec(memory_space=pl.ANY),
        scratch_shapes=[pltpu.VMEM((MOE_TM // SUBLANES, ROW_CH, SUBLANES, LANES), F32), pltpu.VMEM((MOE_TM, d), BF16),
                        pltpu.VMEM((MOE_TM, d), F32), w32_set, w32_set, wb_set, wb_set,
                        pltpu.VMEM((MOE_STAGES, MOE_BLK * ROW_CH, LANES), F32),
                        pltpu.SemaphoreType.DMA(()), pltpu.SemaphoreType.DMA((MOE_STAGES,)),
                        pltpu.SemaphoreType.DMA((2,))],
    )
    return pl.pallas_call(
        _moe_ffn_body,
        grid_spec=grid_spec,
        out_shape=jax.ShapeDtypeStruct((rows_pad * ROW_CH, LANES), F32),
        compiler_params=_cparams(("arbitrary",), 60),
        name="moe_ffn",
    )(e_u, blk0, nblk, src, h1, w_gu, w_dn, b_gu.reshape(N_EXPERTS, 1, 2 * D_FF), b_dn.reshape(N_EXPERTS, 1, d))


def _moe_combine_body(dest_ref, h1_ref, gate_ref, g2_ref, b2_ref, ybuf_ref, outp_ref, outs_ref,
                      buf_ref, sem, pre_ref):
    i = pl.program_id(0)
    n_pairs = TILE_T * TOP_K
    slot = i % 2

    def copy(tile, sl, rb, s, k):
        src_row = dest_ref[tile * n_pairs + rb * (SUBLANES * TOP_K) + (s * TOP_K + k)]
        return pltpu.make_async_copy(ybuf_ref.at[pl.ds(src_row * ROW_CH, ROW_CH)],
                                     buf_ref.at[sl, k, rb, pl.ds(0, ROW_CH), s], sem.at[sl])

    def for_all_rows(tile, sl, fn):
        def body(rb, c):
            for s in range(SUBLANES):
                for k in range(TOP_K):
                    fn(copy(tile, sl, rb, s, k))
            return c
        lax.fori_loop(0, TILE_T // SUBLANES, body, 0)

    def issue(tile, sl):
        for_all_rows(tile, sl, lambda cp: cp.start())

    @pl.when(i == 0)
    def _():
        issue(0, 0)

    @pl.when(i + 1 < pl.num_programs(0))
    def _():
        issue(i + 1, 1 - slot)

    row_wait = pltpu.make_async_copy(ybuf_ref.at[pl.ds(0, ROW_CH)],
                                     buf_ref.at[slot, 0, 0, pl.ds(0, ROW_CH), 0], sem.at[slot])

    def wait_rows(j, c):
        for _ in range(SUBLANES * TOP_K):
            row_wait.wait()
        return c
    lax.fori_loop(0, TILE_T // SUBLANES, wait_rows, 0)

    gate = gate_ref[...]
    cs = lambda c: slice(c * LANES, (c + 1) * LANES)
    part = jnp.zeros((TILE_T, LANES), F32)
    gate_b = [jnp.broadcast_to(gate[:, k:k + 1], (TILE_T, LANES)) for k in range(TOP_K)]
    for c in range(ROW_CH):
        fc = None
        for k in range(TOP_K):
            v = buf_ref[slot, k, :, c].reshape(TILE_T, LANES) * gate_b[k]
            fc = v if fc is None else fc + v
        pre = DEEPNORM_ALPHA * _load_row_tiles(h1_ref, 0, TILE_T, c) + fc
        pre_ref[:, cs(c)] = pre
        part = part + pre
    mu = jnp.sum(part, axis=1, keepdims=True) * (1.0 / D_MODEL)
    part = jnp.zeros((TILE_T, LANES), F32)
    for c in range(ROW_CH):
        dlt = pre_ref[:, cs(c)] - mu
        part = part + dlt * dlt
    rstd = lax.rsqrt(jnp.sum(part, axis=1, keepdims=True) * (1.0 / D_MODEL) + LN_EPS)
    is_step = i == pl.num_programs(0) - 1

    def write(out_ref):
        for c in range(ROW_CH):
            out_ref[:, cs(c)] = (pre_ref[:, cs(c)] - mu) * rstd * g2_ref[:, cs(c)] + b2_ref[:, cs(c)]

    @pl.when(jnp.logical_not(is_step))
    def _():
        write(outp_ref)

    @pl.when(is_step)
    def _():
        write(outs_ref)


def _moe_combine(dest, h1, gates, g2, b2, ybuf, n_tok):
    d = D_MODEL
    n_tiles = n_tok // TILE_T
    grid_spec = pltpu.PrefetchScalarGridSpec(
        num_scalar_prefetch=1,
        grid=(n_tiles,),
        in_specs=[pl.BlockSpec((TILE_T * ROW_CH, LANES), lambda i, s: (i, 0)),
                  pl.BlockSpec((TILE_T, LANES), lambda i, s: (i, 0)),
                  pl.BlockSpec((1, d), lambda i, s: (0, 0)),
                  pl.BlockSpec((1, d), lambda i, s: (0, 0)),
                  pl.BlockSpec(memory_space=pl.ANY)],
        out_specs=[pl.BlockSpec((TILE_T, d), lambda i, s: (jnp.minimum(i, n_tiles - 2), 0)),
                   pl.BlockSpec((TILE_T, d), lambda i, s: (0, 0))],
        scratch_shapes=[pltpu.VMEM((2, TOP_K, TILE_T // SUBLANES, ROW_CH, SUBLANES, LANES), F32),
                        pltpu.SemaphoreType.DMA((2,)),
                        pltpu.VMEM((TILE_T, d), F32)],
    )
    return pl.pallas_call(
        _moe_combine_body,
        grid_spec=grid_spec,
        out_shape=[jax.ShapeDtypeStruct((n_tok - TILE_T, d), F32), jax.ShapeDtypeStruct((TILE_T, d), F32)],
        compiler_params=_cparams(("arbitrary",), 48),
        name="moe_combine",
    )(dest.reshape(-1), h1, gates, g2, b2, ybuf)


def kernel(x_prompt, x_sample, state_s5_re, state_s5_im, state_mlstm_c, state_mlstm_n, state_mlstm_m, state_mlstm_conv, meta_tokens, ln_in_g, ln_in_b, w_in, b_in, s5_a_re, s5_a_im, s5_log_dt, s5_b_re, s5_b_im, s5_c_re, s5_c_im, s5_d, s5_w_glu, mlstm_conv_w, mlstm_conv_b, mlstm_wq, mlstm_wk, mlstm_norm_g, w_out, ln1_g, ln1_b, router_w, router_b, w_gate_up, b_gate_up, w_down, b_down, ln2_g, ln2_b):
    bsz, seq, d = x_prompt.shape
    dec_b = x_sample.shape[0]
    n_pad = TILE_T - N_META
    x_small = jnp.concatenate([jnp.zeros((n_pad, d), F32), meta_tokens, x_sample.reshape(dec_b, d)], axis=0)
    w_in_p = jnp.pad(w_in[0], ((0, 0), (0, N_IN_PAD - N_IN))).astype(BF16)
    b_in_p = jnp.pad(b_in[0], (0, N_IN_PAD - N_IN)).reshape(1, N_IN_PAD)
    g_in = ln_in_g.reshape(1, d)
    bb_in = ln_in_b.reshape(1, d)
    z_p = _inproj(x_prompt.reshape(bsz * seq, d), g_in, bb_in, w_in_p, b_in_p, 512, 1408)
    z_s = _inproj(x_small, g_in, bb_in, w_in_p, b_in_p, 256, 1408)
    tabs = _s5_prep(s5_a_re[0], s5_a_im[0], s5_log_dt[0], s5_b_re[0], s5_b_im[0])
    bd_b, bd_c = _s5_block_diag(tabs[6], tabs[7], s5_c_re[0], s5_c_im[0])
    wglu_bf = s5_w_glu[0].astype(BF16)
    d_skip = s5_d[0].reshape(1, D_S5)
    n_tiles = seq // TILE_T + 1
    zero_state = jnp.zeros((bsz, 1, N_STATE), F32)
    y_s5_p, s5r_p, s5i_p = _s5_seq(z_s, z_p, zero_state, zero_state, tabs[:6], bd_b, bd_c, d_skip, wglu_bf,
                                   bsz, n_tiles, n_pad)
    y_s5_s, s5r_s, s5i_s = _s5_step(z_s, 1, state_s5_re[0].reshape(dec_b, N_STATE),
                                    state_s5_im[0].reshape(dec_b, N_STATE), tabs[:6], bd_b, bd_c, d_skip, wglu_bf)

    conv_w = mlstm_conv_w[0]
    conv_b = mlstm_conv_b[0].reshape(1, D_ML)
    wq_bf = mlstm_wq[0].astype(BF16)
    wk_bf = mlstm_wk[0].astype(BF16)
    norm_g = mlstm_norm_g[0].reshape(1, D_ML)
    y_ml_p, c_p, n_p, m_p = _mlstm_seq(z_s, z_p, conv_w, conv_b, wq_bf, wk_bf, norm_g, bsz, n_tiles, n_pad)
    conv0 = state_mlstm_conv[0]
    y_ml_s, c_s, n_s, m_s = _mlstm_step(z_s, 1, conv0, state_mlstm_c[0], state_mlstm_n[0].reshape(dec_b, D_ML),
                                        state_mlstm_m[0], conv_w, conv_b, wq_bf, wk_bf, norm_g)

    rw = jnp.pad(router_w[0], ((0, 0), (0, LANES - N_EXPERTS)))
    rw_hi = rw.astype(BF16)
    rw_lo = (rw - rw_hi.astype(F32)).astype(BF16)
    rb = jnp.pad(router_b[0], (0, LANES - N_EXPERTS)).reshape(1, LANES)
    consts = (g_in, bb_in, w_out[0].astype(BF16), ln1_g[0].reshape(1, d), ln1_b[0].reshape(1, d), rw_hi, rw_lo, rb)
    h1, top_e, gates = _mix_out(x_prompt.reshape(bsz * seq, d), x_sample.reshape(dec_b, d),
                                y_s5_p, y_s5_s, y_ml_p, y_ml_s, consts, seq // TILE_T)
    n_tok = bsz * seq + dec_b
    dest, src, e_u, blk0, nblk = _moe_routing(top_e[:n_tok, :TOP_K], n_tok)
    ybuf = _moe_ffn(h1, src, e_u, blk0, nblk, w_gate_up[0], b_gate_up[0], w_down[0], b_down[0], src.shape[0])
    out_p, out_s = _moe_combine(dest, h1, gates, ln2_g[0].reshape(1, d), ln2_b[0].reshape(1, d), ybuf, n_tok)

    y_prompt = out_p.reshape(bsz, seq, d)
    y_sample = out_s.reshape(dec_b, 1, d)
    xm_p = z_p.reshape(bsz, seq, N_IN_PAD)[:, seq - (CONV_W - 1):, D_S5:D_S5 + D_ML]
    xm_s = z_s[TILE_T:, D_S5:D_S5 + D_ML]
    conv_s = jnp.concatenate([conv0[:, 1:], xm_s[:, None, :]], axis=1)
    return (y_prompt, y_sample,
            s5r_p.reshape(1, bsz, N_GROUPS, S5_STATE), s5i_p.reshape(1, bsz, N_GROUPS, S5_STATE),
            c_p[None], n_p.reshape(1, bsz, N_HEADS, DH), m_p[None, :, :N_HEADS, 0], xm_p[None],
            s5r_s.reshape(1, dec_b, N_GROUPS, S5_STATE), s5i_s.reshape(1, dec_b, N_GROUPS, S5_STATE),
            c_s[None], n_s.reshape(1, dec_b, N_HEADS, DH), m_s[None], conv_s[None])
```

```python
import functools
import math

import jax
import jax.numpy as jnp
from jax import lax
from jax.experimental import pallas as pl
from jax.experimental.pallas import tpu as pltpu

F32 = jnp.float32
BF16 = jnp.bfloat16

D_MODEL = 2048
N_META = 16
D_S5 = 1024
D_ML = 1024
S5_CH = 16
N_GROUPS = 64
S5_STATE = 64
N_STATE = N_GROUPS * S5_STATE
N_HEADS = 4
DH = 256
CONV_W = 4
N_EXPERTS = 32
TOP_K = 4
D_FF = 2048
SWIGLU_LIMIT = 7.0
SWIGLU_ALPHA = 1.702
LN_EPS = 1e-5
DEEPNORM_ALPHA = 2.0 ** 0.25
N_IN = D_S5 + 3 * D_ML + 2 * N_HEADS

LANES = 128
SUBLANES = 8
MXU_DIM = 256
MXU_ROWS = 16

TILE_T = 128
SEG_LEN = TILE_T // SUBLANES
N_IN_PAD = 33 * LANES
GATE_COL = 4 * D_S5 // LANES
S5_KCH = D_S5 // MXU_DIM
S5_SCH = N_STATE // S5_KCH
SCAN_LW = 256


def _cparams(sem, vmem_mb=None):
    kw = dict(dimension_semantics=sem)
    if vmem_mb is not None:
        kw["vmem_limit_bytes"] = vmem_mb * 1024 * 1024
    return pltpu.CompilerParams(**kw)


def _layer_norm(x, g, b):
    mu = jnp.mean(x, axis=-1, keepdims=True)
    xc = x - mu
    var = jnp.mean(xc * xc, axis=-1, keepdims=True)
    return xc * lax.rsqrt(var + LN_EPS) * g + b


def _inproj_body(x_ref, g_ref, b_ref, w_ref, bias_ref, z_ref, hn_ref):
    @pl.when(pl.program_id(1) == 0)
    def _():
        hn_ref[...] = _layer_norm(x_ref[...], g_ref[...], b_ref[...]).astype(BF16)

    z_ref[...] = jnp.dot(hn_ref[...], w_ref[...], preferred_element_type=F32) + bias_ref[...]


def _inproj(x, g, b, w, bias, tm, tn):
    rows, d = x.shape
    n = w.shape[1]
    return pl.pallas_call(
        _inproj_body,
        grid=(rows // tm, n // tn),
        in_specs=[
            pl.BlockSpec((tm, d), lambda i, j: (i, 0)),
            pl.BlockSpec((1, d), lambda i, j: (0, 0)),
            pl.BlockSpec((1, d), lambda i, j: (0, 0)),
            pl.BlockSpec((d, tn), lambda i, j: (0, j)),
            pl.BlockSpec((1, tn), lambda i, j: (0, j)),
        ],
        out_specs=pl.BlockSpec((tm, tn), lambda i, j: (i, j)),
        out_shape=jax.ShapeDtypeStruct((rows, n), F32),
        scratch_shapes=[pltpu.VMEM((tm, d), BF16)],
        compiler_params=_cparams(("arbitrary", "arbitrary"), 48),
        name="inproj",
    )(x, g, b, w, bias)


def _cmul(ar, ai, br, bi):
    return ar * br - ai * bi, ar * bi + ai * br


def _s5_prep_body(are_ref, aim_ref, dt_ref, bre_ref, bim_ref,
                  pre_ref, pim_ref, hre_ref, him_ref, qre_ref, qim_ref, bbre_ref, bbim_ref):
    lr = are_ref[...]
    li = aim_ref[...]
    dt = jnp.exp(dt_ref[...])
    mag = jnp.exp(lr * dt)
    ar = mag * jnp.cos(li * dt)
    ai = mag * jnp.sin(li * dt)
    nr = ar - 1.0
    ni = ai
    den = lr * lr + li * li
    cr = (nr * lr + ni * li) / den
    ci = (ni * lr - nr * li) / den
    bbre_ref[...] = cr * bre_ref[...] - ci * bim_ref[...]
    bbim_ref[...] = cr * bim_ref[...] + ci * bre_ref[...]
    pr, pi = ar, ai
    for j in range(SEG_LEN):
        pre_ref[j:j + 1, :] = pr
        pim_ref[j:j + 1, :] = pi
        if j + 1 < SEG_LEN:
            pr, pi = _cmul(pr, pi, ar, ai)
    row = lax.broadcasted_iota(jnp.int32, (SUBLANES, N_STATE), 0)
    kr, ki = pr, pi
    for idx, k in enumerate((1, 2, 4)):
        hre_ref[idx * 8:(idx + 1) * 8, :] = jnp.where(row >= k, jnp.broadcast_to(kr, (SUBLANES, N_STATE)), 0.0)
        him_ref[idx * 8:(idx + 1) * 8, :] = jnp.where(row >= k, jnp.broadcast_to(ki, (SUBLANES, N_STATE)), 0.0)
        kr, ki = _cmul(kr, ki, kr, ki)
    qr, qi = pr, pi
    for s in range(SUBLANES):
        qre_ref[s:s + 1, :] = qr
        qim_ref[s:s + 1, :] = qi
        if s + 1 < SUBLANES:
            qr, qi = _cmul(qr, qi, pr, pi)


def _s5_prep(a_re, a_im, log_dt, b_re, b_im):
    n = N_STATE
    are = a_re.reshape(1, n)
    aim = a_im.reshape(1, n)
    dtl = jnp.broadcast_to(log_dt[:, None], (N_GROUPS, S5_STATE)).reshape(1, n)
    bre = b_re.transpose(2, 0, 1).reshape(S5_CH, n)
    bim = b_im.transpose(2, 0, 1).reshape(S5_CH, n)
    shp = lambda r: jax.ShapeDtypeStruct((r, n), F32)
    return pl.pallas_call(
        _s5_prep_body,
        out_shape=(shp(SEG_LEN), shp(SEG_LEN), shp(24), shp(24), shp(8), shp(8), shp(S5_CH), shp(S5_CH)),
        name="s5_prep",
    )(are, aim, dtl, bre, bim)


def _s5_block_diag(bb_re, bb_im, c_re, c_im):
    gpc = 16
    eye = jnp.eye(gpc, dtype=bool)

    def bd_in(bb):
        t = bb.reshape(S5_CH, S5_KCH, gpc, S5_STATE).transpose(1, 2, 0, 3)
        t = jnp.where(eye[None, :, None, :, None], t[:, :, :, None, :], 0.0)
        return t.reshape(S5_KCH, gpc * S5_CH, gpc * S5_STATE)

    def bd_out(c):
        t = c.reshape(S5_KCH, gpc, S5_CH, S5_STATE).transpose(0, 1, 3, 2)
        t = jnp.where(eye[None, :, None, :, None], t[:, :, :, None, :], 0.0)
        return t.reshape(S5_KCH, gpc * S5_STATE, gpc * S5_CH)

    bd_b = jnp.concatenate([bd_in(bb_re), bd_in(bb_im)], axis=2).astype(BF16)
    bd_c = jnp.concatenate([bd_out(c_re), bd_out(-c_im)], axis=1).astype(BF16)
    return bd_b, bd_c


def _gelu_glu(y, wglu_ref):
    y = 0.5 * y * (1.0 + lax.erf(y * math.sqrt(0.5)))
    gate = jnp.dot(y.astype(BF16), wglu_ref[...], preferred_element_type=F32)
    return y * jax.nn.sigmoid(gate)


def _s5_in_proj(u_bf, bdb_ref, bur_ref, bui_ref):
    for c in range(S5_KCH):
        r = jnp.dot(u_bf[:, c * MXU_DIM:(c + 1) * MXU_DIM], bdb_ref[c], preferred_element_type=F32)
        bur_ref[:, c * S5_SCH:(c + 1) * S5_SCH] = r[:, :S5_SCH]
        bui_ref[:, c * S5_SCH:(c + 1) * S5_SCH] = r[:, S5_SCH:]


def _s5_out_proj(xr_ref, xi_ref, bdc_ref):
    ys = []
    for c in range(S5_KCH):
        xr = xr_ref[:, c * S5_SCH:(c + 1) * S5_SCH].astype(BF16)
        xi = xi_ref[:, c * S5_SCH:(c + 1) * S5_SCH].astype(BF16)
        ys.append(jnp.dot(xr, bdc_ref[c, :S5_SCH, :], preferred_element_type=F32)
                  + jnp.dot(xi, bdc_ref[c, S5_SCH:, :], preferred_element_type=F32))
    return jnp.concatenate(ys, axis=1)


def _s5_seq_body(n_pad, us_ref, upa_ref, upb_ref, x0r_ref, x0i_ref, bdb_ref, bdc_ref, pre_ref, pim_ref,
                 hre_ref, him_ref, qre_ref, qim_ref, d_ref, wglu_ref,
                 y_ref, xr_out_ref, xi_out_ref, *scratch):
    t = pl.program_id(1)
    n = N_STATE
    n_lc = D_S5 // LANES
    scan_per_chunk = S5_SCH // SCAN_LW
    seqs = [dict(zip(("bur", "bui", "cr", "ci", "perm"), scratch[5 * i:5 * i + 5]), idx=i, up=up)
            for i, up in enumerate((upa_ref, upb_ref))]

    @pl.when(t == 0)
    def _():
        for sq in seqs:
            sq["cr"][...] = jnp.broadcast_to(x0r_ref[sq["idx"]], (SUBLANES, n))
            sq["ci"][...] = jnp.broadcast_to(x0i_ref[sq["idx"]], (SUBLANES, n))

    def load_perm(ref, perm_ref):
        for c in range(n_lc):
            perm_ref[c] = ref[:, c * LANES:(c + 1) * LANES]
        return jnp.concatenate(
            [jnp.concatenate([perm_ref[c, pl.ds(j, SUBLANES, stride=SEG_LEN), :] for c in range(n_lc)], axis=1)
             for j in range(SEG_LEN)], axis=0)

    prow = lax.broadcasted_iota(jnp.int32, (TILE_T, 1), 0)
    time = (prow % SUBLANES) * SEG_LEN + prow // SUBLANES
    row8 = lax.broadcasted_iota(jnp.int32, (SUBLANES, SCAN_LW), 0)

    def load_u(sq):
        u_first = jnp.where(time >= n_pad, load_perm(us_ref, sq["perm"]), 0.0)
        sq["u"] = jnp.where(t == 0, u_first, load_perm(sq["up"], sq["perm"]))
        sq["u_bf"] = sq["u"].astype(BF16)

    def in_proj(sq, c):
        r = jnp.dot(sq["u_bf"][:, c * MXU_DIM:(c + 1) * MXU_DIM], bdb_ref[c], preferred_element_type=F32)
        sq["bur"][:, c * S5_SCH:(c + 1) * S5_SCH] = r[:, :S5_SCH]
        sq["bui"][:, c * S5_SCH:(c + 1) * S5_SCH] = r[:, S5_SCH:]

    def out_proj(sq, c):
        xr = sq["bur"][:, c * S5_SCH:(c + 1) * S5_SCH].astype(BF16)
        xi = sq["bui"][:, c * S5_SCH:(c + 1) * S5_SCH].astype(BF16)
        sq.setdefault("y", []).append(
            jnp.dot(xr, bdc_ref[c, :S5_SCH, :], preferred_element_type=F32)
            + jnp.dot(xi, bdc_ref[c, S5_SCH:, :], preferred_element_type=F32))

    def finish(sq):
        y = jnp.concatenate(sq["y"], axis=1) + d_ref[...] * sq["u"]
        y = _gelu_glu(y, wglu_ref)
        perm_ref = sq["perm"]
        for j in range(SEG_LEN):
            for c in range(n_lc):
                perm_ref[c, pl.ds(j, SUBLANES, stride=SEG_LEN), :] = y[j * SUBLANES:(j + 1) * SUBLANES,
                                                                       c * LANES:(c + 1) * LANES]
        r0 = sq["idx"] * TILE_T
        for c in range(n_lc):
            y_ref[r0:r0 + TILE_T, c * LANES:(c + 1) * LANES] = perm_ref[c]

    def scan_lanes(sq, lc):
        bur_ref, bui_ref, cr_ref, ci_ref = sq["bur"], sq["bui"], sq["cr"], sq["ci"]
        ls = slice(lc * SCAN_LW, (lc + 1) * SCAN_LW)
        bc = lambda ref, j: jnp.broadcast_to(ref[j:j + 1, ls], (SUBLANES, SCAN_LW))
        ar, ai = bc(pre_ref, 0), bc(pim_ref, 0)
        xr = jnp.zeros((SUBLANES, SCAN_LW), F32)
        xi = jnp.zeros((SUBLANES, SCAN_LW), F32)
        for j in range(SEG_LEN):
            rs = slice(j * SUBLANES, (j + 1) * SUBLANES)
            nr = ar * xr - ai * xi + bur_ref[rs, ls]
            ni = ar * xi + ai * xr + bui_ref[rs, ls]
            xr, xi = nr, ni
            bur_ref[rs, ls] = xr
            bui_ref[rs, ls] = xi
        er, ei = xr, xi
        for idx, k in enumerate((1, 2, 4)):
            sr = pltpu.roll(er, k, axis=0)
            si = pltpu.roll(ei, k, axis=0)
            hr = hre_ref[idx * 8:(idx + 1) * 8, ls]
            hi = him_ref[idx * 8:(idx + 1) * 8, ls]
            er, ei = er + (hr * sr - hi * si), ei + (hr * si + hi * sr)
        cpr = cr_ref[:, ls]
        cpi = ci_ref[:, ls]
        qr = qre_ref[:, ls]
        qi = qim_ref[:, ls]
        er, ei = er + (qr * cpr - qi * cpi), ei + (qr * cpi + qi * cpr)
        inr = jnp.where(row8 == 0, cpr, pltpu.roll(er, 1, axis=0))
        ini = jnp.where(row8 == 0, cpi, pltpu.roll(ei, 1, axis=0))
        cr_ref[:, ls] = jnp.broadcast_to(er[SUBLANES - 1:SUBLANES, :], (SUBLANES, SCAN_LW))
        ci_ref[:, ls] = jnp.broadcast_to(ei[SUBLANES - 1:SUBLANES, :], (SUBLANES, SCAN_LW))
        for j in range(SEG_LEN):
            rs = slice(j * SUBLANES, (j + 1) * SUBLANES)
            pr, pi = bc(pre_ref, j), bc(pim_ref, j)
            bur_ref[rs, ls] = bur_ref[rs, ls] + (pr * inr - pi * ini)
            bui_ref[rs, ls] = bui_ref[rs, ls] + (pr * ini + pi * inr)

    def scan_chunk(sq, c):
        for k in range(scan_per_chunk):
            scan_lanes(sq, c * scan_per_chunk + k)

    sa, sb = seqs
    load_u(sa)
    load_u(sb)
    for c in range(S5_KCH):
        in_proj(sa, c)
    for c in range(S5_KCH):
        in_proj(sb, c)
        scan_chunk(sa, c)
    for c in range(S5_KCH):
        out_proj(sa, c)
        scan_chunk(sb, c)
    finish(sa)
    for c in range(S5_KCH):
        out_proj(sb, c)
    finish(sb)

    @pl.when(t == pl.num_programs(1) - 1)
    def _():
        for sq in seqs:
            xr_out_ref[sq["idx"]] = sq["cr"][0:1, :]
            xi_out_ref[sq["idx"]] = sq["ci"][0:1, :]


def _s5_seq(z_small, z_seq, x0r, x0i, tabs, bd_b, bd_c, d_skip, wglu_bf, n_batch, n_tiles, n_pad):
    pre, pim, hre, him, qre, qim = tabs
    n = N_STATE
    full = lambda a: pl.BlockSpec(a.shape, lambda b, t: (0,) * a.ndim)
    per_seq = n_tiles - 1
    seq_tile = lambda s: pl.BlockSpec((TILE_T, D_S5),
                                      lambda p, t: ((2 * p + s) * per_seq + jnp.maximum(t - 1, 0), 0))
    seq_scratch = [pltpu.VMEM((TILE_T, n), F32), pltpu.VMEM((TILE_T, n), F32),
                   pltpu.VMEM((SUBLANES, n), F32), pltpu.VMEM((SUBLANES, n), F32),
                   pltpu.VMEM((D_S5 // LANES, TILE_T, LANES), F32)]
    return pl.pallas_call(
        functools.partial(_s5_seq_body, n_pad),
        grid=(n_batch // 2, n_tiles),
        in_specs=[
            pl.BlockSpec((TILE_T, D_S5), lambda p, t: (0, 0)),
            seq_tile(0), seq_tile(1),
            pl.BlockSpec((2, 1, n), lambda p, t: (p, 0, 0)),
            pl.BlockSpec((2, 1, n), lambda p, t: (p, 0, 0)),
            full(bd_b), full(bd_c), full(pre), full(pim), full(hre), full(him), full(qre), full(qim),
            full(d_skip), full(wglu_bf),
        ],
        out_specs=[
            pl.BlockSpec((2 * TILE_T, D_S5), lambda p, t: (p * n_tiles + t, 0)),
            pl.BlockSpec((2, 1, n), lambda p, t: (p, 0, 0)),
            pl.BlockSpec((2, 1, n), lambda p, t: (p, 0, 0)),
        ],
        out_shape=[
            jax.ShapeDtypeStruct((n_batch * n_tiles * TILE_T, D_S5), F32),
            jax.ShapeDtypeStruct((n_batch, 1, n), F32),
            jax.ShapeDtypeStruct((n_batch, 1, n), F32),
        ],
        scratch_shapes=seq_scratch + seq_scratch,
        compiler_params=_cparams(("arbitrary", "arbitrary"), 48),
        name="s5_seq",
    )(z_small, z_seq, z_seq, x0r, x0i, bd_b, bd_c, pre, pim, hre, him, qre, qim, d_skip, wglu_bf)


def _s5_frame_block(b, tile, n_tiles):
    return ((b // 2) * n_tiles + tile) * 2 + b % 2


def _s5_step_body(u_ref, x0r_ref, x0i_ref, bdb_ref, bdc_ref, pre_ref, pim_ref, d_ref, wglu_ref,
                  y_ref, xr_ref, xi_ref):
    u = u_ref[...]
    _s5_in_proj(u.astype(BF16), bdb_ref, xr_ref, xi_ref)
    ar = pre_ref[0:1, :]
    ai = pim_ref[0:1, :]
    x0r = x0r_ref[...]
    x0i = x0i_ref[...]
    xr_ref[...] = xr_ref[...] + (ar * x0r - ai * x0i)
    xi_ref[...] = xi_ref[...] + (ar * x0i + ai * x0r)
    y = _s5_out_proj(xr_ref, xi_ref, bdc_ref) + d_ref[...] * u
    y_ref[...] = _gelu_glu(y, wglu_ref)


def _s5_step(z_small, row_blk, x0r, x0i, tabs, bd_b, bd_c, d_skip, wglu_bf):
    pre, pim = tabs[0], tabs[1]
    rows = x0r.shape[0]
    n = N_STATE
    full = lambda a: pl.BlockSpec(a.shape, lambda i: (0,) * a.ndim)
    return pl.pallas_call(
        _s5_step_body,
        grid=(1,),
        in_specs=[pl.BlockSpec((rows, D_S5), lambda i: (row_blk, 0)), full(x0r), full(x0i),
                  full(bd_b), full(bd_c), full(pre), full(pim), full(d_skip), full(wglu_bf)],
        out_specs=[pl.BlockSpec((rows, D_S5), lambda i: (0, 0)),
                   pl.BlockSpec((rows, n), lambda i: (0, 0)), pl.BlockSpec((rows, n), lambda i: (0, 0))],
        out_shape=[jax.ShapeDtypeStruct((rows, D_S5), F32), jax.ShapeDtypeStruct((rows, n), F32),
                   jax.ShapeDtypeStruct((rows, n), F32)],
        compiler_params=_cparams(("arbitrary",), 48),
        name="s5_step",
    )(z_small, x0r, x0i, bd_b, bd_c, pre, pim, d_skip, wglu_bf)


NEG_INF = float("-inf")
N_STEP_SCALARS = 5 * N_HEADS


def _log_sigmoid(x):
    return jnp.minimum(x, 0.0) - jnp.log1p(jnp.exp(-jnp.abs(x)))


def _split3(x):
    hi = x.astype(BF16)
    r1 = x - hi.astype(F32)
    mid = r1.astype(BF16)
    lo = (r1 - mid.astype(F32)).astype(BF16)
    return hi, mid, lo


def _head_norm_gate(h, o, g):
    mu = jnp.mean(h, axis=-1, keepdims=True)
    hc = h - mu
    var = jnp.mean(hc * hc, axis=-1, keepdims=True)
    return jax.nn.sigmoid(o) * (hc * lax.rsqrt(var + LN_EPS) * g)


def _dot_nt(a, b):
    return lax.dot_general(a, b, (((1,), (1,)), ((), ())), preferred_element_type=F32)


def _dot_tn(a, b):
    return lax.dot_general(a, b, (((0,), (0,)), ((), ())), preferred_element_type=F32)


def _mlstm_seq_body(n_pad, sxm_ref, sv_ref, so_ref, sg_ref, pxm_ref, pv_ref, po_ref, pg_ref,
                    cw_ref, cb_ref, wq_ref, wk_ref, ng_ref,
                    y_ref, c_out_ref, n_out_ref, m_out_ref, c_ref, n_ref, m_ref, prev_ref):
    t = pl.program_id(1)
    L = TILE_T

    @pl.when(t == 0)
    def _():
        c_ref[...] = jnp.zeros_like(c_ref)
        n_ref[...] = jnp.zeros_like(n_ref)
        m_ref[...] = jnp.zeros_like(m_ref)
        prev_ref[...] = jnp.zeros_like(prev_ref)

    first = t == 0
    row = lax.broadcasted_iota(jnp.int32, (L, 1), 0)
    valid = jnp.logical_or(jnp.logical_not(first), row >= n_pad)
    xm = jnp.where(valid, jnp.where(first, sxm_ref[...], pxm_ref[...]), 0.0)
    v = jnp.where(first, sv_ref[...], pv_ref[...])
    o = jnp.where(first, so_ref[...], po_ref[...])
    gt = jnp.where(first, sg_ref[...], pg_ref[...])

    prev = prev_ref[...]

    def shifted(j):
        if j == 0:
            return xm
        return pltpu.roll(jnp.where(row >= L - j, prev, xm), j, axis=0)

    xc = cb_ref[...]
    for j in range(CONV_W):
        xc = xc + shifted(CONV_W - 1 - j) * cw_ref[j:j + 1, :]
    prev_ref[...] = xm
    xc = xc * jax.nn.sigmoid(xc)

    ig = jnp.where(valid, gt, NEG_INF)
    lf = jnp.where(valid, _log_sigmoid(gt), 0.0)
    ti = lax.broadcasted_iota(jnp.int32, (L, L), 0)
    si = lax.broadcasted_iota(jnp.int32, (L, L), 1)
    causal = si <= ti
    tri = jnp.where(causal, 1.0, 0.0).astype(BF16)
    bc = sum(jnp.dot(tri, p, preferred_element_type=F32) for p in _split3(lf))
    ig_t = ig.T
    bc_t = bc.T

    for h in range(N_HEADS):
        hs = slice(h * DH, (h + 1) * DH)
        b_col = bc[:, N_HEADS + h:N_HEADS + h + 1]
        b_row = bc_t[N_HEADS + h:N_HEADS + h + 1, :]
        ig_row = ig_t[h:h + 1, :]
        ig_col = ig[:, h:h + 1]
        m_prev = m_ref[h:h + 1, 0:1]
        dlog = jnp.where(causal, b_col - b_row + ig_row, NEG_INF)
        inter = b_col + m_prev
        m_t = jnp.maximum(jnp.max(dlog, axis=1, keepdims=True), inter)
        w = jnp.exp(dlog - m_t)
        g = jnp.exp(inter - m_t)
        xh = xc[:, hs].astype(BF16)
        q = jnp.dot(xh, wq_ref[h], preferred_element_type=F32)
        k = jnp.dot(xh, wk_ref[h], preferred_element_type=F32) * (DH ** -0.5)
        qb = q.astype(BF16)
        kb = k.astype(BF16)
        s = _dot_nt(qb, kb) * w
        vh = v[:, hs]
        cmat = c_ref[h]
        n_row = n_ref[h]
        num = jnp.dot(s.astype(BF16), vh.astype(BF16), preferred_element_type=F32) \
            + g * _dot_nt(qb, cmat.astype(BF16))
        den = jnp.sum(s, axis=1, keepdims=True) + g * jnp.sum(q * n_row, axis=1, keepdims=True)
        hh = num / jnp.maximum(jnp.abs(den), jnp.exp(-m_t))
        b_last = b_col[L - 1:L, :]
        wlog = b_last - b_col + ig_col
        m_new = jnp.maximum(b_last + m_prev, jnp.max(wlog, axis=0, keepdims=True))
        w_end = jnp.exp(wlog - m_new)
        g_end = jnp.exp(b_last + m_prev - m_new)
        c_ref[h] = g_end * cmat + _dot_tn((vh * w_end).astype(BF16), kb)
        n_ref[h] = g_end * n_row + jnp.sum(w_end * k, axis=0, keepdims=True)
        m_ref[h:h + 1, :] = jnp.broadcast_to(m_new, (1, LANES))
        y_ref[:, hs] = _head_norm_gate(hh, o[:, hs], ng_ref[:, hs])

    @pl.when(t == pl.num_programs(1) - 1)
    def _():
        c_out_ref[0] = c_ref[...]
        n_out_ref[0] = n_ref[...]
        m_out_ref[0] = m_ref[...]


def _mlstm_seq(z_small, z_seq, conv_w, conv_b, wq_bf, wk_bf, norm_g, n_batch, n_tiles, n_pad):
    per_seq = n_tiles - 1
    full = lambda a: pl.BlockSpec(a.shape, lambda b, t: (0,) * a.ndim)
    nb = D_ML // LANES

    def small(col, width):
        return pl.BlockSpec((TILE_T, width), lambda b, t: (0, col))

    def seq(col, width):
        return pl.BlockSpec((TILE_T, width), lambda b, t: (b * per_seq + jnp.maximum(t - 1, 0), col))

    return pl.pallas_call(
        functools.partial(_mlstm_seq_body, n_pad),
        grid=(n_batch, n_tiles),
        in_specs=[small(1, D_ML), small(2, D_ML), small(3, D_ML), small(GATE_COL, LANES),
                  seq(1, D_ML), seq(2, D_ML), seq(3, D_ML), seq(GATE_COL, LANES),
                  full(conv_w), full(conv_b), full(wq_bf), full(wk_bf), full(norm_g)],
        out_specs=[
            pl.BlockSpec((TILE_T, D_ML), lambda b, t: (b * n_tiles + t, 0)),
            pl.BlockSpec((1, N_HEADS, DH, DH), lambda b, t: (b, 0, 0, 0)),
            pl.BlockSpec((1, N_HEADS, 1, DH), lambda b, t: (b, 0, 0, 0)),
            pl.BlockSpec((1, SUBLANES, LANES), lambda b, t: (b, 0, 0)),
        ],
        out_shape=[
            jax.ShapeDtypeStruct((n_batch * n_tiles * TILE_T, D_ML), F32),
            jax.ShapeDtypeStruct((n_batch, N_HEADS, DH, DH), F32),
            jax.ShapeDtypeStruct((n_batch, N_HEADS, 1, DH), F32),
            jax.ShapeDtypeStruct((n_batch, SUBLANES, LANES), F32),
        ],
        scratch_shapes=[pltpu.VMEM((N_HEADS, DH, DH), F32), pltpu.VMEM((N_HEADS, 1, DH), F32),
                        pltpu.VMEM((SUBLANES, LANES), F32), pltpu.VMEM((TILE_T, D_ML), F32)],
        compiler_params=_cparams(("arbitrary", "arbitrary"), 48),
        name="mlstm_seq",
    )(z_small, z_small, z_small, z_small, z_seq, z_seq, z_seq, z_seq, conv_w, conv_b, wq_bf, wk_bf, norm_g)


def _mlstm_step_a_body(xm_ref, g_ref, conv0_ref, m0_ref, cw_ref, cb_ref, wq_ref, wk_ref,
                       q_ref, k_ref, sc_ref):
    xc = cb_ref[...]
    for j in range(CONV_W - 1):
        xc = xc + conv0_ref[j] * cw_ref[j:j + 1, :]
    xc = xc + xm_ref[...] * cw_ref[CONV_W - 1:CONV_W, :]
    xc = xc * jax.nn.sigmoid(xc)
    gt = g_ref[...]
    ig = gt[:, 0:N_HEADS]
    lf = _log_sigmoid(gt[:, N_HEADS:2 * N_HEADS])
    inter = lf + m0_ref[...]
    m_t = jnp.maximum(ig, inter)
    w = jnp.exp(ig - m_t)
    g = jnp.exp(inter - m_t)
    qks = []
    for h in range(N_HEADS):
        hs = slice(h * DH, (h + 1) * DH)
        xh = xc[:, hs].astype(BF16)
        q = jnp.dot(xh, wq_ref[h], preferred_element_type=F32)
        k = jnp.dot(xh, wk_ref[h], preferred_element_type=F32) * (DH ** -0.5)
        q_ref[:, hs] = q
        k_ref[:, hs] = k
        qks.append(jnp.sum(q * k, axis=1, keepdims=True))
    s = jnp.concatenate(qks, axis=1) * w
    rows = s.shape[0]
    sc_ref[...] = jnp.concatenate(
        [s, w, g, m_t, jnp.exp(-m_t), jnp.zeros((rows, LANES - 5 * N_HEADS), F32)], axis=1)


def _mlstm_step_b_body(bb, sc_ref, q_ref, k_ref, n_ref, v_ref, o_ref, ng_ref, c_ref,
                       y_ref, c_out_ref, n_out_ref):
    i0 = pl.program_id(0) * bb
    row = lax.broadcasted_iota(jnp.int32, (MXU_ROWS, DH), 0)

    def hi_lo(x):
        hi = x.astype(BF16).astype(F32)
        return jnp.broadcast_to(hi, (MXU_ROWS, DH)), jnp.broadcast_to(x - hi, (MXU_ROWS, DH))

    for i in range(bb):
        for h in range(N_HEADS):
            hs = slice(h * DH, (h + 1) * DH)
            base = (i0 + i) * N_STEP_SCALARS
            s = sc_ref[base + h]
            w = sc_ref[base + N_HEADS + h]
            g = sc_ref[base + 2 * N_HEADS + h]
            em = sc_ref[base + 4 * N_HEADS + h]
            rsel = pl.ds(i0 + i, 1)
            q_row = q_ref[rsel, hs]
            k_row = k_ref[rsel, hs]
            n_row = n_ref[rsel, hs]
            v_row = v_ref[rsel, hs]
            cmat = c_ref[i, h]
            qh, ql = hi_lo(q_row)
            qmat = jnp.where(row == 0, qh, jnp.where(row == 1, ql, 0.0)).astype(BF16)
            cq = _dot_nt(qmat, cmat.astype(BF16))
            num = s * v_row + g * (cq[0:1, :] + cq[1:2, :])
            den = s + g * jnp.sum(n_row * q_row, axis=1, keepdims=True)
            hh = num / jnp.maximum(jnp.abs(den), em)
            vh, vl = hi_lo(w * v_row)
            kh, kl = hi_lo(k_row)
            a = jnp.where(row < 2, vh, jnp.where(row < 4, vl, 0.0)).astype(BF16)
            b = jnp.where(row < 4, jnp.where(row % 2 == 0, kh, kl), 0.0).astype(BF16)
            c_out_ref[i, h] = g * cmat + _dot_tn(a, b)
            n_out_ref[rsel, hs] = g * n_row + w * k_row
            y_ref[rsel, hs] = _head_norm_gate(hh, o_ref[rsel, hs], ng_ref[:, hs])


def _mlstm_step(z_small, row_blk, conv0, c0, n0, m0, conv_w, conv_b, wq_bf, wk_bf, norm_g, bb=4):
    rows = c0.shape[0]
    full = lambda a: pl.BlockSpec(a.shape, lambda i: (0,) * a.ndim)
    conv0_t = conv0.transpose(1, 0, 2)
    q, k, sc = pl.pallas_call(
        _mlstm_step_a_body,
        grid=(1,),
        in_specs=[pl.BlockSpec((rows, D_ML), lambda i: (row_blk, 1)),
                  pl.BlockSpec((rows, LANES), lambda i: (row_blk, GATE_COL)),
                  full(conv0_t), full(m0), full(conv_w), full(conv_b), full(wq_bf), full(wk_bf)],
        out_specs=[pl.BlockSpec((rows, D_ML), lambda i: (0, 0)), pl.BlockSpec((rows, D_ML), lambda i: (0, 0)),
                   pl.BlockSpec((rows, LANES), lambda i: (0, 0))],
        out_shape=[jax.ShapeDtypeStruct((rows, D_ML), F32), jax.ShapeDtypeStruct((rows, D_ML), F32),
                   jax.ShapeDtypeStruct((rows, LANES), F32)],
        compiler_params=_cparams(("arbitrary",), 48),
        name="mlstm_step_a",
    )(z_small, z_small, conv0_t, m0, conv_w, conv_b, wq_bf, wk_bf)
    row_spec = pl.BlockSpec((rows, D_ML), lambda i: (0, 0))
    c_spec = pl.BlockSpec((bb, N_HEADS, DH, DH), lambda i: (i, 0, 0, 0))
    y, c_new, n_new = pl.pallas_call(
        functools.partial(_mlstm_step_b_body, bb),
        grid=(rows // bb,),
        in_specs=[pl.BlockSpec(memory_space=pltpu.SMEM), row_spec, row_spec, row_spec,
                  pl.BlockSpec((rows, D_ML), lambda i: (row_blk, 2)),
                  pl.BlockSpec((rows, D_ML), lambda i: (row_blk, 3)),
                  pl.BlockSpec((1, D_ML), lambda i: (0, 0)), c_spec],
        out_specs=[row_spec, c_spec, row_spec],
        out_shape=[jax.ShapeDtypeStruct((rows, D_ML), F32), jax.ShapeDtypeStruct(c0.shape, F32),
                   jax.ShapeDtypeStruct((rows, D_ML), F32)],
        compiler_params=_cparams(("arbitrary",), 56),
        name="mlstm_step_b",
    )(sc[:, :N_STEP_SCALARS].reshape(rows * N_STEP_SCALARS), q, k, n0, z_small, z_small, norm_g, c0)
    return y, c_new, n_new, sc[:, 3 * N_HEADS:4 * N_HEADS]


ROW_CH = D_MODEL // LANES


def _load_row_tiles(ref, row0, m, c):
    return ref[pl.ds(row0 * ROW_CH + c, m, stride=ROW_CH), :]


def _store_row_tiles(ref, row0, val):
    m = val.shape[0]
    for c in range(ROW_CH):
        ref[pl.ds(row0 * ROW_CH + c, m, stride=ROW_CH), :] = val[:, c * LANES:(c + 1) * LANES]


def _mix_out_body(xp_ref, xs_ref, ysp0_ref, ysp1_ref, yss_ref, ymp0_ref, ymp1_ref, yms_ref,
                  gin_ref, bin_ref, wout_ref, g1_ref, b1_ref, rwh_ref, rwl_ref, rb_ref,
                  h1_ref, e_ref, gate_ref):
    is_step = pl.program_id(0) == pl.num_programs(0) - 1
    consts = (gin_ref, bin_ref, wout_ref, g1_ref, b1_ref, rwh_ref, rwl_ref, rb_ref)
    outs = (h1_ref, e_ref, gate_ref)
    two = lambda a, b: jnp.concatenate([a[...], b[...]], axis=0)

    @pl.when(jnp.logical_not(is_step))
    def _():
        _mix_out_tile(xp_ref[...], two(ysp0_ref, ysp1_ref), two(ymp0_ref, ymp1_ref), consts, outs)

    @pl.when(is_step)
    def _():
        _mix_out_tile(two(xs_ref, xs_ref), two(yss_ref, yss_ref), two(yms_ref, yms_ref), consts, outs)


def _mix_out_tile(x, ys, ym, consts, outs):
    gin_ref, bin_ref, wout_ref, g1_ref, b1_ref, rwh_ref, rwl_ref, rb_ref = consts
    h1_ref, e_ref, gate_ref = outs
    hp = _layer_norm(x, gin_ref[...], bin_ref[...])
    ycat = jnp.concatenate([ys, ym], axis=1).astype(BF16)
    mix = jnp.dot(ycat, wout_ref[...], preferred_element_type=F32)
    h1 = _layer_norm(DEEPNORM_ALPHA * hp + mix, g1_ref[...], b1_ref[...])
    _store_row_tiles(h1_ref, 0, h1)
    xh = h1.astype(BF16)
    xl = (h1 - xh.astype(F32)).astype(BF16)
    logits = (jnp.dot(xh, rwh_ref[...], preferred_element_type=F32)
              + jnp.dot(xh, rwl_ref[...], preferred_element_type=F32)
              + jnp.dot(xl, rwh_ref[...], preferred_element_type=F32)) + rb_ref[...]
    rows = logits.shape[0]
    lane = lax.broadcasted_iota(jnp.int32, (rows, LANES), 1)
    logits = jnp.where(lane < N_EXPERTS, logits, NEG_INF)
    e_acc = jnp.zeros((rows, LANES), jnp.int32)
    v_acc = jnp.full((rows, LANES), NEG_INF, F32)
    for k in range(TOP_K):
        mx = jnp.max(logits, axis=1, keepdims=True)
        idx = jnp.min(jnp.where(logits == mx, lane, LANES), axis=1, keepdims=True)
        e_acc = jnp.where(lane == k, idx, e_acc)
        v_acc = jnp.where(lane == k, mx, v_acc)
        logits = jnp.where(lane == idx, NEG_INF, logits)
    p = jnp.exp(v_acc - jnp.max(v_acc, axis=1, keepdims=True))
    e_ref[...] = e_acc
    gate_ref[...] = p / jnp.sum(p, axis=1, keepdims=True)


def _mix_out(x_p, x_s, ys_p, ys_s, ym_p, ym_s, consts, tiles_per_seq):
    d = D_MODEL
    full = lambda a: pl.BlockSpec(a.shape, lambda i: (0,) * a.ndim)
    tm = 2 * TILE_T
    n_p = x_p.shape[0] // tm
    n_tiles = n_p + 1
    rows = n_tiles * tm

    def p_idx(i):
        return jnp.minimum(i, n_p - 1)

    def frame_idx(i, half):
        j = 2 * p_idx(i) + half
        return (j // tiles_per_seq) * (tiles_per_seq + 1) + j % tiles_per_seq + 1

    def s5_idx(i, half):
        j = 2 * p_idx(i) + half
        return _s5_frame_block(j // tiles_per_seq, j % tiles_per_seq + 1, tiles_per_seq + 1)

    frame = lambda width, half: pl.BlockSpec((TILE_T, width), lambda i: (frame_idx(i, half), 0))
    frame_s5 = lambda half: pl.BlockSpec((TILE_T, D_S5), lambda i: (s5_idx(i, half), 0))
    first = lambda width: pl.BlockSpec((TILE_T, width), lambda i: (0, 0))
    return pl.pallas_call(
        _mix_out_body,
        grid=(n_tiles,),
        in_specs=[pl.BlockSpec((tm, d), lambda i: (p_idx(i), 0)), first(d),
                  frame_s5(0), frame_s5(1), first(D_S5),
                  frame(D_ML, 0), frame(D_ML, 1), first(D_ML)] + [full(a) for a in consts],
        out_specs=[pl.BlockSpec((tm * ROW_CH, LANES), lambda i: (i, 0)),
                   pl.BlockSpec((tm, LANES), lambda i: (i, 0)),
                   pl.BlockSpec((tm, LANES), lambda i: (i, 0))],
        out_shape=[jax.ShapeDtypeStruct((rows * ROW_CH, LANES), F32), jax.ShapeDtypeStruct((rows, LANES), jnp.int32),
                   jax.ShapeDtypeStruct((rows, LANES), F32)],
        compiler_params=_cparams(("arbitrary",), 56),
        name="mix_out",
    )(x_p, x_s, ys_p, ys_p, ys_s, ym_p, ym_p, ym_s, *consts)


MOE_BLK = 128
MOE_TM = 1536
MOE_TF = 256
MOE_SUB_MAX = 5
DMA_UNROLL = 8
MOE_STAGES = 4
W_DMA_SPLIT = 4
SCALAR_UNROLL = 32


def _invert_rows_body(dest_ref, src_ref):
    def zero(i, c):
        for q in range(SCALAR_UNROLL):
            src_ref[i * SCALAR_UNROLL + q] = 0
        return c
    lax.fori_loop(0, src_ref.shape[0] // SCALAR_UNROLL, zero, 0)

    def put(i, c):
        for q in range(SCALAR_UNROLL):
            src_ref[dest_ref[i * SCALAR_UNROLL + q]] = i * (SCALAR_UNROLL // TOP_K) + q // TOP_K
        return c
    lax.fori_loop(0, dest_ref.shape[0] // SCALAR_UNROLL, put, 0)


def _invert_rows(dest_flat, rows_pad):
    return pl.pallas_call(
        _invert_rows_body,
        in_specs=[pl.BlockSpec(memory_space=pltpu.SMEM)],
        out_specs=pl.BlockSpec(memory_space=pltpu.SMEM),
        out_shape=jax.ShapeDtypeStruct((rows_pad,), jnp.int32),
        name="invert_rows",
    )(dest_flat)


def _moe_routing(top_e, n_tok):
    n_pairs = n_tok * TOP_K
    rows_pad = n_pairs + N_EXPERTS * MOE_BLK
    n_pass = N_EXPERTS + rows_pad // MOE_TM
    onehot = (top_e[:, :, None] == jnp.arange(N_EXPERTS, dtype=jnp.int32)).astype(jnp.int32).sum(1)
    incl = jnp.cumsum(onehot, axis=0)
    counts = incl[-1]
    rank = jnp.take_along_axis(incl - onehot, top_e, axis=1)
    padded = (counts + MOE_BLK - 1) // MOE_BLK * MOE_BLK
    pad_end = jnp.cumsum(padded)
    pad_start = pad_end - padded
    dest = pad_start[top_e] + rank
    src = _invert_rows(dest.reshape(-1).astype(jnp.int32), rows_pad)
    passes_e = (padded + MOE_TM - 1) // MOE_TM
    pass_end = jnp.cumsum(passes_e)
    u = jnp.arange(n_pass, dtype=jnp.int32)
    e_u = jnp.minimum(jnp.searchsorted(pass_end, u, side="right"), N_EXPERTS - 1).astype(jnp.int32)
    j_u = u - (pass_end - passes_e)[e_u]
    rem = padded[e_u] - j_u * MOE_TM
    nblk = jnp.where(u < pass_end[-1], jnp.clip(rem, 0, MOE_TM) // MOE_BLK, 0).astype(jnp.int32)
    blk0 = ((pad_start[e_u] + j_u * MOE_TM) // MOE_BLK).astype(jnp.int32)
    blk0 = jnp.where(nblk > 0, blk0, 0)
    last_e = e_u[jnp.maximum(pass_end[-1] - 1, 0)]
    e_u = jnp.where(nblk > 0, e_u, last_e)
    nblk = jnp.concatenate([nblk, (pad_end[-1:] // MOE_BLK).astype(jnp.int32)])
    return dest.astype(jnp.int32), src, e_u, blk0, nblk


def _moe_ffn_body(e_ref, blk0_ref, nblk_ref, src_ref, h1_ref, wgu_ref, wdn_ref, bgu_ref, bd_ref,
                  ybuf_ref, xg_ref, xb_ref, acc_ref, w32_a, w32_b, wb_a, wb_b, stage_ref, gsem, osem, wsem):
    u = pl.program_id(0)
    n_pass = pl.num_programs(0)
    n_f = D_FF // MOE_TF
    nblk = nblk_ref[u]
    row0 = blk0_ref[u] * MOE_BLK
    blk_rt = MOE_BLK * ROW_CH
    nxt = jnp.minimum(u + 1, n_pass - 1)
    nxt_active = jnp.logical_and(u + 1 < n_pass, nblk_ref[nxt] > 0)
    w32 = (w32_a, w32_b)
    wb = (wb_a, wb_b)

    def w_copies(p, f, par):
        e = e_ref[p]
        cols = pl.ds(pl.multiple_of(f * MOE_TF, MOE_TF), MOE_TF)
        cols_up = pl.ds(pl.multiple_of(D_FF + f * MOE_TF, MOE_TF), MOE_TF)
        g32, u32, d32 = w32[par]
        out = []
        for i in range(W_DMA_SPLIT):
            r1 = pl.ds(i * (D_MODEL // W_DMA_SPLIT), D_MODEL // W_DMA_SPLIT)
            r2 = pl.ds(i * (MOE_TF // W_DMA_SPLIT), MOE_TF // W_DMA_SPLIT)
            rows_dn = pl.ds(pl.multiple_of(f * MOE_TF + i * (MOE_TF // W_DMA_SPLIT), MOE_TF // W_DMA_SPLIT),
                            MOE_TF // W_DMA_SPLIT)
            out.append(pltpu.make_async_copy(wgu_ref.at[e, r1, cols], g32.at[r1], wsem.at[par]))
            out.append(pltpu.make_async_copy(wgu_ref.at[e, r1, cols_up], u32.at[r1], wsem.at[par]))
            out.append(pltpu.make_async_copy(wdn_ref.at[e, rows_dn, :], d32.at[r2], wsem.at[par]))
        return out

    def cast_w_half(par, half):
        g32, u32, d32 = w32[par]
        w1, w2 = wb[par]
        r1 = pl.ds(pl.multiple_of(half * (D_MODEL // 2), D_MODEL // 2), D_MODEL // 2)
        r2 = pl.ds(pl.multiple_of(half * (MOE_TF // 2), MOE_TF // 2), MOE_TF // 2)
        w1[r1, :MOE_TF] = g32[r1, :].astype(BF16)
        w1[r1, MOE_TF:] = u32[r1, :].astype(BF16)
        w2[r2, :] = d32[r2, :].astype(BF16)

    def cast_w(par):
        cast_w_half(par, 0)
        cast_w_half(par, 1)

    def for_all_rows(p, fn, part=0, n_parts=1):
        base = blk0_ref[p] * MOE_BLK
        n_groups = nblk_ref[p] * (MOE_BLK // SUBLANES)

        def body(rb, c):
            for s in range(SUBLANES):
                tok = src_ref[base + rb * SUBLANES + s]
                fn(pltpu.make_async_copy(h1_ref.at[pl.ds(tok * ROW_CH, ROW_CH)],
                                         xg_ref.at[rb, pl.ds(0, ROW_CH), s], gsem))
            return c
        lax.fori_loop((part * n_groups) // n_parts, ((part + 1) * n_groups) // n_parts, body, 0)

    def issue_gather(p, part=0, n_parts=1):
        for_all_rows(p, lambda cp: cp.start(), part, n_parts)

    def out_copy(b):
        dst = pl.ds(pl.multiple_of((row0 + b * MOE_BLK) * ROW_CH, blk_rt), blk_rt)
        return pltpu.make_async_copy(stage_ref.at[b % MOE_STAGES], ybuf_ref.at[dst], osem.at[b % MOE_STAGES])

    @pl.when(nblk > 0)
    def _():
        @pl.when(u == 0)
        def _():
            issue_gather(u)
            for cp in w_copies(u, 0, 0):
                cp.start()
            for cp in w_copies(u, 0, 0):
                cp.wait()
            cast_w(0)
            for cp in w_copies(u, 1, 1):
                cp.start()

        for_all_rows(u, lambda cp: cp.wait())

        def cast(b, c):
            rs = pl.ds(pl.multiple_of(b * MOE_BLK, MOE_BLK), MOE_BLK)
            gs = pl.ds(pl.multiple_of(b * (MOE_BLK // SUBLANES), MOE_BLK // SUBLANES), MOE_BLK // SUBLANES)
            for ch in range(ROW_CH):
                xb_ref[rs, ch * LANES:(ch + 1) * LANES] = xg_ref[gs, ch].reshape(MOE_BLK, LANES).astype(BF16)
            acc_ref[rs, :] = jnp.zeros((MOE_BLK, acc_ref.shape[1]), F32)
            return c
        lax.fori_loop(0, nblk, cast, 0)

        def ffn_rows(r0, m, f, par):
            w1, w2 = wb[par]
            bg = bgu_ref[0, :, pl.ds(pl.multiple_of(f * MOE_TF, MOE_TF), MOE_TF)]
            bu = bgu_ref[0, :, pl.ds(pl.multiple_of(D_FF + f * MOE_TF, MOE_TF), MOE_TF)]
            rs = pl.ds(pl.multiple_of(r0, MOE_BLK), m)
            h = jnp.dot(xb_ref[rs, :], w1[...], preferred_element_type=F32)
            x_glu = jnp.minimum(h[:, :MOE_TF] + bg, SWIGLU_LIMIT)
            x_lin = jnp.clip(h[:, MOE_TF:] + bu, -SWIGLU_LIMIT, SWIGLU_LIMIT)
            act = x_glu * jax.nn.sigmoid(SWIGLU_ALPHA * x_glu) * (x_lin + 1.0)
            acc_ref[rs, :] += jnp.dot(act.astype(BF16), w2[...], preferred_element_type=F32)

        def tile_step(f, par):
            in_pass1 = f + 1 < n_f
            in_pass2 = f + 2 < n_f

            @pl.when(jnp.logical_or(in_pass1, nxt_active))
            def _():
                for cp in w_copies(jnp.where(in_pass1, u, nxt), jnp.where(in_pass1, f + 1, 0), 1 - par):
                    cp.wait()

            @pl.when(jnp.logical_or(in_pass2, nxt_active))
            def _():
                for cp in w_copies(jnp.where(in_pass2, u, nxt), jnp.where(in_pass2, f + 2, f + 2 - n_f), par):
                    cp.start()

            @pl.when(jnp.logical_and(f >= 1, nxt_active))
            def _():
                issue_gather(nxt, f - 1, n_f - 1)

            n_sub = (nblk + (MOE_SUB_MAX - 1)) // MOE_SUB_MAX
            q = nblk // n_sub
            n_hi = nblk - q * n_sub

            def sub(i, blk):
                sz = q + (i < n_hi).astype(jnp.int32)
                for m in range(1, MOE_SUB_MAX + 1):
                    @pl.when(sz == m)
                    def _(m=m):
                        ffn_rows(blk * MOE_BLK, m * MOE_BLK, f, par)
                        cast_w_half(1 - par, jnp.minimum(i, 1))
                return blk + sz
            lax.fori_loop(0, n_sub, sub, 0)

            @pl.when(n_sub == 1)
            def _():
                cast_w_half(1 - par, 1)

        def tile_pair(f2, c):
            tile_step(2 * f2, 0)
            tile_step(2 * f2 + 1, 1)
            return c
        lax.fori_loop(0, n_f // 2, tile_pair, 0)

        def emit(b, c):
            @pl.when(b >= MOE_STAGES)
            def _():
                out_copy(b - MOE_STAGES).wait()
            rs = pl.ds(pl.multiple_of(b * MOE_BLK, MOE_BLK), MOE_BLK)
            val = acc_ref[rs, :] + bd_ref[0]
            for ch in range(ROW_CH):
                stage_ref[b % MOE_STAGES, pl.ds(ch, MOE_BLK, stride=ROW_CH), :] = val[:, ch * LANES:(ch + 1) * LANES]
            out_copy(b).start()
            return c
        lax.fori_loop(0, nblk, emit, 0)

        def drain(b, c):
            out_copy(b).wait()
            return c
        lax.fori_loop(jnp.maximum(nblk - MOE_STAGES, 0), nblk, drain, 0)

    @pl.when(u == n_pass - 1)
    def _():
        used = nblk_ref[n_pass]
        total = ybuf_ref.shape[0] // blk_rt
        stage_ref[0] = jnp.zeros((blk_rt, LANES), F32)

        def zero_copy(b):
            dst = pl.ds(pl.multiple_of(b * blk_rt, blk_rt), blk_rt)
            return pltpu.make_async_copy(stage_ref.at[0], ybuf_ref.at[dst], osem.at[0])

        def start(b, c):
            zero_copy(b).start()
            return c
        lax.fori_loop(used, total, start, 0)

        def wait(b, c):
            zero_copy(b).wait()
            return c
        lax.fori_loop(used, total, wait, 0)


def _moe_ffn(h1, src, e_u, blk0, nblk, w_gu, b_gu, w_dn, b_dn, rows_pad):
    d = D_MODEL
    n_pass = e_u.shape[0]
    w32_set = (pltpu.VMEM((d, MOE_TF), F32), pltpu.VMEM((d, MOE_TF), F32), pltpu.VMEM((MOE_TF, d), F32))
    wb_set = (pltpu.VMEM((d, 2 * MOE_TF), BF16), pltpu.VMEM((MOE_TF, d), BF16))
    grid_spec = pltpu.PrefetchScalarGridSpec(
        num_scalar_prefetch=4,
        grid=(n_pass,),
        in_specs=[
            pl.BlockSpec(memory_space=pl.ANY),
            pl.BlockSpec(memory_space=pl.ANY),
            pl.BlockSpec(memory_space=pl.ANY),
            pl.BlockSpec((1, 1, 2 * D_FF), lambda u, e, b0, nb, s: (e[u], 0, 0)),
            pl.BlockSpec((1, 1, d), lambda u, e, b0, nb, s: (e[u], 0, 0)),
        ],
        out_specs=pl.BlockSpec(memory_space=pl.ANY),
        scratch_shapes=[pltpu.VMEM((MOE_TM // SUBLANES, ROW_CH, SUBLANES, LANES), F32), pltpu.VMEM((MOE_TM, d), BF16),
                        pltpu.VMEM((MOE_TM, d), F32), w32_set, w32_set, wb_set, wb_set,
                        pltpu.VMEM((MOE_STAGES, MOE_BLK * ROW_CH, LANES), F32),
                        pltpu.SemaphoreType.DMA(()), pltpu.SemaphoreType.DMA((MOE_STAGES,)),
                        pltpu.SemaphoreType.DMA((2,))],
    )
    return pl.pallas_call(
        _moe_ffn_body,
        grid_spec=grid_spec,
        out_shape=jax.ShapeDtypeStruct((rows_pad * ROW_CH, LANES), F32),
        compiler_params=_cparams(("arbitrary",), 60),
        name="moe_ffn",
    )(e_u, blk0, nblk, src, h1, w_gu, w_dn, b_gu.reshape(N_EXPERTS, 1, 2 * D_FF), b_dn.reshape(N_EXPERTS, 1, d))


def _moe_combine_body(dest_ref, h1_ref, gate_ref, g2_ref, b2_ref, ybuf_ref, outp_ref, outs_ref,
                      buf_ref, sem, pre_ref):
    i = pl.program_id(0)
    n_pairs = TILE_T * TOP_K
    slot = i % 2

    def copy(tile, sl, rb, s, k):
        src_row = dest_ref[tile * n_pairs + rb * (SUBLANES * TOP_K) + (s * TOP_K + k)]
        return pltpu.make_async_copy(ybuf_ref.at[pl.ds(src_row * ROW_CH, ROW_CH)],
                                     buf_ref.at[sl, k, rb, pl.ds(0, ROW_CH), s], sem.at[sl])

    def for_all_rows(tile, sl, fn):
        def body(rb, c):
            for s in range(SUBLANES):
                for k in range(TOP_K):
                    fn(copy(tile, sl, rb, s, k))
            return c
        lax.fori_loop(0, TILE_T // SUBLANES, body, 0)

    def issue(tile, sl):
        for_all_rows(tile, sl, lambda cp: cp.start())

    @pl.when(i == 0)
    def _():
        issue(0, 0)

    @pl.when(i + 1 < pl.num_programs(0))
    def _():
        issue(i + 1, 1 - slot)

    for_all_rows(i, slot, lambda cp: cp.wait())

    gate = gate_ref[...]
    cs = lambda c: slice(c * LANES, (c + 1) * LANES)
    part = jnp.zeros((TILE_T, LANES), F32)
    gate_b = [jnp.broadcast_to(gate[:, k:k + 1], (TILE_T, LANES)) for k in range(TOP_K)]
    for c in range(ROW_CH):
        fc = None
        for k in range(TOP_K):
            v = buf_ref[slot, k, :, c].reshape(TILE_T, LANES) * gate_b[k]
            fc = v if fc is None else fc + v
        pre = DEEPNORM_ALPHA * _load_row_tiles(h1_ref, 0, TILE_T, c) + fc
        pre_ref[:, cs(c)] = pre
        part = part + pre
    mu = jnp.sum(part, axis=1, keepdims=True) * (1.0 / D_MODEL)
    part = jnp.zeros((TILE_T, LANES), F32)
    for c in range(ROW_CH):
        dlt = pre_ref[:, cs(c)] - mu
        part = part + dlt * dlt
    rstd = lax.rsqrt(jnp.sum(part, axis=1, keepdims=True) * (1.0 / D_MODEL) + LN_EPS)
    is_step = i == pl.num_programs(0) - 1

    def write(out_ref):
        for c in range(ROW_CH):
            out_ref[:, cs(c)] = (pre_ref[:, cs(c)] - mu) * rstd * g2_ref[:, cs(c)] + b2_ref[:, cs(c)]

    @pl.when(jnp.logical_not(is_step))
    def _():
        write(outp_ref)

    @pl.when(is_step)
    def _():
        write(outs_ref)


def _moe_combine(dest, h1, gates, g2, b2, ybuf, n_tok):
    d = D_MODEL
    n_tiles = n_tok // TILE_T
    grid_spec = pltpu.PrefetchScalarGridSpec(
        num_scalar_prefetch=1,
        grid=(n_tiles,),
        in_specs=[pl.BlockSpec((TILE_T * ROW_CH, LANES), lambda i, s: (i, 0)),
                  pl.BlockSpec((TILE_T, LANES), lambda i, s: (i, 0)),
                  pl.BlockSpec((1, d), lambda i, s: (0, 0)),
                  pl.BlockSpec((1, d), lambda i, s: (0, 0)),
                  pl.BlockSpec(memory_space=pl.ANY)],
        out_specs=[pl.BlockSpec((TILE_T, d), lambda i, s: (jnp.minimum(i, n_tiles - 2), 0)),
                   pl.BlockSpec((TILE_T, d), lambda i, s: (0, 0))],
        scratch_shapes=[pltpu.VMEM((2, TOP_K, TILE_T // SUBLANES, ROW_CH, SUBLANES, LANES), F32),
                        pltpu.SemaphoreType.DMA((2,)),
                        pltpu.VMEM((TILE_T, d), F32)],
    )
    return pl.pallas_call(
        _moe_combine_body,
        grid_spec=grid_spec,
        out_shape=[jax.ShapeDtypeStruct((n_tok - TILE_T, d), F32), jax.ShapeDtypeStruct((TILE_T, d), F32)],
        compiler_params=_cparams(("arbitrary",), 48),
        name="moe_combine",
    )(dest.reshape(-1), h1, gates, g2, b2, ybuf)


def kernel(x_prompt, x_sample, state_s5_re, state_s5_im, state_mlstm_c, state_mlstm_n, state_mlstm_m, state_mlstm_conv, meta_tokens, ln_in_g, ln_in_b, w_in, b_in, s5_a_re, s5_a_im, s5_log_dt, s5_b_re, s5_b_im, s5_c_re, s5_c_im, s5_d, s5_w_glu, mlstm_conv_w, mlstm_conv_b, mlstm_wq, mlstm_wk, mlstm_norm_g, w_out, ln1_g, ln1_b, router_w, router_b, w_gate_up, b_gate_up, w_down, b_down, ln2_g, ln2_b):
    bsz, seq, d = x_prompt.shape
    dec_b = x_sample.shape[0]
    n_pad = TILE_T - N_META
    x_small = jnp.concatenate([jnp.zeros((n_pad, d), F32), meta_tokens, x_sample.reshape(dec_b, d)], axis=0)
    w_in_p = jnp.pad(w_in[0], ((0, 0), (0, N_IN_PAD - N_IN))).astype(BF16)
    b_in_p = jnp.pad(b_in[0], (0, N_IN_PAD - N_IN)).reshape(1, N_IN_PAD)
    g_in = ln_in_g.reshape(1, d)
    bb_in = ln_in_b.reshape(1, d)
    z_p = _inproj(x_prompt.reshape(bsz * seq, d), g_in, bb_in, w_in_p, b_in_p, 512, 1408)
    z_s = _inproj(x_small, g_in, bb_in, w_in_p, b_in_p, 256, 1408)
    tabs = _s5_prep(s5_a_re[0], s5_a_im[0], s5_log_dt[0], s5_b_re[0], s5_b_im[0])
    bd_b, bd_c = _s5_block_diag(tabs[6], tabs[7], s5_c_re[0], s5_c_im[0])
    wglu_bf = s5_w_glu[0].astype(BF16)
    d_skip = s5_d[0].reshape(1, D_S5)
    n_tiles = seq // TILE_T + 1
    zero_state = jnp.zeros((bsz, 1, N_STATE), F32)
    y_s5_p, s5r_p, s5i_p = _s5_seq(z_s, z_p, zero_state, zero_state, tabs[:6], bd_b, bd_c, d_skip, wglu_bf,
                                   bsz, n_tiles, n_pad)
    y_s5_s, s5r_s, s5i_s = _s5_step(z_s, 1, state_s5_re[0].reshape(dec_b, N_STATE),
                                    state_s5_im[0].reshape(dec_b, N_STATE), tabs[:6], bd_b, bd_c, d_skip, wglu_bf)

    conv_w = mlstm_conv_w[0]
    conv_b = mlstm_conv_b[0].reshape(1, D_ML)
    wq_bf = mlstm_wq[0].astype(BF16)
    wk_bf = mlstm_wk[0].astype(BF16)
    norm_g = mlstm_norm_g[0].reshape(1, D_ML)
    y_ml_p, c_p, n_p, m_p = _mlstm_seq(z_s, z_p, conv_w, conv_b, wq_bf, wk_bf, norm_g, bsz, n_tiles, n_pad)
    conv0 = state_mlstm_conv[0]
    y_ml_s, c_s, n_s, m_s = _mlstm_step(z_s, 1, conv0, state_mlstm_c[0], state_mlstm_n[0].reshape(dec_b, D_ML),
                                        state_mlstm_m[0], conv_w, conv_b, wq_bf, wk_bf, norm_g)

    rw = jnp.pad(router_w[0], ((0, 0), (0, LANES - N_EXPERTS)))
    rw_hi = rw.astype(BF16)
    rw_lo = (rw - rw_hi.astype(F32)).astype(BF16)
    rb = jnp.pad(router_b[0], (0, LANES - N_EXPERTS)).reshape(1, LANES)
    consts = (g_in, bb_in, w_out[0].astype(BF16), ln1_g[0].reshape(1, d), ln1_b[0].reshape(1, d), rw_hi, rw_lo, rb)
    h1, top_e, gates = _mix_out(x_prompt.reshape(bsz * seq, d), x_sample.reshape(dec_b, d),
                                y_s5_p, y_s5_s, y_ml_p, y_ml_s, consts, seq // TILE_T)
    n_tok = bsz * seq + dec_b
    dest, src, e_u, blk0, nblk = _moe_routing(top_e[:n_tok, :TOP_K], n_tok)
    ybuf = _moe_ffn(h1, src, e_u, blk0, nblk, w_gate_up[0], b_gate_up[0], w_down[0], b_down[0], src.shape[0])
    out_p, out_s = _moe_combine(dest, h1, gates, ln2_g[0].reshape(1, d), ln2_b[0].reshape(1, d), ybuf, n_tok)

    y_prompt = out_p.reshape(bsz, seq, d)
    y_sample = out_s.reshape(dec_b, 1, d)
    xm_p = z_p.reshape(bsz, seq, N_IN_PAD)[:, seq - (CONV_W - 1):, D_S5:D_S5 + D_ML]
    xm_s = z_s[TILE_T:, D_S5:D_S5 + D_ML]
    conv_s = jnp.concatenate([conv0[:, 1:], xm_s[:, None, :]], axis=1)
    return (y_prompt, y_sample,
            s5r_p.reshape(1, bsz, N_GROUPS, S5_STATE), s5i_p.reshape(1, bsz, N_GROUPS, S5_STATE),
            c_p[None], n_p.reshape(1, bsz, N_HEADS, DH), m_p[None, :, :N_HEADS, 0], xm_p[None],
            s5r_s.reshape(1, dec_b, N_GROUPS, S5_STATE), s5i_s.reshape(1, dec_b, N_GROUPS, S5_STATE),
            c_s[None], n_s.reshape(1, dec_b, N_HEADS, DH), m_s[None], conv_s[None])
```
